```python
import math
import jax, jax.numpy as jnp
from jax import lax
import numpy as np

D_MODEL = 1024
BATCH = 8
SEQ = 2048
DEPTH = 2

CHUNK = 64
RET_HEADS = 8
RET_QK = D_MODEL // 2
RET_V = D_MODEL
RET_QK_DIM = RET_QK // RET_HEADS
RET_V_DIM = RET_V // RET_HEADS
ROPE_BASE = 10000.0
S5_WIDTH = D_MODEL // 2
S5_GROUP_CH = 16
S5_GROUPS = S5_WIDTH // S5_GROUP_CH
S5_STATE = 64
MOE_GROUPS = 4
EXPERTS_PER_GROUP = 8
N_EXPERTS = MOE_GROUPS * EXPERTS_PER_GROUP
TOP_K_IN_GROUP = 2
EXPERT_FF = D_MODEL // 4
LN_EPS = 1e-5
HEAD_NORM_EPS = 1e-6
DN_ALPHA = (2 * DEPTH) ** 0.25
DN_BETA = (8 * DEPTH) ** -0.25
IN_WIDTH = 2 * RET_QK + 2 * RET_V + S5_WIDTH + 2 * D_MODEL
IN_SPLITS = (RET_QK, 2 * RET_QK, 2 * RET_QK + RET_V, 2 * RET_QK + 2 * RET_V,
             2 * RET_QK + 2 * RET_V + S5_WIDTH, 2 * RET_QK + 2 * RET_V + S5_WIDTH + D_MODEL)

kernel_name = 'hybrid_retention_s5_hmoe'


def layer_norm(x, g, b):
    xf = x.astype(jnp.float32)
    mu = jnp.mean(xf, -1, keepdims=True)
    var = jnp.mean(jnp.square(xf - mu), -1, keepdims=True)
    return ((xf - mu) * lax.rsqrt(var + LN_EPS) * g.astype(jnp.float32) + b.astype(jnp.float32)).astype(x.dtype)


def rope_tables(seq_len, dim):
    half = dim // 2
    inv_freq = ROPE_BASE ** (-jnp.arange(half, dtype=jnp.float32) / half)
    ang = jnp.arange(seq_len, dtype=jnp.float32)[:, None] * inv_freq[None, :]
    return jnp.cos(ang), jnp.sin(ang)


def rotary(x, cos, sin):
    x1, x2 = jnp.split(x, 2, axis=-1)
    c = cos[None, :, None, :]
    s = sin[None, :, None, :]
    return jnp.concatenate([x1 * c - x2 * s, x1 * s + x2 * c], axis=-1)


def chunk_retention(q, k, v):
    b, s, h, dk = q.shape
    dv = v.shape[-1]
    nc = s // CHUNK
    log_gamma = jnp.log1p(-(2.0 ** (-5.0 - jnp.arange(h, dtype=jnp.float32))))
    pos = jnp.arange(CHUNK, dtype=jnp.float32)
    intra = jnp.exp(log_gamma[:, None, None] * jnp.abs(pos[:, None] - pos[None, :]))
    k_decay = jnp.exp(log_gamma[None, :] * (CHUNK - 1.0 - pos)[:, None])
    q_decay = jnp.exp(log_gamma[None, :] * (pos + 1.0)[:, None])
    chunk_decay = jnp.exp(log_gamma * CHUNK)
    qc = q.reshape(b, nc, CHUNK, h, dk)
    kc = k.reshape(b, nc, CHUNK, h, dk) * (dk ** -0.5)
    vc = v.reshape(b, nc, CHUNK, h, dv)
    scores = jnp.einsum('bnihd,bnjhd->bnhij', qc, kc) * intra[None, None]
    o_intra = jnp.einsum('bnhij,bnjhe->bnihe', scores, vc)
    kv = jnp.einsum('bnjhd,bnjhe->nbhde', kc * k_decay[:, :, None], vc)

    def step(state, kv_n):
        return chunk_decay[None, :, None, None] * state + kv_n, state

    _, states = lax.scan(step, jnp.zeros_like(kv[0]), kv)
    o_inter = jnp.einsum('bnihd,nbhde->bnihe', qc * q_decay[:, :, None], states)
    return (o_intra + o_inter).reshape(b, s, h, dv)


def s5_ssm(u, lam_re, lam_im, log_step, b_re, b_im, c_re, c_im, d_skip):
    f32 = jnp.float32
    bsz, s, _ = u.shape
    lam_re = jnp.minimum(lam_re.astype(f32), -1e-4)
    lam_im = lam_im.astype(f32)
    b_re, b_im = b_re.astype(f32), b_im.astype(f32)
    c_re, c_im = c_re.astype(f32), c_im.astype(f32)
    ug = u.reshape(bsz, s, S5_GROUPS, S5_GROUP_CH)
    step = jnp.exp(log_step.astype(f32))[:, None]
    mag = jnp.exp(lam_re * step)
    ang = lam_im * step
    ab_re = mag * jnp.cos(ang)
    ab_im = mag * jnp.sin(ang)
    den = lam_re * lam_re + lam_im * lam_im
    n_re = ab_re - 1.0
    zc_re = (n_re * lam_re + ab_im * lam_im) / den
    zc_im = (ab_im * lam_re - n_re * lam_im) / den
    bb_re = zc_re[..., None] * b_re - zc_im[..., None] * b_im
    bb_im = zc_re[..., None] * b_im + zc_im[..., None] * b_re
    bu_re = jnp.einsum('bsgc,gnc->bsgn', ug, bb_re)
    bu_im = jnp.einsum('bsgc,gnc->bsgn', ug, bb_im)
    a_re = jnp.broadcast_to(ab_re[None, None], (1, s, S5_GROUPS, S5_STATE))
    a_im = jnp.broadcast_to(ab_im[None, None], (1, s, S5_GROUPS, S5_STATE))

    def combine(e1, e2):
        a1r, a1i, b1r, b1i = e1
        a2r, a2i, b2r, b2i = e2
        return (a2r * a1r - a2i * a1i, a2r * a1i + a2i * a1r,
                a2r * b1r - a2i * b1i + b2r, a2r * b1i + a2i * b1r + b2i)

    _, _, h_re, h_im = lax.associative_scan(combine, (a_re, a_im, bu_re, bu_im), axis=1)
    y = jnp.einsum('bsgn,gcn->bsgc', h_re, c_re) - jnp.einsum('bsgn,gcn->bsgc', h_im, c_im)
    return y.reshape(bsz, s, S5_WIDTH) + d_skip.astype(f32) * u


def hybrid_mixer(h, w_in, lam_re, lam_im, log_step, b_re, b_im, c_re, c_im, d_skip,
                 w_glu, w_branch_ret, w_branch_s5, w_out, cos, sin):
    f32 = jnp.float32
    bsz, s, _ = h.shape
    proj = h @ w_in
    q, k, v, g_ret, u, gate_ret, gate_s5 = jnp.split(proj, IN_SPLITS, axis=-1)
    q = rotary(q.reshape(bsz, s, RET_HEADS, RET_QK_DIM).astype(f32), cos, sin)
    k = rotary(k.reshape(bsz, s, RET_HEADS, RET_QK_DIM).astype(f32), cos, sin)
    v = v.reshape(bsz, s, RET_HEADS, RET_V_DIM).astype(f32)
    o = chunk_retention(q, k, v)
    mu = jnp.mean(o, -1, keepdims=True)
    var = jnp.mean(jnp.square(o - mu), -1, keepdims=True)
    o = ((o - mu) * lax.rsqrt(var + HEAD_NORM_EPS)).reshape(bsz, s, RET_V).astype(h.dtype)
    ret_branch = (jax.nn.silu(g_ret) * o) @ w_branch_ret
    y = s5_ssm(u.astype(f32), lam_re, lam_im, log_step, b_re, b_im, c_re, c_im, d_skip).astype(h.dtype)
    z = jax.nn.gelu(y)
    s5_branch = (z * jax.nn.sigmoid(z @ w_glu)) @ w_branch_s5
    merged = jax.nn.sigmoid(gate_ret) * ret_branch + jax.nn.sigmoid(gate_s5) * s5_branch
    return merged @ w_out


def hier_moe(h, w_rg, b_rg, w_re, b_re, w_gate, w_up, w_down):
    f32 = jnp.float32
    bsz, s, d = h.shape
    xt = h.reshape(-1, d)
    t = xt.shape[0]
    g_prob = jax.nn.softmax((xt @ w_rg + b_rg).astype(f32), axis=-1)
    g_top, g_idx = lax.top_k(g_prob, 1)
    g_onehot = jax.nn.one_hot(g_idx[:, 0], MOE_GROUPS, dtype=f32)
    e_logits = (xt @ w_re + b_re).astype(f32).reshape(t, MOE_GROUPS, EXPERTS_PER_GROUP)
    e_logits = jnp.einsum('tge,tg->te', e_logits, g_onehot)
    e_prob = jax.nn.softmax(e_logits, axis=-1)
    e_top, e_idx = lax.top_k(e_prob, TOP_K_IN_GROUP)
    e_top = e_top / jnp.sum(e_top, -1, keepdims=True)
    w_local = jnp.sum(jax.nn.one_hot(e_idx, EXPERTS_PER_GROUP, dtype=f32) * e_top[..., None], axis=1)
    comb = (g_onehot[:, :, None] * (g_top[:, :, None] * w_local[:, None, :])).astype(h.dtype)
    out = jnp.zeros_like(xt)
    for gi in range(MOE_GROUPS):
        sl = slice(gi * EXPERTS_PER_GROUP, (gi + 1) * EXPERTS_PER_GROUP)
        act = jax.nn.silu(jnp.einsum('td,edf->tef', xt, w_gate[sl])) * jnp.einsum('td,edf->tef', xt, w_up[sl])
        out = out + jnp.einsum('tef,efd->td', act * comb[:, gi, :, None], w_down[sl])
    return out.reshape(bsz, s, d)


def setup_inputs(seed: int = 0) -> dict:
    key = jax.random.key(seed)
    ks = jax.random.split(key, 32)
    f32 = jnp.float32
    L, D = DEPTH, D_MODEL
    G, N, GC = S5_GROUPS, S5_STATE, S5_GROUP_CH

    def nrm(k, shape, scale):
        return scale * jax.random.normal(k, shape, f32)

    col_scale = jnp.concatenate([jnp.ones((2 * RET_QK,), f32), jnp.full((RET_V,), DN_BETA, f32),
                                 jnp.ones((IN_WIDTH - 2 * RET_QK - RET_V,), f32)])
    return {
        'x': jax.random.normal(ks[0], (BATCH, SEQ, D), f32),
        'ln_in_g': 1.0 + nrm(ks[1], (D,), 0.01),
        'ln_in_b': nrm(ks[2], (D,), 0.01),
        'w_in': nrm(ks[3], (L, D, IN_WIDTH), D ** -0.5) * col_scale,
        's5_lambda_re': -0.5 + nrm(ks[4], (L, G, N), 0.01),
        's5_lambda_im': math.pi * jnp.arange(N, dtype=f32) + nrm(ks[5], (L, G, N), 0.01),
        's5_log_step': jax.random.uniform(ks[6], (L, G), f32, math.log(1e-3), math.log(1e-1)),
        's5_b_re': nrm(ks[7], (L, G, N, GC), (2.0 * GC) ** -0.5),
        's5_b_im': nrm(ks[8], (L, G, N, GC), (2.0 * GC) ** -0.5),
        's5_c_re': nrm(ks[9], (L, G, GC, N), N ** -0.5),
        's5_c_im': nrm(ks[10], (L, G, GC, N), N ** -0.5),
        's5_d': nrm(ks[11], (L, S5_WIDTH), 1.0),
        'w_glu': nrm(ks[12], (L, S5_WIDTH, S5_WIDTH), S5_WIDTH ** -0.5),
        'w_branch_ret': nrm(ks[13], (L, RET_V, D), RET_V ** -0.5 * DN_BETA),
        'w_branch_s5': nrm(ks[14], (L, S5_WIDTH, D), S5_WIDTH ** -0.5 * DN_BETA),
        'w_out': nrm(ks[15], (L, D, D), D ** -0.5 * DN_BETA),
        'ln_mix_g': 1.0 + nrm(ks[16], (L, D), 0.01),
        'ln_mix_b': nrm(ks[17], (L, D), 0.01),
        'w_router_group': nrm(ks[18], (L, D, MOE_GROUPS), D ** -0.5),
        'b_router_group': nrm(ks[19], (L, MOE_GROUPS), 0.01),
        'w_router_expert': nrm(ks[20], (L, D, N_EXPERTS), D ** -0.5),
        'b_router_expert': nrm(ks[21], (L, N_EXPERTS), 0.01),
        'w_exp_gate': nrm(ks[22], (L, N_EXPERTS, D, EXPERT_FF), D ** -0.5),
        'w_exp_up': nrm(ks[23], (L, N_EXPERTS, D, EXPERT_FF), D ** -0.5 * DN_BETA),
        'w_exp_down': nrm(ks[24], (L, N_EXPERTS, EXPERT_FF, D), EXPERT_FF ** -0.5 * DN_BETA),
        'ln_ffn_g': 1.0 + nrm(ks[25], (L, D), 0.01),
        'ln_ffn_b': nrm(ks[26], (L, D), 0.01),
    }


def reference(x, ln_in_g, ln_in_b, w_in, s5_lambda_re, s5_lambda_im, s5_log_step,
              s5_b_re, s5_b_im, s5_c_re, s5_c_im, s5_d, w_glu, w_branch_ret, w_branch_s5,
              w_out, ln_mix_g, ln_mix_b, w_router_group, b_router_group, w_router_expert,
              b_router_expert, w_exp_gate, w_exp_up, w_exp_down, ln_ffn_g, ln_ffn_b):
    cos, sin = rope_tables(x.shape[1], RET_QK_DIM)
    h = layer_norm(x, ln_in_g, ln_in_b)
    for l in range(DEPTH):
        mix = hybrid_mixer(h, w_in[l], s5_lambda_re[l], s5_lambda_im[l], s5_log_step[l],
                           s5_b_re[l], s5_b_im[l], s5_c_re[l], s5_c_im[l], s5_d[l],
                           w_glu[l], w_branch_ret[l], w_branch_s5[l], w_out[l], cos, sin)
        h = layer_norm(DN_ALPHA * h + mix, ln_mix_g[l], ln_mix_b[l])
        ffn = hier_moe(h, w_router_group[l], b_router_group[l], w_router_expert[l],
                       b_router_expert[l], w_exp_gate[l], w_exp_up[l], w_exp_down[l])
        h = layer_norm(DN_ALPHA * h + ffn, ln_ffn_g[l], ln_ffn_b[l])
    return h
```

```python
import functools
import math

import jax
import jax.numpy as jnp
from jax import lax
from jax.experimental import pallas as pl
from jax.experimental.pallas import tpu as pltpu

D_MODEL = 1024
CHUNK = 64
RET_HEADS = 8
RET_QK = 512
RET_V = 1024
RET_QK_DIM = 64
RET_V_DIM = 128
ROPE_BASE = 10000.0
S5_WIDTH = 512
S5_GROUP_CH = 16
S5_GROUPS = 32
S5_STATE = 64
MOE_GROUPS = 4
EXPERTS_PER_GROUP = 8
N_EXPERTS = 32
EXPERT_FF = 256
LN_EPS = 1e-5
HEAD_NORM_EPS = 1e-6
DEPTH = 2
DN_ALPHA = (2 * DEPTH) ** 0.25
IN_WIDTH = 2 * RET_QK + 2 * RET_V + S5_WIDTH + 2 * D_MODEL
PROJ_BLK = 512
N_PROJ_BLK = IN_WIDTH // PROJ_BLK

RET_SUPER = 128
S5_TT = 64
S5_COLS = 128
S5_NSLICE = S5_WIDTH // S5_COLS
S5_SLICE_STATE = (S5_COLS // S5_GROUP_CH) * S5_STATE
ROUTE_LANES = 128
VMEM_LIMIT = 56 * 1024 * 1024

F32 = jnp.float32
BF16 = jnp.bfloat16


def _cparams(sem):
    return pltpu.CompilerParams(dimension_semantics=sem, vmem_limit_bytes=VMEM_LIMIT)


def _layer_norm(x, g, b):
    mu = jnp.mean(x, axis=-1, keepdims=True)
    xc = x - mu
    var = jnp.mean(xc * xc, axis=-1, keepdims=True)
    return xc * lax.rsqrt(var + LN_EPS) * g + b


def _ln_in_kernel(x_ref, g_ref, b_ref, h_ref, hb_ref):
    h = _layer_norm(x_ref[...], g_ref[...], b_ref[...])
    h_ref[...] = h
    hb_ref[...] = h.astype(BF16)


def ln_in(x, g, b, ts=512):
    bsz, s, d = x.shape
    return pl.pallas_call(
        _ln_in_kernel,
        grid=(bsz, s // ts),
        in_specs=[pl.BlockSpec((None, ts, d), lambda bi, si: (bi, si, 0)),
                  pl.BlockSpec((1, d), lambda bi, si: (0, 0)),
                  pl.BlockSpec((1, d), lambda bi, si: (0, 0))],
        out_specs=[pl.BlockSpec((ts, d), lambda bi, si: (si, bi)),
                   pl.BlockSpec((ts, d), lambda bi, si: (si, bi))],
        out_shape=[jax.ShapeDtypeStruct((s, bsz * d), F32),
                   jax.ShapeDtypeStruct((s, bsz * d), BF16)],
        compiler_params=_cparams(("parallel", "parallel")),
        name="ln_in",
    )(x, g.reshape(1, d), b.reshape(1, d))


def _inproj_kernel(h_ref, w_ref, o_ref):
    o_ref[...] = jnp.dot(h_ref[...], w_ref[...], preferred_element_type=F32).astype(BF16)


def in_proj(hb, w, tm=2048):
    t, d = hb.shape
    n = w.shape[1]
    return pl.pallas_call(
        _inproj_kernel,
        grid=(t // tm, n // PROJ_BLK),
        in_specs=[pl.BlockSpec((tm, d), lambda i, j: (i, 0)),
                  pl.BlockSpec((d, PROJ_BLK), lambda i, j: (0, j))],
        out_specs=pl.BlockSpec((tm, PROJ_BLK), lambda i, j: (i, j)),
        out_shape=jax.ShapeDtypeStruct((t, n), BF16),
        compiler_params=_cparams(("parallel", "arbitrary")),
        name="in_proj",
    )(hb, w)


def _swap_halves(x):
    lane = lax.broadcasted_iota(jnp.int32, x.shape, 1)
    first = (lane % RET_QK_DIM) < (RET_QK_DIM // 2)
    n = x.shape[1]
    return jnp.where(first, pltpu.roll(x, n - RET_QK_DIM // 2, 1), pltpu.roll(x, RET_QK_DIM // 2, 1))


def _retention_kernel(q_ref, k_ref, v0_ref, v1_ref, g0_ref, g1_ref, cos_ref, sin_ref,
                      qd_ref, kd_ref, mask_ref, cd_ref, o_ref, state_ref):
    @pl.when(pl.program_id(1) == 0)
    def _():
        state_ref[...] = jnp.zeros_like(state_ref)

    cos = cos_ref[...]
    sin = sin_ref[...]
    q = q_ref[...].astype(F32)
    k = k_ref[...].astype(F32)
    q = q * cos + _swap_halves(q) * sin
    k = (k * cos + _swap_halves(k) * sin) * (RET_QK_DIM ** -0.5)
    qb = q.astype(BF16)
    kb = k.astype(BF16)
    qdb = (q * qd_ref[...]).astype(BF16)
    kdb = (k * kd_ref[...]).astype(BF16)
    for hd in range(RET_HEADS):
        qs = slice(hd * RET_QK_DIM, (hd + 1) * RET_QK_DIM)
        half, off = divmod(hd * RET_V_DIM, PROJ_BLK)
        vs = slice(off, off + RET_V_DIM)
        vh = (v0_ref, v1_ref)[half][:, vs]
        gh = (g0_ref, g1_ref)[half][:, vs].astype(F32)
        sc = lax.dot_general(qb[:, qs], kb[:, qs], (((1,), (1,)), ((), ())),
                             preferred_element_type=F32)
        sc = (sc * mask_ref[hd]).astype(BF16)
        st = state_ref[hd]
        o = jnp.dot(sc, vh, preferred_element_type=F32)
        o = o + jnp.dot(qdb[:, qs], st.astype(BF16), preferred_element_type=F32)
        kv = lax.dot_general(kdb[:, qs], vh, (((0,), (0,)), ((), ())),
                             preferred_element_type=F32)
        state_ref[hd] = st * cd_ref[hd] + kv
        mu = jnp.mean(o, axis=-1, keepdims=True)
        oc = o - mu
        var = jnp.mean(oc * oc, axis=-1, keepdims=True)
        on = oc * lax.rsqrt(var + HEAD_NORM_EPS)
        o_ref[:, hd * RET_V_DIM:(hd + 1) * RET_V_DIM] = (gh * jax.nn.sigmoid(gh) * on).astype(BF16)


def retention(proj_sb, tabs, bsz, s):
    cos_t, sin_t, qd_t, kd_t, mask, cd = tabs
    L = RET_SUPER
    nb = N_PROJ_BLK

    def pspec(col):
        return pl.BlockSpec((L, PROJ_BLK), lambda b, i, col=col: (i, b * nb + col))

    full2 = pl.BlockSpec((L, RET_QK), lambda b, i: (0, 0))
    return pl.pallas_call(
        _retention_kernel,
        grid=(bsz, s // L),
        in_specs=[pspec(0), pspec(1), pspec(2), pspec(3), pspec(4), pspec(5),
                  pl.BlockSpec((L, RET_QK), lambda b, i: (i, 0)),
                  pl.BlockSpec((L, RET_QK), lambda b, i: (i, 0)),
                  full2, full2,
                  pl.BlockSpec((RET_HEADS, L, L), lambda b, i: (0, 0, 0)),
                  pl.BlockSpec((RET_HEADS, 1, RET_V_DIM), lambda b, i: (0, 0, 0))],
        out_specs=pl.BlockSpec((L, RET_V), lambda b, i: (i, b)),
        out_shape=jax.ShapeDtypeStruct((s, bsz * RET_V), BF16),
        scratch_shapes=[pltpu.VMEM((RET_HEADS, RET_QK_DIM, RET_V_DIM), F32)],
        compiler_params=_cparams(("parallel", "arbitrary")),
        name="retention",
    )(proj_sb, proj_sb, proj_sb, proj_sb, proj_sb, proj_sb, cos_t, sin_t, qd_t, kd_t, mask, cd)


def retention_tables(s):
    L = RET_SUPER
    half = RET_QK_DIM // 2
    inv_freq = ROPE_BASE ** (-jnp.arange(half, dtype=F32) / half)
    ang = jnp.arange(s, dtype=F32)[:, None] * inv_freq[None, :]
    cos, sin = jnp.cos(ang), jnp.sin(ang)
    cos_t = jnp.tile(jnp.concatenate([cos, cos], -1), (1, RET_HEADS))
    sin_t = jnp.tile(jnp.concatenate([-sin, sin], -1), (1, RET_HEADS))
    log_gamma = jnp.log1p(-(2.0 ** (-5.0 - jnp.arange(RET_HEADS, dtype=F32))))
    pos = jnp.arange(L, dtype=F32)
    qd = jnp.exp(log_gamma[None, :] * (pos + 1.0)[:, None])
    kd = jnp.exp(log_gamma[None, :] * (L - 1.0 - pos)[:, None])
    qd_t = jnp.repeat(qd, RET_QK_DIM, axis=1)
    kd_t = jnp.repeat(kd, RET_QK_DIM, axis=1)
    chunk_id = jnp.arange(L) // CHUNK
    visible = (chunk_id[None, :] <= chunk_id[:, None]).astype(F32)
    mask = jnp.exp(log_gamma[:, None, None] * jnp.abs(pos[:, None] - pos[None, :])) * visible[None]
    cd = jnp.broadcast_to(jnp.exp(log_gamma * L)[:, None, None], (RET_HEADS, 1, RET_V_DIM))
    return cos_t, sin_t, qd_t, kd_t, mask, cd


def _s5_param_kernel(lre_ref, lim_ref, ls_ref, bre_ref, bim_ref, are_ref, aim_ref, bbre_ref, bbim_ref):
    lam_re = jnp.minimum(lre_ref[...], -1e-4)
    lam_im = lim_ref[...]
    step = jnp.exp(ls_ref[...])
    mag = jnp.exp(lam_re * step)
    ang = lam_im * step
    ab_re = mag * jnp.cos(ang)
    ab_im = mag * jnp.sin(ang)
    den = lam_re * lam_re + lam_im * lam_im
    n_re = ab_re - 1.0
    zc_re = (n_re * lam_re + ab_im * lam_im) / den
    zc_im = (ab_im * lam_re - n_re * lam_im) / den
    are_ref[...] = ab_re
    aim_ref[...] = ab_im
    b_re = bre_ref[...]
    b_im = bim_ref[...]
    bbre_ref[...] = zc_re * b_re - zc_im * b_im
    bbim_ref[...] = zc_re * b_im + zc_im * b_re


def s5_params(lam_re, lam_im, log_step, b_re, b_im):
    g, n = lam_re.shape
    c = b_re.shape[-1]
    outs = pl.pallas_call(
        _s5_param_kernel,
        out_shape=[jax.ShapeDtypeStruct((g, 1, n), F32), jax.ShapeDtypeStruct((g, 1, n), F32),
                   jax.ShapeDtypeStruct((g, c, n), F32), jax.ShapeDtypeStruct((g, c, n), F32)],
        name="s5_params",
    )(lam_re.reshape(g, 1, n), lam_im.reshape(g, 1, n), log_step.reshape(g, 1, 1),
      jnp.swapaxes(b_re, 1, 2), jnp.swapaxes(b_im, 1, 2))
    return outs


def _block_diag(x):
    ns, gl, r, c = x.shape
    eye = jnp.eye(gl, dtype=x.dtype)
    return jnp.einsum('sgrc,gh->sgrhc', x, eye).reshape(ns, gl * r, gl * c)


def s5_matrices(a_re, a_im, bb_re, bb_im, c_re, c_im):
    gl = S5_COLS // S5_GROUP_CH
    ns = S5_NSLICE
    bre = _block_diag(bb_re.reshape(ns, gl, S5_GROUP_CH, S5_STATE))
    bim = _block_diag(bb_im.reshape(ns, gl, S5_GROUP_CH, S5_STATE))
    bq = jnp.concatenate([bre, bim], axis=-1).astype(BF16)
    cre = _block_diag(jnp.swapaxes(c_re, 1, 2).reshape(ns, gl, S5_STATE, S5_GROUP_CH))
    cim = _block_diag(jnp.swapaxes(c_im, 1, 2).reshape(ns, gl, S5_STATE, S5_GROUP_CH))
    cq = jnp.concatenate([cre, -cim], axis=1).astype(BF16)
    are = a_re.reshape(ns, 1, S5_SLICE_STATE)
    aim = a_im.reshape(ns, 1, S5_SLICE_STATE)
    return bq, cq, are, aim


def _s5_kernel(u_ref, bq_ref, cq_ref, are_ref, aim_ref, d_ref, y_ref, bu_ref, st_ref, *, bsz, tt):
    @pl.when(pl.program_id(0) == 0)
    def _():
        st_ref[...] = jnp.zeros_like(st_ref)

    ns2 = S5_SLICE_STATE
    for cs in range(S5_NSLICE):
        cols = slice(cs * S5_COLS, (cs + 1) * S5_COLS)
        ub = u_ref[:, cols]
        bu_ref[...] = jnp.dot(ub, bq_ref[cs], preferred_element_type=F32)
        a_re = jnp.broadcast_to(are_ref[cs], (bsz, ns2))
        a_im = jnp.broadcast_to(aim_ref[cs], (bsz, ns2))

        def step(t, carry):
            h_re, h_im = carry
            rows = pl.ds(pl.multiple_of(t * bsz, bsz), bsz)
            n_re = a_re * h_re - a_im * h_im + bu_ref[rows, 0:ns2]
            n_im = a_re * h_im + a_im * h_re + bu_ref[rows, ns2:2 * ns2]
            bu_ref[rows, 0:ns2] = n_re
            bu_ref[rows, ns2:2 * ns2] = n_im
            return n_re, n_im

        h_re, h_im = lax.fori_loop(0, tt, step, (st_ref[cs, 0], st_ref[cs, 1]), unroll=8)
        st_ref[cs, 0] = h_re
        st_ref[cs, 1] = h_im
        y = jnp.dot(bu_ref[...].astype(BF16), cq_ref[cs], preferred_element_type=F32)
        y_ref[:, cols] = y + d_ref[:, cols] * ub.astype(F32)


def s5_scan(proj, mats, d_skip, bsz, s):
    bq, cq, are, aim = mats
    tt = S5_TT
    rows = tt * bsz
    kern = functools.partial(_s5_kernel, bsz=bsz, tt=tt)
    return pl.pallas_call(
        kern,
        grid=(s // tt,),
        in_specs=[pl.BlockSpec((rows, S5_WIDTH), lambda i: (i, 6)),
                  pl.BlockSpec(bq.shape, lambda i: (0, 0, 0)),
                  pl.BlockSpec(cq.shape, lambda i: (0, 0, 0)),
                  pl.BlockSpec(are.shape, lambda i: (0, 0, 0)),
                  pl.BlockSpec(aim.shape, lambda i: (0, 0, 0)),
                  pl.BlockSpec((1, S5_WIDTH), lambda i: (0, 0))],
        out_specs=pl.BlockSpec((rows, S5_WIDTH), lambda i: (i, 0)),
        out_shape=jax.ShapeDtypeStruct((s * bsz, S5_WIDTH), F32),
        scratch_shapes=[pltpu.VMEM((rows, 2 * S5_SLICE_STATE), F32),
                        pltpu.VMEM((S5_NSLICE, 2, bsz, S5_SLICE_STATE), F32)],
        compiler_params=_cparams(("arbitrary",)),
        name="s5_scan",
    )(proj, bq, cq, are, aim, d_skip.reshape(1, S5_WIDTH))


def _gelu_tanh(x):
    c = math.sqrt(2.0 / math.pi)
    return 0.5 * x * (1.0 + jnp.tanh(c * (x + 0.044715 * (x * x * x))))


def _route(logits):
    lane = lax.broadcasted_iota(jnp.int32, logits.shape, 1)
    neg = jnp.float32(-jnp.inf)
    big = jnp.int32(1 << 20)
    is_g = lane < MOE_GROUPS
    lg = jnp.where(is_g, logits, neg)
    mg = jnp.max(lg, axis=-1, keepdims=True)
    sg = jnp.sum(jnp.where(is_g, jnp.exp(lg - mg), 0.0), axis=-1, keepdims=True)
    g_top = 1.0 / sg
    g_idx = jnp.min(jnp.where(lg == mg, lane, big), axis=-1, keepdims=True)
    lo = MOE_GROUPS + g_idx * EXPERTS_PER_GROUP
    in_grp = (lane >= lo) & (lane < lo + EXPERTS_PER_GROUP)
    le = jnp.where(in_grp, logits, neg)
    m1 = jnp.max(le, axis=-1, keepdims=True)
    se = jnp.sum(jnp.where(in_grp, jnp.exp(le - m1), 0.0), axis=-1, keepdims=True)
    i1 = jnp.min(jnp.where(le == m1, lane, big), axis=-1, keepdims=True)
    le2 = jnp.where(lane == i1, neg, le)
    m2 = jnp.max(le2, axis=-1, keepdims=True)
    i2 = jnp.min(jnp.where(le2 == m2, lane, big), axis=-1, keepdims=True)
    p1 = 1.0 / se
    p2 = jnp.exp(m2 - m1) / se
    tot = p1 + p2
    w1 = g_top * (p1 / tot)
    w2 = g_top * (p2 / tot)
    return jnp.where(lane == i1, w1, 0.0) + jnp.where(lane == i2, w2, 0.0)


def _mix_kernel(ro_ref, y_ref, gr0_ref, gr1_ref, gs0_ref, gs1_ref, h_ref,
                wglu_ref, wbr_ref, wbs_ref, wout_ref, g_ref, b_ref, wr_ref, br_ref,
                h1_ref, h1b_ref, comb_ref):
    z = _gelu_tanh(y_ref[...])
    zg = jnp.dot(z.astype(BF16), wglu_ref[...], preferred_element_type=F32)
    zz = (z * jax.nn.sigmoid(zg)).astype(BF16)
    s5b = jnp.dot(zz, wbs_ref[...], preferred_element_type=F32)
    rb = jnp.dot(ro_ref[...], wbr_ref[...], preferred_element_type=F32)
    gr = jnp.concatenate([gr0_ref[...], gr1_ref[...]], axis=-1).astype(F32)
    gs = jnp.concatenate([gs0_ref[...], gs1_ref[...]], axis=-1).astype(F32)
    merged = jax.nn.sigmoid(gr) * rb + jax.nn.sigmoid(gs) * s5b
    mix = jnp.dot(merged.astype(BF16), wout_ref[...], preferred_element_type=F32)
    h1 = _layer_norm(DN_ALPHA * h_ref[...] + mix, g_ref[...], b_ref[...])
    h1_ref[...] = h1
    h1b_ref[...] = h1.astype(BF16)
    logits = jnp.dot(h1, wr_ref[...], preferred_element_type=F32,
                     precision=lax.Precision.HIGHEST) + br_ref[...]
    comb_ref[...] = _route(logits)


def mix_out(ro, y, proj, h, wglu, wbr, wbs, wout, g, b, wr, br, tm=512):
    t, d = h.shape

    def const(a):
        return pl.BlockSpec(a.shape, lambda i: (0,) * a.ndim)

    def pspec(col):
        return pl.BlockSpec((tm, PROJ_BLK), lambda i, col=col: (i, col))

    g2, b2 = g.reshape(1, d), b.reshape(1, d)
    return pl.pallas_call(
        _mix_kernel,
        grid=(t // tm,),
        in_specs=[pl.BlockSpec((tm, RET_V), lambda i: (i, 0)),
                  pl.BlockSpec((tm, S5_WIDTH), lambda i: (i, 0)),
                  pspec(7), pspec(8), pspec(9), pspec(10),
                  pl.BlockSpec((tm, d), lambda i: (i, 0)),
                  const(wglu), const(wbr), const(wbs), const(wout), const(g2), const(b2),
                  const(wr), const(br)],
        out_specs=[pl.BlockSpec((tm, d), lambda i: (i, 0)),
                   pl.BlockSpec((tm, d), lambda i: (i, 0)),
                   pl.BlockSpec((tm, ROUTE_LANES), lambda i: (i, 0))],
        out_shape=[jax.ShapeDtypeStruct((t, d), F32), jax.ShapeDtypeStruct((t, d), BF16),
                   jax.ShapeDtypeStruct((t, ROUTE_LANES), F32)],
        compiler_params=_cparams(("parallel",)),
        name="mix_out",
    )(ro, y, proj, proj, proj, proj, h, wglu, wbr, wbs, wout, g2, b2, wr, br)


def _moe_kernel(x_ref, h_ref, comb_ref, wg_ref, wu_ref, wd_ref, g_ref, b_ref, o_ref, ob_ref, acc_ref):
    e = pl.program_id(2)

    @pl.when(e == 0)
    def _():
        acc_ref[...] = jnp.zeros_like(acc_ref)

    x = x_ref[...]
    gate = jnp.dot(x, wg_ref[...], preferred_element_type=F32)
    up = jnp.dot(x, wu_ref[...], preferred_element_type=F32)
    comb = comb_ref[...]
    lane = lax.broadcasted_iota(jnp.int32, comb.shape, 1)
    c = jnp.sum(jnp.where(lane == e + MOE_GROUPS, comb, 0.0), axis=-1, keepdims=True)
    act = gate * jax.nn.sigmoid(gate) * up * c
    acc_ref[...] += jnp.dot(act.astype(BF16), wd_ref[...], preferred_element_type=F32)

    @pl.when(e == N_EXPERTS - 1)
    def _():
        h2 = _layer_norm(DN_ALPHA * h_ref[...] + acc_ref[...], g_ref[...], b_ref[...])
        o_ref[...] = h2
        ob_ref[...] = h2.astype(BF16)


def moe_dense(h1b_sb, h1_sb, comb_sb, wg, wu, wd, g, b, bsz, s, last, ts=1024):
    d = D_MODEL
    g2, b2 = g.reshape(1, d), b.reshape(1, d)
    if last:
        o_spec = pl.BlockSpec((None, ts, d), lambda bi, si, e: (bi, si, 0))
        o_shape = jax.ShapeDtypeStruct((bsz, s, d), F32)
    else:
        o_spec = pl.BlockSpec((ts, d), lambda bi, si, e: (si, bi))
        o_shape = jax.ShapeDtypeStruct((s, bsz * d), F32)
    return pl.pallas_call(
        _moe_kernel,
        grid=(bsz, s // ts, N_EXPERTS),
        in_specs=[pl.BlockSpec((ts, d), lambda bi, si, e: (si, bi)),
                  pl.BlockSpec((ts, d), lambda bi, si, e: (si, bi)),
                  pl.BlockSpec((ts, ROUTE_LANES), lambda bi, si, e: (si, bi)),
                  pl.BlockSpec((None, d, EXPERT_FF), lambda bi, si, e: (e, 0, 0)),
                  pl.BlockSpec((None, d, EXPERT_FF), lambda bi, si, e: (e, 0, 0)),
                  pl.BlockSpec((None, EXPERT_FF, d), lambda bi, si, e: (e, 0, 0)),
                  pl.BlockSpec((1, d), lambda bi, si, e: (0, 0)),
                  pl.BlockSpec((1, d), lambda bi, si, e: (0, 0))],
        out_specs=[o_spec, pl.BlockSpec((ts, d), lambda bi, si, e: (si, bi))],
        out_shape=[o_shape, jax.ShapeDtypeStruct((s, bsz * d), BF16)],
        scratch_shapes=[pltpu.VMEM((ts, d), F32)],
        compiler_params=_cparams(("parallel", "parallel", "arbitrary")),
        name="moe_dense",
    )(h1b_sb, h1_sb, comb_sb, wg, wu, wd, g2, b2)


def kernel(x, ln_in_g, ln_in_b, w_in, s5_lambda_re, s5_lambda_im, s5_log_step, s5_b_re, s5_b_im,
           s5_c_re, s5_c_im, s5_d, w_glu, w_branch_ret, w_branch_s5, w_out, ln_mix_g, ln_mix_b,
           w_router_group, b_router_group, w_router_expert, b_router_expert, w_exp_gate, w_exp_up,
           w_exp_down, ln_ffn_g, ln_ffn_b):
    bsz, s, d = x.shape
    t = bsz * s
    depth = w_in.shape[0]
    tabs = retention_tables(s)
    h_sb, hb_sb = ln_in(x, ln_in_g, ln_in_b)
    out = None
    for l in range(depth):
        proj = in_proj(hb_sb.reshape(t, d), w_in[l].astype(BF16))
        ro = retention(proj.reshape(s, bsz * IN_WIDTH), tabs, bsz, s)
        a_re, a_im, bb_re, bb_im = s5_params(s5_lambda_re[l], s5_lambda_im[l], s5_log_step[l],
                                             s5_b_re[l], s5_b_im[l])
        mats = s5_matrices(a_re, a_im, bb_re, bb_im, s5_c_re[l], s5_c_im[l])
        y = s5_scan(proj, mats, s5_d[l], bsz, s)
        pad = ROUTE_LANES - MOE_GROUPS - N_EXPERTS
        wr = jnp.concatenate([w_router_group[l], w_router_expert[l], jnp.zeros((d, pad), F32)], axis=1)
        br = jnp.concatenate([b_router_group[l], b_router_expert[l], jnp.zeros((pad,), F32)]).reshape(1, -1)
        h1, h1b, comb = mix_out(ro.reshape(t, RET_V), y, proj, h_sb.reshape(t, d),
                                w_glu[l].astype(BF16), w_branch_ret[l].astype(BF16),
                                w_branch_s5[l].astype(BF16), w_out[l].astype(BF16),
                                ln_mix_g[l], ln_mix_b[l], wr, br)
        last = l == depth - 1
        h2, h2b = moe_dense(h1b.reshape(s, bsz * d), h1.reshape(s, bsz * d),
                            comb.reshape(s, bsz * ROUTE_LANES),
                            w_exp_gate[l].astype(BF16), w_exp_up[l].astype(BF16),
                            w_exp_down[l].astype(BF16), ln_ffn_g[l], ln_ffn_b[l], bsz, s, last)
        if last:
            out = h2
        else:
            h_sb, hb_sb = h2, h2b
    return out
```

```python
import functools
import math

import jax
import jax.numpy as jnp
from jax import lax
from jax.experimental import pallas as pl
from jax.experimental.pallas import tpu as pltpu

D_MODEL = 1024
CHUNK = 64
RET_HEADS = 8
RET_QK = 512
RET_V = 1024
RET_QK_DIM = 64
RET_V_DIM = 128
ROPE_BASE = 10000.0
S5_WIDTH = 512
S5_GROUP_CH = 16
S5_GROUPS = 32
S5_STATE = 64
MOE_GROUPS = 4
EXPERTS_PER_GROUP = 8
N_EXPERTS = 32
EXPERT_FF = 256
LN_EPS = 1e-5
HEAD_NORM_EPS = 1e-6
DEPTH = 2
DN_ALPHA = (2 * DEPTH) ** 0.25
IN_WIDTH = 2 * RET_QK + 2 * RET_V + S5_WIDTH + 2 * D_MODEL
PROJ_BLK = 512
N_PROJ_BLK = IN_WIDTH // PROJ_BLK

RET_SUPER = 128
S5_TT = 64
S5_COLS = 128
S5_NSLICE = S5_WIDTH // S5_COLS
S5_SLICE_STATE = (S5_COLS // S5_GROUP_CH) * S5_STATE
ROUTE_LANES = 128
VMEM_LIMIT = 56 * 1024 * 1024

F32 = jnp.float32
BF16 = jnp.bfloat16


def _cparams(sem):
    return pltpu.CompilerParams(dimension_semantics=sem, vmem_limit_bytes=VMEM_LIMIT)


def _layer_norm(x, g, b):
    mu = jnp.mean(x, axis=-1, keepdims=True)
    xc = x - mu
    var = jnp.mean(xc * xc, axis=-1, keepdims=True)
    return xc * lax.rsqrt(var + LN_EPS) * g + b


def _ln_in_kernel(x_ref, g_ref, b_ref, h_ref, hb_ref):
    h = _layer_norm(x_ref[...], g_ref[...], b_ref[...])
    h_ref[...] = h
    hb_ref[...] = h.astype(BF16)


def ln_in(x, g, b, ts=512):
    bsz, s, d = x.shape
    return pl.pallas_call(
        _ln_in_kernel,
        grid=(bsz, s // ts),
        in_specs=[pl.BlockSpec((None, ts, d), lambda bi, si: (bi, si, 0)),
                  pl.BlockSpec((1, d), lambda bi, si: (0, 0)),
                  pl.BlockSpec((1, d), lambda bi, si: (0, 0))],
        out_specs=[pl.BlockSpec((ts, d), lambda bi, si: (si, bi)),
                   pl.BlockSpec((ts, d), lambda bi, si: (si, bi))],
        out_shape=[jax.ShapeDtypeStruct((s, bsz * d), F32),
                   jax.ShapeDtypeStruct((s, bsz * d), BF16)],
        compiler_params=_cparams(("parallel", "parallel")),
        name="ln_in",
    )(x, g.reshape(1, d), b.reshape(1, d))


def _inproj_kernel(h_ref, w_ref, o_ref):
    o_ref[...] = jnp.dot(h_ref[...], w_ref[...], preferred_element_type=F32).astype(BF16)


def in_proj(hb_sb, w, bsz, ts=2048):
    s = hb_sb.shape[0]
    d, n = w.shape
    nb = n // PROJ_BLK
    return pl.pallas_call(
        _inproj_kernel,
        grid=(bsz, s // ts, nb),
        in_specs=[pl.BlockSpec((ts, d), lambda b, i, j: (i, b)),
                  pl.BlockSpec((d, PROJ_BLK), lambda b, i, j: (0, j))],
        out_specs=pl.BlockSpec((ts, PROJ_BLK), lambda b, i, j: (i, b * nb + j)),
        out_shape=jax.ShapeDtypeStruct((s, bsz * n), BF16),
        compiler_params=_cparams(("parallel", "parallel", "arbitrary")),
        name="in_proj",
    )(hb_sb, w)


def _swap_halves(x):
    lane = lax.broadcasted_iota(jnp.int32, x.shape, 1)
    first = (lane % RET_QK_DIM) < (RET_QK_DIM // 2)
    n = x.shape[1]
    return jnp.where(first, pltpu.roll(x, n - RET_QK_DIM // 2, 1), pltpu.roll(x, RET_QK_DIM // 2, 1))


def _retention_kernel(q_ref, k_ref, v0_ref, v1_ref, g0_ref, g1_ref, cos_ref, sin_ref,
                      qd_ref, kd_ref, mask_ref, cd_ref, o_ref, state_ref):
    @pl.when(pl.program_id(1) == 0)
    def _():
        state_ref[...] = jnp.zeros_like(state_ref)

    cos = cos_ref[...]
    sin = sin_ref[...]
    q = q_ref[...].astype(F32)
    k = k_ref[...].astype(F32)
    q = q * cos + _swap_halves(q) * sin
    k = (k * cos + _swap_halves(k) * sin) * (RET_QK_DIM ** -0.5)
    qb = q.astype(BF16)
    kb = k.astype(BF16)
    qdb = (q * qd_ref[...]).astype(BF16)
    kdb = (k * kd_ref[...]).astype(BF16)
    for hd in range(RET_HEADS):
        qs = slice(hd * RET_QK_DIM, (hd + 1) * RET_QK_DIM)
        half, off = divmod(hd * RET_V_DIM, PROJ_BLK)
        vs = slice(off, off + RET_V_DIM)
        vh = (v0_ref, v1_ref)[half][:, vs]
        gh = (g0_ref, g1_ref)[half][:, vs].astype(F32)
        sc = lax.dot_general(qb[:, qs], kb[:, qs], (((1,), (1,)), ((), ())),
                             preferred_element_type=F32)
        sc = (sc * mask_ref[hd]).astype(BF16)
        st = state_ref[hd]
        o = jnp.dot(sc, vh, preferred_element_type=F32)
        o = o + jnp.dot(qdb[:, qs], st.astype(BF16), preferred_element_type=F32)
        kv = lax.dot_general(kdb[:, qs], vh, (((0,), (0,)), ((), ())),
                             preferred_element_type=F32)
        state_ref[hd] = st * cd_ref[hd] + kv
        mu = jnp.mean(o, axis=-1, keepdims=True)
        oc = o - mu
        var = jnp.mean(oc * oc, axis=-1, keepdims=True)
        on = oc * lax.rsqrt(var + HEAD_NORM_EPS)
        o_ref[:, hd * RET_V_DIM:(hd + 1) * RET_V_DIM] = (gh * jax.nn.sigmoid(gh) * on).astype(BF16)


def retention(proj_sb, tabs, bsz, s):
    cos_t, sin_t, qd_t, kd_t, mask, cd = tabs
    L = RET_SUPER
    nb = N_PROJ_BLK

    def pspec(col):
        return pl.BlockSpec((L, PROJ_BLK), lambda b, i, col=col: (i, b * nb + col))

    full2 = pl.BlockSpec((L, RET_QK), lambda b, i: (0, 0))
    return pl.pallas_call(
        _retention_kernel,
        grid=(bsz, s // L),
        in_specs=[pspec(0), pspec(1), pspec(2), pspec(3), pspec(4), pspec(5),
                  pl.BlockSpec((L, RET_QK), lambda b, i: (i, 0)),
                  pl.BlockSpec((L, RET_QK), lambda b, i: (i, 0)),
                  full2, full2,
                  pl.BlockSpec((RET_HEADS, L, L), lambda b, i: (0, 0, 0)),
                  pl.BlockSpec((RET_HEADS, 1, RET_V_DIM), lambda b, i: (0, 0, 0))],
        out_specs=pl.BlockSpec((L, RET_V), lambda b, i: (i, b)),
        out_shape=jax.ShapeDtypeStruct((s, bsz * RET_V), BF16),
        scratch_shapes=[pltpu.VMEM((RET_HEADS, RET_QK_DIM, RET_V_DIM), F32)],
        compiler_params=_cparams(("parallel", "arbitrary")),
        name="retention",
    )(proj_sb, proj_sb, proj_sb, proj_sb, proj_sb, proj_sb, cos_t, sin_t, qd_t, kd_t, mask, cd)


def retention_tables(s):
    L = RET_SUPER
    half = RET_QK_DIM // 2
    inv_freq = ROPE_BASE ** (-jnp.arange(half, dtype=F32) / half)
    ang = jnp.arange(s, dtype=F32)[:, None] * inv_freq[None, :]
    cos, sin = jnp.cos(ang), jnp.sin(ang)
    cos_t = jnp.tile(jnp.concatenate([cos, cos], -1), (1, RET_HEADS))
    sin_t = jnp.tile(jnp.concatenate([-sin, sin], -1), (1, RET_HEADS))
    log_gamma = jnp.log1p(-(2.0 ** (-5.0 - jnp.arange(RET_HEADS, dtype=F32))))
    pos = jnp.arange(L, dtype=F32)
    qd = jnp.exp(log_gamma[None, :] * (pos + 1.0)[:, None])
    kd = jnp.exp(log_gamma[None, :] * (L - 1.0 - pos)[:, None])
    qd_t = jnp.repeat(qd, RET_QK_DIM, axis=1)
    kd_t = jnp.repeat(kd, RET_QK_DIM, axis=1)
    chunk_id = jnp.arange(L) // CHUNK
    visible = (chunk_id[None, :] <= chunk_id[:, None]).astype(F32)
    mask = jnp.exp(log_gamma[:, None, None] * jnp.abs(pos[:, None] - pos[None, :])) * visible[None]
    cd = jnp.broadcast_to(jnp.exp(log_gamma * L)[:, None, None], (RET_HEADS, 1, RET_V_DIM))
    return cos_t, sin_t, qd_t, kd_t, mask, cd


def _s5_param_kernel(lre_ref, lim_ref, ls_ref, bre_ref, bim_ref, are_ref, aim_ref, bbre_ref, bbim_ref):
    lam_re = jnp.minimum(lre_ref[...], -1e-4)
    lam_im = lim_ref[...]
    step = jnp.exp(ls_ref[...])
    mag = jnp.exp(lam_re * step)
    ang = lam_im * step
    ab_re = mag * jnp.cos(ang)
    ab_im = mag * jnp.sin(ang)
    den = lam_re * lam_re + lam_im * lam_im
    n_re = ab_re - 1.0
    zc_re = (n_re * lam_re + ab_im * lam_im) / den
    zc_im = (ab_im * lam_re - n_re * lam_im) / den
    are_ref[...] = ab_re
    aim_ref[...] = ab_im
    b_re = bre_ref[...]
    b_im = bim_ref[...]
    bbre_ref[...] = zc_re * b_re - zc_im * b_im
    bbim_ref[...] = zc_re * b_im + zc_im * b_re


def s5_params(lam_re, lam_im, log_step, b_re, b_im):
    g, n = lam_re.shape
    c = b_re.shape[-1]
    outs = pl.pallas_call(
        _s5_param_kernel,
        out_shape=[jax.ShapeDtypeStruct((g, 1, n), F32), jax.ShapeDtypeStruct((g, 1, n), F32),
                   jax.ShapeDtypeStruct((g, c, n), F32), jax.ShapeDtypeStruct((g, c, n), F32)],
        name="s5_params",
    )(lam_re.reshape(g, 1, n), lam_im.reshape(g, 1, n), log_step.reshape(g, 1, 1),
      jnp.swapaxes(b_re, 1, 2), jnp.swapaxes(b_im, 1, 2))
    return outs


def _block_diag(x):
    ns, gl, r, c = x.shape
    eye = jnp.eye(gl, dtype=x.dtype)
    return jnp.einsum('sgrc,gh->sgrhc', x, eye).reshape(ns, gl * r, gl * c)


def s5_matrices(a_re, a_im, bb_re, bb_im, c_re, c_im):
    gl = S5_COLS // S5_GROUP_CH
    ns = S5_NSLICE
    bre = _block_diag(bb_re.reshape(ns, gl, S5_GROUP_CH, S5_STATE))
    bim = _block_diag(bb_im.reshape(ns, gl, S5_GROUP_CH, S5_STATE))
    bq = jnp.concatenate([bre, bim], axis=-1).astype(BF16)
    cre = _block_diag(jnp.swapaxes(c_re, 1, 2).reshape(ns, gl, S5_STATE, S5_GROUP_CH))
    cim = _block_diag(jnp.swapaxes(c_im, 1, 2).reshape(ns, gl, S5_STATE, S5_GROUP_CH))
    cq = jnp.concatenate([cre, -cim], axis=1).astype(BF16)
    are = a_re.reshape(ns, 1, S5_SLICE_STATE)
    aim = a_im.reshape(ns, 1, S5_SLICE_STATE)
    return bq, cq, are, aim


def _s5_kernel(*refs, bsz, tt):
    u_refs = refs[:bsz]
    bq_ref, cq_ref, are_ref, aim_ref, d_ref, y_ref, us_ref, ys_ref, bu_ref, st_ref = refs[bsz:]

    @pl.when(pl.program_id(0) == 0)
    def _():
        st_ref[...] = jnp.zeros_like(st_ref)

    for b in range(bsz):
        ub = u_refs[b][...].astype(F32)
        for cs in range(S5_NSLICE):
            us_ref[cs, pl.ds(b, tt, stride=bsz), :] = ub[:, cs * S5_COLS:(cs + 1) * S5_COLS]

    ns2 = S5_SLICE_STATE
    for cs in range(S5_NSLICE):
        cols = slice(cs * S5_COLS, (cs + 1) * S5_COLS)
        uf = us_ref[cs]
        bu_ref[...] = jnp.dot(uf.astype(BF16), bq_ref[cs], preferred_element_type=F32)
        a_re = jnp.broadcast_to(are_ref[cs], (bsz, ns2))
        a_im = jnp.broadcast_to(aim_ref[cs], (bsz, ns2))

        def step(t, carry):
            h_re, h_im = carry
            rows = pl.ds(pl.multiple_of(t * bsz, bsz), bsz)
            n_re = a_re * h_re - a_im * h_im + bu_ref[rows, 0:ns2]
            n_im = a_re * h_im + a_im * h_re + bu_ref[rows, ns2:2 * ns2]
            bu_ref[rows, 0:ns2] = n_re
            bu_ref[rows, ns2:2 * ns2] = n_im
            return n_re, n_im

        h_re, h_im = lax.fori_loop(0, tt, step, (st_ref[cs, 0], st_ref[cs, 1]), unroll=8)
        st_ref[cs, 0] = h_re
        st_ref[cs, 1] = h_im
        y = jnp.dot(bu_ref[...].astype(BF16), cq_ref[cs], preferred_element_type=F32)
        ys_ref[cs] = y + d_ref[:, cols] * uf

    for b in range(bsz):
        for cs in range(S5_NSLICE):
            lo = b * S5_WIDTH + cs * S5_COLS
            y_ref[:, lo:lo + S5_COLS] = ys_ref[cs, pl.ds(b, tt, stride=bsz), :]


def s5_scan(proj_sb, mats, d_skip, bsz, s):
    bq, cq, are, aim = mats
    tt = S5_TT
    rows = tt * bsz
    nb = N_PROJ_BLK
    kern = functools.partial(_s5_kernel, bsz=bsz, tt=tt)
    u_specs = [pl.BlockSpec((tt, PROJ_BLK), lambda i, b=b: (i, b * nb + 6)) for b in range(bsz)]
    return pl.pallas_call(
        kern,
        grid=(s // tt,),
        in_specs=u_specs + [
                  pl.BlockSpec(bq.shape, lambda i: (0, 0, 0)),
                  pl.BlockSpec(cq.shape, lambda i: (0, 0, 0)),
                  pl.BlockSpec(are.shape, lambda i: (0, 0, 0)),
                  pl.BlockSpec(aim.shape, lambda i: (0, 0, 0)),
                  pl.BlockSpec((1, S5_WIDTH), lambda i: (0, 0))],
        out_specs=pl.BlockSpec((tt, bsz * S5_WIDTH), lambda i: (i, 0)),
        out_shape=jax.ShapeDtypeStruct((s, bsz * S5_WIDTH), F32),
        scratch_shapes=[pltpu.VMEM((S5_NSLICE, rows, S5_COLS), F32),
                        pltpu.VMEM((S5_NSLICE, rows, S5_COLS), F32),
                        pltpu.VMEM((rows, 2 * S5_SLICE_STATE), F32),
                        pltpu.VMEM((S5_NSLICE, 2, bsz, S5_SLICE_STATE), F32)],
        compiler_params=_cparams(("arbitrary",)),
        name="s5_scan",
    )(*([proj_sb] * bsz), bq, cq, are, aim, d_skip.reshape(1, S5_WIDTH))


def _gelu_tanh(x):
    c = math.sqrt(2.0 / math.pi)
    return 0.5 * x * (1.0 + jnp.tanh(c * (x + 0.044715 * (x * x * x))))


def _route(logits):
    lane = lax.broadcasted_iota(jnp.int32, logits.shape, 1)
    neg = jnp.float32(-jnp.inf)
    big = jnp.int32(1 << 20)
    is_g = lane < MOE_GROUPS
    lg = jnp.where(is_g, logits, neg)
    mg = jnp.max(lg, axis=-1, keepdims=True)
    sg = jnp.sum(jnp.where(is_g, jnp.exp(lg - mg), 0.0), axis=-1, keepdims=True)
    g_top = 1.0 / sg
    g_idx = jnp.min(jnp.where(lg == mg, lane, big), axis=-1, keepdims=True)
    lo = MOE_GROUPS + g_idx * EXPERTS_PER_GROUP
    in_grp = (lane >= lo) & (lane < lo + EXPERTS_PER_GROUP)
    le = jnp.where(in_grp, logits, neg)
    m1 = jnp.max(le, axis=-1, keepdims=True)
    se = jnp.sum(jnp.where(in_grp, jnp.exp(le - m1), 0.0), axis=-1, keepdims=True)
    i1 = jnp.min(jnp.where(le == m1, lane, big), axis=-1, keepdims=True)
    le2 = jnp.where(lane == i1, neg, le)
    m2 = jnp.max(le2, axis=-1, keepdims=True)
    i2 = jnp.min(jnp.where(le2 == m2, lane, big), axis=-1, keepdims=True)
    p1 = 1.0 / se
    p2 = jnp.exp(m2 - m1) / se
    tot = p1 + p2
    w1 = g_top * (p1 / tot)
    w2 = g_top * (p2 / tot)
    return jnp.where(lane == i1, w1, 0.0) + jnp.where(lane == i2, w2, 0.0)


def _mix_kernel(ro_ref, y_ref, gr0_ref, gr1_ref, gs0_ref, gs1_ref, h_ref,
                wglu_ref, wbr_ref, wbs_ref, wout_ref, g_ref, b_ref, wr_ref, br_ref,
                h1_ref, h1b_ref, comb_ref):
    z = _gelu_tanh(y_ref[...])
    zg = jnp.dot(z.astype(BF16), wglu_ref[...], preferred_element_type=F32)
    zz = (z * jax.nn.sigmoid(zg)).astype(BF16)
    s5b = jnp.dot(zz, wbs_ref[...], preferred_element_type=F32)
    rb = jnp.dot(ro_ref[...], wbr_ref[...], preferred_element_type=F32)
    gr = jnp.concatenate([gr0_ref[...], gr1_ref[...]], axis=-1).astype(F32)
    gs = jnp.concatenate([gs0_ref[...], gs1_ref[...]], axis=-1).astype(F32)
    merged = jax.nn.sigmoid(gr) * rb + jax.nn.sigmoid(gs) * s5b
    mix = jnp.dot(merged.astype(BF16), wout_ref[...], preferred_element_type=F32)
    h1 = _layer_norm(DN_ALPHA * h_ref[...] + mix, g_ref[...], b_ref[...])
    h1_ref[...] = h1
    h1b_ref[...] = h1.astype(BF16)
    logits = jnp.dot(h1, wr_ref[...], preferred_element_type=F32,
                     precision=lax.Precision.HIGHEST) + br_ref[...]
    comb_ref[...] = _route(logits)


def mix_out(ro, y, proj, h, wglu, wbr, wbs, wout, g, b, wr, br, bsz, ts=512):
    s = h.shape[0]
    d = D_MODEL
    nb = N_PROJ_BLK

    def const(a):
        return pl.BlockSpec(a.shape, lambda bi, i: (0,) * a.ndim)

    def pspec(col):
        return pl.BlockSpec((ts, PROJ_BLK), lambda bi, i, col=col: (i, bi * nb + col))

    def tok(w):
        return pl.BlockSpec((ts, w), lambda bi, i: (i, bi))

    g2, b2 = g.reshape(1, d), b.reshape(1, d)
    return pl.pallas_call(
        _mix_kernel,
        grid=(bsz, s // ts),
        in_specs=[tok(RET_V), tok(S5_WIDTH), pspec(7), pspec(8), pspec(9), pspec(10), tok(d),
                  const(wglu), const(wbr), const(wbs), const(wout), const(g2), const(b2),
                  const(wr), const(br)],
        out_specs=[tok(d), tok(d), tok(ROUTE_LANES)],
        out_shape=[jax.ShapeDtypeStruct((s, bsz * d), F32), jax.ShapeDtypeStruct((s, bsz * d), BF16),
                   jax.ShapeDtypeStruct((s, bsz * ROUTE_LANES), F32)],
        compiler_params=_cparams(("parallel", "parallel")),
        name="mix_out",
    )(ro, y, proj, proj, proj, proj, h, wglu, wbr, wbs, wout, g2, b2, wr, br)


def _moe_kernel(x_ref, h_ref, comb_ref, wg_ref, wu_ref, wd_ref, g_ref, b_ref, o_ref, ob_ref, acc_ref):
    e = pl.program_id(2)

    @pl.when(e == 0)
    def _():
        acc_ref[...] = jnp.zeros_like(acc_ref)

    x = x_ref[...]
    gate = jnp.dot(x, wg_ref[...], preferred_element_type=F32)
    up = jnp.dot(x, wu_ref[...], preferred_element_type=F32)
    comb = comb_ref[...]
    lane = lax.broadcasted_iota(jnp.int32, comb.shape, 1)
    c = jnp.sum(jnp.where(lane == e + MOE_GROUPS, comb, 0.0), axis=-1, keepdims=True)
    act = gate * jax.nn.sigmoid(gate) * up * c
    acc_ref[...] += jnp.dot(act.astype(BF16), wd_ref[...], preferred_element_type=F32)

    @pl.when(e == N_EXPERTS - 1)
    def _():
        h2 = _layer_norm(DN_ALPHA * h_ref[...] + acc_ref[...], g_ref[...], b_ref[...])
        o_ref[...] = h2
        ob_ref[...] = h2.astype(BF16)


def moe_dense(h1b_sb, h1_sb, comb_sb, wg, wu, wd, g, b, bsz, s, last, ts=1024):
    d = D_MODEL
    g2, b2 = g.reshape(1, d), b.reshape(1, d)
    if last:
        o_spec = pl.BlockSpec((None, ts, d), lambda bi, si, e: (bi, si, 0))
        o_shape = jax.ShapeDtypeStruct((bsz, s, d), F32)
    else:
        o_spec = pl.BlockSpec((ts, d), lambda bi, si, e: (si, bi))
        o_shape = jax.ShapeDtypeStruct((s, bsz * d), F32)
    return pl.pallas_call(
        _moe_kernel,
        grid=(bsz, s // ts, N_EXPERTS),
        in_specs=[pl.BlockSpec((ts, d), lambda bi, si, e: (si, bi)),
                  pl.BlockSpec((ts, d), lambda bi, si, e: (si, bi)),
                  pl.BlockSpec((ts, ROUTE_LANES), lambda bi, si, e: (si, bi)),
                  pl.BlockSpec((None, d, EXPERT_FF), lambda bi, si, e: (e, 0, 0)),
                  pl.BlockSpec((None, d, EXPERT_FF), lambda bi, si, e: (e, 0, 0)),
                  pl.BlockSpec((None, EXPERT_FF, d), lambda bi, si, e: (e, 0, 0)),
                  pl.BlockSpec((1, d), lambda bi, si, e: (0, 0)),
                  pl.BlockSpec((1, d), lambda bi, si, e: (0, 0))],
        out_specs=[o_spec, pl.BlockSpec((ts, d), lambda bi, si, e: (si, bi))],
        out_shape=[o_shape, jax.ShapeDtypeStruct((s, bsz * d), BF16)],
        scratch_shapes=[pltpu.VMEM((ts, d), F32)],
        compiler_params=_cparams(("parallel", "parallel", "arbitrary")),
        name="moe_dense",
    )(h1b_sb, h1_sb, comb_sb, wg, wu, wd, g2, b2)


def kernel(x, ln_in_g, ln_in_b, w_in, s5_lambda_re, s5_lambda_im, s5_log_step, s5_b_re, s5_b_im,
           s5_c_re, s5_c_im, s5_d, w_glu, w_branch_ret, w_branch_s5, w_out, ln_mix_g, ln_mix_b,
           w_router_group, b_router_group, w_router_expert, b_router_expert, w_exp_gate, w_exp_up,
           w_exp_down, ln_ffn_g, ln_ffn_b):
    bsz, s, d = x.shape
    t = bsz * s
    depth = w_in.shape[0]
    tabs = retention_tables(s)
    h_sb, hb_sb = ln_in(x, ln_in_g, ln_in_b)
    out = None
    for l in range(depth):
        proj = in_proj(hb_sb, w_in[l].astype(BF16), bsz)
        ro = retention(proj, tabs, bsz, s)
        a_re, a_im, bb_re, bb_im = s5_params(s5_lambda_re[l], s5_lambda_im[l], s5_log_step[l],
                                             s5_b_re[l], s5_b_im[l])
        mats = s5_matrices(a_re, a_im, bb_re, bb_im, s5_c_re[l], s5_c_im[l])
        y = s5_scan(proj, mats, s5_d[l], bsz, s)
        pad = ROUTE_LANES - MOE_GROUPS - N_EXPERTS
        wr = jnp.concatenate([w_router_group[l], w_router_expert[l], jnp.zeros((d, pad), F32)], axis=1)
        br = jnp.concatenate([b_router_group[l], b_router_expert[l], jnp.zeros((pad,), F32)]).reshape(1, -1)
        h1, h1b, comb = mix_out(ro, y, proj, h_sb,
                                w_glu[l].astype(BF16), w_branch_ret[l].astype(BF16),
                                w_branch_s5[l].astype(BF16), w_out[l].astype(BF16),
                                ln_mix_g[l], ln_mix_b[l], wr, br, bsz)
        last = l == depth - 1
        h2, h2b = moe_dense(h1b, h1, comb,
                            w_exp_gate[l].astype(BF16), w_exp_up[l].astype(BF16),
                            w_exp_down[l].astype(BF16), ln_ffn_g[l], ln_ffn_b[l], bsz, s, last)
        if last:
            out = h2
        else:
            h_sb, hb_sb = h2, h2b
    return out
```

```python
import functools
import math

import jax
import jax.numpy as jnp
from jax import lax
from jax.experimental import pallas as pl
from jax.experimental.pallas import tpu as pltpu

D_MODEL = 1024
CHUNK = 64
RET_HEADS = 8
RET_QK = 512
RET_V = 1024
RET_QK_DIM = 64
RET_V_DIM = 128
ROPE_BASE = 10000.0
S5_WIDTH = 512
S5_GROUP_CH = 16
S5_GROUPS = 32
S5_STATE = 64
MOE_GROUPS = 4
EXPERTS_PER_GROUP = 8
N_EXPERTS = 32
EXPERT_FF = 256
LN_EPS = 1e-5
HEAD_NORM_EPS = 1e-6
DEPTH = 2
DN_ALPHA = (2 * DEPTH) ** 0.25
IN_WIDTH = 2 * RET_QK + 2 * RET_V + S5_WIDTH + 2 * D_MODEL
PROJ_BLK = 512
N_PROJ_BLK = IN_WIDTH // PROJ_BLK

RET_SUPER = 128
S5_TT = 64
S5_COLS = 128
S5_NSLICE = S5_WIDTH // S5_COLS
S5_SLICE_STATE = (S5_COLS // S5_GROUP_CH) * S5_STATE
ROUTE_LANES = 128
TOP_K = 2
MOE_TM = 256
SUBLANES = 8
VMEM_LIMIT = 56 * 1024 * 1024

F32 = jnp.float32
BF16 = jnp.bfloat16


def _cparams(sem):
    return pltpu.CompilerParams(dimension_semantics=sem, vmem_limit_bytes=VMEM_LIMIT)


def _layer_norm(x, g, b):
    mu = jnp.mean(x, axis=-1, keepdims=True)
    xc = x - mu
    var = jnp.mean(xc * xc, axis=-1, keepdims=True)
    return xc * lax.rsqrt(var + LN_EPS) * g + b


def _ln_in_kernel(x_ref, g_ref, b_ref, h_ref, hb_ref):
    h = _layer_norm(x_ref[...], g_ref[...], b_ref[...])
    h_ref[...] = h
    hb_ref[...] = h.astype(BF16)


def ln_in(x, g, b, ts=512):
    bsz, s, d = x.shape
    return pl.pallas_call(
        _ln_in_kernel,
        grid=(bsz, s // ts),
        in_specs=[pl.BlockSpec((None, ts, d), lambda bi, si: (bi, si, 0)),
                  pl.BlockSpec((1, d), lambda bi, si: (0, 0)),
                  pl.BlockSpec((1, d), lambda bi, si: (0, 0))],
        out_specs=[pl.BlockSpec((ts, d), lambda bi, si: (si, bi)),
                   pl.BlockSpec((ts, d), lambda bi, si: (si, bi))],
        out_shape=[jax.ShapeDtypeStruct((s, bsz * d), F32),
                   jax.ShapeDtypeStruct((s, bsz * d), BF16)],
        compiler_params=_cparams(("parallel", "parallel")),
        name="ln_in",
    )(x, g.reshape(1, d), b.reshape(1, d))


def _inproj_kernel(h_ref, w_ref, o_ref):
    o_ref[...] = jnp.dot(h_ref[...], w_ref[...], preferred_element_type=F32).astype(BF16)


def in_proj(hb_sb, w, bsz, ts=2048):
    s = hb_sb.shape[0]
    d, n = w.shape
    nb = n // PROJ_BLK
    return pl.pallas_call(
        _inproj_kernel,
        grid=(bsz, s // ts, nb),
        in_specs=[pl.BlockSpec((ts, d), lambda b, i, j: (i, b)),
                  pl.BlockSpec((d, PROJ_BLK), lambda b, i, j: (0, j))],
        out_specs=pl.BlockSpec((ts, PROJ_BLK), lambda b, i, j: (i, b * nb + j)),
        out_shape=jax.ShapeDtypeStruct((s, bsz * n), BF16),
        compiler_params=_cparams(("parallel", "parallel", "arbitrary")),
        name="in_proj",
    )(hb_sb, w)


def _swap_halves(x):
    lane = lax.broadcasted_iota(jnp.int32, x.shape, 1)
    first = (lane % RET_QK_DIM) < (RET_QK_DIM // 2)
    n = x.shape[1]
    return jnp.where(first, pltpu.roll(x, n - RET_QK_DIM // 2, 1), pltpu.roll(x, RET_QK_DIM // 2, 1))


def _retention_kernel(q_ref, k_ref, v0_ref, v1_ref, g0_ref, g1_ref, cos_ref, sin_ref,
                      qd_ref, kd_ref, mask_ref, cd_ref, o_ref, state_ref):
    @pl.when(pl.program_id(1) == 0)
    def _():
        state_ref[...] = jnp.zeros_like(state_ref)

    cos = cos_ref[...]
    sin = sin_ref[...]
    q = q_ref[...].astype(F32)
    k = k_ref[...].astype(F32)
    q = q * cos + _swap_halves(q) * sin
    k = (k * cos + _swap_halves(k) * sin) * (RET_QK_DIM ** -0.5)
    qb = q.astype(BF16)
    kb = k.astype(BF16)
    qdb = (q * qd_ref[...]).astype(BF16)
    kdb = (k * kd_ref[...]).astype(BF16)
    for hd in range(RET_HEADS):
        qs = slice(hd * RET_QK_DIM, (hd + 1) * RET_QK_DIM)
        half, off = divmod(hd * RET_V_DIM, PROJ_BLK)
        vs = slice(off, off + RET_V_DIM)
        vh = (v0_ref, v1_ref)[half][:, vs]
        gh = (g0_ref, g1_ref)[half][:, vs].astype(F32)
        sc = lax.dot_general(qb[:, qs], kb[:, qs], (((1,), (1,)), ((), ())),
                             preferred_element_type=F32)
        sc = (sc * mask_ref[hd]).astype(BF16)
        st = state_ref[hd]
        o = jnp.dot(sc, vh, preferred_element_type=F32)
        o = o + jnp.dot(qdb[:, qs], st.astype(BF16), preferred_element_type=F32)
        kv = lax.dot_general(kdb[:, qs], vh, (((0,), (0,)), ((), ())),
                             preferred_element_type=F32)
        state_ref[hd] = st * cd_ref[hd] + kv
        mu = jnp.mean(o, axis=-1, keepdims=True)
        oc = o - mu
        var = jnp.mean(oc * oc, axis=-1, keepdims=True)
        on = oc * lax.rsqrt(var + HEAD_NORM_EPS)
        o_ref[:, hd * RET_V_DIM:(hd + 1) * RET_V_DIM] = (gh * jax.nn.sigmoid(gh) * on).astype(BF16)


def retention(proj_sb, tabs, bsz, s):
    cos_t, sin_t, qd_t, kd_t, mask, cd = tabs
    L = RET_SUPER
    nb = N_PROJ_BLK

    def pspec(col):
        return pl.BlockSpec((L, PROJ_BLK), lambda b, i, col=col: (i, b * nb + col))

    full2 = pl.BlockSpec((L, RET_QK), lambda b, i: (0, 0))
    return pl.pallas_call(
        _retention_kernel,
        grid=(bsz, s // L),
        in_specs=[pspec(0), pspec(1), pspec(2), pspec(3), pspec(4), pspec(5),
                  pl.BlockSpec((L, RET_QK), lambda b, i: (i, 0)),
                  pl.BlockSpec((L, RET_QK), lambda b, i: (i, 0)),
                  full2, full2,
                  pl.BlockSpec((RET_HEADS, L, L), lambda b, i: (0, 0, 0)),
                  pl.BlockSpec((RET_HEADS, 1, RET_V_DIM), lambda b, i: (0, 0, 0))],
        out_specs=pl.BlockSpec((L, RET_V), lambda b, i: (i, b)),
        out_shape=jax.ShapeDtypeStruct((s, bsz * RET_V), BF16),
        scratch_shapes=[pltpu.VMEM((RET_HEADS, RET_QK_DIM, RET_V_DIM), F32)],
        compiler_params=_cparams(("parallel", "arbitrary")),
        name="retention",
    )(proj_sb, proj_sb, proj_sb, proj_sb, proj_sb, proj_sb, cos_t, sin_t, qd_t, kd_t, mask, cd)


def retention_tables(s):
    L = RET_SUPER
    half = RET_QK_DIM // 2
    inv_freq = ROPE_BASE ** (-jnp.arange(half, dtype=F32) / half)
    ang = jnp.arange(s, dtype=F32)[:, None] * inv_freq[None, :]
    cos, sin = jnp.cos(ang), jnp.sin(ang)
    cos_t = jnp.tile(jnp.concatenate([cos, cos], -1), (1, RET_HEADS))
    sin_t = jnp.tile(jnp.concatenate([-sin, sin], -1), (1, RET_HEADS))
    log_gamma = jnp.log1p(-(2.0 ** (-5.0 - jnp.arange(RET_HEADS, dtype=F32))))
    pos = jnp.arange(L, dtype=F32)
    qd = jnp.exp(log_gamma[None, :] * (pos + 1.0)[:, None])
    kd = jnp.exp(log_gamma[None, :] * (L - 1.0 - pos)[:, None])
    qd_t = jnp.repeat(qd, RET_QK_DIM, axis=1)
    kd_t = jnp.repeat(kd, RET_QK_DIM, axis=1)
    chunk_id = jnp.arange(L) // CHUNK
    visible = (chunk_id[None, :] <= chunk_id[:, None]).astype(F32)
    mask = jnp.exp(log_gamma[:, None, None] * jnp.abs(pos[:, None] - pos[None, :])) * visible[None]
    cd = jnp.broadcast_to(jnp.exp(log_gamma * L)[:, None, None], (RET_HEADS, 1, RET_V_DIM))
    return cos_t, sin_t, qd_t, kd_t, mask, cd


def _s5_param_kernel(lre_ref, lim_ref, ls_ref, bre_ref, bim_ref, are_ref, aim_ref, bbre_ref, bbim_ref):
    lam_re = jnp.minimum(lre_ref[...], -1e-4)
    lam_im = lim_ref[...]
    step = jnp.exp(ls_ref[...])
    mag = jnp.exp(lam_re * step)
    ang = lam_im * step
    ab_re = mag * jnp.cos(ang)
    ab_im = mag * jnp.sin(ang)
    den = lam_re * lam_re + lam_im * lam_im
    n_re = ab_re - 1.0
    zc_re = (n_re * lam_re + ab_im * lam_im) / den
    zc_im = (ab_im * lam_re - n_re * lam_im) / den
    are_ref[...] = ab_re
    aim_ref[...] = ab_im
    b_re = bre_ref[...]
    b_im = bim_ref[...]
    bbre_ref[...] = zc_re * b_re - zc_im * b_im
    bbim_ref[...] = zc_re * b_im + zc_im * b_re


def s5_params(lam_re, lam_im, log_step, b_re, b_im):
    g, n = lam_re.shape
    c = b_re.shape[-1]
    outs = pl.pallas_call(
        _s5_param_kernel,
        out_shape=[jax.ShapeDtypeStruct((g, 1, n), F32), jax.ShapeDtypeStruct((g, 1, n), F32),
                   jax.ShapeDtypeStruct((g, c, n), F32), jax.ShapeDtypeStruct((g, c, n), F32)],
        name="s5_params",
    )(lam_re.reshape(g, 1, n), lam_im.reshape(g, 1, n), log_step.reshape(g, 1, 1),
      jnp.swapaxes(b_re, 1, 2), jnp.swapaxes(b_im, 1, 2))
    return outs


def _block_diag(x):
    ns, gl, r, c = x.shape
    eye = jnp.eye(gl, dtype=x.dtype)
    return jnp.einsum('sgrc,gh->sgrhc', x, eye).reshape(ns, gl * r, gl * c)


def s5_matrices(a_re, a_im, bb_re, bb_im, c_re, c_im):
    gl = S5_COLS // S5_GROUP_CH
    ns = S5_NSLICE
    bre = _block_diag(bb_re.reshape(ns, gl, S5_GROUP_CH, S5_STATE))
    bim = _block_diag(bb_im.reshape(ns, gl, S5_GROUP_CH, S5_STATE))
    bq = jnp.concatenate([bre, bim], axis=-1).astype(BF16)
    cre = _block_diag(jnp.swapaxes(c_re, 1, 2).reshape(ns, gl, S5_STATE, S5_GROUP_CH))
    cim = _block_diag(jnp.swapaxes(c_im, 1, 2).reshape(ns, gl, S5_STATE, S5_GROUP_CH))
    cq = jnp.concatenate([cre, -cim], axis=1).astype(BF16)
    are = a_re.reshape(ns, 1, S5_SLICE_STATE)
    aim = a_im.reshape(ns, 1, S5_SLICE_STATE)
    return bq, cq, are, aim


def _s5_kernel(*refs, bsz, tt):
    u_refs = refs[:bsz]
    bq_ref, cq_ref, are_ref, aim_ref, d_ref, y_ref, us_ref, ys_ref, bu_ref, st_ref = refs[bsz:]

    @pl.when(pl.program_id(0) == 0)
    def _():
        st_ref[...] = jnp.zeros_like(st_ref)

    for b in range(bsz):
        ub = u_refs[b][...].astype(F32)
        for cs in range(S5_NSLICE):
            us_ref[cs, pl.ds(b, tt, stride=bsz), :] = ub[:, cs * S5_COLS:(cs + 1) * S5_COLS]

    ns2 = S5_SLICE_STATE
    for cs in range(S5_NSLICE):
        cols = slice(cs * S5_COLS, (cs + 1) * S5_COLS)
        uf = us_ref[cs]
        bu_ref[...] = jnp.dot(uf.astype(BF16), bq_ref[cs], preferred_element_type=F32)
        a_re = jnp.broadcast_to(are_ref[cs], (bsz, ns2))
        a_im = jnp.broadcast_to(aim_ref[cs], (bsz, ns2))

        def step(t, carry):
            h_re, h_im = carry
            rows = pl.ds(pl.multiple_of(t * bsz, bsz), bsz)
            n_re = a_re * h_re - a_im * h_im + bu_ref[rows, 0:ns2]
            n_im = a_re * h_im + a_im * h_re + bu_ref[rows, ns2:2 * ns2]
            bu_ref[rows, 0:ns2] = n_re
            bu_ref[rows, ns2:2 * ns2] = n_im
            return n_re, n_im

        h_re, h_im = lax.fori_loop(0, tt, step, (st_ref[cs, 0], st_ref[cs, 1]), unroll=8)
        st_ref[cs, 0] = h_re
        st_ref[cs, 1] = h_im
        y = jnp.dot(bu_ref[...].astype(BF16), cq_ref[cs], preferred_element_type=F32)
        ys_ref[cs] = y + d_ref[:, cols] * uf

    for b in range(bsz):
        for cs in range(S5_NSLICE):
            lo = b * S5_WIDTH + cs * S5_COLS
            y_ref[:, lo:lo + S5_COLS] = ys_ref[cs, pl.ds(b, tt, stride=bsz), :]


def s5_scan(proj_sb, mats, d_skip, bsz, s):
    bq, cq, are, aim = mats
    tt = S5_TT
    rows = tt * bsz
    nb = N_PROJ_BLK
    kern = functools.partial(_s5_kernel, bsz=bsz, tt=tt)
    u_specs = [pl.BlockSpec((tt, PROJ_BLK), lambda i, b=b: (i, b * nb + 6)) for b in range(bsz)]
    return pl.pallas_call(
        kern,
        grid=(s // tt,),
        in_specs=u_specs + [
                  pl.BlockSpec(bq.shape, lambda i: (0, 0, 0)),
                  pl.BlockSpec(cq.shape, lambda i: (0, 0, 0)),
                  pl.BlockSpec(are.shape, lambda i: (0, 0, 0)),
                  pl.BlockSpec(aim.shape, lambda i: (0, 0, 0)),
                  pl.BlockSpec((1, S5_WIDTH), lambda i: (0, 0))],
        out_specs=pl.BlockSpec((tt, bsz * S5_WIDTH), lambda i: (i, 0)),
        out_shape=jax.ShapeDtypeStruct((s, bsz * S5_WIDTH), F32),
        scratch_shapes=[pltpu.VMEM((S5_NSLICE, rows, S5_COLS), F32),
                        pltpu.VMEM((S5_NSLICE, rows, S5_COLS), F32),
                        pltpu.VMEM((rows, 2 * S5_SLICE_STATE), F32),
                        pltpu.VMEM((S5_NSLICE, 2, bsz, S5_SLICE_STATE), F32)],
        compiler_params=_cparams(("arbitrary",)),
        name="s5_scan",
    )(*([proj_sb] * bsz), bq, cq, are, aim, d_skip.reshape(1, S5_WIDTH))


def _gelu_tanh(x):
    c = math.sqrt(2.0 / math.pi)
    return 0.5 * x * (1.0 + jnp.tanh(c * (x + 0.044715 * (x * x * x))))


def _route(logits):
    lane = lax.broadcasted_iota(jnp.int32, logits.shape, 1)
    neg = jnp.float32(-jnp.inf)
    big = jnp.int32(1 << 20)
    is_g = lane < MOE_GROUPS
    lg = jnp.where(is_g, logits, neg)
    mg = jnp.max(lg, axis=-1, keepdims=True)
    sg = jnp.sum(jnp.where(is_g, jnp.exp(lg - mg), 0.0), axis=-1, keepdims=True)
    g_top = 1.0 / sg
    g_idx = jnp.min(jnp.where(lg == mg, lane, big), axis=-1, keepdims=True)
    lo = MOE_GROUPS + g_idx * EXPERTS_PER_GROUP
    in_grp = (lane >= lo) & (lane < lo + EXPERTS_PER_GROUP)
    le = jnp.where(in_grp, logits, neg)
    m1 = jnp.max(le, axis=-1, keepdims=True)
    se = jnp.sum(jnp.where(in_grp, jnp.exp(le - m1), 0.0), axis=-1, keepdims=True)
    i1 = jnp.min(jnp.where(le == m1, lane, big), axis=-1, keepdims=True)
    le2 = jnp.where(lane == i1, neg, le)
    m2 = jnp.max(le2, axis=-1, keepdims=True)
    i2 = jnp.min(jnp.where(le2 == m2, lane, big), axis=-1, keepdims=True)
    p1 = 1.0 / se
    p2 = jnp.exp(m2 - m1) / se
    tot = p1 + p2
    w1 = g_top * (p1 / tot)
    w2 = g_top * (p2 / tot)
    return lane, i1, i2, w1, w2


def _mix_kernel(ro_ref, y_ref, gr0_ref, gr1_ref, gs0_ref, gs1_ref, h_ref,
                wglu_ref, wbr_ref, wbs_ref, wout_ref, g_ref, b_ref, wr_ref, br_ref, tri_ref,
                h1_ref, route_ref, cnt_ref, run_ref):
    @pl.when((pl.program_id(0) == 0) & (pl.program_id(1) == 0))
    def _():
        run_ref[...] = jnp.zeros_like(run_ref)

    z = _gelu_tanh(y_ref[...])
    zg = jnp.dot(z.astype(BF16), wglu_ref[...], preferred_element_type=F32)
    zz = (z * jax.nn.sigmoid(zg)).astype(BF16)
    s5b = jnp.dot(zz, wbs_ref[...], preferred_element_type=F32)
    rb = jnp.dot(ro_ref[...], wbr_ref[...], preferred_element_type=F32)
    gr = jnp.concatenate([gr0_ref[...], gr1_ref[...]], axis=-1).astype(F32)
    gs = jnp.concatenate([gs0_ref[...], gs1_ref[...]], axis=-1).astype(F32)
    merged = jax.nn.sigmoid(gr) * rb + jax.nn.sigmoid(gs) * s5b
    mix = jnp.dot(merged.astype(BF16), wout_ref[...], preferred_element_type=F32)
    h1 = _layer_norm(DN_ALPHA * h_ref[...] + mix, g_ref[...], b_ref[...])
    h1_ref[...] = h1
    logits = jnp.dot(h1, wr_ref[...], preferred_element_type=F32,
                     precision=lax.Precision.HIGHEST) + br_ref[...]
    lane, i1, i2, w1, w2 = _route(logits)
    oh1 = lane == i1
    oh2 = lane == i2
    oh = jnp.where(oh1 | oh2, 1.0, 0.0)
    before = run_ref[0:1, :] + jnp.dot(tri_ref[...], oh.astype(BF16), preferred_element_type=F32)
    r1 = jnp.sum(jnp.where(oh1, before, 0.0), axis=-1, keepdims=True)
    r2 = jnp.sum(jnp.where(oh2, before, 0.0), axis=-1, keepdims=True)
    cnt = run_ref[0:1, :] + jnp.sum(oh, axis=0, keepdims=True)
    run_ref[...] = jnp.broadcast_to(cnt, run_ref.shape)
    cnt_ref[...] = jnp.broadcast_to(cnt, cnt_ref.shape)
    vals = (i1.astype(F32) - MOE_GROUPS, i2.astype(F32) - MOE_GROUPS, w1, w2, r1, r2)
    route = jnp.zeros(logits.shape, F32)
    for k, v in enumerate(vals):
        route = jnp.where(lane == k, v, route)
    route_ref[...] = route


def mix_out(ro, y, proj, h, wglu, wbr, wbs, wout, g, b, wr, br, bsz, ts=512):
    s = h.shape[0]
    d = D_MODEL
    nb = N_PROJ_BLK

    def const(a):
        return pl.BlockSpec(a.shape, lambda bi, i: (0,) * a.ndim)

    def pspec(col):
        return pl.BlockSpec((ts, PROJ_BLK), lambda bi, i, col=col: (i, bi * nb + col))

    def tok(w):
        return pl.BlockSpec((ts, w), lambda bi, i: (i, bi))

    g2, b2 = g.reshape(1, d), b.reshape(1, d)
    idx = jnp.arange(ts)
    tri = (idx[None, :] < idx[:, None]).astype(BF16)
    return pl.pallas_call(
        _mix_kernel,
        grid=(bsz, s // ts),
        in_specs=[tok(RET_V), tok(S5_WIDTH), pspec(7), pspec(8), pspec(9), pspec(10), tok(d),
                  const(wglu), const(wbr), const(wbs), const(wout), const(g2), const(b2),
                  const(wr), const(br), const(tri)],
        out_specs=[tok(d), tok(ROUTE_LANES), pl.BlockSpec((8, ROUTE_LANES), lambda bi, i: (0, 0))],
        out_shape=[jax.ShapeDtypeStruct((s, bsz * d), F32),
                   jax.ShapeDtypeStruct((s, bsz * ROUTE_LANES), F32),
                   jax.ShapeDtypeStruct((8, ROUTE_LANES), F32)],
        scratch_shapes=[pltpu.VMEM((8, ROUTE_LANES), F32)],
        compiler_params=_cparams(("arbitrary", "arbitrary")),
        name="mix_out",
    )(ro, y, proj, proj, proj, proj, h, wglu, wbr, wbs, wout, g2, b2, wr, br, tri)


def moe_rows(t):
    return TOP_K * t + (N_EXPERTS + 1) * MOE_TM


def moe_plan(route, cnt, bsz, s):
    t = bsz * s
    rec = route.reshape(s, bsz, ROUTE_LANES)[:, :, :6]
    rec = jnp.transpose(rec, (1, 0, 2)).reshape(t, 6)
    e1 = rec[:, 0].astype(jnp.int32)
    e2 = rec[:, 1].astype(jnp.int32)
    r1 = rec[:, 4].astype(jnp.int32)
    r2 = rec[:, 5].astype(jnp.int32)
    counts = cnt[0, MOE_GROUPS:MOE_GROUPS + N_EXPERTS].astype(jnp.int32)
    padded = ((counts + MOE_TM - 1) // MOE_TM) * MOE_TM
    ends = jnp.cumsum(padded)
    off = ends - padded
    d1 = off[e1] + r1
    d2 = off[e2] + r2
    pad_start = ((off + counts) // SUBLANES) * SUBLANES
    n_tiles = moe_rows(t) // MOE_TM
    n_used = (ends[-1:] // MOE_TM).astype(jnp.int32)
    tile_start = jnp.arange(n_tiles, dtype=jnp.int32) * MOE_TM
    tile_expert = jnp.sum((ends[None, :] <= tile_start[:, None]).astype(jnp.int32), axis=1)
    tile_expert = jnp.minimum(tile_expert, N_EXPERTS - 1)
    return d1, d2, pad_start, tile_expert, n_used


def _dispatch_kernel(d1_ref, d2_ref, pad_ref, nu_ref, x_ref, xs_ref, zero_ref, sem, *, ts):
    step = pl.program_id(0) * pl.num_programs(1) + pl.program_id(1)

    @pl.when(step == 0)
    def _():
        zero_ref[...] = jnp.zeros_like(zero_ref)

        def zero_copy(start):
            start = pl.multiple_of(start, SUBLANES)
            return pltpu.make_async_copy(zero_ref, xs_ref.at[pl.ds(start, MOE_TM)], sem)

        for e in range(N_EXPERTS):
            zero_copy(pad_ref[e]).start()
        for e in range(N_EXPERTS):
            zero_copy(pad_ref[e]).wait()

        def zero_tile(r, c):
            cp = zero_copy(r * MOE_TM)
            cp.start()
            cp.wait()
            return c

        lax.fori_loop(nu_ref[0], xs_ref.shape[0] // MOE_TM, zero_tile, 0)

    base = step * ts

    def row_copy(i, dst):
        return pltpu.make_async_copy(x_ref.at[pl.ds(i, 1)], xs_ref.at[pl.ds(dst, 1)], sem)

    def issue(i, c):
        row_copy(i, d1_ref[base + i]).start()
        row_copy(i, d2_ref[base + i]).start()
        return c

    lax.fori_loop(0, ts, issue, 0)

    def drain(i, c):
        row_copy(i, d1_ref[base + i]).wait()
        row_copy(i, d2_ref[base + i]).wait()
        return c

    lax.fori_loop(0, ts, drain, 0)


def moe_dispatch(h1, d1, d2, pad_start, n_used, bsz, ts=512):
    s = h1.shape[0]
    d = D_MODEL
    kern = functools.partial(_dispatch_kernel, ts=ts)
    grid_spec = pltpu.PrefetchScalarGridSpec(
        num_scalar_prefetch=4,
        grid=(bsz, s // ts),
        in_specs=[pl.BlockSpec((ts, d), lambda bi, i, *_: (i, bi))],
        out_specs=pl.BlockSpec(memory_space=pl.ANY),
        scratch_shapes=[pltpu.VMEM((MOE_TM, d), F32), pltpu.SemaphoreType.DMA(())],
    )
    return pl.pallas_call(
        kern,
        grid_spec=grid_spec,
        out_shape=jax.ShapeDtypeStruct((moe_rows(bsz * s), d), F32),
        compiler_params=_cparams(("arbitrary", "arbitrary")),
        name="moe_dispatch",
    )(d1, d2, pad_start, n_used, h1)


def _experts_kernel(te_ref, nu_ref, xs_ref, wg_ref, wu_ref, wd_ref, ys_ref, wgb_ref, wub_ref, wdb_ref):
    r = pl.program_id(0)

    @pl.when(r < nu_ref[0])
    def _():
        changed = (r == 0) | (te_ref[r] != te_ref[jnp.maximum(r - 1, 0)])

        @pl.when(changed)
        def _():
            wgb_ref[...] = wg_ref[...].astype(BF16)
            wub_ref[...] = wu_ref[...].astype(BF16)
            wdb_ref[...] = wd_ref[...].astype(BF16)

        x = xs_ref[...].astype(BF16)
        gate = jnp.dot(x, wgb_ref[...], preferred_element_type=F32)
        up = jnp.dot(x, wub_ref[...], preferred_element_type=F32)
        act = (gate * jax.nn.sigmoid(gate) * up).astype(BF16)
        ys_ref[...] = jnp.dot(act, wdb_ref[...], preferred_element_type=F32)

    @pl.when(r >= nu_ref[0])
    def _():
        ys_ref[...] = jnp.zeros_like(ys_ref)


def moe_experts(xs, tile_expert, n_used, wg, wu, wd):
    rows, d = xs.shape
    f = wg.shape[-1]

    def tile(r, te, nu):
        return jnp.minimum(r, nu[0] - 1)

    grid_spec = pltpu.PrefetchScalarGridSpec(
        num_scalar_prefetch=2,
        grid=(rows // MOE_TM,),
        in_specs=[pl.BlockSpec((MOE_TM, d), lambda r, te, nu: (tile(r, te, nu), 0)),
                  pl.BlockSpec((None, d, f), lambda r, te, nu: (te[tile(r, te, nu)], 0, 0)),
                  pl.BlockSpec((None, d, f), lambda r, te, nu: (te[tile(r, te, nu)], 0, 0)),
                  pl.BlockSpec((None, f, d), lambda r, te, nu: (te[tile(r, te, nu)], 0, 0))],
        out_specs=pl.BlockSpec((MOE_TM, d), lambda r, te, nu: (r, 0)),
        scratch_shapes=[pltpu.VMEM((d, f), BF16), pltpu.VMEM((d, f), BF16), pltpu.VMEM((f, d), BF16)],
    )
    return pl.pallas_call(
        _experts_kernel,
        grid_spec=grid_spec,
        out_shape=jax.ShapeDtypeStruct((rows, d), F32),
        compiler_params=_cparams(("arbitrary",)),
        name="moe_experts",
    )(tile_expert, n_used, xs, wg, wu, wd)


def _combine_kernel(d1_ref, d2_ref, ys_ref, route_ref, h_ref, g_ref, b_ref, o_ref, ob_ref,
                    buf1_ref, buf2_ref, sem, *, ts):
    nsteps = pl.num_programs(0) * pl.num_programs(1)
    step = pl.program_id(0) * pl.num_programs(1) + pl.program_id(1)

    def row_copies(blk, slot, i):
        n = blk * ts + i
        return (pltpu.make_async_copy(ys_ref.at[pl.ds(d1_ref[n], 1)], buf1_ref.at[slot, pl.ds(i, 1)],
                                      sem.at[slot]),
                pltpu.make_async_copy(ys_ref.at[pl.ds(d2_ref[n], 1)], buf2_ref.at[slot, pl.ds(i, 1)],
                                      sem.at[slot]))

    def issue(blk, slot):
        def body(i, c):
            c1, c2 = row_copies(blk, slot, i)
            c1.start()
            c2.start()
            return c
        lax.fori_loop(0, ts, body, 0)

    @pl.when(step == 0)
    def _():
        issue(0, 0)

    @pl.when(step + 1 < nsteps)
    def _():
        issue(step + 1, (step + 1) % 2)

    slot = step % 2

    def drain(i, c):
        c1, c2 = row_copies(step, slot, i)
        c1.wait()
        c2.wait()
        return c

    lax.fori_loop(0, ts, drain, 0)

    route = route_ref[...]
    w1 = route[:, 2:3]
    w2 = route[:, 3:4]
    ffn = w1 * buf1_ref[slot] + w2 * buf2_ref[slot]
    h2 = _layer_norm(DN_ALPHA * h_ref[...] + ffn, g_ref[...], b_ref[...])
    o_ref[...] = h2
    ob_ref[...] = h2.astype(BF16)


def moe_combine(ys, d1, d2, route, h1, g, b, bsz, last, ts=256):
    s = h1.shape[0]
    d = D_MODEL
    g2, b2 = g.reshape(1, d), b.reshape(1, d)
    if last:
        o_spec = pl.BlockSpec((None, ts, d), lambda bi, i, *_: (bi, i, 0))
        o_shape = jax.ShapeDtypeStruct((bsz, s, d), F32)
    else:
        o_spec = pl.BlockSpec((ts, d), lambda bi, i, *_: (i, bi))
        o_shape = jax.ShapeDtypeStruct((s, bsz * d), F32)
    kern = functools.partial(_combine_kernel, ts=ts)
    grid_spec = pltpu.PrefetchScalarGridSpec(
        num_scalar_prefetch=2,
        grid=(bsz, s // ts),
        in_specs=[pl.BlockSpec(memory_space=pl.ANY),
                  pl.BlockSpec((ts, ROUTE_LANES), lambda bi, i, *_: (i, bi)),
                  pl.BlockSpec((ts, d), lambda bi, i, *_: (i, bi)),
                  pl.BlockSpec((1, d), lambda bi, i, *_: (0, 0)),
                  pl.BlockSpec((1, d), lambda bi, i, *_: (0, 0))],
        out_specs=[o_spec, pl.BlockSpec((ts, d), lambda bi, i, *_: (i, bi))],
        scratch_shapes=[pltpu.VMEM((2, ts, d), F32), pltpu.VMEM((2, ts, d), F32),
                        pltpu.SemaphoreType.DMA((2,))],
    )
    return pl.pallas_call(
        kern,
        grid_spec=grid_spec,
        out_shape=[o_shape, jax.ShapeDtypeStruct((s, bsz * d), BF16)],
        compiler_params=_cparams(("arbitrary", "arbitrary")),
        name="moe_combine",
    )(d1, d2, ys, route, h1, g2, b2)


def kernel(x, ln_in_g, ln_in_b, w_in, s5_lambda_re, s5_lambda_im, s5_log_step, s5_b_re, s5_b_im,
           s5_c_re, s5_c_im, s5_d, w_glu, w_branch_ret, w_branch_s5, w_out, ln_mix_g, ln_mix_b,
           w_router_group, b_router_group, w_router_expert, b_router_expert, w_exp_gate, w_exp_up,
           w_exp_down, ln_ffn_g, ln_ffn_b):
    bsz, s, d = x.shape
    t = bsz * s
    depth = w_in.shape[0]
    tabs = retention_tables(s)
    h_sb, hb_sb = ln_in(x, ln_in_g, ln_in_b)
    out = None
    for l in range(depth):
        proj = in_proj(hb_sb, w_in[l].astype(BF16), bsz)
        ro = retention(proj, tabs, bsz, s)
        a_re, a_im, bb_re, bb_im = s5_params(s5_lambda_re[l], s5_lambda_im[l], s5_log_step[l],
                                             s5_b_re[l], s5_b_im[l])
        mats = s5_matrices(a_re, a_im, bb_re, bb_im, s5_c_re[l], s5_c_im[l])
        y = s5_scan(proj, mats, s5_d[l], bsz, s)
        pad = ROUTE_LANES - MOE_GROUPS - N_EXPERTS
        wr = jnp.concatenate([w_router_group[l], w_router_expert[l], jnp.zeros((d, pad), F32)], axis=1)
        br = jnp.concatenate([b_router_group[l], b_router_expert[l], jnp.zeros((pad,), F32)]).reshape(1, -1)
        h1, route, cnt = mix_out(ro, y, proj, h_sb,
                                 w_glu[l].astype(BF16), w_branch_ret[l].astype(BF16),
                                 w_branch_s5[l].astype(BF16), w_out[l].astype(BF16),
                                 ln_mix_g[l], ln_mix_b[l], wr, br, bsz)
        last = l == depth - 1
        d1, d2, pad_start, tile_expert, n_used = moe_plan(route, cnt, bsz, s)
        xs = moe_dispatch(h1, d1, d2, pad_start, n_used, bsz)
        ys = moe_experts(xs, tile_expert, n_used, w_exp_gate[l], w_exp_up[l], w_exp_down[l])
        h2, h2b = moe_combine(ys, d1, d2, route, h1, ln_ffn_g[l], ln_ffn_b[l], bsz, last)
        if last:
            out = h2
        else:
            h_sb, hb_sb = h2, h2b
    return out
```

```python
import functools
import math

import jax
import jax.numpy as jnp
from jax import lax
from jax.experimental import pallas as pl
from jax.experimental.pallas import tpu as pltpu

D_MODEL = 1024
CHUNK = 64
RET_HEADS = 8
RET_QK = 512
RET_V = 1024
RET_QK_DIM = 64
RET_V_DIM = 128
ROPE_BASE = 10000.0
S5_WIDTH = 512
S5_GROUP_CH = 16
S5_GROUPS = 32
S5_STATE = 64
MOE_GROUPS = 4
EXPERTS_PER_GROUP = 8
N_EXPERTS = 32
EXPERT_FF = 256
LN_EPS = 1e-5
HEAD_NORM_EPS = 1e-6
DEPTH = 2
DN_ALPHA = (2 * DEPTH) ** 0.25
IN_WIDTH = 2 * RET_QK + 2 * RET_V + S5_WIDTH + 2 * D_MODEL
PROJ_BLK = 512
N_PROJ_BLK = IN_WIDTH // PROJ_BLK

RET_SUPER = 128
S5_TT = 64
S5_COLS = 128
S5_NSLICE = S5_WIDTH // S5_COLS
S5_SLICE_STATE = (S5_COLS // S5_GROUP_CH) * S5_STATE
ROUTE_LANES = 128
TOP_K = 2
MOE_TM = 256
LANES = 128
ROW_TILE = D_MODEL // LANES
VMEM_LIMIT = 56 * 1024 * 1024

F32 = jnp.float32
BF16 = jnp.bfloat16


def _cparams(sem):
    return pltpu.CompilerParams(dimension_semantics=sem, vmem_limit_bytes=VMEM_LIMIT)


def _layer_norm(x, g, b):
    mu = jnp.mean(x, axis=-1, keepdims=True)
    xc = x - mu
    var = jnp.mean(xc * xc, axis=-1, keepdims=True)
    return xc * lax.rsqrt(var + LN_EPS) * g + b


def _ln_in_kernel(x_ref, g_ref, b_ref, h_ref, hb_ref):
    h = _layer_norm(x_ref[...], g_ref[...], b_ref[...])
    h_ref[...] = h
    hb_ref[...] = h.astype(BF16)


def ln_in(x, g, b, ts=512):
    bsz, s, d = x.shape
    return pl.pallas_call(
        _ln_in_kernel,
        grid=(bsz, s // ts),
        in_specs=[pl.BlockSpec((None, ts, d), lambda bi, si: (bi, si, 0)),
                  pl.BlockSpec((1, d), lambda bi, si: (0, 0)),
                  pl.BlockSpec((1, d), lambda bi, si: (0, 0))],
        out_specs=[pl.BlockSpec((ts, d), lambda bi, si: (si, bi)),
                   pl.BlockSpec((ts, d), lambda bi, si: (si, bi))],
        out_shape=[jax.ShapeDtypeStruct((s, bsz * d), F32),
                   jax.ShapeDtypeStruct((s, bsz * d), BF16)],
        compiler_params=_cparams(("parallel", "parallel")),
        name="ln_in",
    )(x, g.reshape(1, d), b.reshape(1, d))


def _inproj_kernel(h_ref, w_ref, o_ref):
    o_ref[...] = jnp.dot(h_ref[...], w_ref[...], preferred_element_type=F32).astype(BF16)


def in_proj(hb_sb, w, bsz, ts=2048):
    s = hb_sb.shape[0]
    d, n = w.shape
    nb = n // PROJ_BLK
    return pl.pallas_call(
        _inproj_kernel,
        grid=(bsz, s // ts, nb),
        in_specs=[pl.BlockSpec((ts, d), lambda b, i, j: (i, b)),
                  pl.BlockSpec((d, PROJ_BLK), lambda b, i, j: (0, j))],
        out_specs=pl.BlockSpec((ts, PROJ_BLK), lambda b, i, j: (i, b * nb + j)),
        out_shape=jax.ShapeDtypeStruct((s, bsz * n), BF16),
        compiler_params=_cparams(("parallel", "parallel", "arbitrary")),
        name="in_proj",
    )(hb_sb, w)


def _swap_halves(x):
    lane = lax.broadcasted_iota(jnp.int32, x.shape, 1)
    first = (lane % RET_QK_DIM) < (RET_QK_DIM // 2)
    n = x.shape[1]
    return jnp.where(first, pltpu.roll(x, n - RET_QK_DIM // 2, 1), pltpu.roll(x, RET_QK_DIM // 2, 1))


def _retention_kernel(q_ref, k_ref, v0_ref, v1_ref, g0_ref, g1_ref, cos_ref, sin_ref,
                      qd_ref, kd_ref, mask_ref, cd_ref, o_ref, state_ref):
    @pl.when(pl.program_id(1) == 0)
    def _():
        state_ref[...] = jnp.zeros_like(state_ref)

    cos = cos_ref[...]
    sin = sin_ref[...]
    q = q_ref[...].astype(F32)
    k = k_ref[...].astype(F32)
    q = q * cos + _swap_halves(q) * sin
    k = (k * cos + _swap_halves(k) * sin) * (RET_QK_DIM ** -0.5)
    qb = q.astype(BF16)
    kb = k.astype(BF16)
    qdb = (q * qd_ref[...]).astype(BF16)
    kdb = (k * kd_ref[...]).astype(BF16)
    for hd in range(RET_HEADS):
        qs = slice(hd * RET_QK_DIM, (hd + 1) * RET_QK_DIM)
        half, off = divmod(hd * RET_V_DIM, PROJ_BLK)
        vs = slice(off, off + RET_V_DIM)
        vh = (v0_ref, v1_ref)[half][:, vs]
        gh = (g0_ref, g1_ref)[half][:, vs].astype(F32)
        sc = lax.dot_general(qb[:, qs], kb[:, qs], (((1,), (1,)), ((), ())),
                             preferred_element_type=F32)
        sc = (sc * mask_ref[hd]).astype(BF16)
        st = state_ref[hd]
        o = jnp.dot(sc, vh, preferred_element_type=F32)
        o = o + jnp.dot(qdb[:, qs], st.astype(BF16), preferred_element_type=F32)
        kv = lax.dot_general(kdb[:, qs], vh, (((0,), (0,)), ((), ())),
                             preferred_element_type=F32)
        state_ref[hd] = st * cd_ref[hd] + kv
        mu = jnp.mean(o, axis=-1, keepdims=True)
        oc = o - mu
        var = jnp.mean(oc * oc, axis=-1, keepdims=True)
        on = oc * lax.rsqrt(var + HEAD_NORM_EPS)
        o_ref[:, hd * RET_V_DIM:(hd + 1) * RET_V_DIM] = (gh * jax.nn.sigmoid(gh) * on).astype(BF16)


def retention(proj_sb, tabs, bsz, s):
    cos_t, sin_t, qd_t, kd_t, mask, cd = tabs
    L = RET_SUPER
    nb = N_PROJ_BLK

    def pspec(col):
        return pl.BlockSpec((L, PROJ_BLK), lambda b, i, col=col: (i, b * nb + col))

    full2 = pl.BlockSpec((L, RET_QK), lambda b, i: (0, 0))
    return pl.pallas_call(
        _retention_kernel,
        grid=(bsz, s // L),
        in_specs=[pspec(0), pspec(1), pspec(2), pspec(3), pspec(4), pspec(5),
                  pl.BlockSpec((L, RET_QK), lambda b, i: (i, 0)),
                  pl.BlockSpec((L, RET_QK), lambda b, i: (i, 0)),
                  full2, full2,
                  pl.BlockSpec((RET_HEADS, L, L), lambda b, i: (0, 0, 0)),
                  pl.BlockSpec((RET_HEADS, 1, RET_V_DIM), lambda b, i: (0, 0, 0))],
        out_specs=pl.BlockSpec((L, RET_V), lambda b, i: (i, b)),
        out_shape=jax.ShapeDtypeStruct((s, bsz * RET_V), BF16),
        scratch_shapes=[pltpu.VMEM((RET_HEADS, RET_QK_DIM, RET_V_DIM), F32)],
        compiler_params=_cparams(("parallel", "arbitrary")),
        name="retention",
    )(proj_sb, proj_sb, proj_sb, proj_sb, proj_sb, proj_sb, cos_t, sin_t, qd_t, kd_t, mask, cd)


def retention_tables(s):
    L = RET_SUPER
    half = RET_QK_DIM // 2
    inv_freq = ROPE_BASE ** (-jnp.arange(half, dtype=F32) / half)
    ang = jnp.arange(s, dtype=F32)[:, None] * inv_freq[None, :]
    cos, sin = jnp.cos(ang), jnp.sin(ang)
    cos_t = jnp.tile(jnp.concatenate([cos, cos], -1), (1, RET_HEADS))
    sin_t = jnp.tile(jnp.concatenate([-sin, sin], -1), (1, RET_HEADS))
    log_gamma = jnp.log1p(-(2.0 ** (-5.0 - jnp.arange(RET_HEADS, dtype=F32))))
    pos = jnp.arange(L, dtype=F32)
    qd = jnp.exp(log_gamma[None, :] * (pos + 1.0)[:, None])
    kd = jnp.exp(log_gamma[None, :] * (L - 1.0 - pos)[:, None])
    qd_t = jnp.repeat(qd, RET_QK_DIM, axis=1)
    kd_t = jnp.repeat(kd, RET_QK_DIM, axis=1)
    chunk_id = jnp.arange(L) // CHUNK
    visible = (chunk_id[None, :] <= chunk_id[:, None]).astype(F32)
    mask = jnp.exp(log_gamma[:, None, None] * jnp.abs(pos[:, None] - pos[None, :])) * visible[None]
    cd = jnp.broadcast_to(jnp.exp(log_gamma * L)[:, None, None], (RET_HEADS, 1, RET_V_DIM))
    return cos_t, sin_t, qd_t, kd_t, mask, cd


def _s5_param_kernel(lre_ref, lim_ref, ls_ref, bre_ref, bim_ref, are_ref, aim_ref, bbre_ref, bbim_ref):
    lam_re = jnp.minimum(lre_ref[...], -1e-4)
    lam_im = lim_ref[...]
    step = jnp.exp(ls_ref[...])
    mag = jnp.exp(lam_re * step)
    ang = lam_im * step
    ab_re = mag * jnp.cos(ang)
    ab_im = mag * jnp.sin(ang)
    den = lam_re * lam_re + lam_im * lam_im
    n_re = ab_re - 1.0
    zc_re = (n_re * lam_re + ab_im * lam_im) / den
    zc_im = (ab_im * lam_re - n_re * lam_im) / den
    are_ref[...] = ab_re
    aim_ref[...] = ab_im
    b_re = bre_ref[...]
    b_im = bim_ref[...]
    bbre_ref[...] = zc_re * b_re - zc_im * b_im
    bbim_ref[...] = zc_re * b_im + zc_im * b_re


def s5_params(lam_re, lam_im, log_step, b_re, b_im):
    g, n = lam_re.shape
    c = b_re.shape[-1]
    outs = pl.pallas_call(
        _s5_param_kernel,
        out_shape=[jax.ShapeDtypeStruct((g, 1, n), F32), jax.ShapeDtypeStruct((g, 1, n), F32),
                   jax.ShapeDtypeStruct((g, c, n), F32), jax.ShapeDtypeStruct((g, c, n), F32)],
        name="s5_params",
    )(lam_re.reshape(g, 1, n), lam_im.reshape(g, 1, n), log_step.reshape(g, 1, 1),
      jnp.swapaxes(b_re, 1, 2), jnp.swapaxes(b_im, 1, 2))
    return outs


def _block_diag(x):
    ns, gl, r, c = x.shape
    eye = jnp.eye(gl, dtype=x.dtype)
    return jnp.einsum('sgrc,gh->sgrhc', x, eye).reshape(ns, gl * r, gl * c)


def s5_matrices(a_re, a_im, bb_re, bb_im, c_re, c_im):
    gl = S5_COLS // S5_GROUP_CH
    ns = S5_NSLICE
    bre = _block_diag(bb_re.reshape(ns, gl, S5_GROUP_CH, S5_STATE))
    bim = _block_diag(bb_im.reshape(ns, gl, S5_GROUP_CH, S5_STATE))
    bq = jnp.concatenate([bre, bim], axis=-1).astype(BF16)
    cre = _block_diag(jnp.swapaxes(c_re, 1, 2).reshape(ns, gl, S5_STATE, S5_GROUP_CH))
    cim = _block_diag(jnp.swapaxes(c_im, 1, 2).reshape(ns, gl, S5_STATE, S5_GROUP_CH))
    cq = jnp.concatenate([cre, -cim], axis=1).astype(BF16)
    are = a_re.reshape(ns, 1, S5_SLICE_STATE)
    aim = a_im.reshape(ns, 1, S5_SLICE_STATE)
    return bq, cq, are, aim


def _s5_kernel(*refs, bsz, tt):
    u_refs = refs[:bsz]
    bq_ref, cq_ref, are_ref, aim_ref, d_ref, y_ref, us_ref, ys_ref, bu_ref, st_ref = refs[bsz:]

    @pl.when(pl.program_id(0) == 0)
    def _():
        st_ref[...] = jnp.zeros_like(st_ref)

    for b in range(bsz):
        ub = u_refs[b][...].astype(F32)
        for cs in range(S5_NSLICE):
            us_ref[cs, pl.ds(b, tt, stride=bsz), :] = ub[:, cs * S5_COLS:(cs + 1) * S5_COLS]

    ns2 = S5_SLICE_STATE
    for cs in range(S5_NSLICE):
        cols = slice(cs * S5_COLS, (cs + 1) * S5_COLS)
        uf = us_ref[cs]
        bu_ref[...] = jnp.dot(uf.astype(BF16), bq_ref[cs], preferred_element_type=F32)
        a_re = jnp.broadcast_to(are_ref[cs], (bsz, ns2))
        a_im = jnp.broadcast_to(aim_ref[cs], (bsz, ns2))

        def step(t, carry):
            h_re, h_im = carry
            rows = pl.ds(pl.multiple_of(t * bsz, bsz), bsz)
            n_re = a_re * h_re - a_im * h_im + bu_ref[rows, 0:ns2]
            n_im = a_re * h_im + a_im * h_re + bu_ref[rows, ns2:2 * ns2]
            bu_ref[rows, 0:ns2] = n_re
            bu_ref[rows, ns2:2 * ns2] = n_im
            return n_re, n_im

        h_re, h_im = lax.fori_loop(0, tt, step, (st_ref[cs, 0], st_ref[cs, 1]), unroll=8)
        st_ref[cs, 0] = h_re
        st_ref[cs, 1] = h_im
        y = jnp.dot(bu_ref[...].astype(BF16), cq_ref[cs], preferred_element_type=F32)
        ys_ref[cs] = y + d_ref[:, cols] * uf

    for b in range(bsz):
        for cs in range(S5_NSLICE):
            lo = b * S5_WIDTH + cs * S5_COLS
            y_ref[:, lo:lo + S5_COLS] = ys_ref[cs, pl.ds(b, tt, stride=bsz), :]


def s5_scan(proj_sb, mats, d_skip, bsz, s):
    bq, cq, are, aim = mats
    tt = S5_TT
    rows = tt * bsz
    nb = N_PROJ_BLK
    kern = functools.partial(_s5_kernel, bsz=bsz, tt=tt)
    u_specs = [pl.BlockSpec((tt, PROJ_BLK), lambda i, b=b: (i, b * nb + 6)) for b in range(bsz)]
    return pl.pallas_call(
        kern,
        grid=(s // tt,),
        in_specs=u_specs + [
                  pl.BlockSpec(bq.shape, lambda i: (0, 0, 0)),
                  pl.BlockSpec(cq.shape, lambda i: (0, 0, 0)),
                  pl.BlockSpec(are.shape, lambda i: (0, 0, 0)),
                  pl.BlockSpec(aim.shape, lambda i: (0, 0, 0)),
                  pl.BlockSpec((1, S5_WIDTH), lambda i: (0, 0))],
        out_specs=pl.BlockSpec((tt, bsz * S5_WIDTH), lambda i: (i, 0)),
        out_shape=jax.ShapeDtypeStruct((s, bsz * S5_WIDTH), F32),
        scratch_shapes=[pltpu.VMEM((S5_NSLICE, rows, S5_COLS), F32),
                        pltpu.VMEM((S5_NSLICE, rows, S5_COLS), F32),
                        pltpu.VMEM((rows, 2 * S5_SLICE_STATE), F32),
                        pltpu.VMEM((S5_NSLICE, 2, bsz, S5_SLICE_STATE), F32)],
        compiler_params=_cparams(("arbitrary",)),
        name="s5_scan",
    )(*([proj_sb] * bsz), bq, cq, are, aim, d_skip.reshape(1, S5_WIDTH))


def _gelu_tanh(x):
    c = math.sqrt(2.0 / math.pi)
    return 0.5 * x * (1.0 + jnp.tanh(c * (x + 0.044715 * (x * x * x))))


def _route(logits):
    lane = lax.broadcasted_iota(jnp.int32, logits.shape, 1)
    neg = jnp.float32(-jnp.inf)
    big = jnp.int32(1 << 20)
    is_g = lane < MOE_GROUPS
    lg = jnp.where(is_g, logits, neg)
    mg = jnp.max(lg, axis=-1, keepdims=True)
    sg = jnp.sum(jnp.where(is_g, jnp.exp(lg - mg), 0.0), axis=-1, keepdims=True)
    g_top = 1.0 / sg
    g_idx = jnp.min(jnp.where(lg == mg, lane, big), axis=-1, keepdims=True)
    lo = MOE_GROUPS + g_idx * EXPERTS_PER_GROUP
    in_grp = (lane >= lo) & (lane < lo + EXPERTS_PER_GROUP)
    le = jnp.where(in_grp, logits, neg)
    m1 = jnp.max(le, axis=-1, keepdims=True)
    se = jnp.sum(jnp.where(in_grp, jnp.exp(le - m1), 0.0), axis=-1, keepdims=True)
    i1 = jnp.min(jnp.where(le == m1, lane, big), axis=-1, keepdims=True)
    le2 = jnp.where(lane == i1, neg, le)
    m2 = jnp.max(le2, axis=-1, keepdims=True)
    i2 = jnp.min(jnp.where(le2 == m2, lane, big), axis=-1, keepdims=True)
    p1 = 1.0 / se
    p2 = jnp.exp(m2 - m1) / se
    tot = p1 + p2
    w1 = g_top * (p1 / tot)
    w2 = g_top * (p2 / tot)
    return lane, i1, i2, w1, w2


def _mix_kernel(ro_ref, y_ref, gr0_ref, gr1_ref, gs0_ref, gs1_ref, h_ref,
                wglu_ref, wbr_ref, wbs_ref, wout_ref, g_ref, b_ref, wr_ref, br_ref, tri_ref,
                h1_ref, route_ref, cnt_ref, run_ref):
    @pl.when((pl.program_id(0) == 0) & (pl.program_id(1) == 0))
    def _():
        run_ref[...] = jnp.zeros_like(run_ref)

    z = _gelu_tanh(y_ref[...])
    zg = jnp.dot(z.astype(BF16), wglu_ref[...], preferred_element_type=F32)
    zz = (z * jax.nn.sigmoid(zg)).astype(BF16)
    s5b = jnp.dot(zz, wbs_ref[...], preferred_element_type=F32)
    rb = jnp.dot(ro_ref[...], wbr_ref[...], preferred_element_type=F32)
    gr = jnp.concatenate([gr0_ref[...], gr1_ref[...]], axis=-1).astype(F32)
    gs = jnp.concatenate([gs0_ref[...], gs1_ref[...]], axis=-1).astype(F32)
    merged = jax.nn.sigmoid(gr) * rb + jax.nn.sigmoid(gs) * s5b
    mix = jnp.dot(merged.astype(BF16), wout_ref[...], preferred_element_type=F32)
    h1 = _layer_norm(DN_ALPHA * h_ref[...] + mix, g_ref[...], b_ref[...])
    h1_ref[...] = h1
    logits = jnp.dot(h1, wr_ref[...], preferred_element_type=F32,
                     precision=lax.Precision.HIGHEST) + br_ref[...]
    lane, i1, i2, w1, w2 = _route(logits)
    oh1 = lane == i1
    oh2 = lane == i2
    oh = jnp.where(oh1 | oh2, 1.0, 0.0)
    before = run_ref[0:1, :] + jnp.dot(tri_ref[...], oh.astype(BF16), preferred_element_type=F32)
    r1 = jnp.sum(jnp.where(oh1, before, 0.0), axis=-1, keepdims=True)
    r2 = jnp.sum(jnp.where(oh2, before, 0.0), axis=-1, keepdims=True)
    cnt = run_ref[0:1, :] + jnp.sum(oh, axis=0, keepdims=True)
    run_ref[...] = jnp.broadcast_to(cnt, run_ref.shape)
    cnt_ref[...] = jnp.broadcast_to(cnt, cnt_ref.shape)
    vals = (i1.astype(F32) - MOE_GROUPS, i2.astype(F32) - MOE_GROUPS, w1, w2, r1, r2)
    route = jnp.zeros(logits.shape, F32)
    for k, v in enumerate(vals):
        route = jnp.where(lane == k, v, route)
    route_ref[...] = route


def mix_out(ro, y, proj, h, wglu, wbr, wbs, wout, g, b, wr, br, bsz, ts=512):
    s = h.shape[0]
    d = D_MODEL
    nb = N_PROJ_BLK

    def const(a):
        return pl.BlockSpec(a.shape, lambda bi, i: (0,) * a.ndim)

    def pspec(col):
        return pl.BlockSpec((ts, PROJ_BLK), lambda bi, i, col=col: (i, bi * nb + col))

    def tok(w):
        return pl.BlockSpec((ts, w), lambda bi, i: (i, bi))

    g2, b2 = g.reshape(1, d), b.reshape(1, d)
    idx = jnp.arange(ts)
    tri = (idx[None, :] < idx[:, None]).astype(BF16)
    return pl.pallas_call(
        _mix_kernel,
        grid=(bsz, s // ts),
        in_specs=[tok(RET_V), tok(S5_WIDTH), pspec(7), pspec(8), pspec(9), pspec(10), tok(d),
                  const(wglu), const(wbr), const(wbs), const(wout), const(g2), const(b2),
                  const(wr), const(br), const(tri)],
        out_specs=[tok(d), tok(ROUTE_LANES), pl.BlockSpec((8, ROUTE_LANES), lambda bi, i: (0, 0))],
        out_shape=[jax.ShapeDtypeStruct((s, bsz * d), F32),
                   jax.ShapeDtypeStruct((s, bsz * ROUTE_LANES), F32),
                   jax.ShapeDtypeStruct((8, ROUTE_LANES), F32)],
        scratch_shapes=[pltpu.VMEM((8, ROUTE_LANES), F32)],
        compiler_params=_cparams(("arbitrary", "arbitrary")),
        name="mix_out",
    )(ro, y, proj, proj, proj, proj, h, wglu, wbr, wbs, wout, g2, b2, wr, br, tri)


def moe_rows(t):
    return TOP_K * t + (N_EXPERTS + 1) * MOE_TM


def moe_plan(route, cnt, bsz, s):
    t = bsz * s
    rec = route.reshape(s, bsz, ROUTE_LANES)[:, :, :6]
    rec = jnp.transpose(rec, (1, 0, 2)).reshape(t, 6)
    e1 = rec[:, 0].astype(jnp.int32)
    e2 = rec[:, 1].astype(jnp.int32)
    r1 = rec[:, 4].astype(jnp.int32)
    r2 = rec[:, 5].astype(jnp.int32)
    counts = cnt[0, MOE_GROUPS:MOE_GROUPS + N_EXPERTS].astype(jnp.int32)
    padded = ((counts + MOE_TM - 1) // MOE_TM) * MOE_TM
    ends = jnp.cumsum(padded)
    off = ends - padded
    d1 = off[e1] + r1
    d2 = off[e2] + r2
    pad_start = off + counts
    n_tiles = moe_rows(t) // MOE_TM
    n_used = (ends[-1:] // MOE_TM).astype(jnp.int32)
    tile_start = jnp.arange(n_tiles, dtype=jnp.int32) * MOE_TM
    tile_expert = jnp.sum((ends[None, :] <= tile_start[:, None]).astype(jnp.int32), axis=1)
    tile_expert = jnp.minimum(tile_expert, N_EXPERTS - 1)
    return d1, d2, pad_start, tile_expert, n_used


def _to_token_tiles(ref, x):
    n = x.shape[0]
    for j in range(ROW_TILE):
        ref[pl.ds(j, n, stride=ROW_TILE), :] = x[:, j * LANES:(j + 1) * LANES]


def _from_token_tiles(ref, n):
    return jnp.concatenate([ref[pl.ds(j, n, stride=ROW_TILE), :] for j in range(ROW_TILE)], axis=-1)


def _dispatch_kernel(d1_ref, d2_ref, pad_ref, nu_ref, x_ref, xs_ref, xt_ref, zero_ref, sem, *, ts):
    step = pl.program_id(0) * pl.num_programs(1) + pl.program_id(1)
    tile_rows = MOE_TM * ROW_TILE

    @pl.when(step == 0)
    def _():
        zero_ref[...] = jnp.zeros_like(zero_ref)

        def zero_copy(start):
            start = pl.multiple_of(start * ROW_TILE, ROW_TILE)
            return pltpu.make_async_copy(zero_ref, xs_ref.at[pl.ds(start, tile_rows)], sem)

        for e in range(N_EXPERTS):
            zero_copy(pad_ref[e]).start()
        for e in range(N_EXPERTS):
            zero_copy(pad_ref[e]).wait()

        def zero_tile(r, c):
            cp = zero_copy(r * MOE_TM)
            cp.start()
            cp.wait()
            return c

        lax.fori_loop(nu_ref[0], xs_ref.shape[0] // tile_rows, zero_tile, 0)

    _to_token_tiles(xt_ref, x_ref[...])
    base = step * ts

    def row_copy(i, dst):
        src = xt_ref.at[pl.ds(pl.multiple_of(i * ROW_TILE, ROW_TILE), ROW_TILE)]
        return pltpu.make_async_copy(src, xs_ref.at[pl.ds(pl.multiple_of(dst * ROW_TILE, ROW_TILE), ROW_TILE)], sem)

    def issue(i, c):
        row_copy(i, d1_ref[base + i]).start()
        row_copy(i, d2_ref[base + i]).start()
        return c

    lax.fori_loop(0, ts, issue, 0, unroll=8)

    def drain(i, c):
        row_copy(i, d1_ref[base + i]).wait()
        row_copy(i, d2_ref[base + i]).wait()
        return c

    lax.fori_loop(0, ts, drain, 0, unroll=8)


def moe_dispatch(h1, d1, d2, pad_start, n_used, bsz, ts=512):
    s = h1.shape[0]
    d = D_MODEL
    kern = functools.partial(_dispatch_kernel, ts=ts)
    grid_spec = pltpu.PrefetchScalarGridSpec(
        num_scalar_prefetch=4,
        grid=(bsz, s // ts),
        in_specs=[pl.BlockSpec((ts, d), lambda bi, i, *_: (i, bi))],
        out_specs=pl.BlockSpec(memory_space=pl.ANY),
        scratch_shapes=[pltpu.VMEM((ts * ROW_TILE, LANES), F32),
                        pltpu.VMEM((MOE_TM * ROW_TILE, LANES), F32), pltpu.SemaphoreType.DMA(())],
    )
    return pl.pallas_call(
        kern,
        grid_spec=grid_spec,
        out_shape=jax.ShapeDtypeStruct((moe_rows(bsz * s) * ROW_TILE, LANES), F32),
        compiler_params=_cparams(("arbitrary", "arbitrary")),
        name="moe_dispatch",
    )(d1, d2, pad_start, n_used, h1)


def _experts_kernel(te_ref, nu_ref, xs_ref, wg_ref, wu_ref, wd_ref, ys_ref, wgb_ref, wub_ref, wdb_ref):
    r = pl.program_id(0)

    @pl.when(r < nu_ref[0])
    def _():
        changed = (r == 0) | (te_ref[r] != te_ref[jnp.maximum(r - 1, 0)])

        @pl.when(changed)
        def _():
            wgb_ref[...] = wg_ref[...].astype(BF16)
            wub_ref[...] = wu_ref[...].astype(BF16)
            wdb_ref[...] = wd_ref[...].astype(BF16)

        x = _from_token_tiles(xs_ref, MOE_TM).astype(BF16)
        gate = jnp.dot(x, wgb_ref[...], preferred_element_type=F32)
        up = jnp.dot(x, wub_ref[...], preferred_element_type=F32)
        act = (gate * jax.nn.sigmoid(gate) * up).astype(BF16)
        _to_token_tiles(ys_ref, jnp.dot(act, wdb_ref[...], preferred_element_type=F32))

    @pl.when(r >= nu_ref[0])
    def _():
        ys_ref[...] = jnp.zeros_like(ys_ref)


def moe_experts(xs, tile_expert, n_used, wg, wu, wd, layer):
    d, f = wg.shape[-2:]
    rows = xs.shape[0] // ROW_TILE
    blk = MOE_TM * ROW_TILE

    def tile(r, te, nu):
        return jnp.minimum(r, nu[0] - 1)

    def wspec(a, b):
        return pl.BlockSpec((None, None, a, b), lambda r, te, nu: (layer, te[tile(r, te, nu)], 0, 0))

    grid_spec = pltpu.PrefetchScalarGridSpec(
        num_scalar_prefetch=2,
        grid=(rows // MOE_TM,),
        in_specs=[pl.BlockSpec((blk, LANES), lambda r, te, nu: (tile(r, te, nu), 0)),
                  wspec(d, f), wspec(d, f), wspec(f, d)],
        out_specs=pl.BlockSpec((blk, LANES), lambda r, te, nu: (r, 0)),
        scratch_shapes=[pltpu.VMEM((d, f), BF16), pltpu.VMEM((d, f), BF16), pltpu.VMEM((f, d), BF16)],
    )
    return pl.pallas_call(
        _experts_kernel,
        grid_spec=grid_spec,
        out_shape=jax.ShapeDtypeStruct(xs.shape, F32),
        compiler_params=_cparams(("arbitrary",)),
        name="moe_experts",
    )(tile_expert, n_used, xs, wg, wu, wd)


def _combine_kernel(d1_ref, d2_ref, ys_ref, route_ref, h_ref, g_ref, b_ref, o_ref, ob_ref,
                    buf1_ref, buf2_ref, sem, *, ts):
    nsteps = pl.num_programs(0) * pl.num_programs(1)
    step = pl.program_id(0) * pl.num_programs(1) + pl.program_id(1)

    def row_copies(blk, slot, i):
        n = blk * ts + i
        dst = pl.ds(pl.multiple_of(i * ROW_TILE, ROW_TILE), ROW_TILE)

        def src(row):
            return ys_ref.at[pl.ds(pl.multiple_of(row * ROW_TILE, ROW_TILE), ROW_TILE)]

        return (pltpu.make_async_copy(src(d1_ref[n]), buf1_ref.at[slot, dst], sem.at[slot]),
                pltpu.make_async_copy(src(d2_ref[n]), buf2_ref.at[slot, dst], sem.at[slot]))

    def issue(blk, slot):
        def body(i, c):
            c1, c2 = row_copies(blk, slot, i)
            c1.start()
            c2.start()
            return c
        lax.fori_loop(0, ts, body, 0, unroll=8)

    @pl.when(step == 0)
    def _():
        issue(0, 0)

    @pl.when(step + 1 < nsteps)
    def _():
        issue(step + 1, (step + 1) % 2)

    slot = step % 2

    def drain(i, c):
        c1, c2 = row_copies(step, slot, i)
        c1.wait()
        c2.wait()
        return c

    lax.fori_loop(0, ts, drain, 0, unroll=8)

    route = route_ref[...]
    w1 = route[:, 2:3]
    w2 = route[:, 3:4]
    ffn = w1 * _from_token_tiles(buf1_ref.at[slot], ts) + w2 * _from_token_tiles(buf2_ref.at[slot], ts)
    h2 = _layer_norm(DN_ALPHA * h_ref[...] + ffn, g_ref[...], b_ref[...])
    o_ref[...] = h2
    ob_ref[...] = h2.astype(BF16)


def moe_combine(ys, d1, d2, route, h1, g, b, bsz, last, ts=256):
    s = h1.shape[0]
    d = D_MODEL
    g2, b2 = g.reshape(1, d), b.reshape(1, d)
    if last:
        o_spec = pl.BlockSpec((None, ts, d), lambda bi, i, *_: (bi, i, 0))
        o_shape = jax.ShapeDtypeStruct((bsz, s, d), F32)
    else:
        o_spec = pl.BlockSpec((ts, d), lambda bi, i, *_: (i, bi))
        o_shape = jax.ShapeDtypeStruct((s, bsz * d), F32)
    kern = functools.partial(_combine_kernel, ts=ts)
    grid_spec = pltpu.PrefetchScalarGridSpec(
        num_scalar_prefetch=2,
        grid=(bsz, s // ts),
        in_specs=[pl.BlockSpec(memory_space=pl.ANY),
                  pl.BlockSpec((ts, ROUTE_LANES), lambda bi, i, *_: (i, bi)),
                  pl.BlockSpec((ts, d), lambda bi, i, *_: (i, bi)),
                  pl.BlockSpec((1, d), lambda bi, i, *_: (0, 0)),
                  pl.BlockSpec((1, d), lambda bi, i, *_: (0, 0))],
        out_specs=[o_spec, pl.BlockSpec((ts, d), lambda bi, i, *_: (i, bi))],
        scratch_shapes=[pltpu.VMEM((2, ts * ROW_TILE, LANES), F32), pltpu.VMEM((2, ts * ROW_TILE, LANES), F32),
                        pltpu.SemaphoreType.DMA((2,))],
    )
    return pl.pallas_call(
        kern,
        grid_spec=grid_spec,
        out_shape=[o_shape, jax.ShapeDtypeStruct((s, bsz * d), BF16)],
        compiler_params=_cparams(("arbitrary", "arbitrary")),
        name="moe_combine",
    )(d1, d2, ys, route, h1, g2, b2)


def kernel(x, ln_in_g, ln_in_b, w_in, s5_lambda_re, s5_lambda_im, s5_log_step, s5_b_re, s5_b_im,
           s5_c_re, s5_c_im, s5_d, w_glu, w_branch_ret, w_branch_s5, w_out, ln_mix_g, ln_mix_b,
           w_router_group, b_router_group, w_router_expert, b_router_expert, w_exp_gate, w_exp_up,
           w_exp_down, ln_ffn_g, ln_ffn_b):
    bsz, s, d = x.shape
    t = bsz * s
    depth = w_in.shape[0]
    tabs = retention_tables(s)
    h_sb, hb_sb = ln_in(x, ln_in_g, ln_in_b)
    out = None
    for l in range(depth):
        proj = in_proj(hb_sb, w_in[l].astype(BF16), bsz)
        ro = retention(proj, tabs, bsz, s)
        a_re, a_im, bb_re, bb_im = s5_params(s5_lambda_re[l], s5_lambda_im[l], s5_log_step[l],
                                             s5_b_re[l], s5_b_im[l])
        mats = s5_matrices(a_re, a_im, bb_re, bb_im, s5_c_re[l], s5_c_im[l])
        y = s5_scan(proj, mats, s5_d[l], bsz, s)
        pad = ROUTE_LANES - MOE_GROUPS - N_EXPERTS
        wr = jnp.concatenate([w_router_group[l], w_router_expert[l], jnp.zeros((d, pad), F32)], axis=1)
        br = jnp.concatenate([b_router_group[l], b_router_expert[l], jnp.zeros((pad,), F32)]).reshape(1, -1)
        h1, route, cnt = mix_out(ro, y, proj, h_sb,
                                 w_glu[l].astype(BF16), w_branch_ret[l].astype(BF16),
                                 w_branch_s5[l].astype(BF16), w_out[l].astype(BF16),
                                 ln_mix_g[l], ln_mix_b[l], wr, br, bsz)
        last = l == depth - 1
        d1, d2, pad_start, tile_expert, n_used = moe_plan(route, cnt, bsz, s)
        xs = moe_dispatch(h1, d1, d2, pad_start, n_used, bsz)
        ys = moe_experts(xs, tile_expert, n_used, w_exp_gate, w_exp_up, w_exp_down, l)
        h2, h2b = moe_combine(ys, d1, d2, route, h1, ln_ffn_g[l], ln_ffn_b[l], bsz, last)
        if last:
            out = h2
        else:
            h_sb, hb_sb = h2, h2b
    return out
```

```python
import functools
import math

import jax
import jax.numpy as jnp
from jax import lax
from jax.experimental import pallas as pl
from jax.experimental.pallas import tpu as pltpu

D_MODEL = 1024
CHUNK = 64
RET_HEADS = 8
RET_QK = 512
RET_V = 1024
RET_QK_DIM = 64
RET_V_DIM = 128
ROPE_BASE = 10000.0
S5_WIDTH = 512
S5_GROUP_CH = 16
S5_GROUPS = 32
S5_STATE = 64
MOE_GROUPS = 4
EXPERTS_PER_GROUP = 8
N_EXPERTS = 32
EXPERT_FF = 256
LN_EPS = 1e-5
HEAD_NORM_EPS = 1e-6
DEPTH = 2
DN_ALPHA = (2 * DEPTH) ** 0.25
IN_WIDTH = 2 * RET_QK + 2 * RET_V + S5_WIDTH + 2 * D_MODEL
PROJ_BLK = 512
N_PROJ_BLK = IN_WIDTH // PROJ_BLK

RET_SUPER = 256
S5_TT = 64
S5_COLS = 128
S5_NSLICE = S5_WIDTH // S5_COLS
S5_SLICE_STATE = (S5_COLS // S5_GROUP_CH) * S5_STATE
ROUTE_LANES = 128
TOP_K = 2
MOE_TM = 256
RANK_RANGE = 1 << 16
LANES = 128
ROW_TILE = D_MODEL // LANES
VMEM_LIMIT = 56 * 1024 * 1024

F32 = jnp.float32
BF16 = jnp.bfloat16


def _cparams(sem):
    return pltpu.CompilerParams(dimension_semantics=sem, vmem_limit_bytes=VMEM_LIMIT)


def _layer_norm(x, g, b):
    mu = jnp.mean(x, axis=-1, keepdims=True)
    xc = x - mu
    var = jnp.mean(xc * xc, axis=-1, keepdims=True)
    return xc * lax.rsqrt(var + LN_EPS) * g + b


def _ln_in_kernel(x_ref, g_ref, b_ref, h_ref, hb_ref):
    h = _layer_norm(x_ref[...], g_ref[...], b_ref[...])
    h_ref[...] = h
    hb_ref[...] = h.astype(BF16)


def ln_in(x, g, b, ts=512):
    bsz, s, d = x.shape
    return pl.pallas_call(
        _ln_in_kernel,
        grid=(bsz, s // ts),
        in_specs=[pl.BlockSpec((None, ts, d), lambda bi, si: (bi, si, 0)),
                  pl.BlockSpec((1, d), lambda bi, si: (0, 0)),
                  pl.BlockSpec((1, d), lambda bi, si: (0, 0))],
        out_specs=[pl.BlockSpec((ts, d), lambda bi, si: (si, bi)),
                   pl.BlockSpec((ts, d), lambda bi, si: (si, bi))],
        out_shape=[jax.ShapeDtypeStruct((s, bsz * d), F32),
                   jax.ShapeDtypeStruct((s, bsz * d), BF16)],
        compiler_params=_cparams(("parallel", "parallel")),
        name="ln_in",
    )(x, g.reshape(1, d), b.reshape(1, d))


def _inproj_kernel(h_ref, w_ref, o_ref):
    w = w_ref[...].astype(BF16)
    o_ref[...] = jnp.dot(h_ref[...], w, preferred_element_type=F32).astype(BF16)


def in_proj(hb_sb, w, layer, bsz, ts=2048):
    s = hb_sb.shape[0]
    d, n = w.shape[-2:]
    nb = n // PROJ_BLK
    return pl.pallas_call(
        _inproj_kernel,
        grid=(bsz, s // ts, nb),
        in_specs=[pl.BlockSpec((ts, d), lambda b, i, j: (i, b)),
                  pl.BlockSpec((None, d, PROJ_BLK), lambda b, i, j: (layer, 0, j))],
        out_specs=pl.BlockSpec((ts, PROJ_BLK), lambda b, i, j: (i, b * nb + j)),
        out_shape=jax.ShapeDtypeStruct((s, bsz * n), BF16),
        compiler_params=_cparams(("parallel", "parallel", "arbitrary")),
        name="in_proj",
    )(hb_sb, w)


def _swap_halves(x):
    lane = lax.broadcasted_iota(jnp.int32, x.shape, 1)
    first = (lane % RET_QK_DIM) < (RET_QK_DIM // 2)
    n = x.shape[1]
    return jnp.where(first, pltpu.roll(x, n - RET_QK_DIM // 2, 1), pltpu.roll(x, RET_QK_DIM // 2, 1))


def _retention_kernel(q_ref, k_ref, v0_ref, v1_ref, g0_ref, g1_ref, cos_ref, sin_ref,
                      qd_ref, kd_ref, mask_ref, cd_ref, o_ref, state_ref):
    @pl.when(pl.program_id(1) == 0)
    def _():
        state_ref[...] = jnp.zeros_like(state_ref)

    cos = cos_ref[...]
    sin = sin_ref[...]
    q = q_ref[...].astype(F32)
    k = k_ref[...].astype(F32)
    q = q * cos + _swap_halves(q) * sin
    k = (k * cos + _swap_halves(k) * sin) * (RET_QK_DIM ** -0.5)
    qb = q.astype(BF16)
    kb = k.astype(BF16)
    qdb = (q * qd_ref[...]).astype(BF16)
    kdb = (k * kd_ref[...]).astype(BF16)
    for hd in range(RET_HEADS):
        qs = slice(hd * RET_QK_DIM, (hd + 1) * RET_QK_DIM)
        half, off = divmod(hd * RET_V_DIM, PROJ_BLK)
        vs = slice(off, off + RET_V_DIM)
        vh = (v0_ref, v1_ref)[half][:, vs]
        gh = (g0_ref, g1_ref)[half][:, vs].astype(F32)
        sc = lax.dot_general(qb[:, qs], kb[:, qs], (((1,), (1,)), ((), ())),
                             preferred_element_type=F32)
        sc = (sc * mask_ref[hd]).astype(BF16)
        st = state_ref[hd]
        o = jnp.dot(sc, vh, preferred_element_type=F32)
        o = o + jnp.dot(qdb[:, qs], st.astype(BF16), preferred_element_type=F32)
        kv = lax.dot_general(kdb[:, qs], vh, (((0,), (0,)), ((), ())),
                             preferred_element_type=F32)
        state_ref[hd] = st * cd_ref[hd] + kv
        mu = jnp.mean(o, axis=-1, keepdims=True)
        oc = o - mu
        var = jnp.mean(oc * oc, axis=-1, keepdims=True)
        on = oc * lax.rsqrt(var + HEAD_NORM_EPS)
        o_ref[:, hd * RET_V_DIM:(hd + 1) * RET_V_DIM] = (gh * jax.nn.sigmoid(gh) * on).astype(BF16)


def retention(proj_sb, tabs, bsz, s):
    cos_t, sin_t, qd_t, kd_t, mask, cd = tabs
    L = RET_SUPER
    nb = N_PROJ_BLK

    def pspec(col):
        return pl.BlockSpec((L, PROJ_BLK), lambda b, i, col=col: (i, b * nb + col))

    full2 = pl.BlockSpec((L, RET_QK), lambda b, i: (0, 0))
    return pl.pallas_call(
        _retention_kernel,
        grid=(bsz, s // L),
        in_specs=[pspec(0), pspec(1), pspec(2), pspec(3), pspec(4), pspec(5),
                  pl.BlockSpec((L, RET_QK), lambda b, i: (i, 0)),
                  pl.BlockSpec((L, RET_QK), lambda b, i: (i, 0)),
                  full2, full2,
                  pl.BlockSpec((RET_HEADS, L, L), lambda b, i: (0, 0, 0)),
                  pl.BlockSpec((RET_HEADS, 1, RET_V_DIM), lambda b, i: (0, 0, 0))],
        out_specs=pl.BlockSpec((L, RET_V), lambda b, i: (i, b)),
        out_shape=jax.ShapeDtypeStruct((s, bsz * RET_V), BF16),
        scratch_shapes=[pltpu.VMEM((RET_HEADS, RET_QK_DIM, RET_V_DIM), F32)],
        compiler_params=_cparams(("parallel", "arbitrary")),
        name="retention",
    )(proj_sb, proj_sb, proj_sb, proj_sb, proj_sb, proj_sb, cos_t, sin_t, qd_t, kd_t, mask, cd)


def retention_tables(s):
    L = RET_SUPER
    half = RET_QK_DIM // 2
    inv_freq = ROPE_BASE ** (-jnp.arange(half, dtype=F32) / half)
    ang = jnp.arange(s, dtype=F32)[:, None] * inv_freq[None, :]
    cos, sin = jnp.cos(ang), jnp.sin(ang)
    cos_t = jnp.tile(jnp.concatenate([cos, cos], -1), (1, RET_HEADS))
    sin_t = jnp.tile(jnp.concatenate([-sin, sin], -1), (1, RET_HEADS))
    log_gamma = jnp.log1p(-(2.0 ** (-5.0 - jnp.arange(RET_HEADS, dtype=F32))))
    pos = jnp.arange(L, dtype=F32)
    qd = jnp.exp(log_gamma[None, :] * (pos + 1.0)[:, None])
    kd = jnp.exp(log_gamma[None, :] * (L - 1.0 - pos)[:, None])
    qd_t = jnp.repeat(qd, RET_QK_DIM, axis=1)
    kd_t = jnp.repeat(kd, RET_QK_DIM, axis=1)
    chunk_id = jnp.arange(L) // CHUNK
    visible = (chunk_id[None, :] <= chunk_id[:, None]).astype(F32)
    mask = jnp.exp(log_gamma[:, None, None] * jnp.abs(pos[:, None] - pos[None, :])) * visible[None]
    cd = jnp.broadcast_to(jnp.exp(log_gamma * L)[:, None, None], (RET_HEADS, 1, RET_V_DIM))
    return cos_t, sin_t, qd_t, kd_t, mask, cd


def _s5_param_kernel(lre_ref, lim_ref, ls_ref, bre_ref, bim_ref, are_ref, aim_ref, bbre_ref, bbim_ref):
    lam_re = jnp.minimum(lre_ref[...], -1e-4)
    lam_im = lim_ref[...]
    step = jnp.exp(ls_ref[...])
    mag = jnp.exp(lam_re * step)
    ang = lam_im * step
    ab_re = mag * jnp.cos(ang)
    ab_im = mag * jnp.sin(ang)
    den = lam_re * lam_re + lam_im * lam_im
    n_re = ab_re - 1.0
    zc_re = (n_re * lam_re + ab_im * lam_im) / den
    zc_im = (ab_im * lam_re - n_re * lam_im) / den
    are_ref[...] = ab_re
    aim_ref[...] = ab_im
    b_re = bre_ref[...]
    b_im = bim_ref[...]
    bbre_ref[...] = zc_re * b_re - zc_im * b_im
    bbim_ref[...] = zc_re * b_im + zc_im * b_re


def s5_params(lam_re, lam_im, log_step, b_re, b_im):
    g, n = lam_re.shape
    c = b_re.shape[-1]
    outs = pl.pallas_call(
        _s5_param_kernel,
        out_shape=[jax.ShapeDtypeStruct((g, 1, n), F32), jax.ShapeDtypeStruct((g, 1, n), F32),
                   jax.ShapeDtypeStruct((g, c, n), F32), jax.ShapeDtypeStruct((g, c, n), F32)],
        name="s5_params",
    )(lam_re.reshape(g, 1, n), lam_im.reshape(g, 1, n), log_step.reshape(g, 1, 1),
      jnp.swapaxes(b_re, 1, 2), jnp.swapaxes(b_im, 1, 2))
    return outs


def _block_diag(x):
    ns, gl, r, c = x.shape
    eye = jnp.eye(gl, dtype=x.dtype)
    return jnp.einsum('sgrc,gh->sgrhc', x, eye).reshape(ns, gl * r, gl * c)


def s5_matrices(a_re, a_im, bb_re, bb_im, c_re, c_im):
    gl = S5_COLS // S5_GROUP_CH
    ns = S5_NSLICE
    bre = _block_diag(bb_re.reshape(ns, gl, S5_GROUP_CH, S5_STATE))
    bim = _block_diag(bb_im.reshape(ns, gl, S5_GROUP_CH, S5_STATE))
    bq = jnp.concatenate([bre, bim], axis=-1).astype(BF16)
    cre = _block_diag(jnp.swapaxes(c_re, 1, 2).reshape(ns, gl, S5_STATE, S5_GROUP_CH))
    cim = _block_diag(jnp.swapaxes(c_im, 1, 2).reshape(ns, gl, S5_STATE, S5_GROUP_CH))
    cq = jnp.concatenate([cre, -cim], axis=1).astype(BF16)
    are = a_re.reshape(ns, 1, S5_SLICE_STATE)
    aim = a_im.reshape(ns, 1, S5_SLICE_STATE)
    return bq, cq, are, aim


def _s5_kernel(*refs, bsz, tt):
    u_refs = refs[:bsz]
    bq_ref, cq_ref, are_ref, aim_ref, d_ref, y_ref, us_ref, ys_ref, bu_ref, st_ref = refs[bsz:]

    @pl.when(pl.program_id(0) == 0)
    def _():
        st_ref[...] = jnp.zeros_like(st_ref)

    for b in range(bsz):
        ub = u_refs[b][...].astype(F32)
        for cs in range(S5_NSLICE):
            us_ref[cs, pl.ds(b, tt, stride=bsz), :] = ub[:, cs * S5_COLS:(cs + 1) * S5_COLS]

    ns2 = S5_SLICE_STATE
    for cs in range(S5_NSLICE):
        cols = slice(cs * S5_COLS, (cs + 1) * S5_COLS)
        uf = us_ref[cs]
        bu_ref[...] = jnp.dot(uf.astype(BF16), bq_ref[cs], preferred_element_type=F32)
        a_re = jnp.broadcast_to(are_ref[cs], (bsz, ns2))
        a_im = jnp.broadcast_to(aim_ref[cs], (bsz, ns2))

        def step(t, carry):
            h_re, h_im = carry
            rows = pl.ds(pl.multiple_of(t * bsz, bsz), bsz)
            n_re = a_re * h_re - a_im * h_im + bu_ref[rows, 0:ns2]
            n_im = a_re * h_im + a_im * h_re + bu_ref[rows, ns2:2 * ns2]
            bu_ref[rows, 0:ns2] = n_re
            bu_ref[rows, ns2:2 * ns2] = n_im
            return n_re, n_im

        h_re, h_im = lax.fori_loop(0, tt, step, (st_ref[cs, 0], st_ref[cs, 1]), unroll=8)
        st_ref[cs, 0] = h_re
        st_ref[cs, 1] = h_im
        y = jnp.dot(bu_ref[...].astype(BF16), cq_ref[cs], preferred_element_type=F32)
        ys_ref[cs] = y + d_ref[:, cols] * uf

    for b in range(bsz):
        for cs in range(S5_NSLICE):
            lo = b * S5_WIDTH + cs * S5_COLS
            y_ref[:, lo:lo + S5_COLS] = ys_ref[cs, pl.ds(b, tt, stride=bsz), :]


def s5_scan(proj_sb, mats, d_skip, bsz, s):
    bq, cq, are, aim = mats
    tt = S5_TT
    rows = tt * bsz
    nb = N_PROJ_BLK
    kern = functools.partial(_s5_kernel, bsz=bsz, tt=tt)
    u_specs = [pl.BlockSpec((tt, PROJ_BLK), lambda i, b=b: (i, b * nb + 6)) for b in range(bsz)]
    return pl.pallas_call(
        kern,
        grid=(s // tt,),
        in_specs=u_specs + [
                  pl.BlockSpec(bq.shape, lambda i: (0, 0, 0)),
                  pl.BlockSpec(cq.shape, lambda i: (0, 0, 0)),
                  pl.BlockSpec(are.shape, lambda i: (0, 0, 0)),
                  pl.BlockSpec(aim.shape, lambda i: (0, 0, 0)),
                  pl.BlockSpec((1, S5_WIDTH), lambda i: (0, 0))],
        out_specs=pl.BlockSpec((tt, bsz * S5_WIDTH), lambda i: (i, 0)),
        out_shape=jax.ShapeDtypeStruct((s, bsz * S5_WIDTH), F32),
        scratch_shapes=[pltpu.VMEM((S5_NSLICE, rows, S5_COLS), F32),
                        pltpu.VMEM((S5_NSLICE, rows, S5_COLS), F32),
                        pltpu.VMEM((rows, 2 * S5_SLICE_STATE), F32),
                        pltpu.VMEM((S5_NSLICE, 2, bsz, S5_SLICE_STATE), F32)],
        compiler_params=_cparams(("arbitrary",)),
        name="s5_scan",
    )(*([proj_sb] * bsz), bq, cq, are, aim, d_skip.reshape(1, S5_WIDTH))


def _gelu_tanh(x):
    c = math.sqrt(2.0 / math.pi)
    return 0.5 * x * (1.0 + jnp.tanh(c * (x + 0.044715 * (x * x * x))))


def _route(logits):
    lane = lax.broadcasted_iota(jnp.int32, logits.shape, 1)
    neg = jnp.float32(-jnp.inf)
    big = jnp.int32(1 << 20)
    is_g = lane < MOE_GROUPS
    lg = jnp.where(is_g, logits, neg)
    mg = jnp.max(lg, axis=-1, keepdims=True)
    sg = jnp.sum(jnp.where(is_g, jnp.exp(lg - mg), 0.0), axis=-1, keepdims=True)
    g_top = 1.0 / sg
    g_idx = jnp.min(jnp.where(lg == mg, lane, big), axis=-1, keepdims=True)
    lo = MOE_GROUPS + g_idx * EXPERTS_PER_GROUP
    in_grp = (lane >= lo) & (lane < lo + EXPERTS_PER_GROUP)
    le = jnp.where(in_grp, logits, neg)
    m1 = jnp.max(le, axis=-1, keepdims=True)
    se = jnp.sum(jnp.where(in_grp, jnp.exp(le - m1), 0.0), axis=-1, keepdims=True)
    i1 = jnp.min(jnp.where(le == m1, lane, big), axis=-1, keepdims=True)
    le2 = jnp.where(lane == i1, neg, le)
    m2 = jnp.max(le2, axis=-1, keepdims=True)
    i2 = jnp.min(jnp.where(le2 == m2, lane, big), axis=-1, keepdims=True)
    p1 = 1.0 / se
    p2 = jnp.exp(m2 - m1) / se
    tot = p1 + p2
    w1 = g_top * (p1 / tot)
    w2 = g_top * (p2 / tot)
    return lane, i1, i2, w1, w2


def _mix_kernel(ro_ref, y_ref, gr0_ref, gr1_ref, gs0_ref, gs1_ref, h_ref,
                wglu_ref, wbr_ref, wbs_ref, wout_ref, g_ref, b_ref, wr_ref, br_ref, tri_ref,
                h1_ref, route_ref, cnt_ref, run_ref):
    @pl.when((pl.program_id(0) == 0) & (pl.program_id(1) == 0))
    def _():
        run_ref[...] = jnp.zeros_like(run_ref)

    z = _gelu_tanh(y_ref[...])
    zg = jnp.dot(z.astype(BF16), wglu_ref[...], preferred_element_type=F32)
    zz = (z * jax.nn.sigmoid(zg)).astype(BF16)
    s5b = jnp.dot(zz, wbs_ref[...], preferred_element_type=F32)
    rb = jnp.dot(ro_ref[...], wbr_ref[...], preferred_element_type=F32)
    gr = jnp.concatenate([gr0_ref[...], gr1_ref[...]], axis=-1).astype(F32)
    gs = jnp.concatenate([gs0_ref[...], gs1_ref[...]], axis=-1).astype(F32)
    merged = jax.nn.sigmoid(gr) * rb + jax.nn.sigmoid(gs) * s5b
    mix = jnp.dot(merged.astype(BF16), wout_ref[...], preferred_element_type=F32)
    h1 = _layer_norm(DN_ALPHA * h_ref[...] + mix, g_ref[...], b_ref[...])
    h1_ref[...] = h1
    h_hi = h1.astype(BF16)
    h_lo = (h1 - h_hi.astype(F32)).astype(BF16)
    both = jnp.dot(h_hi, wr_ref[...], preferred_element_type=F32)
    logits = (both[:, :ROUTE_LANES] + both[:, ROUTE_LANES:]
              + jnp.dot(h_lo, wr_ref[:, :ROUTE_LANES], preferred_element_type=F32)) + br_ref[...]
    lane, i1, i2, w1, w2 = _route(logits)
    oh1 = lane == i1
    oh2 = lane == i2
    oh = jnp.where(oh1 | oh2, 1.0, 0.0)
    before = run_ref[0:1, :] + jnp.dot(tri_ref[...], oh.astype(BF16), preferred_element_type=F32)
    r1 = jnp.sum(jnp.where(oh1, before, 0.0), axis=-1, keepdims=True)
    r2 = jnp.sum(jnp.where(oh2, before, 0.0), axis=-1, keepdims=True)
    cnt = run_ref[0:1, :] + jnp.sum(oh, axis=0, keepdims=True)
    run_ref[...] = jnp.broadcast_to(cnt, run_ref.shape)
    cnt_ref[...] = jnp.broadcast_to(cnt, cnt_ref.shape)
    vals = (i1.astype(F32) - MOE_GROUPS, i2.astype(F32) - MOE_GROUPS, w1, w2, r1, r2)
    route = jnp.zeros(logits.shape, F32)
    for k, v in enumerate(vals):
        route = jnp.where(lane == k, v, route)
    route_ref[...] = route


def mix_out(ro, y, proj, h, wglu, wbr, wbs, wout, g, b, wr, br, bsz, ts=512):
    s = h.shape[0]
    d = D_MODEL
    nb = N_PROJ_BLK

    def const(a):
        return pl.BlockSpec(a.shape, lambda bi, i: (0,) * a.ndim)

    def pspec(col):
        return pl.BlockSpec((ts, PROJ_BLK), lambda bi, i, col=col: (i, bi * nb + col))

    def tok(w):
        return pl.BlockSpec((ts, w), lambda bi, i: (i, bi))

    g2, b2 = g.reshape(1, d), b.reshape(1, d)
    idx = jnp.arange(ts)
    tri = (idx[None, :] < idx[:, None]).astype(BF16)
    return pl.pallas_call(
        _mix_kernel,
        grid=(bsz, s // ts),
        in_specs=[tok(RET_V), tok(S5_WIDTH), pspec(7), pspec(8), pspec(9), pspec(10), tok(d),
                  const(wglu), const(wbr), const(wbs), const(wout), const(g2), const(b2),
                  const(wr), const(br), const(tri)],
        out_specs=[tok(d), tok(ROUTE_LANES), pl.BlockSpec((8, ROUTE_LANES), lambda bi, i: (0, 0))],
        out_shape=[jax.ShapeDtypeStruct((s, bsz * d), F32),
                   jax.ShapeDtypeStruct((s, bsz * ROUTE_LANES), F32),
                   jax.ShapeDtypeStruct((8, ROUTE_LANES), F32)],
        scratch_shapes=[pltpu.VMEM((8, ROUTE_LANES), F32)],
        compiler_params=_cparams(("arbitrary", "arbitrary")),
        name="mix_out",
    )(ro, y, proj, proj, proj, proj, h, wglu, wbr, wbs, wout, g2, b2, wr, br, tri)


def moe_rows(t):
    return TOP_K * t + (N_EXPERTS + 1) * MOE_TM


def moe_plan(route, cnt, bsz, s):
    t = bsz * s
    rec = route.reshape(s, bsz, ROUTE_LANES)[:, :, :6]
    rec = jnp.transpose(rec, (1, 0, 2)).reshape(t, 6)
    e1 = rec[:, 0].astype(jnp.int32)
    e2 = rec[:, 1].astype(jnp.int32)
    r1 = rec[:, 4].astype(jnp.int32)
    r2 = rec[:, 5].astype(jnp.int32)
    counts = cnt[0, MOE_GROUPS:MOE_GROUPS + N_EXPERTS].astype(jnp.int32)
    padded = ((counts + MOE_TM - 1) // MOE_TM) * MOE_TM
    ends = jnp.cumsum(padded)
    off = ends - padded
    p1 = e1 * RANK_RANGE + r1
    p2 = e2 * RANK_RANGE + r2
    pad_start = off + counts
    n_tiles = moe_rows(t) // MOE_TM
    n_used = (ends[-1:] // MOE_TM).astype(jnp.int32)
    tile_start = jnp.arange(n_tiles, dtype=jnp.int32) * MOE_TM
    tile_expert = jnp.sum((ends[None, :] <= tile_start[:, None]).astype(jnp.int32), axis=1)
    tile_expert = jnp.minimum(tile_expert, N_EXPERTS - 1)
    return p1, p2, off, pad_start, tile_expert, n_used


def _dest_row(p, off_ref):
    return off_ref[p // RANK_RANGE] + p % RANK_RANGE


def _to_token_tiles(ref, x):
    n = x.shape[0]
    for j in range(ROW_TILE):
        ref[pl.ds(j, n, stride=ROW_TILE), :] = x[:, j * LANES:(j + 1) * LANES]


def _from_token_tiles(ref, n):
    return jnp.concatenate([ref[pl.ds(j, n, stride=ROW_TILE), :] for j in range(ROW_TILE)], axis=-1)


def _dispatch_kernel(p1_ref, p2_ref, off_ref, pad_ref, nu_ref, x_ref, xs_ref, xt_ref, zero_ref,
                     sem, zsem, *, ts):
    nsteps = pl.num_programs(0) * pl.num_programs(1)
    step = pl.program_id(0) * pl.num_programs(1) + pl.program_id(1)
    tile_rows = MOE_TM * ROW_TILE

    @pl.when(step == 0)
    def _():
        zero_ref[...] = jnp.zeros_like(zero_ref)

        def zero_copy(start):
            start = pl.multiple_of(start * ROW_TILE, ROW_TILE)
            return pltpu.make_async_copy(zero_ref, xs_ref.at[pl.ds(start, tile_rows)], zsem)

        for e in range(N_EXPERTS):
            zero_copy(pad_ref[e]).start()
        for e in range(N_EXPERTS):
            zero_copy(pad_ref[e]).wait()

        def zero_tile(r, c):
            cp = zero_copy(r * MOE_TM)
            cp.start()
            cp.wait()
            return c

        lax.fori_loop(nu_ref[0], xs_ref.shape[0] // tile_rows, zero_tile, 0)

    slot = step % 2
    _to_token_tiles(xt_ref.at[slot], x_ref[...])

    def row_copies(blk, sl, i):
        n = blk * ts + i
        src = xt_ref.at[sl, pl.ds(pl.multiple_of(i * ROW_TILE, ROW_TILE), ROW_TILE)]

        def dst(p):
            row = _dest_row(p, off_ref)
            return xs_ref.at[pl.ds(pl.multiple_of(row * ROW_TILE, ROW_TILE), ROW_TILE)]

        return (pltpu.make_async_copy(src, dst(p1_ref[n]), sem.at[sl]),
                pltpu.make_async_copy(src, dst(p2_ref[n]), sem.at[sl]))

    def issue(i, c):
        c1, c2 = row_copies(step, slot, i)
        c1.start()
        c2.start()
        return c

    lax.fori_loop(0, ts, issue, 0, unroll=8)

    def drain(blk, sl):
        def body(i, c):
            c1, c2 = row_copies(blk, sl, i)
            c1.wait()
            c2.wait()
            return c
        lax.fori_loop(0, ts, body, 0, unroll=8)

    @pl.when(step > 0)
    def _():
        drain(step - 1, 1 - slot)

    @pl.when(step == nsteps - 1)
    def _():
        drain(step, slot)


def moe_dispatch(h1, p1, p2, off, pad_start, n_used, bsz, ts=512):
    s = h1.shape[0]
    d = D_MODEL
    kern = functools.partial(_dispatch_kernel, ts=ts)
    grid_spec = pltpu.PrefetchScalarGridSpec(
        num_scalar_prefetch=5,
        grid=(bsz, s // ts),
        in_specs=[pl.BlockSpec((ts, d), lambda bi, i, *_: (i, bi))],
        out_specs=pl.BlockSpec(memory_space=pl.ANY),
        scratch_shapes=[pltpu.VMEM((2, ts * ROW_TILE, LANES), F32),
                        pltpu.VMEM((MOE_TM * ROW_TILE, LANES), F32),
                        pltpu.SemaphoreType.DMA((2,)), pltpu.SemaphoreType.DMA(())],
    )
    return pl.pallas_call(
        kern,
        grid_spec=grid_spec,
        out_shape=jax.ShapeDtypeStruct((moe_rows(bsz * s) * ROW_TILE, LANES), F32),
        compiler_params=_cparams(("arbitrary", "arbitrary")),
        name="moe_dispatch",
    )(p1, p2, off, pad_start, n_used, h1)


def _experts_kernel(te_ref, nu_ref, xs_ref, wg_ref, wu_ref, wd_ref, ys_ref, wgb_ref, wub_ref, wdb_ref):
    r = pl.program_id(0)

    @pl.when(r < nu_ref[0])
    def _():
        changed = (r == 0) | (te_ref[r] != te_ref[jnp.maximum(r - 1, 0)])

        @pl.when(changed)
        def _():
            wgb_ref[...] = wg_ref[...].astype(BF16)
            wub_ref[...] = wu_ref[...].astype(BF16)
            wdb_ref[...] = wd_ref[...].astype(BF16)

        x = _from_token_tiles(xs_ref, MOE_TM).astype(BF16)
        gate = jnp.dot(x, wgb_ref[...], preferred_element_type=F32)
        up = jnp.dot(x, wub_ref[...], preferred_element_type=F32)
        act = (gate * jax.nn.sigmoid(gate) * up).astype(BF16)
        _to_token_tiles(ys_ref, jnp.dot(act, wdb_ref[...], preferred_element_type=F32))

    @pl.when(r >= nu_ref[0])
    def _():
        ys_ref[...] = jnp.zeros_like(ys_ref)


def moe_experts(xs, tile_expert, n_used, wg, wu, wd, layer):
    d, f = wg.shape[-2:]
    rows = xs.shape[0] // ROW_TILE
    blk = MOE_TM * ROW_TILE

    def tile(r, te, nu):
        return jnp.minimum(r, nu[0] - 1)

    def wspec(a, b):
        return pl.BlockSpec((None, None, a, b), lambda r, te, nu: (layer, te[tile(r, te, nu)], 0, 0))

    grid_spec = pltpu.PrefetchScalarGridSpec(
        num_scalar_prefetch=2,
        grid=(rows // MOE_TM,),
        in_specs=[pl.BlockSpec((blk, LANES), lambda r, te, nu: (tile(r, te, nu), 0)),
                  wspec(d, f), wspec(d, f), wspec(f, d)],
        out_specs=pl.BlockSpec((blk, LANES), lambda r, te, nu: (r, 0)),
        scratch_shapes=[pltpu.VMEM((d, f), BF16), pltpu.VMEM((d, f), BF16), pltpu.VMEM((f, d), BF16)],
    )
    return pl.pallas_call(
        _experts_kernel,
        grid_spec=grid_spec,
        out_shape=jax.ShapeDtypeStruct(xs.shape, F32),
        compiler_params=_cparams(("arbitrary",)),
        name="moe_experts",
    )(tile_expert, n_used, xs, wg, wu, wd)


def _combine_kernel(p1_ref, p2_ref, off_ref, ys_ref, route_ref, h_ref, g_ref, b_ref, o_ref, ob_ref,
                    buf1_ref, buf2_ref, sem, *, ts):
    nsteps = pl.num_programs(0) * pl.num_programs(1)
    step = pl.program_id(0) * pl.num_programs(1) + pl.program_id(1)

    def row_copies(blk, slot, i):
        n = blk * ts + i
        dst = pl.ds(pl.multiple_of(i * ROW_TILE, ROW_TILE), ROW_TILE)

        def src(p):
            row = _dest_row(p, off_ref)
            return ys_ref.at[pl.ds(pl.multiple_of(row * ROW_TILE, ROW_TILE), ROW_TILE)]

        return (pltpu.make_async_copy(src(p1_ref[n]), buf1_ref.at[slot, dst], sem.at[slot]),
                pltpu.make_async_copy(src(p2_ref[n]), buf2_ref.at[slot, dst], sem.at[slot]))

    def issue(blk, slot):
        def body(i, c):
            c1, c2 = row_copies(blk, slot, i)
            c1.start()
            c2.start()
            return c
        lax.fori_loop(0, ts, body, 0, unroll=8)

    @pl.when(step == 0)
    def _():
        issue(0, 0)

    @pl.when(step + 1 < nsteps)
    def _():
        issue(step + 1, (step + 1) % 2)

    slot = step % 2

    def drain(i, c):
        c1, c2 = row_copies(step, slot, i)
        c1.wait()
        c2.wait()
        return c

    lax.fori_loop(0, ts, drain, 0, unroll=8)

    route = route_ref[...]
    w1 = route[:, 2:3]
    w2 = route[:, 3:4]
    ffn = w1 * _from_token_tiles(buf1_ref.at[slot], ts) + w2 * _from_token_tiles(buf2_ref.at[slot], ts)
    h2 = _layer_norm(DN_ALPHA * h_ref[...] + ffn, g_ref[...], b_ref[...])
    o_ref[...] = h2
    ob_ref[...] = h2.astype(BF16)


def moe_combine(ys, p1, p2, off, route, h1, g, b, bsz, last, ts=256):
    s = h1.shape[0]
    d = D_MODEL
    g2, b2 = g.reshape(1, d), b.reshape(1, d)
    if last:
        o_spec = pl.BlockSpec((None, ts, d), lambda bi, i, *_: (bi, i, 0))
        o_shape = jax.ShapeDtypeStruct((bsz, s, d), F32)
    else:
        o_spec = pl.BlockSpec((ts, d), lambda bi, i, *_: (i, bi))
        o_shape = jax.ShapeDtypeStruct((s, bsz * d), F32)
    kern = functools.partial(_combine_kernel, ts=ts)
    grid_spec = pltpu.PrefetchScalarGridSpec(
        num_scalar_prefetch=3,
        grid=(bsz, s // ts),
        in_specs=[pl.BlockSpec(memory_space=pl.ANY),
                  pl.BlockSpec((ts, ROUTE_LANES), lambda bi, i, *_: (i, bi)),
                  pl.BlockSpec((ts, d), lambda bi, i, *_: (i, bi)),
                  pl.BlockSpec((1, d), lambda bi, i, *_: (0, 0)),
                  pl.BlockSpec((1, d), lambda bi, i, *_: (0, 0))],
        out_specs=[o_spec, pl.BlockSpec((ts, d), lambda bi, i, *_: (i, bi))],
        scratch_shapes=[pltpu.VMEM((2, ts * ROW_TILE, LANES), F32), pltpu.VMEM((2, ts * ROW_TILE, LANES), F32),
                        pltpu.SemaphoreType.DMA((2,))],
    )
    return pl.pallas_call(
        kern,
        grid_spec=grid_spec,
        out_shape=[o_shape, jax.ShapeDtypeStruct((s, bsz * d), BF16)],
        compiler_params=_cparams(("arbitrary", "arbitrary")),
        name="moe_combine",
    )(p1, p2, off, ys, route, h1, g2, b2)


def kernel(x, ln_in_g, ln_in_b, w_in, s5_lambda_re, s5_lambda_im, s5_log_step, s5_b_re, s5_b_im,
           s5_c_re, s5_c_im, s5_d, w_glu, w_branch_ret, w_branch_s5, w_out, ln_mix_g, ln_mix_b,
           w_router_group, b_router_group, w_router_expert, b_router_expert, w_exp_gate, w_exp_up,
           w_exp_down, ln_ffn_g, ln_ffn_b):
    bsz, s, d = x.shape
    t = bsz * s
    depth = w_in.shape[0]
    tabs = retention_tables(s)
    h_sb, hb_sb = ln_in(x, ln_in_g, ln_in_b)
    out = None
    for l in range(depth):
        proj = in_proj(hb_sb, w_in, l, bsz)
        ro = retention(proj, tabs, bsz, s)
        a_re, a_im, bb_re, bb_im = s5_params(s5_lambda_re[l], s5_lambda_im[l], s5_log_step[l],
                                             s5_b_re[l], s5_b_im[l])
        mats = s5_matrices(a_re, a_im, bb_re, bb_im, s5_c_re[l], s5_c_im[l])
        y = s5_scan(proj, mats, s5_d[l], bsz, s)
        pad = ROUTE_LANES - MOE_GROUPS - N_EXPERTS
        wr = jnp.concatenate([w_router_group[l], w_router_expert[l], jnp.zeros((d, pad), F32)], axis=1)
        wr_hi = wr.astype(BF16)
        wr = jnp.concatenate([wr_hi, (wr - wr_hi.astype(F32)).astype(BF16)], axis=1)
        br = jnp.concatenate([b_router_group[l], b_router_expert[l], jnp.zeros((pad,), F32)]).reshape(1, -1)
        h1, route, cnt = mix_out(ro, y, proj, h_sb,
                                 w_glu[l].astype(BF16), w_branch_ret[l].astype(BF16),
                                 w_branch_s5[l].astype(BF16), w_out[l].astype(BF16),
                                 ln_mix_g[l], ln_mix_b[l], wr, br, bsz)
        last = l == depth - 1
        p1, p2, off, pad_start, tile_expert, n_used = moe_plan(route, cnt, bsz, s)
        xs = moe_dispatch(h1, p1, p2, off, pad_start, n_used, bsz)
        ys = moe_experts(xs, tile_expert, n_used, w_exp_gate, w_exp_up, w_exp_down, l)
        h2, h2b = moe_combine(ys, p1, p2, off, route, h1, ln_ffn_g[l], ln_ffn_b[l], bsz, last)
        if last:
            out = h2
        else:
            h_sb, hb_sb = h2, h2b
    return out
```

```python
import functools
import math

import jax
import jax.numpy as jnp
from jax import lax
from jax.experimental import pallas as pl
from jax.experimental.pallas import tpu as pltpu

D_MODEL = 1024
CHUNK = 64
RET_HEADS = 8
RET_QK = 512
RET_V = 1024
RET_QK_DIM = 64
RET_V_DIM = 128
ROPE_BASE = 10000.0
S5_WIDTH = 512
S5_GROUP_CH = 16
S5_GROUPS = 32
S5_STATE = 64
MOE_GROUPS = 4
EXPERTS_PER_GROUP = 8
N_EXPERTS = 32
EXPERT_FF = 256
LN_EPS = 1e-5
HEAD_NORM_EPS = 1e-6
DEPTH = 2
DN_ALPHA = (2 * DEPTH) ** 0.25
IN_WIDTH = 2 * RET_QK + 2 * RET_V + S5_WIDTH + 2 * D_MODEL
PROJ_BLK = 512
N_PROJ_BLK = IN_WIDTH // PROJ_BLK

RET_SUPER = 256
S5_TT = 64
S5_COLS = 128
S5_NSLICE = S5_WIDTH // S5_COLS
S5_SLICE_STATE = (S5_COLS // S5_GROUP_CH) * S5_STATE
ROUTE_LANES = 128
TOP_K = 2
MOE_TM = 256
LANES = 128
ROW_TILE = D_MODEL // LANES
VMEM_LIMIT = 56 * 1024 * 1024

F32 = jnp.float32
BF16 = jnp.bfloat16


def _cparams(sem):
    return pltpu.CompilerParams(dimension_semantics=sem, vmem_limit_bytes=VMEM_LIMIT)


def _layer_norm(x, g, b):
    mu = jnp.mean(x, axis=-1, keepdims=True)
    xc = x - mu
    var = jnp.mean(xc * xc, axis=-1, keepdims=True)
    return xc * lax.rsqrt(var + LN_EPS) * g + b


def _ln_in_kernel(x_ref, g_ref, b_ref, h_ref, hb_ref):
    h = _layer_norm(x_ref[...], g_ref[...], b_ref[...])
    h_ref[...] = h
    hb_ref[...] = h.astype(BF16)


def ln_in(x, g, b, ts=512):
    bsz, s, d = x.shape
    return pl.pallas_call(
        _ln_in_kernel,
        grid=(bsz, s // ts),
        in_specs=[pl.BlockSpec((None, ts, d), lambda bi, si: (bi, si, 0)),
                  pl.BlockSpec((1, d), lambda bi, si: (0, 0)),
                  pl.BlockSpec((1, d), lambda bi, si: (0, 0))],
        out_specs=[pl.BlockSpec((ts, d), lambda bi, si: (si, bi)),
                   pl.BlockSpec((ts, d), lambda bi, si: (si, bi))],
        out_shape=[jax.ShapeDtypeStruct((s, bsz * d), F32),
                   jax.ShapeDtypeStruct((s, bsz * d), BF16)],
        compiler_params=_cparams(("parallel", "parallel")),
        name="ln_in",
    )(x, g.reshape(1, d), b.reshape(1, d))


def _inproj_kernel(h_ref, w_ref, o_ref):
    w = w_ref[...].astype(BF16)
    o_ref[...] = jnp.dot(h_ref[...], w, preferred_element_type=F32).astype(BF16)


def in_proj(hb_sb, w, layer, bsz, ts=2048):
    s = hb_sb.shape[0]
    d, n = w.shape[-2:]
    nb = n // PROJ_BLK
    return pl.pallas_call(
        _inproj_kernel,
        grid=(bsz, s // ts, nb),
        in_specs=[pl.BlockSpec((ts, d), lambda b, i, j: (i, b)),
                  pl.BlockSpec((None, d, PROJ_BLK), lambda b, i, j: (layer, 0, j))],
        out_specs=pl.BlockSpec((ts, PROJ_BLK), lambda b, i, j: (i, b * nb + j)),
        out_shape=jax.ShapeDtypeStruct((s, bsz * n), BF16),
        compiler_params=_cparams(("parallel", "parallel", "arbitrary")),
        name="in_proj",
    )(hb_sb, w)


def _swap_halves(x):
    lane = lax.broadcasted_iota(jnp.int32, x.shape, 1)
    first = (lane % RET_QK_DIM) < (RET_QK_DIM // 2)
    n = x.shape[1]
    return jnp.where(first, pltpu.roll(x, n - RET_QK_DIM // 2, 1), pltpu.roll(x, RET_QK_DIM // 2, 1))


def _retention_kernel(q_ref, k_ref, v0_ref, v1_ref, g0_ref, g1_ref, cos_ref, sin_ref,
                      qd_ref, kd_ref, mask_ref, cd_ref, o_ref, state_ref):
    @pl.when(pl.program_id(1) == 0)
    def _():
        state_ref[...] = jnp.zeros_like(state_ref)

    cos = cos_ref[...]
    sin = sin_ref[...]
    q = q_ref[...].astype(F32)
    k = k_ref[...].astype(F32)
    q = q * cos + _swap_halves(q) * sin
    k = (k * cos + _swap_halves(k) * sin) * (RET_QK_DIM ** -0.5)
    qb = q.astype(BF16)
    kb = k.astype(BF16)
    qdb = (q * qd_ref[...]).astype(BF16)
    kdb = (k * kd_ref[...]).astype(BF16)
    for hd in range(RET_HEADS):
        qs = slice(hd * RET_QK_DIM, (hd + 1) * RET_QK_DIM)
        half, off = divmod(hd * RET_V_DIM, PROJ_BLK)
        vs = slice(off, off + RET_V_DIM)
        vh = (v0_ref, v1_ref)[half][:, vs]
        gh = (g0_ref, g1_ref)[half][:, vs].astype(F32)
        sc = lax.dot_general(qb[:, qs], kb[:, qs], (((1,), (1,)), ((), ())),
                             preferred_element_type=F32)
        sc = (sc * mask_ref[hd]).astype(BF16)
        st = state_ref[hd]
        o = jnp.dot(sc, vh, preferred_element_type=F32)
        o = o + jnp.dot(qdb[:, qs], st.astype(BF16), preferred_element_type=F32)
        kv = lax.dot_general(kdb[:, qs], vh, (((0,), (0,)), ((), ())),
                             preferred_element_type=F32)
        state_ref[hd] = st * cd_ref[hd] + kv
        mu = jnp.mean(o, axis=-1, keepdims=True)
        oc = o - mu
        var = jnp.mean(oc * oc, axis=-1, keepdims=True)
        on = oc * lax.rsqrt(var + HEAD_NORM_EPS)
        o_ref[:, hd * RET_V_DIM:(hd + 1) * RET_V_DIM] = (gh * jax.nn.sigmoid(gh) * on).astype(BF16)


def retention(proj_sb, tabs, bsz, s):
    cos_t, sin_t, qd_t, kd_t, mask, cd = tabs
    L = RET_SUPER
    nb = N_PROJ_BLK

    def pspec(col):
        return pl.BlockSpec((L, PROJ_BLK), lambda b, i, col=col: (i, b * nb + col))

    full2 = pl.BlockSpec((L, RET_QK), lambda b, i: (0, 0))
    return pl.pallas_call(
        _retention_kernel,
        grid=(bsz, s // L),
        in_specs=[pspec(0), pspec(1), pspec(2), pspec(3), pspec(4), pspec(5),
                  pl.BlockSpec((L, RET_QK), lambda b, i: (i, 0)),
                  pl.BlockSpec((L, RET_QK), lambda b, i: (i, 0)),
                  full2, full2,
                  pl.BlockSpec((RET_HEADS, L, L), lambda b, i: (0, 0, 0)),
                  pl.BlockSpec((RET_HEADS, 1, RET_V_DIM), lambda b, i: (0, 0, 0))],
        out_specs=pl.BlockSpec((L, RET_V), lambda b, i: (i, b)),
        out_shape=jax.ShapeDtypeStruct((s, bsz * RET_V), BF16),
        scratch_shapes=[pltpu.VMEM((RET_HEADS, RET_QK_DIM, RET_V_DIM), F32)],
        compiler_params=_cparams(("parallel", "arbitrary")),
        name="retention",
    )(proj_sb, proj_sb, proj_sb, proj_sb, proj_sb, proj_sb, cos_t, sin_t, qd_t, kd_t, mask, cd)


def retention_tables(s):
    L = RET_SUPER
    half = RET_QK_DIM // 2
    inv_freq = ROPE_BASE ** (-jnp.arange(half, dtype=F32) / half)
    ang = jnp.arange(s, dtype=F32)[:, None] * inv_freq[None, :]
    cos, sin = jnp.cos(ang), jnp.sin(ang)
    cos_t = jnp.tile(jnp.concatenate([cos, cos], -1), (1, RET_HEADS))
    sin_t = jnp.tile(jnp.concatenate([-sin, sin], -1), (1, RET_HEADS))
    log_gamma = jnp.log1p(-(2.0 ** (-5.0 - jnp.arange(RET_HEADS, dtype=F32))))
    pos = jnp.arange(L, dtype=F32)
    qd = jnp.exp(log_gamma[None, :] * (pos + 1.0)[:, None])
    kd = jnp.exp(log_gamma[None, :] * (L - 1.0 - pos)[:, None])
    qd_t = jnp.repeat(qd, RET_QK_DIM, axis=1)
    kd_t = jnp.repeat(kd, RET_QK_DIM, axis=1)
    chunk_id = jnp.arange(L) // CHUNK
    visible = (chunk_id[None, :] <= chunk_id[:, None]).astype(F32)
    mask = jnp.exp(log_gamma[:, None, None] * jnp.abs(pos[:, None] - pos[None, :])) * visible[None]
    cd = jnp.broadcast_to(jnp.exp(log_gamma * L)[:, None, None], (RET_HEADS, 1, RET_V_DIM))
    return cos_t, sin_t, qd_t, kd_t, mask, cd


def _s5_param_kernel(lre_ref, lim_ref, ls_ref, bre_ref, bim_ref, are_ref, aim_ref, bbre_ref, bbim_ref):
    lam_re = jnp.minimum(lre_ref[...], -1e-4)
    lam_im = lim_ref[...]
    step = jnp.exp(ls_ref[...])
    mag = jnp.exp(lam_re * step)
    ang = lam_im * step
    ab_re = mag * jnp.cos(ang)
    ab_im = mag * jnp.sin(ang)
    den = lam_re * lam_re + lam_im * lam_im
    n_re = ab_re - 1.0
    zc_re = (n_re * lam_re + ab_im * lam_im) / den
    zc_im = (ab_im * lam_re - n_re * lam_im) / den
    are_ref[...] = ab_re
    aim_ref[...] = ab_im
    b_re = bre_ref[...]
    b_im = bim_ref[...]
    bbre_ref[...] = zc_re * b_re - zc_im * b_im
    bbim_ref[...] = zc_re * b_im + zc_im * b_re


def s5_params(lam_re, lam_im, log_step, b_re, b_im):
    g, n = lam_re.shape
    c = b_re.shape[-1]
    outs = pl.pallas_call(
        _s5_param_kernel,
        out_shape=[jax.ShapeDtypeStruct((g, 1, n), F32), jax.ShapeDtypeStruct((g, 1, n), F32),
                   jax.ShapeDtypeStruct((g, c, n), F32), jax.ShapeDtypeStruct((g, c, n), F32)],
        name="s5_params",
    )(lam_re.reshape(g, 1, n), lam_im.reshape(g, 1, n), log_step.reshape(g, 1, 1),
      jnp.swapaxes(b_re, 1, 2), jnp.swapaxes(b_im, 1, 2))
    return outs


def _block_diag(x):
    ns, gl, r, c = x.shape
    eye = jnp.eye(gl, dtype=x.dtype)
    return jnp.einsum('sgrc,gh->sgrhc', x, eye).reshape(ns, gl * r, gl * c)


def s5_matrices(a_re, a_im, bb_re, bb_im, c_re, c_im):
    gl = S5_COLS // S5_GROUP_CH
    ns = S5_NSLICE
    bre = _block_diag(bb_re.reshape(ns, gl, S5_GROUP_CH, S5_STATE))
    bim = _block_diag(bb_im.reshape(ns, gl, S5_GROUP_CH, S5_STATE))
    bq = jnp.concatenate([bre, bim], axis=-1).astype(BF16)
    cre = _block_diag(jnp.swapaxes(c_re, 1, 2).reshape(ns, gl, S5_STATE, S5_GROUP_CH))
    cim = _block_diag(jnp.swapaxes(c_im, 1, 2).reshape(ns, gl, S5_STATE, S5_GROUP_CH))
    cq = jnp.concatenate([cre, -cim], axis=1).astype(BF16)
    are = a_re.reshape(ns, 1, S5_SLICE_STATE)
    aim = a_im.reshape(ns, 1, S5_SLICE_STATE)
    return bq, cq, are, aim


def _s5_kernel(*refs, bsz, tt):
    u_refs = refs[:bsz]
    bq_ref, cq_ref, are_ref, aim_ref, d_ref, y_ref, us_ref, ys_ref, bu_ref, st_ref = refs[bsz:]

    @pl.when(pl.program_id(0) == 0)
    def _():
        st_ref[...] = jnp.zeros_like(st_ref)

    for b in range(bsz):
        ub = u_refs[b][...].astype(F32)
        for cs in range(S5_NSLICE):
            us_ref[cs, pl.ds(b, tt, stride=bsz), :] = ub[:, cs * S5_COLS:(cs + 1) * S5_COLS]

    ns2 = S5_SLICE_STATE
    for cs in range(S5_NSLICE):
        cols = slice(cs * S5_COLS, (cs + 1) * S5_COLS)
        uf = us_ref[cs]
        bu_ref[cs] = jnp.dot(uf.astype(BF16), bq_ref[cs], preferred_element_type=F32)
        a_re = jnp.broadcast_to(are_ref[cs], (bsz, ns2))
        a_im = jnp.broadcast_to(aim_ref[cs], (bsz, ns2))

        def step(t, carry):
            h_re, h_im = carry
            rows = pl.ds(pl.multiple_of(t * bsz, bsz), bsz)
            n_re = a_re * h_re - a_im * h_im + bu_ref[cs, rows, 0:ns2]
            n_im = a_re * h_im + a_im * h_re + bu_ref[cs, rows, ns2:2 * ns2]
            bu_ref[cs, rows, 0:ns2] = n_re
            bu_ref[cs, rows, ns2:2 * ns2] = n_im
            return n_re, n_im

        h_re, h_im = lax.fori_loop(0, tt, step, (st_ref[cs, 0], st_ref[cs, 1]), unroll=True)
        st_ref[cs, 0] = h_re
        st_ref[cs, 1] = h_im
        y = jnp.dot(bu_ref[cs].astype(BF16), cq_ref[cs], preferred_element_type=F32)
        ys_ref[cs] = y + d_ref[:, cols] * uf

    for b in range(bsz):
        for cs in range(S5_NSLICE):
            lo = b * S5_WIDTH + cs * S5_COLS
            y_ref[:, lo:lo + S5_COLS] = ys_ref[cs, pl.ds(b, tt, stride=bsz), :]


def s5_scan(proj_sb, mats, d_skip, bsz, s):
    bq, cq, are, aim = mats
    tt = S5_TT
    rows = tt * bsz
    nb = N_PROJ_BLK
    kern = functools.partial(_s5_kernel, bsz=bsz, tt=tt)
    u_specs = [pl.BlockSpec((tt, PROJ_BLK), lambda i, b=b: (i, b * nb + 6)) for b in range(bsz)]
    return pl.pallas_call(
        kern,
        grid=(s // tt,),
        in_specs=u_specs + [
                  pl.BlockSpec(bq.shape, lambda i: (0, 0, 0)),
                  pl.BlockSpec(cq.shape, lambda i: (0, 0, 0)),
                  pl.BlockSpec(are.shape, lambda i: (0, 0, 0)),
                  pl.BlockSpec(aim.shape, lambda i: (0, 0, 0)),
                  pl.BlockSpec((1, S5_WIDTH), lambda i: (0, 0))],
        out_specs=pl.BlockSpec((tt, bsz * S5_WIDTH), lambda i: (i, 0)),
        out_shape=jax.ShapeDtypeStruct((s, bsz * S5_WIDTH), F32),
        scratch_shapes=[pltpu.VMEM((S5_NSLICE, rows, S5_COLS), F32),
                        pltpu.VMEM((S5_NSLICE, rows, S5_COLS), F32),
                        pltpu.VMEM((S5_NSLICE, rows, 2 * S5_SLICE_STATE), F32),
                        pltpu.VMEM((S5_NSLICE, 2, bsz, S5_SLICE_STATE), F32)],
        compiler_params=_cparams(("arbitrary",)),
        name="s5_scan",
    )(*([proj_sb] * bsz), bq, cq, are, aim, d_skip.reshape(1, S5_WIDTH))


def _gelu_tanh(x):
    c = math.sqrt(2.0 / math.pi)
    return 0.5 * x * (1.0 + jnp.tanh(c * (x + 0.044715 * (x * x * x))))


def _route(logits):
    lane = lax.broadcasted_iota(jnp.int32, logits.shape, 1)
    neg = jnp.float32(-jnp.inf)
    big = jnp.int32(1 << 20)
    is_g = lane < MOE_GROUPS
    lg = jnp.where(is_g, logits, neg)
    mg = jnp.max(lg, axis=-1, keepdims=True)
    sg = jnp.sum(jnp.where(is_g, jnp.exp(lg - mg), 0.0), axis=-1, keepdims=True)
    g_top = 1.0 / sg
    g_idx = jnp.min(jnp.where(lg == mg, lane, big), axis=-1, keepdims=True)
    lo = MOE_GROUPS + g_idx * EXPERTS_PER_GROUP
    in_grp = (lane >= lo) & (lane < lo + EXPERTS_PER_GROUP)
    le = jnp.where(in_grp, logits, neg)
    m1 = jnp.max(le, axis=-1, keepdims=True)
    se = jnp.sum(jnp.where(in_grp, jnp.exp(le - m1), 0.0), axis=-1, keepdims=True)
    i1 = jnp.min(jnp.where(le == m1, lane, big), axis=-1, keepdims=True)
    le2 = jnp.where(lane == i1, neg, le)
    m2 = jnp.max(le2, axis=-1, keepdims=True)
    i2 = jnp.min(jnp.where(le2 == m2, lane, big), axis=-1, keepdims=True)
    p1 = 1.0 / se
    p2 = jnp.exp(m2 - m1) / se
    tot = p1 + p2
    w1 = g_top * (p1 / tot)
    w2 = g_top * (p2 / tot)
    return lane, i1, i2, w1, w2


def _mix_kernel(ro_ref, y_ref, gr0_ref, gr1_ref, gs0_ref, gs1_ref, h_ref,
                wglu_ref, wbr_ref, wbs_ref, wout_ref, g_ref, b_ref, wr_ref, br_ref, tri_ref,
                h1_ref, route_ref, cnt_ref, run_ref):
    @pl.when((pl.program_id(0) == 0) & (pl.program_id(1) == 0))
    def _():
        run_ref[...] = jnp.zeros_like(run_ref)

    z = _gelu_tanh(y_ref[...])
    zg = jnp.dot(z.astype(BF16), wglu_ref[...], preferred_element_type=F32)
    zz = (z * jax.nn.sigmoid(zg)).astype(BF16)
    s5b = jnp.dot(zz, wbs_ref[...], preferred_element_type=F32)
    rb = jnp.dot(ro_ref[...], wbr_ref[...], preferred_element_type=F32)
    gr = jnp.concatenate([gr0_ref[...], gr1_ref[...]], axis=-1).astype(F32)
    gs = jnp.concatenate([gs0_ref[...], gs1_ref[...]], axis=-1).astype(F32)
    merged = jax.nn.sigmoid(gr) * rb + jax.nn.sigmoid(gs) * s5b
    mix = jnp.dot(merged.astype(BF16), wout_ref[...], preferred_element_type=F32)
    h1 = _layer_norm(DN_ALPHA * h_ref[...] + mix, g_ref[...], b_ref[...])
    h1_ref[...] = h1
    h_hi = h1.astype(BF16)
    h_lo = (h1 - h_hi.astype(F32)).astype(BF16)
    both = jnp.dot(h_hi, wr_ref[...], preferred_element_type=F32)
    logits = (both[:, :ROUTE_LANES] + both[:, ROUTE_LANES:]
              + jnp.dot(h_lo, wr_ref[:, :ROUTE_LANES], preferred_element_type=F32)) + br_ref[...]
    lane, i1, i2, w1, w2 = _route(logits)
    oh1 = lane == i1
    oh2 = lane == i2
    oh = jnp.where(oh1 | oh2, 1.0, 0.0)
    before = run_ref[0:1, :] + jnp.dot(tri_ref[...], oh.astype(BF16), preferred_element_type=F32)
    r1 = jnp.sum(jnp.where(oh1, before, 0.0), axis=-1, keepdims=True)
    r2 = jnp.sum(jnp.where(oh2, before, 0.0), axis=-1, keepdims=True)
    cnt = run_ref[0:1, :] + jnp.sum(oh, axis=0, keepdims=True)
    run_ref[...] = jnp.broadcast_to(cnt, run_ref.shape)
    cnt_ref[...] = jnp.broadcast_to(cnt, cnt_ref.shape)
    vals = (i1.astype(F32) - MOE_GROUPS, i2.astype(F32) - MOE_GROUPS, w1, w2, r1, r2)
    route = jnp.zeros(logits.shape, F32)
    for k, v in enumerate(vals):
        route = jnp.where(lane == k, v, route)
    route_ref[...] = route


def mix_out(ro, y, proj, h, wglu, wbr, wbs, wout, g, b, wr, br, bsz, ts=512):
    s = h.shape[0]
    d = D_MODEL
    nb = N_PROJ_BLK

    def const(a):
        return pl.BlockSpec(a.shape, lambda bi, i: (0,) * a.ndim)

    def pspec(col):
        return pl.BlockSpec((ts, PROJ_BLK), lambda bi, i, col=col: (i, bi * nb + col))

    def tok(w):
        return pl.BlockSpec((ts, w), lambda bi, i: (i, bi))

    g2, b2 = g.reshape(1, d), b.reshape(1, d)
    idx = jnp.arange(ts)
    tri = (idx[None, :] < idx[:, None]).astype(BF16)
    return pl.pallas_call(
        _mix_kernel,
        grid=(bsz, s // ts),
        in_specs=[tok(RET_V), tok(S5_WIDTH), pspec(7), pspec(8), pspec(9), pspec(10), tok(d),
                  const(wglu), const(wbr), const(wbs), const(wout), const(g2), const(b2),
                  const(wr), const(br), const(tri)],
        out_specs=[tok(d), tok(ROUTE_LANES), pl.BlockSpec((8, ROUTE_LANES), lambda bi, i: (0, 0))],
        out_shape=[jax.ShapeDtypeStruct((s, bsz * d), F32),
                   jax.ShapeDtypeStruct((s, bsz * ROUTE_LANES), F32),
                   jax.ShapeDtypeStruct((8, ROUTE_LANES), F32)],
        scratch_shapes=[pltpu.VMEM((8, ROUTE_LANES), F32)],
        compiler_params=_cparams(("arbitrary", "arbitrary")),
        name="mix_out",
    )(ro, y, proj, proj, proj, proj, h, wglu, wbr, wbs, wout, g2, b2, wr, br, tri)


def moe_rows(t):
    return TOP_K * t + (N_EXPERTS + 1) * MOE_TM


def moe_plan(route, cnt, bsz, s):
    t = bsz * s
    rec = route.reshape(s, bsz, ROUTE_LANES)[:, :, :6]
    rec = jnp.transpose(rec, (1, 0, 2)).reshape(t, 6)
    e1 = rec[:, 0].astype(jnp.int32)
    e2 = rec[:, 1].astype(jnp.int32)
    r1 = rec[:, 4].astype(jnp.int32)
    r2 = rec[:, 5].astype(jnp.int32)
    counts = cnt[0, MOE_GROUPS:MOE_GROUPS + N_EXPERTS].astype(jnp.int32)
    padded = ((counts + MOE_TM - 1) // MOE_TM) * MOE_TM
    ends = jnp.cumsum(padded)
    off = ends - padded
    dest = moe_dest(off, jnp.stack([e1, e2]).reshape(TOP_K, t // LANES, LANES),
                    jnp.stack([r1, r2]).reshape(TOP_K, t // LANES, LANES)).reshape(TOP_K, t)
    pad_start = off + counts
    n_tiles = moe_rows(t) // MOE_TM
    n_used = (ends[-1:] // MOE_TM).astype(jnp.int32)
    tile_start = jnp.arange(n_tiles, dtype=jnp.int32) * MOE_TM
    tile_expert = jnp.sum((ends[None, :] <= tile_start[:, None]).astype(jnp.int32), axis=1)
    tile_expert = jnp.minimum(tile_expert, N_EXPERTS - 1)
    return dest[0], dest[1], pad_start, tile_expert, n_used


def _dest_kernel(off_ref, e_ref, r_ref, d_ref):
    e = e_ref[...]
    d = r_ref[...]
    for k in range(N_EXPERTS):
        d = d + jnp.where(e == k, off_ref[k], 0)
    d_ref[...] = d


def moe_dest(off, e, r):
    return pl.pallas_call(
        _dest_kernel,
        in_specs=[pl.BlockSpec(memory_space=pltpu.SMEM), pl.BlockSpec(memory_space=pltpu.VMEM),
                  pl.BlockSpec(memory_space=pltpu.VMEM)],
        out_specs=pl.BlockSpec(memory_space=pltpu.VMEM),
        out_shape=jax.ShapeDtypeStruct(e.shape, jnp.int32),
        name="moe_dest",
    )(off, e, r)


def _to_token_tiles(ref, x):
    n = x.shape[0]
    for j in range(ROW_TILE):
        ref[pl.ds(j, n, stride=ROW_TILE), :] = x[:, j * LANES:(j + 1) * LANES]


def _from_token_tiles(ref, n):
    return jnp.concatenate([ref[pl.ds(j, n, stride=ROW_TILE), :] for j in range(ROW_TILE)], axis=-1)


def _dispatch_kernel(d1_ref, d2_ref, pad_ref, nu_ref, x_ref, xs_ref, xt_ref, zero_ref,
                     sem, zsem, *, ts):
    nsteps = pl.num_programs(0) * pl.num_programs(1)
    step = pl.program_id(0) * pl.num_programs(1) + pl.program_id(1)
    tile_rows = MOE_TM * ROW_TILE

    @pl.when(step == 0)
    def _():
        zero_ref[...] = jnp.zeros_like(zero_ref)

        def zero_copy(start):
            start = pl.multiple_of(start * ROW_TILE, ROW_TILE)
            return pltpu.make_async_copy(zero_ref, xs_ref.at[pl.ds(start, tile_rows)], zsem)

        for e in range(N_EXPERTS):
            zero_copy(pad_ref[e]).start()
        for e in range(N_EXPERTS):
            zero_copy(pad_ref[e]).wait()

        def zero_tile(r, c):
            cp = zero_copy(r * MOE_TM)
            cp.start()
            cp.wait()
            return c

        lax.fori_loop(nu_ref[0], xs_ref.shape[0] // tile_rows, zero_tile, 0)

    slot = step % 2
    _to_token_tiles(xt_ref.at[slot], x_ref[...])

    def row_copies(blk, sl, i):
        n = blk * ts + i
        src = xt_ref.at[sl, pl.ds(pl.multiple_of(i * ROW_TILE, ROW_TILE), ROW_TILE)]

        def dst(row):
            return xs_ref.at[pl.ds(pl.multiple_of(row * ROW_TILE, ROW_TILE), ROW_TILE)]

        return (pltpu.make_async_copy(src, dst(d1_ref[n]), sem.at[sl]),
                pltpu.make_async_copy(src, dst(d2_ref[n]), sem.at[sl]))

    def issue(i, c):
        c1, c2 = row_copies(step, slot, i)
        c1.start()
        c2.start()
        return c

    lax.fori_loop(0, ts, issue, 0, unroll=8)

    def drain(blk, sl):
        def body(i, c):
            c1, c2 = row_copies(blk, sl, i)
            c1.wait()
            c2.wait()
            return c
        lax.fori_loop(0, ts, body, 0, unroll=8)

    @pl.when(step > 0)
    def _():
        drain(step - 1, 1 - slot)

    @pl.when(step == nsteps - 1)
    def _():
        drain(step, slot)


def moe_dispatch(h1, d1, d2, pad_start, n_used, bsz, ts=512):
    s = h1.shape[0]
    d = D_MODEL
    kern = functools.partial(_dispatch_kernel, ts=ts)
    grid_spec = pltpu.PrefetchScalarGridSpec(
        num_scalar_prefetch=4,
        grid=(bsz, s // ts),
        in_specs=[pl.BlockSpec((ts, d), lambda bi, i, *_: (i, bi))],
        out_specs=pl.BlockSpec(memory_space=pl.ANY),
        scratch_shapes=[pltpu.VMEM((2, ts * ROW_TILE, LANES), F32),
                        pltpu.VMEM((MOE_TM * ROW_TILE, LANES), F32),
                        pltpu.SemaphoreType.DMA((2,)), pltpu.SemaphoreType.DMA(())],
    )
    return pl.pallas_call(
        kern,
        grid_spec=grid_spec,
        out_shape=jax.ShapeDtypeStruct((moe_rows(bsz * s) * ROW_TILE, LANES), F32),
        compiler_params=_cparams(("arbitrary", "arbitrary")),
        name="moe_dispatch",
    )(d1, d2, pad_start, n_used, h1)


def _experts_kernel(te_ref, nu_ref, xs_ref, wg_ref, wu_ref, wd_ref, ys_ref, wgb_ref, wub_ref, wdb_ref):
    r = pl.program_id(0)

    @pl.when(r < nu_ref[0])
    def _():
        changed = (r == 0) | (te_ref[r] != te_ref[jnp.maximum(r - 1, 0)])

        @pl.when(changed)
        def _():
            wgb_ref[...] = wg_ref[...].astype(BF16)
            wub_ref[...] = wu_ref[...].astype(BF16)
            wdb_ref[...] = wd_ref[...].astype(BF16)

        x = _from_token_tiles(xs_ref, MOE_TM).astype(BF16)
        gate = jnp.dot(x, wgb_ref[...], preferred_element_type=F32)
        up = jnp.dot(x, wub_ref[...], preferred_element_type=F32)
        act = (gate * jax.nn.sigmoid(gate) * up).astype(BF16)
        _to_token_tiles(ys_ref, jnp.dot(act, wdb_ref[...], preferred_element_type=F32))

    @pl.when(r >= nu_ref[0])
    def _():
        ys_ref[...] = jnp.zeros_like(ys_ref)


def moe_experts(xs, tile_expert, n_used, wg, wu, wd, layer):
    d, f = wg.shape[-2:]
    rows = xs.shape[0] // ROW_TILE
    blk = MOE_TM * ROW_TILE

    def tile(r, te, nu):
        return jnp.minimum(r, nu[0] - 1)

    def wspec(a, b):
        return pl.BlockSpec((None, None, a, b), lambda r, te, nu: (layer, te[tile(r, te, nu)], 0, 0))

    grid_spec = pltpu.PrefetchScalarGridSpec(
        num_scalar_prefetch=2,
        grid=(rows // MOE_TM,),
        in_specs=[pl.BlockSpec((blk, LANES), lambda r, te, nu: (tile(r, te, nu), 0)),
                  wspec(d, f), wspec(d, f), wspec(f, d)],
        out_specs=pl.BlockSpec((blk, LANES), lambda r, te, nu: (r, 0)),
        scratch_shapes=[pltpu.VMEM((d, f), BF16), pltpu.VMEM((d, f), BF16), pltpu.VMEM((f, d), BF16)],
    )
    return pl.pallas_call(
        _experts_kernel,
        grid_spec=grid_spec,
        out_shape=jax.ShapeDtypeStruct(xs.shape, F32),
        compiler_params=_cparams(("arbitrary",)),
        name="moe_experts",
    )(tile_expert, n_used, xs, wg, wu, wd)


def _combine_kernel(d1_ref, d2_ref, ys_ref, route_ref, h_ref, g_ref, b_ref, o_ref, ob_ref,
                    buf1_ref, buf2_ref, sem, *, ts):
    nsteps = pl.num_programs(0) * pl.num_programs(1)
    step = pl.program_id(0) * pl.num_programs(1) + pl.program_id(1)

    def row_copies(blk, slot, i):
        n = blk * ts + i
        dst = pl.ds(pl.multiple_of(i * ROW_TILE, ROW_TILE), ROW_TILE)

        def src(row):
            return ys_ref.at[pl.ds(pl.multiple_of(row * ROW_TILE, ROW_TILE), ROW_TILE)]

        return (pltpu.make_async_copy(src(d1_ref[n]), buf1_ref.at[slot, dst], sem.at[slot]),
                pltpu.make_async_copy(src(d2_ref[n]), buf2_ref.at[slot, dst], sem.at[slot]))

    def issue(blk, slot):
        def body(i, c):
            c1, c2 = row_copies(blk, slot, i)
            c1.start()
            c2.start()
            return c
        lax.fori_loop(0, ts, body, 0, unroll=8)

    @pl.when(step == 0)
    def _():
        issue(0, 0)

    @pl.when(step + 1 < nsteps)
    def _():
        issue(step + 1, (step + 1) % 2)

    slot = step % 2

    def drain(i, c):
        c1, c2 = row_copies(step, slot, i)
        c1.wait()
        c2.wait()
        return c

    lax.fori_loop(0, ts, drain, 0, unroll=8)

    route = route_ref[...]
    w1 = route[:, 2:3]
    w2 = route[:, 3:4]
    ffn = w1 * _from_token_tiles(buf1_ref.at[slot], ts) + w2 * _from_token_tiles(buf2_ref.at[slot], ts)
    h2 = _layer_norm(DN_ALPHA * h_ref[...] + ffn, g_ref[...], b_ref[...])
    o_ref[...] = h2
    ob_ref[...] = h2.astype(BF16)


def moe_combine(ys, d1, d2, route, h1, g, b, bsz, last, ts=256):
    s = h1.shape[0]
    d = D_MODEL
    g2, b2 = g.reshape(1, d), b.reshape(1, d)
    if last:
        o_spec = pl.BlockSpec((None, ts, d), lambda bi, i, *_: (bi, i, 0))
        o_shape = jax.ShapeDtypeStruct((bsz, s, d), F32)
    else:
        o_spec = pl.BlockSpec((ts, d), lambda bi, i, *_: (i, bi))
        o_shape = jax.ShapeDtypeStruct((s, bsz * d), F32)
    kern = functools.partial(_combine_kernel, ts=ts)
    grid_spec = pltpu.PrefetchScalarGridSpec(
        num_scalar_prefetch=2,
        grid=(bsz, s // ts),
        in_specs=[pl.BlockSpec(memory_space=pl.ANY),
                  pl.BlockSpec((ts, ROUTE_LANES), lambda bi, i, *_: (i, bi)),
                  pl.BlockSpec((ts, d), lambda bi, i, *_: (i, bi)),
                  pl.BlockSpec((1, d), lambda bi, i, *_: (0, 0)),
                  pl.BlockSpec((1, d), lambda bi, i, *_: (0, 0))],
        out_specs=[o_spec, pl.BlockSpec((ts, d), lambda bi, i, *_: (i, bi))],
        scratch_shapes=[pltpu.VMEM((2, ts * ROW_TILE, LANES), F32), pltpu.VMEM((2, ts * ROW_TILE, LANES), F32),
                        pltpu.SemaphoreType.DMA((2,))],
    )
    return pl.pallas_call(
        kern,
        grid_spec=grid_spec,
        out_shape=[o_shape, jax.ShapeDtypeStruct((s, bsz * d), BF16)],
        compiler_params=_cparams(("arbitrary", "arbitrary")),
        name="moe_combine",
    )(d1, d2, ys, route, h1, g2, b2)


def kernel(x, ln_in_g, ln_in_b, w_in, s5_lambda_re, s5_lambda_im, s5_log_step, s5_b_re, s5_b_im,
           s5_c_re, s5_c_im, s5_d, w_glu, w_branch_ret, w_branch_s5, w_out, ln_mix_g, ln_mix_b,
           w_router_group, b_router_group, w_router_expert, b_router_expert, w_exp_gate, w_exp_up,
           w_exp_down, ln_ffn_g, ln_ffn_b):
    bsz, s, d = x.shape
    t = bsz * s
    depth = w_in.shape[0]
    tabs = retention_tables(s)
    h_sb, hb_sb = ln_in(x, ln_in_g, ln_in_b)
    out = None
    for l in range(depth):
        proj = in_proj(hb_sb, w_in, l, bsz)
        ro = retention(proj, tabs, bsz, s)
        a_re, a_im, bb_re, bb_im = s5_params(s5_lambda_re[l], s5_lambda_im[l], s5_log_step[l],
                                             s5_b_re[l], s5_b_im[l])
        mats = s5_matrices(a_re, a_im, bb_re, bb_im, s5_c_re[l], s5_c_im[l])
        y = s5_scan(proj, mats, s5_d[l], bsz, s)
        pad = ROUTE_LANES - MOE_GROUPS - N_EXPERTS
        wr = jnp.concatenate([w_router_group[l], w_router_expert[l], jnp.zeros((d, pad), F32)], axis=1)
        wr_hi = wr.astype(BF16)
        wr = jnp.concatenate([wr_hi, (wr - wr_hi.astype(F32)).astype(BF16)], axis=1)
        br = jnp.concatenate([b_router_group[l], b_router_expert[l], jnp.zeros((pad,), F32)]).reshape(1, -1)
        h1, route, cnt = mix_out(ro, y, proj, h_sb,
                                 w_glu[l].astype(BF16), w_branch_ret[l].astype(BF16),
                                 w_branch_s5[l].astype(BF16), w_out[l].astype(BF16),
                                 ln_mix_g[l], ln_mix_b[l], wr, br, bsz)
        last = l == depth - 1
        d1, d2, pad_start, tile_expert, n_used = moe_plan(route, cnt, bsz, s)
        xs = moe_dispatch(h1, d1, d2, pad_start, n_used, bsz)
        ys = moe_experts(xs, tile_expert, n_used, w_exp_gate, w_exp_up, w_exp_down, l)
        h2, h2b = moe_combine(ys, d1, d2, route, h1, ln_ffn_g[l], ln_ffn_b[l], bsz, last)
        if last:
            out = h2
        else:
            h_sb, hb_sb = h2, h2b
    return out
```

```python
import functools
import math

import jax
import jax.numpy as jnp
from jax import lax
from jax.experimental import pallas as pl
from jax.experimental.pallas import tpu as pltpu

D_MODEL = 1024
CHUNK = 64
RET_HEADS = 8
RET_QK = 512
RET_V = 1024
RET_QK_DIM = 64
RET_V_DIM = 128
ROPE_BASE = 10000.0
S5_WIDTH = 512
S5_GROUP_CH = 16
S5_GROUPS = 32
S5_STATE = 64
MOE_GROUPS = 4
EXPERTS_PER_GROUP = 8
N_EXPERTS = 32
EXPERT_FF = 256
LN_EPS = 1e-5
HEAD_NORM_EPS = 1e-6
DEPTH = 2
DN_ALPHA = (2 * DEPTH) ** 0.25
IN_WIDTH = 2 * RET_QK + 2 * RET_V + S5_WIDTH + 2 * D_MODEL
PROJ_BLK = 512
N_PROJ_BLK = IN_WIDTH // PROJ_BLK

RET_SUPER = 256
S5_TT = 64
S5_COLS = 128
S5_NSLICE = S5_WIDTH // S5_COLS
S5_SLICE_STATE = (S5_COLS // S5_GROUP_CH) * S5_STATE
ROUTE_LANES = 128
TOP_K = 2
MOE_TM = 256
MOE_TS = 512
RUN_BITS = MOE_TS.bit_length()
LANES = 128
ROW_TILE = D_MODEL // LANES
VMEM_LIMIT = 56 * 1024 * 1024

F32 = jnp.float32
BF16 = jnp.bfloat16


def _cparams(sem):
    return pltpu.CompilerParams(dimension_semantics=sem, vmem_limit_bytes=VMEM_LIMIT)


def _layer_norm(x, g, b):
    mu = jnp.mean(x, axis=-1, keepdims=True)
    xc = x - mu
    var = jnp.mean(xc * xc, axis=-1, keepdims=True)
    return xc * lax.rsqrt(var + LN_EPS) * g + b


def _ln_in_kernel(x_ref, g_ref, b_ref, h_ref, hb_ref):
    h = _layer_norm(x_ref[...], g_ref[...], b_ref[...])
    h_ref[...] = h
    hb_ref[...] = h.astype(BF16)


def ln_in(x, g, b, ts=512):
    bsz, s, d = x.shape
    return pl.pallas_call(
        _ln_in_kernel,
        grid=(bsz, s // ts),
        in_specs=[pl.BlockSpec((None, ts, d), lambda bi, si: (bi, si, 0)),
                  pl.BlockSpec((1, d), lambda bi, si: (0, 0)),
                  pl.BlockSpec((1, d), lambda bi, si: (0, 0))],
        out_specs=[pl.BlockSpec((ts, d), lambda bi, si: (si, bi)),
                   pl.BlockSpec((ts, d), lambda bi, si: (si, bi))],
        out_shape=[jax.ShapeDtypeStruct((s, bsz * d), F32),
                   jax.ShapeDtypeStruct((s, bsz * d), BF16)],
        compiler_params=_cparams(("parallel", "parallel")),
        name="ln_in",
    )(x, g.reshape(1, d), b.reshape(1, d))


def _inproj_kernel(h_ref, w_ref, o_ref):
    w = w_ref[...].astype(BF16)
    o_ref[...] = jnp.dot(h_ref[...], w, preferred_element_type=F32).astype(BF16)


def in_proj(hb_sb, w, layer, bsz, ts=2048):
    s = hb_sb.shape[0]
    d, n = w.shape[-2:]
    nb = n // PROJ_BLK
    return pl.pallas_call(
        _inproj_kernel,
        grid=(bsz, s // ts, nb),
        in_specs=[pl.BlockSpec((ts, d), lambda b, i, j: (i, b)),
                  pl.BlockSpec((None, d, PROJ_BLK), lambda b, i, j: (layer, 0, j))],
        out_specs=pl.BlockSpec((ts, PROJ_BLK), lambda b, i, j: (i, b * nb + j)),
        out_shape=jax.ShapeDtypeStruct((s, bsz * n), BF16),
        compiler_params=_cparams(("parallel", "parallel", "arbitrary")),
        name="in_proj",
    )(hb_sb, w)


def _swap_halves(x):
    lane = lax.broadcasted_iota(jnp.int32, x.shape, 1)
    first = (lane % RET_QK_DIM) < (RET_QK_DIM // 2)
    n = x.shape[1]
    return jnp.where(first, pltpu.roll(x, n - RET_QK_DIM // 2, 1), pltpu.roll(x, RET_QK_DIM // 2, 1))


def _retention_kernel(q_ref, k_ref, v0_ref, v1_ref, g0_ref, g1_ref, cos_ref, sin_ref,
                      qd_ref, kd_ref, mask_ref, cd_ref, o_ref, state_ref):
    @pl.when(pl.program_id(1) == 0)
    def _():
        state_ref[...] = jnp.zeros_like(state_ref)

    cos = cos_ref[...]
    sin = sin_ref[...]
    q = q_ref[...].astype(F32)
    k = k_ref[...].astype(F32)
    q = q * cos + _swap_halves(q) * sin
    k = (k * cos + _swap_halves(k) * sin) * (RET_QK_DIM ** -0.5)
    qb = q.astype(BF16)
    kb = k.astype(BF16)
    qdb = (q * qd_ref[...]).astype(BF16)
    kdb = (k * kd_ref[...]).astype(BF16)
    for hd in range(RET_HEADS):
        qs = slice(hd * RET_QK_DIM, (hd + 1) * RET_QK_DIM)
        half, off = divmod(hd * RET_V_DIM, PROJ_BLK)
        vs = slice(off, off + RET_V_DIM)
        vh = (v0_ref, v1_ref)[half][:, vs]
        gh = (g0_ref, g1_ref)[half][:, vs].astype(F32)
        sc = lax.dot_general(qb[:, qs], kb[:, qs], (((1,), (1,)), ((), ())),
                             preferred_element_type=F32)
        sc = (sc * mask_ref[hd]).astype(BF16)
        st = state_ref[hd]
        o = jnp.dot(sc, vh, preferred_element_type=F32)
        o = o + jnp.dot(qdb[:, qs], st.astype(BF16), preferred_element_type=F32)
        kv = lax.dot_general(kdb[:, qs], vh, (((0,), (0,)), ((), ())),
                             preferred_element_type=F32)
        state_ref[hd] = st * cd_ref[hd] + kv
        mu = jnp.mean(o, axis=-1, keepdims=True)
        oc = o - mu
        var = jnp.mean(oc * oc, axis=-1, keepdims=True)
        on = oc * lax.rsqrt(var + HEAD_NORM_EPS)
        o_ref[:, hd * RET_V_DIM:(hd + 1) * RET_V_DIM] = (gh * jax.nn.sigmoid(gh) * on).astype(BF16)


def retention(proj_sb, tabs, bsz, s):
    cos_t, sin_t, qd_t, kd_t, mask, cd = tabs
    L = RET_SUPER
    nb = N_PROJ_BLK

    def pspec(col):
        return pl.BlockSpec((L, PROJ_BLK), lambda b, i, col=col: (i, b * nb + col))

    full2 = pl.BlockSpec((L, RET_QK), lambda b, i: (0, 0))
    return pl.pallas_call(
        _retention_kernel,
        grid=(bsz, s // L),
        in_specs=[pspec(0), pspec(1), pspec(2), pspec(3), pspec(4), pspec(5),
                  pl.BlockSpec((L, RET_QK), lambda b, i: (i, 0)),
                  pl.BlockSpec((L, RET_QK), lambda b, i: (i, 0)),
                  full2, full2,
                  pl.BlockSpec((RET_HEADS, L, L), lambda b, i: (0, 0, 0)),
                  pl.BlockSpec((RET_HEADS, 1, RET_V_DIM), lambda b, i: (0, 0, 0))],
        out_specs=pl.BlockSpec((L, RET_V), lambda b, i: (i, b)),
        out_shape=jax.ShapeDtypeStruct((s, bsz * RET_V), BF16),
        scratch_shapes=[pltpu.VMEM((RET_HEADS, RET_QK_DIM, RET_V_DIM), F32)],
        compiler_params=_cparams(("parallel", "arbitrary")),
        name="retention",
    )(proj_sb, proj_sb, proj_sb, proj_sb, proj_sb, proj_sb, cos_t, sin_t, qd_t, kd_t, mask, cd)


def retention_tables(s):
    L = RET_SUPER
    half = RET_QK_DIM // 2
    inv_freq = ROPE_BASE ** (-jnp.arange(half, dtype=F32) / half)
    ang = jnp.arange(s, dtype=F32)[:, None] * inv_freq[None, :]
    cos, sin = jnp.cos(ang), jnp.sin(ang)
    cos_t = jnp.tile(jnp.concatenate([cos, cos], -1), (1, RET_HEADS))
    sin_t = jnp.tile(jnp.concatenate([-sin, sin], -1), (1, RET_HEADS))
    log_gamma = jnp.log1p(-(2.0 ** (-5.0 - jnp.arange(RET_HEADS, dtype=F32))))
    pos = jnp.arange(L, dtype=F32)
    qd = jnp.exp(log_gamma[None, :] * (pos + 1.0)[:, None])
    kd = jnp.exp(log_gamma[None, :] * (L - 1.0 - pos)[:, None])
    qd_t = jnp.repeat(qd, RET_QK_DIM, axis=1)
    kd_t = jnp.repeat(kd, RET_QK_DIM, axis=1)
    chunk_id = jnp.arange(L) // CHUNK
    visible = (chunk_id[None, :] <= chunk_id[:, None]).astype(F32)
    mask = jnp.exp(log_gamma[:, None, None] * jnp.abs(pos[:, None] - pos[None, :])) * visible[None]
    cd = jnp.broadcast_to(jnp.exp(log_gamma * L)[:, None, None], (RET_HEADS, 1, RET_V_DIM))
    return cos_t, sin_t, qd_t, kd_t, mask, cd


def _s5_param_kernel(lre_ref, lim_ref, ls_ref, bre_ref, bim_ref, are_ref, aim_ref, bbre_ref, bbim_ref):
    lam_re = jnp.minimum(lre_ref[...], -1e-4)
    lam_im = lim_ref[...]
    step = jnp.exp(ls_ref[...])
    mag = jnp.exp(lam_re * step)
    ang = lam_im * step
    ab_re = mag * jnp.cos(ang)
    ab_im = mag * jnp.sin(ang)
    den = lam_re * lam_re + lam_im * lam_im
    n_re = ab_re - 1.0
    zc_re = (n_re * lam_re + ab_im * lam_im) / den
    zc_im = (ab_im * lam_re - n_re * lam_im) / den
    are_ref[...] = ab_re
    aim_ref[...] = ab_im
    b_re = bre_ref[...]
    b_im = bim_ref[...]
    bbre_ref[...] = zc_re * b_re - zc_im * b_im
    bbim_ref[...] = zc_re * b_im + zc_im * b_re


def s5_params(lam_re, lam_im, log_step, b_re, b_im):
    g, n = lam_re.shape
    c = b_re.shape[-1]
    outs = pl.pallas_call(
        _s5_param_kernel,
        out_shape=[jax.ShapeDtypeStruct((g, 1, n), F32), jax.ShapeDtypeStruct((g, 1, n), F32),
                   jax.ShapeDtypeStruct((g, c, n), F32), jax.ShapeDtypeStruct((g, c, n), F32)],
        name="s5_params",
    )(lam_re.reshape(g, 1, n), lam_im.reshape(g, 1, n), log_step.reshape(g, 1, 1),
      jnp.swapaxes(b_re, 1, 2), jnp.swapaxes(b_im, 1, 2))
    return outs


def _block_diag(x):
    ns, gl, r, c = x.shape
    eye = jnp.eye(gl, dtype=x.dtype)
    return jnp.einsum('sgrc,gh->sgrhc', x, eye).reshape(ns, gl * r, gl * c)


def s5_matrices(a_re, a_im, bb_re, bb_im, c_re, c_im):
    gl = S5_COLS // S5_GROUP_CH
    ns = S5_NSLICE
    bre = _block_diag(bb_re.reshape(ns, gl, S5_GROUP_CH, S5_STATE))
    bim = _block_diag(bb_im.reshape(ns, gl, S5_GROUP_CH, S5_STATE))
    bq = jnp.concatenate([bre, bim], axis=-1).astype(BF16)
    cre = _block_diag(jnp.swapaxes(c_re, 1, 2).reshape(ns, gl, S5_STATE, S5_GROUP_CH))
    cim = _block_diag(jnp.swapaxes(c_im, 1, 2).reshape(ns, gl, S5_STATE, S5_GROUP_CH))
    cq = jnp.concatenate([cre, -cim], axis=1).astype(BF16)
    are = a_re.reshape(ns, 1, S5_SLICE_STATE)
    aim = a_im.reshape(ns, 1, S5_SLICE_STATE)
    return bq, cq, are, aim


def _s5_kernel(*refs, bsz, tt):
    u_refs = refs[:bsz]
    bq_ref, cq_ref, are_ref, aim_ref, d_ref, y_ref, us_ref, ys_ref, bu_ref, st_ref = refs[bsz:]

    @pl.when(pl.program_id(0) == 0)
    def _():
        st_ref[...] = jnp.zeros_like(st_ref)

    for b in range(bsz):
        ub = u_refs[b][...].astype(F32)
        for cs in range(S5_NSLICE):
            us_ref[cs, pl.ds(b, tt, stride=bsz), :] = ub[:, cs * S5_COLS:(cs + 1) * S5_COLS]

    ns2 = S5_SLICE_STATE
    for cs in range(S5_NSLICE):
        cols = slice(cs * S5_COLS, (cs + 1) * S5_COLS)
        uf = us_ref[cs]
        bu_ref[cs] = jnp.dot(uf.astype(BF16), bq_ref[cs], preferred_element_type=F32)
        a_re = jnp.broadcast_to(are_ref[cs], (bsz, ns2))
        a_im = jnp.broadcast_to(aim_ref[cs], (bsz, ns2))

        def step(t, carry):
            h_re, h_im = carry
            rows = pl.ds(pl.multiple_of(t * bsz, bsz), bsz)
            n_re = a_re * h_re - a_im * h_im + bu_ref[cs, rows, 0:ns2]
            n_im = a_re * h_im + a_im * h_re + bu_ref[cs, rows, ns2:2 * ns2]
            bu_ref[cs, rows, 0:ns2] = n_re
            bu_ref[cs, rows, ns2:2 * ns2] = n_im
            return n_re, n_im

        h_re, h_im = lax.fori_loop(0, tt, step, (st_ref[cs, 0], st_ref[cs, 1]), unroll=True)
        st_ref[cs, 0] = h_re
        st_ref[cs, 1] = h_im
        y = jnp.dot(bu_ref[cs].astype(BF16), cq_ref[cs], preferred_element_type=F32)
        ys_ref[cs] = y + d_ref[:, cols] * uf

    for b in range(bsz):
        for cs in range(S5_NSLICE):
            lo = b * S5_WIDTH + cs * S5_COLS
            y_ref[:, lo:lo + S5_COLS] = ys_ref[cs, pl.ds(b, tt, stride=bsz), :]


def s5_scan(proj_sb, mats, d_skip, bsz, s):
    bq, cq, are, aim = mats
    tt = S5_TT
    rows = tt * bsz
    nb = N_PROJ_BLK
    kern = functools.partial(_s5_kernel, bsz=bsz, tt=tt)
    u_specs = [pl.BlockSpec((tt, PROJ_BLK), lambda i, b=b: (i, b * nb + 6)) for b in range(bsz)]
    return pl.pallas_call(
        kern,
        grid=(s // tt,),
        in_specs=u_specs + [
                  pl.BlockSpec(bq.shape, lambda i: (0, 0, 0)),
                  pl.BlockSpec(cq.shape, lambda i: (0, 0, 0)),
                  pl.BlockSpec(are.shape, lambda i: (0, 0, 0)),
                  pl.BlockSpec(aim.shape, lambda i: (0, 0, 0)),
                  pl.BlockSpec((1, S5_WIDTH), lambda i: (0, 0))],
        out_specs=pl.BlockSpec((tt, bsz * S5_WIDTH), lambda i: (i, 0)),
        out_shape=jax.ShapeDtypeStruct((s, bsz * S5_WIDTH), F32),
        scratch_shapes=[pltpu.VMEM((S5_NSLICE, rows, S5_COLS), F32),
                        pltpu.VMEM((S5_NSLICE, rows, S5_COLS), F32),
                        pltpu.VMEM((S5_NSLICE, rows, 2 * S5_SLICE_STATE), F32),
                        pltpu.VMEM((S5_NSLICE, 2, bsz, S5_SLICE_STATE), F32)],
        compiler_params=_cparams(("arbitrary",)),
        name="s5_scan",
    )(*([proj_sb] * bsz), bq, cq, are, aim, d_skip.reshape(1, S5_WIDTH))


def _gelu_tanh(x):
    c = math.sqrt(2.0 / math.pi)
    return 0.5 * x * (1.0 + jnp.tanh(c * (x + 0.044715 * (x * x * x))))


def _route(logits):
    lane = lax.broadcasted_iota(jnp.int32, logits.shape, 1)
    neg = jnp.float32(-jnp.inf)
    big = jnp.int32(1 << 20)
    is_g = lane < MOE_GROUPS
    lg = jnp.where(is_g, logits, neg)
    mg = jnp.max(lg, axis=-1, keepdims=True)
    sg = jnp.sum(jnp.where(is_g, jnp.exp(lg - mg), 0.0), axis=-1, keepdims=True)
    g_top = 1.0 / sg
    g_idx = jnp.min(jnp.where(lg == mg, lane, big), axis=-1, keepdims=True)
    lo = MOE_GROUPS + g_idx * EXPERTS_PER_GROUP
    in_grp = (lane >= lo) & (lane < lo + EXPERTS_PER_GROUP)
    le = jnp.where(in_grp, logits, neg)
    m1 = jnp.max(le, axis=-1, keepdims=True)
    se = jnp.sum(jnp.where(in_grp, jnp.exp(le - m1), 0.0), axis=-1, keepdims=True)
    i1 = jnp.min(jnp.where(le == m1, lane, big), axis=-1, keepdims=True)
    le2 = jnp.where(lane == i1, neg, le)
    m2 = jnp.max(le2, axis=-1, keepdims=True)
    i2 = jnp.min(jnp.where(le2 == m2, lane, big), axis=-1, keepdims=True)
    p1 = 1.0 / se
    p2 = jnp.exp(m2 - m1) / se
    tot = p1 + p2
    w1 = g_top * (p1 / tot)
    w2 = g_top * (p2 / tot)
    return lane, i1, i2, w1, w2


def _mix_kernel(ro_ref, y_ref, gr0_ref, gr1_ref, gs0_ref, gs1_ref, h_ref,
                wglu_ref, wbr_ref, wbs_ref, wout_ref, g_ref, b_ref, wr_ref, br_ref, tri_ref,
                h1_ref, route_ref, cnt_ref, blk_ref, run_ref):
    @pl.when((pl.program_id(0) == 0) & (pl.program_id(1) == 0))
    def _():
        run_ref[...] = jnp.zeros_like(run_ref)

    z = _gelu_tanh(y_ref[...])
    zg = jnp.dot(z.astype(BF16), wglu_ref[...], preferred_element_type=F32)
    zz = (z * jax.nn.sigmoid(zg)).astype(BF16)
    s5b = jnp.dot(zz, wbs_ref[...], preferred_element_type=F32)
    rb = jnp.dot(ro_ref[...], wbr_ref[...], preferred_element_type=F32)
    gr = jnp.concatenate([gr0_ref[...], gr1_ref[...]], axis=-1).astype(F32)
    gs = jnp.concatenate([gs0_ref[...], gs1_ref[...]], axis=-1).astype(F32)
    merged = jax.nn.sigmoid(gr) * rb + jax.nn.sigmoid(gs) * s5b
    mix = jnp.dot(merged.astype(BF16), wout_ref[...], preferred_element_type=F32)
    h1 = _layer_norm(DN_ALPHA * h_ref[...] + mix, g_ref[...], b_ref[...])
    h1_ref[...] = h1
    h_hi = h1.astype(BF16)
    h_lo = (h1 - h_hi.astype(F32)).astype(BF16)
    both = jnp.dot(h_hi, wr_ref[...], preferred_element_type=F32)
    logits = (both[:, :ROUTE_LANES] + both[:, ROUTE_LANES:]
              + jnp.dot(h_lo, wr_ref[:, :ROUTE_LANES], preferred_element_type=F32)) + br_ref[...]
    lane, i1, i2, w1, w2 = _route(logits)
    oh1 = lane == i1
    oh2 = lane == i2
    oh = jnp.where(oh1 | oh2, 1.0, 0.0)
    rank_in_blk = jnp.dot(tri_ref[...], oh.astype(BF16), preferred_element_type=F32)
    blk_cnt = jnp.broadcast_to(jnp.sum(oh, axis=0, keepdims=True), run_ref.shape)
    lane8 = lax.broadcasted_iota(jnp.int32, run_ref.shape, 1)
    incl = blk_cnt
    shift = 1
    while shift < ROUTE_LANES:
        incl = incl + jnp.where(lane8 >= shift, pltpu.roll(incl, shift, 1), 0.0)
        shift *= 2
    lstart = incl - blk_cnt
    pos = lstart[0:1, :] + rank_in_blk
    l1 = jnp.sum(jnp.where(oh1, pos, 0.0), axis=-1, keepdims=True)
    l2 = jnp.sum(jnp.where(oh2, pos, 0.0), axis=-1, keepdims=True)
    run_before = run_ref[...]
    row8 = lax.broadcasted_iota(jnp.int32, run_ref.shape, 0)
    blk_ref[...] = jnp.where(row8 == 0, run_before,
                             jnp.where(row8 == 1, blk_cnt, jnp.where(row8 == 2, lstart, 0.0)))
    cnt = run_before + blk_cnt
    run_ref[...] = cnt
    cnt_ref[...] = cnt
    vals = (i1.astype(F32) - MOE_GROUPS, i2.astype(F32) - MOE_GROUPS, w1, w2, l1, l2)
    route = jnp.zeros(logits.shape, F32)
    for k, v in enumerate(vals):
        route = jnp.where(lane == k, v, route)
    route_ref[...] = route


def mix_out(ro, y, proj, h, wglu, wbr, wbs, wout, g, b, wr, br, bsz, ts=MOE_TS):
    s = h.shape[0]
    d = D_MODEL
    nb = N_PROJ_BLK

    def const(a):
        return pl.BlockSpec(a.shape, lambda bi, i: (0,) * a.ndim)

    def pspec(col):
        return pl.BlockSpec((ts, PROJ_BLK), lambda bi, i, col=col: (i, bi * nb + col))

    def tok(w):
        return pl.BlockSpec((ts, w), lambda bi, i: (i, bi))

    g2, b2 = g.reshape(1, d), b.reshape(1, d)
    idx = jnp.arange(ts)
    tri = (idx[None, :] < idx[:, None]).astype(BF16)
    return pl.pallas_call(
        _mix_kernel,
        grid=(bsz, s // ts),
        in_specs=[tok(RET_V), tok(S5_WIDTH), pspec(7), pspec(8), pspec(9), pspec(10), tok(d),
                  const(wglu), const(wbr), const(wbs), const(wout), const(g2), const(b2),
                  const(wr), const(br), const(tri)],
        out_specs=[tok(d), tok(ROUTE_LANES), pl.BlockSpec((8, ROUTE_LANES), lambda bi, i: (0, 0)),
                   pl.BlockSpec((None, 8, ROUTE_LANES), lambda bi, i: (bi * (s // ts) + i, 0, 0))],
        out_shape=[jax.ShapeDtypeStruct((s, bsz * d), F32),
                   jax.ShapeDtypeStruct((s, bsz * ROUTE_LANES), F32),
                   jax.ShapeDtypeStruct((8, ROUTE_LANES), F32),
                   jax.ShapeDtypeStruct((bsz * (s // ts), 8, ROUTE_LANES), F32)],
        scratch_shapes=[pltpu.VMEM((8, ROUTE_LANES), F32)],
        compiler_params=_cparams(("arbitrary", "arbitrary")),
        name="mix_out",
    )(ro, y, proj, proj, proj, proj, h, wglu, wbr, wbs, wout, g2, b2, wr, br, tri)


def moe_rows(t):
    return TOP_K * t + (N_EXPERTS + 1) * MOE_TM


def moe_plan(route, cnt, blk, bsz, s):
    t = bsz * s
    rec = route.reshape(s, bsz, ROUTE_LANES)[:, :, 4:6]
    rec = jnp.transpose(rec, (1, 0, 2)).reshape(t, 2).astype(jnp.int32)
    experts = slice(MOE_GROUPS, MOE_GROUPS + N_EXPERTS)
    counts = cnt[0, experts].astype(jnp.int32)
    padded = ((counts + MOE_TM - 1) // MOE_TM) * MOE_TM
    ends = jnp.cumsum(padded)
    off = ends - padded
    run_start = (off[None, :] + blk[:, 0, experts].astype(jnp.int32)).reshape(-1)
    run_len = blk[:, 1, experts].astype(jnp.int32).reshape(-1)
    run_local = blk[:, 2, experts].astype(jnp.int32).reshape(-1)
    pad_start = off + counts
    n_tiles = moe_rows(t) // MOE_TM
    n_used = (ends[-1:] // MOE_TM).astype(jnp.int32)
    tile_start = jnp.arange(n_tiles, dtype=jnp.int32) * MOE_TM
    tile_expert = jnp.sum((ends[None, :] <= tile_start[:, None]).astype(jnp.int32), axis=1)
    tile_expert = jnp.minimum(tile_expert, N_EXPERTS - 1)
    return (rec[:, 0], rec[:, 1], run_start, run_len, run_local), pad_start, tile_expert, n_used


def _token_rows(start, n):
    return pl.ds(pl.multiple_of(start * ROW_TILE, ROW_TILE), n * ROW_TILE)


def _run_copies(blk, run_refs, make_copy):
    run_start_ref, run_len_ref, run_local_ref = run_refs
    for e in range(N_EXPERTS):
        idx = blk * N_EXPERTS + e
        n = run_len_ref[idx]
        g0 = run_start_ref[idx]
        l0 = run_local_ref[idx]
        for bit in reversed(range(RUN_BITS)):
            size = 1 << bit
            done = (n >> (bit + 1)) << (bit + 1)

            @pl.when((n & size) != 0)
            def _(size=size, done=done, g0=g0, l0=l0):
                make_copy(l0 + done, g0 + done, size).start()


def _to_token_tiles(ref, x):
    n = x.shape[0]
    for j in range(ROW_TILE):
        ref[pl.ds(j, n, stride=ROW_TILE), :] = x[:, j * LANES:(j + 1) * LANES]


def _from_token_tiles(ref, n):
    return jnp.concatenate([ref[pl.ds(j, n, stride=ROW_TILE), :] for j in range(ROW_TILE)], axis=-1)


def _dispatch_kernel(lp1_ref, lp2_ref, run_start_ref, run_len_ref, run_local_ref, pad_ref, nu_ref,
                     x_ref, xs_ref, xt_ref, srt_ref, zero_ref, sem, zsem, *, ts):
    run_refs = (run_start_ref, run_len_ref, run_local_ref)
    nsteps = pl.num_programs(0) * pl.num_programs(1)
    step = pl.program_id(0) * pl.num_programs(1) + pl.program_id(1)
    tile_rows = MOE_TM * ROW_TILE

    @pl.when(step == 0)
    def _():
        zero_ref[...] = jnp.zeros_like(zero_ref)

        def zero_copy(start):
            start = pl.multiple_of(start * ROW_TILE, ROW_TILE)
            return pltpu.make_async_copy(zero_ref, xs_ref.at[pl.ds(start, tile_rows)], zsem)

        for e in range(N_EXPERTS):
            zero_copy(pad_ref[e]).start()
        for e in range(N_EXPERTS):
            zero_copy(pad_ref[e]).wait()

        def zero_tile(r, c):
            cp = zero_copy(r * MOE_TM)
            cp.start()
            cp.wait()
            return c

        lax.fori_loop(nu_ref[0], xs_ref.shape[0] // tile_rows, zero_tile, 0)

    slot = step % 2
    _to_token_tiles(xt_ref, x_ref[...])
    base = step * ts

    def place(i, c):
        tile = xt_ref[_token_rows(i, 1), :]
        srt_ref[slot, _token_rows(lp1_ref[base + i], 1), :] = tile
        srt_ref[slot, _token_rows(lp2_ref[base + i], 1), :] = tile
        return c

    lax.fori_loop(0, ts, place, 0, unroll=8)

    def run_copy(local_start, sorted_start, size):
        return pltpu.make_async_copy(srt_ref.at[slot, _token_rows(local_start, size)],
                                     xs_ref.at[_token_rows(sorted_start, size)], sem.at[slot])

    _run_copies(step, run_refs, run_copy)

    def drain(sl):
        pltpu.make_async_copy(srt_ref.at[sl], xs_ref.at[_token_rows(0, TOP_K * ts)], sem.at[sl]).wait()

    @pl.when(step > 0)
    def _():
        drain(1 - slot)

    @pl.when(step == nsteps - 1)
    def _():
        drain(slot)


def moe_dispatch(h1, plan, pad_start, n_used, bsz, ts=MOE_TS):
    s = h1.shape[0]
    d = D_MODEL
    kern = functools.partial(_dispatch_kernel, ts=ts)
    grid_spec = pltpu.PrefetchScalarGridSpec(
        num_scalar_prefetch=len(plan) + 2,
        grid=(bsz, s // ts),
        in_specs=[pl.BlockSpec((ts, d), lambda bi, i, *_: (i, bi))],
        out_specs=pl.BlockSpec(memory_space=pl.ANY),
        scratch_shapes=[pltpu.VMEM((ts * ROW_TILE, LANES), F32),
                        pltpu.VMEM((2, TOP_K * ts * ROW_TILE, LANES), F32),
                        pltpu.VMEM((MOE_TM * ROW_TILE, LANES), F32),
                        pltpu.SemaphoreType.DMA((2,)), pltpu.SemaphoreType.DMA(())],
    )
    return pl.pallas_call(
        kern,
        grid_spec=grid_spec,
        out_shape=jax.ShapeDtypeStruct((moe_rows(bsz * s) * ROW_TILE, LANES), F32),
        compiler_params=_cparams(("arbitrary", "arbitrary")),
        name="moe_dispatch",
    )(*plan, pad_start, n_used, h1)


def _experts_kernel(te_ref, nu_ref, xs_ref, wg_ref, wu_ref, wd_ref, ys_ref, wgb_ref, wub_ref, wdb_ref):
    r = pl.program_id(0)

    @pl.when(r < nu_ref[0])
    def _():
        changed = (r == 0) | (te_ref[r] != te_ref[jnp.maximum(r - 1, 0)])

        @pl.when(changed)
        def _():
            wgb_ref[...] = wg_ref[...].astype(BF16)
            wub_ref[...] = wu_ref[...].astype(BF16)
            wdb_ref[...] = wd_ref[...].astype(BF16)

        x = _from_token_tiles(xs_ref, MOE_TM).astype(BF16)
        gate = jnp.dot(x, wgb_ref[...], preferred_element_type=F32)
        up = jnp.dot(x, wub_ref[...], preferred_element_type=F32)
        act = (gate * jax.nn.sigmoid(gate) * up).astype(BF16)
        _to_token_tiles(ys_ref, jnp.dot(act, wdb_ref[...], preferred_element_type=F32))

    @pl.when(r >= nu_ref[0])
    def _():
        ys_ref[...] = jnp.zeros_like(ys_ref)


def moe_experts(xs, tile_expert, n_used, wg, wu, wd, layer):
    d, f = wg.shape[-2:]
    rows = xs.shape[0] // ROW_TILE
    blk = MOE_TM * ROW_TILE

    def tile(r, te, nu):
        return jnp.minimum(r, nu[0] - 1)

    def wspec(a, b):
        return pl.BlockSpec((None, None, a, b), lambda r, te, nu: (layer, te[tile(r, te, nu)], 0, 0))

    grid_spec = pltpu.PrefetchScalarGridSpec(
        num_scalar_prefetch=2,
        grid=(rows // MOE_TM,),
        in_specs=[pl.BlockSpec((blk, LANES), lambda r, te, nu: (tile(r, te, nu), 0)),
                  wspec(d, f), wspec(d, f), wspec(f, d)],
        out_specs=pl.BlockSpec((blk, LANES), lambda r, te, nu: (r, 0)),
        scratch_shapes=[pltpu.VMEM((d, f), BF16), pltpu.VMEM((d, f), BF16), pltpu.VMEM((f, d), BF16)],
    )
    return pl.pallas_call(
        _experts_kernel,
        grid_spec=grid_spec,
        out_shape=jax.ShapeDtypeStruct(xs.shape, F32),
        compiler_params=_cparams(("arbitrary",)),
        name="moe_experts",
    )(tile_expert, n_used, xs, wg, wu, wd)


def _combine_kernel(lp1_ref, lp2_ref, run_start_ref, run_len_ref, run_local_ref,
                    ys_ref, route_ref, h_ref, g_ref, b_ref, o_ref, ob_ref,
                    srt_ref, t1_ref, t2_ref, sem, *, ts):
    run_refs = (run_start_ref, run_len_ref, run_local_ref)
    nsteps = pl.num_programs(0) * pl.num_programs(1)
    step = pl.program_id(0) * pl.num_programs(1) + pl.program_id(1)

    def issue(blk, sl):
        def run_copy(local_start, sorted_start, size):
            return pltpu.make_async_copy(ys_ref.at[_token_rows(sorted_start, size)],
                                         srt_ref.at[sl, _token_rows(local_start, size)], sem.at[sl])
        _run_copies(blk, run_refs, run_copy)

    @pl.when(step == 0)
    def _():
        issue(0, 0)

    @pl.when(step + 1 < nsteps)
    def _():
        issue(step + 1, (step + 1) % 2)

    slot = step % 2
    pltpu.make_async_copy(ys_ref.at[_token_rows(0, TOP_K * ts)], srt_ref.at[slot], sem.at[slot]).wait()

    base = step * ts

    def pick(i, c):
        t1_ref[_token_rows(i, 1), :] = srt_ref[slot, _token_rows(lp1_ref[base + i], 1), :]
        t2_ref[_token_rows(i, 1), :] = srt_ref[slot, _token_rows(lp2_ref[base + i], 1), :]
        return c

    lax.fori_loop(0, ts, pick, 0, unroll=8)

    route = route_ref[...]
    w1 = route[:, 2:3]
    w2 = route[:, 3:4]
    ffn = w1 * _from_token_tiles(t1_ref, ts) + w2 * _from_token_tiles(t2_ref, ts)
    h2 = _layer_norm(DN_ALPHA * h_ref[...] + ffn, g_ref[...], b_ref[...])
    o_ref[...] = h2
    ob_ref[...] = h2.astype(BF16)


def moe_combine(ys, plan, route, h1, g, b, bsz, last, ts=MOE_TS):
    s = h1.shape[0]
    d = D_MODEL
    g2, b2 = g.reshape(1, d), b.reshape(1, d)
    if last:
        o_spec = pl.BlockSpec((None, ts, d), lambda bi, i, *_: (bi, i, 0))
        o_shape = jax.ShapeDtypeStruct((bsz, s, d), F32)
    else:
        o_spec = pl.BlockSpec((ts, d), lambda bi, i, *_: (i, bi))
        o_shape = jax.ShapeDtypeStruct((s, bsz * d), F32)
    kern = functools.partial(_combine_kernel, ts=ts)
    grid_spec = pltpu.PrefetchScalarGridSpec(
        num_scalar_prefetch=len(plan),
        grid=(bsz, s // ts),
        in_specs=[pl.BlockSpec(memory_space=pl.ANY),
                  pl.BlockSpec((ts, ROUTE_LANES), lambda bi, i, *_: (i, bi)),
                  pl.BlockSpec((ts, d), lambda bi, i, *_: (i, bi)),
                  pl.BlockSpec((1, d), lambda bi, i, *_: (0, 0)),
                  pl.BlockSpec((1, d), lambda bi, i, *_: (0, 0))],
        out_specs=[o_spec, pl.BlockSpec((ts, d), lambda bi, i, *_: (i, bi))],
        scratch_shapes=[pltpu.VMEM((2, TOP_K * ts * ROW_TILE, LANES), F32),
                        pltpu.VMEM((ts * ROW_TILE, LANES), F32), pltpu.VMEM((ts * ROW_TILE, LANES), F32),
                        pltpu.SemaphoreType.DMA((2,))],
    )
    return pl.pallas_call(
        kern,
        grid_spec=grid_spec,
        out_shape=[o_shape, jax.ShapeDtypeStruct((s, bsz * d), BF16)],
        compiler_params=_cparams(("arbitrary", "arbitrary")),
        name="moe_combine",
    )(*plan, ys, route, h1, g2, b2)


def kernel(x, ln_in_g, ln_in_b, w_in, s5_lambda_re, s5_lambda_im, s5_log_step, s5_b_re, s5_b_im,
           s5_c_re, s5_c_im, s5_d, w_glu, w_branch_ret, w_branch_s5, w_out, ln_mix_g, ln_mix_b,
           w_router_group, b_router_group, w_router_expert, b_router_expert, w_exp_gate, w_exp_up,
           w_exp_down, ln_ffn_g, ln_ffn_b):
    bsz, s, d = x.shape
    t = bsz * s
    depth = w_in.shape[0]
    tabs = retention_tables(s)
    h_sb, hb_sb = ln_in(x, ln_in_g, ln_in_b)
    out = None
    for l in range(depth):
        proj = in_proj(hb_sb, w_in, l, bsz)
        ro = retention(proj, tabs, bsz, s)
        a_re, a_im, bb_re, bb_im = s5_params(s5_lambda_re[l], s5_lambda_im[l], s5_log_step[l],
                                             s5_b_re[l], s5_b_im[l])
        mats = s5_matrices(a_re, a_im, bb_re, bb_im, s5_c_re[l], s5_c_im[l])
        y = s5_scan(proj, mats, s5_d[l], bsz, s)
        pad = ROUTE_LANES - MOE_GROUPS - N_EXPERTS
        wr = jnp.concatenate([w_router_group[l], w_router_expert[l], jnp.zeros((d, pad), F32)], axis=1)
        wr_hi = wr.astype(BF16)
        wr = jnp.concatenate([wr_hi, (wr - wr_hi.astype(F32)).astype(BF16)], axis=1)
        br = jnp.concatenate([b_router_group[l], b_router_expert[l], jnp.zeros((pad,), F32)]).reshape(1, -1)
        h1, route, cnt, blk = mix_out(ro, y, proj, h_sb,
                                 w_glu[l].astype(BF16), w_branch_ret[l].astype(BF16),
                                 w_branch_s5[l].astype(BF16), w_out[l].astype(BF16),
                                 ln_mix_g[l], ln_mix_b[l], wr, br, bsz)
        last = l == depth - 1
        plan, pad_start, tile_expert, n_used = moe_plan(route, cnt, blk, bsz, s)
        xs = moe_dispatch(h1, plan, pad_start, n_used, bsz)
        ys = moe_experts(xs, tile_expert, n_used, w_exp_gate, w_exp_up, w_exp_down, l)
        h2, h2b = moe_combine(ys, plan, route, h1, ln_ffn_g[l], ln_ffn_b[l], bsz, last)
        if last:
            out = h2
        else:
            h_sb, hb_sb = h2, h2b
    return out
```

```python
import functools
import math

import jax
import jax.numpy as jnp
from jax import lax
from jax.experimental import pallas as pl
from jax.experimental.pallas import tpu as pltpu

D_MODEL = 1024
CHUNK = 64
RET_HEADS = 8
RET_QK = 512
RET_V = 1024
RET_QK_DIM = 64
RET_V_DIM = 128
ROPE_BASE = 10000.0
S5_WIDTH = 512
S5_GROUP_CH = 16
S5_GROUPS = 32
S5_STATE = 64
MOE_GROUPS = 4
EXPERTS_PER_GROUP = 8
N_EXPERTS = 32
EXPERT_FF = 256
LN_EPS = 1e-5
HEAD_NORM_EPS = 1e-6
DEPTH = 2
DN_ALPHA = (2 * DEPTH) ** 0.25
IN_WIDTH = 2 * RET_QK + 2 * RET_V + S5_WIDTH + 2 * D_MODEL
PROJ_BLK = 512
N_PROJ_BLK = IN_WIDTH // PROJ_BLK

RET_SUPER = 256
S5_TT = 64
S5_COLS = 128
S5_NSLICE = S5_WIDTH // S5_COLS
S5_SLICE_STATE = (S5_COLS // S5_GROUP_CH) * S5_STATE
ROUTE_LANES = 128
TOP_K = 2
MOE_TM = 512
MOE_TS = 512
RUN_BITS = MOE_TS.bit_length()
LANES = 128
ROW_TILE = D_MODEL // LANES
VMEM_LIMIT = 56 * 1024 * 1024

F32 = jnp.float32
BF16 = jnp.bfloat16


def _cparams(sem):
    return pltpu.CompilerParams(dimension_semantics=sem, vmem_limit_bytes=VMEM_LIMIT)


def _layer_norm(x, g, b):
    mu = jnp.mean(x, axis=-1, keepdims=True)
    xc = x - mu
    var = jnp.mean(xc * xc, axis=-1, keepdims=True)
    return xc * lax.rsqrt(var + LN_EPS) * g + b


def _ln_in_kernel(x_ref, g_ref, b_ref, h_ref, hb_ref):
    h = _layer_norm(x_ref[...], g_ref[...], b_ref[...])
    h_ref[...] = h
    hb_ref[...] = h.astype(BF16)


def ln_in(x, g, b, ts=512):
    bsz, s, d = x.shape
    return pl.pallas_call(
        _ln_in_kernel,
        grid=(bsz, s // ts),
        in_specs=[pl.BlockSpec((None, ts, d), lambda bi, si: (bi, si, 0)),
                  pl.BlockSpec((1, d), lambda bi, si: (0, 0)),
                  pl.BlockSpec((1, d), lambda bi, si: (0, 0))],
        out_specs=[pl.BlockSpec((ts, d), lambda bi, si: (si, bi)),
                   pl.BlockSpec((ts, d), lambda bi, si: (si, bi))],
        out_shape=[jax.ShapeDtypeStruct((s, bsz * d), F32),
                   jax.ShapeDtypeStruct((s, bsz * d), BF16)],
        compiler_params=_cparams(("parallel", "parallel")),
        name="ln_in",
    )(x, g.reshape(1, d), b.reshape(1, d))


def _inproj_kernel(h_ref, w_ref, o_ref):
    w = w_ref[...].astype(BF16)
    o_ref[...] = jnp.dot(h_ref[...], w, preferred_element_type=F32).astype(BF16)


def in_proj(hb_sb, w, layer, bsz, ts=2048):
    s = hb_sb.shape[0]
    d, n = w.shape[-2:]
    nb = n // PROJ_BLK
    return pl.pallas_call(
        _inproj_kernel,
        grid=(bsz, s // ts, nb),
        in_specs=[pl.BlockSpec((ts, d), lambda b, i, j: (i, b)),
                  pl.BlockSpec((None, d, PROJ_BLK), lambda b, i, j: (layer, 0, j))],
        out_specs=pl.BlockSpec((ts, PROJ_BLK), lambda b, i, j: (i, b * nb + j)),
        out_shape=jax.ShapeDtypeStruct((s, bsz * n), BF16),
        compiler_params=_cparams(("parallel", "parallel", "arbitrary")),
        name="in_proj",
    )(hb_sb, w)


def _swap_halves(x):
    lane = lax.broadcasted_iota(jnp.int32, x.shape, 1)
    first = (lane % RET_QK_DIM) < (RET_QK_DIM // 2)
    n = x.shape[1]
    return jnp.where(first, pltpu.roll(x, n - RET_QK_DIM // 2, 1), pltpu.roll(x, RET_QK_DIM // 2, 1))


def _retention_kernel(q_ref, k_ref, v0_ref, v1_ref, g0_ref, g1_ref, cos_ref, sin_ref,
                      qd_ref, kd_ref, mask_ref, cd_ref, o_ref, state_ref):
    @pl.when(pl.program_id(1) == 0)
    def _():
        state_ref[...] = jnp.zeros_like(state_ref)

    cos = cos_ref[...]
    sin = sin_ref[...]
    q = q_ref[...].astype(F32)
    k = k_ref[...].astype(F32)
    q = q * cos + _swap_halves(q) * sin
    k = k * cos + _swap_halves(k) * sin
    qb = q.astype(BF16)
    kb = k.astype(BF16)
    qdb = (q * qd_ref[...]).astype(BF16)
    kdb = (k * kd_ref[...]).astype(BF16)
    for hd in range(RET_HEADS):
        qs = slice(hd * RET_QK_DIM, (hd + 1) * RET_QK_DIM)
        half, off = divmod(hd * RET_V_DIM, PROJ_BLK)
        vs = slice(off, off + RET_V_DIM)
        vh = (v0_ref, v1_ref)[half][:, vs]
        gh = (g0_ref, g1_ref)[half][:, vs].astype(F32)
        sc = lax.dot_general(qb[:, qs], kb[:, qs], (((1,), (1,)), ((), ())),
                             preferred_element_type=F32)
        sc = (sc * mask_ref[hd]).astype(BF16)
        st = state_ref[hd]
        o = jnp.dot(sc, vh, preferred_element_type=F32)
        o = o + jnp.dot(qdb[:, qs], st.astype(BF16), preferred_element_type=F32)
        kv = lax.dot_general(kdb[:, qs], vh, (((0,), (0,)), ((), ())),
                             preferred_element_type=F32)
        state_ref[hd] = st * cd_ref[hd] + kv
        mu = jnp.mean(o, axis=-1, keepdims=True)
        oc = o - mu
        var = jnp.mean(oc * oc, axis=-1, keepdims=True)
        on = oc * lax.rsqrt(var + HEAD_NORM_EPS)
        o_ref[:, hd * RET_V_DIM:(hd + 1) * RET_V_DIM] = (gh * jax.nn.sigmoid(gh) * on).astype(BF16)


def retention(proj_sb, tabs, bsz, s):
    cos_t, sin_t, qd_t, kd_t, mask, cd = tabs
    L = RET_SUPER
    nb = N_PROJ_BLK

    def pspec(col):
        return pl.BlockSpec((L, PROJ_BLK), lambda b, i, col=col: (i, b * nb + col))

    full2 = pl.BlockSpec((L, RET_QK), lambda b, i: (0, 0))
    return pl.pallas_call(
        _retention_kernel,
        grid=(bsz, s // L),
        in_specs=[pspec(0), pspec(1), pspec(2), pspec(3), pspec(4), pspec(5),
                  pl.BlockSpec((L, RET_QK), lambda b, i: (i, 0)),
                  pl.BlockSpec((L, RET_QK), lambda b, i: (i, 0)),
                  full2, full2,
                  pl.BlockSpec((RET_HEADS, L, L), lambda b, i: (0, 0, 0)),
                  pl.BlockSpec((RET_HEADS, 1, RET_V_DIM), lambda b, i: (0, 0, 0))],
        out_specs=pl.BlockSpec((L, RET_V), lambda b, i: (i, b)),
        out_shape=jax.ShapeDtypeStruct((s, bsz * RET_V), BF16),
        scratch_shapes=[pltpu.VMEM((RET_HEADS, RET_QK_DIM, RET_V_DIM), F32)],
        compiler_params=_cparams(("parallel", "arbitrary")),
        name="retention",
    )(proj_sb, proj_sb, proj_sb, proj_sb, proj_sb, proj_sb, cos_t, sin_t, qd_t, kd_t, mask, cd)


def retention_tables(s):
    L = RET_SUPER
    half = RET_QK_DIM // 2
    inv_freq = ROPE_BASE ** (-jnp.arange(half, dtype=F32) / half)
    ang = jnp.arange(s, dtype=F32)[:, None] * inv_freq[None, :]
    cos, sin = jnp.cos(ang), jnp.sin(ang)
    cos_t = jnp.tile(jnp.concatenate([cos, cos], -1), (1, RET_HEADS))
    sin_t = jnp.tile(jnp.concatenate([-sin, sin], -1), (1, RET_HEADS))
    log_gamma = jnp.log1p(-(2.0 ** (-5.0 - jnp.arange(RET_HEADS, dtype=F32))))
    pos = jnp.arange(L, dtype=F32)
    qd = jnp.exp(log_gamma[None, :] * (pos + 1.0)[:, None])
    k_scale = RET_QK_DIM ** -0.5
    kd = jnp.exp(log_gamma[None, :] * (L - 1.0 - pos)[:, None]) * k_scale
    qd_t = jnp.repeat(qd, RET_QK_DIM, axis=1)
    kd_t = jnp.repeat(kd, RET_QK_DIM, axis=1)
    chunk_id = jnp.arange(L) // CHUNK
    visible = (chunk_id[None, :] <= chunk_id[:, None]).astype(F32)
    mask = jnp.exp(log_gamma[:, None, None] * jnp.abs(pos[:, None] - pos[None, :])) * visible[None] * k_scale
    cd = jnp.broadcast_to(jnp.exp(log_gamma * L)[:, None, None], (RET_HEADS, 1, RET_V_DIM))
    return cos_t, sin_t, qd_t, kd_t, mask, cd


def _s5_param_kernel(lre_ref, lim_ref, ls_ref, bre_ref, bim_ref, are_ref, aim_ref, bbre_ref, bbim_ref):
    lam_re = jnp.minimum(lre_ref[...], -1e-4)
    lam_im = lim_ref[...]
    step = jnp.exp(ls_ref[...])
    mag = jnp.exp(lam_re * step)
    ang = lam_im * step
    ab_re = mag * jnp.cos(ang)
    ab_im = mag * jnp.sin(ang)
    den = lam_re * lam_re + lam_im * lam_im
    n_re = ab_re - 1.0
    zc_re = (n_re * lam_re + ab_im * lam_im) / den
    zc_im = (ab_im * lam_re - n_re * lam_im) / den
    are_ref[...] = ab_re
    aim_ref[...] = ab_im
    b_re = bre_ref[...]
    b_im = bim_ref[...]
    bbre_ref[...] = zc_re * b_re - zc_im * b_im
    bbim_ref[...] = zc_re * b_im + zc_im * b_re


def s5_params(lam_re, lam_im, log_step, b_re, b_im):
    g, n = lam_re.shape
    c = b_re.shape[-1]
    outs = pl.pallas_call(
        _s5_param_kernel,
        out_shape=[jax.ShapeDtypeStruct((g, 1, n), F32), jax.ShapeDtypeStruct((g, 1, n), F32),
                   jax.ShapeDtypeStruct((g, c, n), F32), jax.ShapeDtypeStruct((g, c, n), F32)],
        name="s5_params",
    )(lam_re.reshape(g, 1, n), lam_im.reshape(g, 1, n), log_step.reshape(g, 1, 1),
      jnp.swapaxes(b_re, 1, 2), jnp.swapaxes(b_im, 1, 2))
    return outs


def _block_diag(x):
    ns, gl, r, c = x.shape
    eye = jnp.eye(gl, dtype=x.dtype)
    return jnp.einsum('sgrc,gh->sgrhc', x, eye).reshape(ns, gl * r, gl * c)


def s5_matrices(a_re, a_im, bb_re, bb_im, c_re, c_im):
    gl = S5_COLS // S5_GROUP_CH
    ns = S5_NSLICE
    bre = _block_diag(bb_re.reshape(ns, gl, S5_GROUP_CH, S5_STATE))
    bim = _block_diag(bb_im.reshape(ns, gl, S5_GROUP_CH, S5_STATE))
    bq = jnp.concatenate([bre, bim], axis=-1).astype(BF16)
    cre = _block_diag(jnp.swapaxes(c_re, 1, 2).reshape(ns, gl, S5_STATE, S5_GROUP_CH))
    cim = _block_diag(jnp.swapaxes(c_im, 1, 2).reshape(ns, gl, S5_STATE, S5_GROUP_CH))
    cq = jnp.concatenate([cre, -cim], axis=1).astype(BF16)
    are = a_re.reshape(ns, 1, S5_SLICE_STATE)
    aim = a_im.reshape(ns, 1, S5_SLICE_STATE)
    return bq, cq, are, aim


def _s5_kernel(*refs, bsz, tt):
    u_refs = refs[:bsz]
    bq_ref, cq_ref, are_ref, aim_ref, d_ref, y_ref, us_ref, ys_ref, bu_ref, st_ref = refs[bsz:]

    @pl.when(pl.program_id(0) == 0)
    def _():
        st_ref[...] = jnp.zeros_like(st_ref)

    for b in range(bsz):
        ub = u_refs[b][...].astype(F32)
        for cs in range(S5_NSLICE):
            us_ref[cs, pl.ds(b, tt, stride=bsz), :] = ub[:, cs * S5_COLS:(cs + 1) * S5_COLS]

    ns2 = S5_SLICE_STATE
    for cs in range(S5_NSLICE):
        cols = slice(cs * S5_COLS, (cs + 1) * S5_COLS)
        uf = us_ref[cs]
        bu_ref[cs] = jnp.dot(uf.astype(BF16), bq_ref[cs], preferred_element_type=F32)
        a_re = jnp.broadcast_to(are_ref[cs], (bsz, ns2))
        a_im = jnp.broadcast_to(aim_ref[cs], (bsz, ns2))

        def step(t, carry):
            h_re, h_im = carry
            rows = pl.ds(pl.multiple_of(t * bsz, bsz), bsz)
            n_re = a_re * h_re - a_im * h_im + bu_ref[cs, rows, 0:ns2]
            n_im = a_re * h_im + a_im * h_re + bu_ref[cs, rows, ns2:2 * ns2]
            bu_ref[cs, rows, 0:ns2] = n_re
            bu_ref[cs, rows, ns2:2 * ns2] = n_im
            return n_re, n_im

        h_re, h_im = lax.fori_loop(0, tt, step, (st_ref[cs, 0], st_ref[cs, 1]), unroll=True)
        st_ref[cs, 0] = h_re
        st_ref[cs, 1] = h_im
        y = jnp.dot(bu_ref[cs].astype(BF16), cq_ref[cs], preferred_element_type=F32)
        ys_ref[cs] = y + d_ref[:, cols] * uf

    for b in range(bsz):
        for cs in range(S5_NSLICE):
            lo = b * S5_WIDTH + cs * S5_COLS
            y_ref[:, lo:lo + S5_COLS] = ys_ref[cs, pl.ds(b, tt, stride=bsz), :]


def s5_scan(proj_sb, mats, d_skip, bsz, s):
    bq, cq, are, aim = mats
    tt = S5_TT
    rows = tt * bsz
    nb = N_PROJ_BLK
    kern = functools.partial(_s5_kernel, bsz=bsz, tt=tt)
    u_specs = [pl.BlockSpec((tt, PROJ_BLK), lambda i, b=b: (i, b * nb + 6)) for b in range(bsz)]
    return pl.pallas_call(
        kern,
        grid=(s // tt,),
        in_specs=u_specs + [
                  pl.BlockSpec(bq.shape, lambda i: (0, 0, 0)),
                  pl.BlockSpec(cq.shape, lambda i: (0, 0, 0)),
                  pl.BlockSpec(are.shape, lambda i: (0, 0, 0)),
                  pl.BlockSpec(aim.shape, lambda i: (0, 0, 0)),
                  pl.BlockSpec((1, S5_WIDTH), lambda i: (0, 0))],
        out_specs=pl.BlockSpec((tt, bsz * S5_WIDTH), lambda i: (i, 0)),
        out_shape=jax.ShapeDtypeStruct((s, bsz * S5_WIDTH), F32),
        scratch_shapes=[pltpu.VMEM((S5_NSLICE, rows, S5_COLS), F32),
                        pltpu.VMEM((S5_NSLICE, rows, S5_COLS), F32),
                        pltpu.VMEM((S5_NSLICE, rows, 2 * S5_SLICE_STATE), F32),
                        pltpu.VMEM((S5_NSLICE, 2, bsz, S5_SLICE_STATE), F32)],
        compiler_params=_cparams(("arbitrary",)),
        name="s5_scan",
    )(*([proj_sb] * bsz), bq, cq, are, aim, d_skip.reshape(1, S5_WIDTH))


def _gelu_tanh(x):
    c = math.sqrt(2.0 / math.pi)
    return 0.5 * x * (1.0 + jnp.tanh(c * (x + 0.044715 * (x * x * x))))


def _route(logits):
    lane = lax.broadcasted_iota(jnp.int32, logits.shape, 1)
    neg = jnp.float32(-jnp.inf)
    big = jnp.int32(1 << 20)
    is_g = lane < MOE_GROUPS
    lg = jnp.where(is_g, logits, neg)
    mg = jnp.max(lg, axis=-1, keepdims=True)
    sg = jnp.sum(jnp.where(is_g, jnp.exp(lg - mg), 0.0), axis=-1, keepdims=True)
    g_top = 1.0 / sg
    g_idx = jnp.min(jnp.where(lg == mg, lane, big), axis=-1, keepdims=True)
    lo = MOE_GROUPS + g_idx * EXPERTS_PER_GROUP
    in_grp = (lane >= lo) & (lane < lo + EXPERTS_PER_GROUP)
    le = jnp.where(in_grp, logits, neg)
    m1 = jnp.max(le, axis=-1, keepdims=True)
    se = jnp.sum(jnp.where(in_grp, jnp.exp(le - m1), 0.0), axis=-1, keepdims=True)
    i1 = jnp.min(jnp.where(le == m1, lane, big), axis=-1, keepdims=True)
    le2 = jnp.where(lane == i1, neg, le)
    m2 = jnp.max(le2, axis=-1, keepdims=True)
    i2 = jnp.min(jnp.where(le2 == m2, lane, big), axis=-1, keepdims=True)
    p1 = 1.0 / se
    p2 = jnp.exp(m2 - m1) / se
    tot = p1 + p2
    w1 = g_top * (p1 / tot)
    w2 = g_top * (p2 / tot)
    return lane, i1, i2, w1, w2


def _mix_kernel(ro_ref, y_ref, gr0_ref, gr1_ref, gs0_ref, gs1_ref, h_ref,
                wglu_ref, wbr_ref, wbs_ref, wout_ref, g_ref, b_ref, wr_ref, br_ref, tri_ref,
                h1_ref, route_ref, cnt_ref, blk_ref, run_ref):
    @pl.when((pl.program_id(0) == 0) & (pl.program_id(1) == 0))
    def _():
        run_ref[...] = jnp.zeros_like(run_ref)

    z = _gelu_tanh(y_ref[...])
    zg = jnp.dot(z.astype(BF16), wglu_ref[...], preferred_element_type=F32)
    zz = (z * jax.nn.sigmoid(zg)).astype(BF16)
    s5b = jnp.dot(zz, wbs_ref[...], preferred_element_type=F32)
    rb = jnp.dot(ro_ref[...], wbr_ref[...], preferred_element_type=F32)
    gr = jnp.concatenate([gr0_ref[...], gr1_ref[...]], axis=-1).astype(F32)
    gs = jnp.concatenate([gs0_ref[...], gs1_ref[...]], axis=-1).astype(F32)
    merged = jax.nn.sigmoid(gr) * rb + jax.nn.sigmoid(gs) * s5b
    mix = jnp.dot(merged.astype(BF16), wout_ref[...], preferred_element_type=F32)
    h1 = _layer_norm(DN_ALPHA * h_ref[...] + mix, g_ref[...], b_ref[...])
    h1_ref[...] = h1
    h_hi = h1.astype(BF16)
    h_lo = (h1 - h_hi.astype(F32)).astype(BF16)
    both = jnp.dot(h_hi, wr_ref[...], preferred_element_type=F32)
    logits = (both[:, :ROUTE_LANES] + both[:, ROUTE_LANES:]
              + jnp.dot(h_lo, wr_ref[:, :ROUTE_LANES], preferred_element_type=F32)) + br_ref[...]
    lane, i1, i2, w1, w2 = _route(logits)
    oh1 = lane == i1
    oh2 = lane == i2
    oh = jnp.where(oh1 | oh2, 1.0, 0.0)
    rank_in_blk = jnp.dot(tri_ref[...], oh.astype(BF16), preferred_element_type=F32)
    blk_cnt = jnp.broadcast_to(jnp.sum(oh, axis=0, keepdims=True), run_ref.shape)
    lane8 = lax.broadcasted_iota(jnp.int32, run_ref.shape, 1)
    incl = blk_cnt
    shift = 1
    while shift < ROUTE_LANES:
        incl = incl + jnp.where(lane8 >= shift, pltpu.roll(incl, shift, 1), 0.0)
        shift *= 2
    lstart = incl - blk_cnt
    pos = lstart[0:1, :] + rank_in_blk
    l1 = jnp.sum(jnp.where(oh1, pos, 0.0), axis=-1, keepdims=True)
    l2 = jnp.sum(jnp.where(oh2, pos, 0.0), axis=-1, keepdims=True)
    run_before = run_ref[...]
    row8 = lax.broadcasted_iota(jnp.int32, run_ref.shape, 0)
    blk_ref[...] = jnp.where(row8 == 0, run_before,
                             jnp.where(row8 == 1, blk_cnt, jnp.where(row8 == 2, lstart, 0.0)))
    cnt = run_before + blk_cnt
    run_ref[...] = cnt
    cnt_ref[...] = cnt
    vals = (i1.astype(F32) - MOE_GROUPS, i2.astype(F32) - MOE_GROUPS, w1, w2, l1, l2)
    route = jnp.zeros(logits.shape, F32)
    for k, v in enumerate(vals):
        route = jnp.where(lane == k, v, route)
    route_ref[...] = route


def mix_out(ro, y, proj, h, wglu, wbr, wbs, wout, g, b, wr, br, bsz, ts=MOE_TS):
    s = h.shape[0]
    d = D_MODEL
    nb = N_PROJ_BLK

    def const(a):
        return pl.BlockSpec(a.shape, lambda bi, i: (0,) * a.ndim)

    def pspec(col):
        return pl.BlockSpec((ts, PROJ_BLK), lambda bi, i, col=col: (i, bi * nb + col))

    def tok(w):
        return pl.BlockSpec((ts, w), lambda bi, i: (i, bi))

    g2, b2 = g.reshape(1, d), b.reshape(1, d)
    idx = jnp.arange(ts)
    tri = (idx[None, :] < idx[:, None]).astype(BF16)
    return pl.pallas_call(
        _mix_kernel,
        grid=(bsz, s // ts),
        in_specs=[tok(RET_V), tok(S5_WIDTH), pspec(7), pspec(8), pspec(9), pspec(10), tok(d),
                  const(wglu), const(wbr), const(wbs), const(wout), const(g2), const(b2),
                  const(wr), const(br), const(tri)],
        out_specs=[tok(d), tok(ROUTE_LANES), pl.BlockSpec((8, ROUTE_LANES), lambda bi, i: (0, 0)),
                   pl.BlockSpec((None, 8, ROUTE_LANES), lambda bi, i: (bi * (s // ts) + i, 0, 0))],
        out_shape=[jax.ShapeDtypeStruct((s, bsz * d), F32),
                   jax.ShapeDtypeStruct((s, bsz * ROUTE_LANES), F32),
                   jax.ShapeDtypeStruct((8, ROUTE_LANES), F32),
                   jax.ShapeDtypeStruct((bsz * (s // ts), 8, ROUTE_LANES), F32)],
        scratch_shapes=[pltpu.VMEM((8, ROUTE_LANES), F32)],
        compiler_params=_cparams(("arbitrary", "arbitrary")),
        name="mix_out",
    )(ro, y, proj, proj, proj, proj, h, wglu, wbr, wbs, wout, g2, b2, wr, br, tri)


def moe_rows(t):
    return TOP_K * t + (N_EXPERTS + 1) * MOE_TM


def moe_plan(route, cnt, blk, bsz, s):
    t = bsz * s
    rec = route.reshape(s, bsz, ROUTE_LANES)[:, :, 4:6]
    rec = jnp.transpose(rec, (1, 0, 2)).reshape(t, 2).astype(jnp.int32)
    experts = slice(MOE_GROUPS, MOE_GROUPS + N_EXPERTS)
    counts = cnt[0, experts].astype(jnp.int32)
    padded = ((counts + MOE_TM - 1) // MOE_TM) * MOE_TM
    ends = jnp.cumsum(padded)
    off = ends - padded
    run_start = (off[None, :] + blk[:, 0, experts].astype(jnp.int32)).reshape(-1)
    run_len = blk[:, 1, experts].astype(jnp.int32).reshape(-1)
    run_local = blk[:, 2, experts].astype(jnp.int32).reshape(-1)
    pad_start = off + counts
    n_tiles = moe_rows(t) // MOE_TM
    n_used = (ends[-1:] // MOE_TM).astype(jnp.int32)
    tile_start = jnp.arange(n_tiles, dtype=jnp.int32) * MOE_TM
    tile_expert = jnp.sum((ends[None, :] <= tile_start[:, None]).astype(jnp.int32), axis=1)
    tile_expert = jnp.minimum(tile_expert, N_EXPERTS - 1)
    return (rec[:, 0], rec[:, 1], run_start, run_len, run_local), pad_start, tile_expert, n_used


def _token_rows(start, n):
    return pl.ds(pl.multiple_of(start * ROW_TILE, ROW_TILE), n * ROW_TILE)


def _run_copies(blk, run_refs, make_copy):
    run_start_ref, run_len_ref, run_local_ref = run_refs
    for e in range(N_EXPERTS):
        idx = blk * N_EXPERTS + e
        n = run_len_ref[idx]
        g0 = run_start_ref[idx]
        l0 = run_local_ref[idx]
        for bit in reversed(range(RUN_BITS)):
            size = 1 << bit
            done = (n >> (bit + 1)) << (bit + 1)

            @pl.when((n & size) != 0)
            def _(size=size, done=done, g0=g0, l0=l0):
                make_copy(l0 + done, g0 + done, size).start()


def _to_token_tiles(ref, x):
    n = x.shape[0]
    for j in range(ROW_TILE):
        ref[pl.ds(j, n, stride=ROW_TILE), :] = x[:, j * LANES:(j + 1) * LANES]


def _from_token_tiles(ref, n):
    return jnp.concatenate([ref[pl.ds(j, n, stride=ROW_TILE), :] for j in range(ROW_TILE)], axis=-1)


def _dispatch_kernel(lp1_ref, lp2_ref, run_start_ref, run_len_ref, run_local_ref, pad_ref, nu_ref,
                     x_ref, xs_ref, xt_ref, srt_ref, zero_ref, sem, zsem, *, ts):
    run_refs = (run_start_ref, run_len_ref, run_local_ref)
    nsteps = pl.num_programs(0) * pl.num_programs(1)
    step = pl.program_id(0) * pl.num_programs(1) + pl.program_id(1)
    tile_rows = MOE_TM * ROW_TILE

    @pl.when(step == 0)
    def _():
        zero_ref[...] = jnp.zeros_like(zero_ref)

        def zero_copy(start):
            start = pl.multiple_of(start * ROW_TILE, ROW_TILE)
            return pltpu.make_async_copy(zero_ref, xs_ref.at[pl.ds(start, tile_rows)], zsem)

        for e in range(N_EXPERTS):
            zero_copy(pad_ref[e]).start()
        for e in range(N_EXPERTS):
            zero_copy(pad_ref[e]).wait()

        def zero_tile(r, c):
            cp = zero_copy(r * MOE_TM)
            cp.start()
            cp.wait()
            return c

        lax.fori_loop(nu_ref[0], xs_ref.shape[0] // tile_rows, zero_tile, 0)

    slot = step % 2
    _to_token_tiles(xt_ref, x_ref[...])
    base = step * ts

    def place(i, c):
        tile = xt_ref[_token_rows(i, 1), :]
        srt_ref[slot, _token_rows(lp1_ref[base + i], 1), :] = tile
        srt_ref[slot, _token_rows(lp2_ref[base + i], 1), :] = tile
        return c

    lax.fori_loop(0, ts, place, 0, unroll=8)

    def run_copy(local_start, sorted_start, size):
        return pltpu.make_async_copy(srt_ref.at[slot, _token_rows(local_start, size)],
                                     xs_ref.at[_token_rows(sorted_start, size)], sem.at[slot])

    _run_copies(step, run_refs, run_copy)

    def drain(sl):
        pltpu.make_async_copy(srt_ref.at[sl], xs_ref.at[_token_rows(0, TOP_K * ts)], sem.at[sl]).wait()

    @pl.when(step > 0)
    def _():
        drain(1 - slot)

    @pl.when(step == nsteps - 1)
    def _():
        drain(slot)


def moe_dispatch(h1, plan, pad_start, n_used, bsz, ts=MOE_TS):
    s = h1.shape[0]
    d = D_MODEL
    kern = functools.partial(_dispatch_kernel, ts=ts)
    grid_spec = pltpu.PrefetchScalarGridSpec(
        num_scalar_prefetch=len(plan) + 2,
        grid=(bsz, s // ts),
        in_specs=[pl.BlockSpec((ts, d), lambda bi, i, *_: (i, bi))],
        out_specs=pl.BlockSpec(memory_space=pl.ANY),
        scratch_shapes=[pltpu.VMEM((ts * ROW_TILE, LANES), F32),
                        pltpu.VMEM((2, TOP_K * ts * ROW_TILE, LANES), F32),
                        pltpu.VMEM((MOE_TM * ROW_TILE, LANES), F32),
                        pltpu.SemaphoreType.DMA((2,)), pltpu.SemaphoreType.DMA(())],
    )
    return pl.pallas_call(
        kern,
        grid_spec=grid_spec,
        out_shape=jax.ShapeDtypeStruct((moe_rows(bsz * s) * ROW_TILE, LANES), F32),
        compiler_params=_cparams(("arbitrary", "arbitrary")),
        name="moe_dispatch",
    )(*plan, pad_start, n_used, h1)


def _experts_kernel(te_ref, nu_ref, xs_ref, wg_ref, wu_ref, wd_ref, ys_ref, wgb_ref, wub_ref, wdb_ref):
    r = pl.program_id(0)

    @pl.when(r < nu_ref[0])
    def _():
        changed = (r == 0) | (te_ref[r] != te_ref[jnp.maximum(r - 1, 0)])

        @pl.when(changed)
        def _():
            wgb_ref[...] = wg_ref[...].astype(BF16)
            wub_ref[...] = wu_ref[...].astype(BF16)
            wdb_ref[...] = wd_ref[...].astype(BF16)

        x = _from_token_tiles(xs_ref, MOE_TM).astype(BF16)
        gate = jnp.dot(x, wgb_ref[...], preferred_element_type=F32)
        up = jnp.dot(x, wub_ref[...], preferred_element_type=F32)
        act = (gate * jax.nn.sigmoid(gate) * up).astype(BF16)
        _to_token_tiles(ys_ref, jnp.dot(act, wdb_ref[...], preferred_element_type=F32))

    @pl.when(r >= nu_ref[0])
    def _():
        ys_ref[...] = jnp.zeros_like(ys_ref)


def moe_experts(xs, tile_expert, n_used, wg, wu, wd, layer):
    d, f = wg.shape[-2:]
    rows = xs.shape[0] // ROW_TILE
    blk = MOE_TM * ROW_TILE

    def tile(r, te, nu):
        return jnp.minimum(r, nu[0] - 1)

    def wspec(a, b):
        return pl.BlockSpec((None, None, a, b), lambda r, te, nu: (layer, te[tile(r, te, nu)], 0, 0))

    grid_spec = pltpu.PrefetchScalarGridSpec(
        num_scalar_prefetch=2,
        grid=(rows // MOE_TM,),
        in_specs=[pl.BlockSpec((blk, LANES), lambda r, te, nu: (tile(r, te, nu), 0)),
                  wspec(d, f), wspec(d, f), wspec(f, d)],
        out_specs=pl.BlockSpec((blk, LANES), lambda r, te, nu: (r, 0)),
        scratch_shapes=[pltpu.VMEM((d, f), BF16), pltpu.VMEM((d, f), BF16), pltpu.VMEM((f, d), BF16)],
    )
    return pl.pallas_call(
        _experts_kernel,
        grid_spec=grid_spec,
        out_shape=jax.ShapeDtypeStruct(xs.shape, F32),
        compiler_params=_cparams(("arbitrary",)),
        name="moe_experts",
    )(tile_expert, n_used, xs, wg, wu, wd)


def _combine_kernel(lp1_ref, lp2_ref, run_start_ref, run_len_ref, run_local_ref,
                    ys_ref, route_ref, h_ref, g_ref, b_ref, *refs, ts):
    o_ref, maybe_ob_ref = refs[0], refs[1:-4]
    srt_ref, t1_ref, t2_ref, sem = refs[-4:]
    run_refs = (run_start_ref, run_len_ref, run_local_ref)
    nsteps = pl.num_programs(0) * pl.num_programs(1)
    step = pl.program_id(0) * pl.num_programs(1) + pl.program_id(1)

    def issue(blk, sl):
        def run_copy(local_start, sorted_start, size):
            return pltpu.make_async_copy(ys_ref.at[_token_rows(sorted_start, size)],
                                         srt_ref.at[sl, _token_rows(local_start, size)], sem.at[sl])
        _run_copies(blk, run_refs, run_copy)

    @pl.when(step == 0)
    def _():
        issue(0, 0)

    @pl.when(step + 1 < nsteps)
    def _():
        issue(step + 1, (step + 1) % 2)

    slot = step % 2
    pltpu.make_async_copy(ys_ref.at[_token_rows(0, TOP_K * ts)], srt_ref.at[slot], sem.at[slot]).wait()

    base = step * ts

    def pick(i, c):
        t1_ref[_token_rows(i, 1), :] = srt_ref[slot, _token_rows(lp1_ref[base + i], 1), :]
        t2_ref[_token_rows(i, 1), :] = srt_ref[slot, _token_rows(lp2_ref[base + i], 1), :]
        return c

    lax.fori_loop(0, ts, pick, 0, unroll=8)

    route = route_ref[...]
    w1 = route[:, 2:3]
    w2 = route[:, 3:4]
    ffn = w1 * _from_token_tiles(t1_ref, ts) + w2 * _from_token_tiles(t2_ref, ts)
    h2 = _layer_norm(DN_ALPHA * h_ref[...] + ffn, g_ref[...], b_ref[...])
    o_ref[...] = h2
    for ob_ref in maybe_ob_ref:
        ob_ref[...] = h2.astype(BF16)


def moe_combine(ys, plan, route, h1, g, b, bsz, last, ts=MOE_TS):
    s = h1.shape[0]
    d = D_MODEL
    g2, b2 = g.reshape(1, d), b.reshape(1, d)
    if last:
        o_specs = [pl.BlockSpec((None, ts, d), lambda bi, i, *_: (bi, i, 0))]
        o_shapes = [jax.ShapeDtypeStruct((bsz, s, d), F32)]
    else:
        o_specs = [pl.BlockSpec((ts, d), lambda bi, i, *_: (i, bi))] * 2
        o_shapes = [jax.ShapeDtypeStruct((s, bsz * d), F32), jax.ShapeDtypeStruct((s, bsz * d), BF16)]
    kern = functools.partial(_combine_kernel, ts=ts)
    grid_spec = pltpu.PrefetchScalarGridSpec(
        num_scalar_prefetch=len(plan),
        grid=(bsz, s // ts),
        in_specs=[pl.BlockSpec(memory_space=pl.ANY),
                  pl.BlockSpec((ts, ROUTE_LANES), lambda bi, i, *_: (i, bi)),
                  pl.BlockSpec((ts, d), lambda bi, i, *_: (i, bi)),
                  pl.BlockSpec((1, d), lambda bi, i, *_: (0, 0)),
                  pl.BlockSpec((1, d), lambda bi, i, *_: (0, 0))],
        out_specs=o_specs,
        scratch_shapes=[pltpu.VMEM((2, TOP_K * ts * ROW_TILE, LANES), F32),
                        pltpu.VMEM((ts * ROW_TILE, LANES), F32), pltpu.VMEM((ts * ROW_TILE, LANES), F32),
                        pltpu.SemaphoreType.DMA((2,))],
    )
    return pl.pallas_call(
        kern,
        grid_spec=grid_spec,
        out_shape=o_shapes,
        compiler_params=_cparams(("arbitrary", "arbitrary")),
        name="moe_combine",
    )(*plan, ys, route, h1, g2, b2)


def kernel(x, ln_in_g, ln_in_b, w_in, s5_lambda_re, s5_lambda_im, s5_log_step, s5_b_re, s5_b_im,
           s5_c_re, s5_c_im, s5_d, w_glu, w_branch_ret, w_branch_s5, w_out, ln_mix_g, ln_mix_b,
           w_router_group, b_router_group, w_router_expert, b_router_expert, w_exp_gate, w_exp_up,
           w_exp_down, ln_ffn_g, ln_ffn_b):
    bsz, s, d = x.shape
    t = bsz * s
    depth = w_in.shape[0]
    tabs = retention_tables(s)
    h_sb, hb_sb = ln_in(x, ln_in_g, ln_in_b)
    out = None
    for l in range(depth):
        proj = in_proj(hb_sb, w_in, l, bsz)
        ro = retention(proj, tabs, bsz, s)
        a_re, a_im, bb_re, bb_im = s5_params(s5_lambda_re[l], s5_lambda_im[l], s5_log_step[l],
                                             s5_b_re[l], s5_b_im[l])
        mats = s5_matrices(a_re, a_im, bb_re, bb_im, s5_c_re[l], s5_c_im[l])
        y = s5_scan(proj, mats, s5_d[l], bsz, s)
        pad = ROUTE_LANES - MOE_GROUPS - N_EXPERTS
        wr = jnp.concatenate([w_router_group[l], w_router_expert[l], jnp.zeros((d, pad), F32)], axis=1)
        wr_hi = wr.astype(BF16)
        wr = jnp.concatenate([wr_hi, (wr - wr_hi.astype(F32)).astype(BF16)], axis=1)
        br = jnp.concatenate([b_router_group[l], b_router_expert[l], jnp.zeros((pad,), F32)]).reshape(1, -1)
        h1, route, cnt, blk = mix_out(ro, y, proj, h_sb,
                                 w_glu[l].astype(BF16), w_branch_ret[l].astype(BF16),
                                 w_branch_s5[l].astype(BF16), w_out[l].astype(BF16),
                                 ln_mix_g[l], ln_mix_b[l], wr, br, bsz)
        last = l == depth - 1
        plan, pad_start, tile_expert, n_used = moe_plan(route, cnt, blk, bsz, s)
        xs = moe_dispatch(h1, plan, pad_start, n_used, bsz)
        ys = moe_experts(xs, tile_expert, n_used, w_exp_gate, w_exp_up, w_exp_down, l)
        outs = moe_combine(ys, plan, route, h1, ln_ffn_g[l], ln_ffn_b[l], bsz, last)
        if last:
            out = outs[0]
        else:
            h_sb, hb_sb = outs
    return out
```

```python
import functools
import math

import jax
import jax.numpy as jnp
from jax import lax
from jax.experimental import pallas as pl
from jax.experimental.pallas import tpu as pltpu

D_MODEL = 1024
CHUNK = 64
RET_HEADS = 8
RET_QK = 512
RET_V = 1024
RET_QK_DIM = 64
RET_V_DIM = 128
ROPE_BASE = 10000.0
S5_WIDTH = 512
S5_GROUP_CH = 16
S5_GROUPS = 32
S5_STATE = 64
MOE_GROUPS = 4
EXPERTS_PER_GROUP = 8
N_EXPERTS = 32
EXPERT_FF = 256
LN_EPS = 1e-5
HEAD_NORM_EPS = 1e-6
DEPTH = 2
DN_ALPHA = (2 * DEPTH) ** 0.25
IN_WIDTH = 2 * RET_QK + 2 * RET_V + S5_WIDTH + 2 * D_MODEL
PROJ_BLK = 512
N_PROJ_BLK = IN_WIDTH // PROJ_BLK

RET_SUPER = 256
S5_TT = 64
S5_COLS = 128
S5_NSLICE = S5_WIDTH // S5_COLS
S5_SLICE_STATE = (S5_COLS // S5_GROUP_CH) * S5_STATE
ROUTE_LANES = 128
TOP_K = 2
MOE_TM = 512
MOE_TS = 512
RUN_BITS = MOE_TS.bit_length()
PAD_BITS = (MOE_TM - 1).bit_length()
LANES = 128
ROW_TILE = D_MODEL // LANES
VMEM_LIMIT = 56 * 1024 * 1024

F32 = jnp.float32
BF16 = jnp.bfloat16


def _cparams(sem):
    return pltpu.CompilerParams(dimension_semantics=sem, vmem_limit_bytes=VMEM_LIMIT)


def _layer_norm(x, g, b):
    mu = jnp.mean(x, axis=-1, keepdims=True)
    xc = x - mu
    var = jnp.mean(xc * xc, axis=-1, keepdims=True)
    return xc * lax.rsqrt(var + LN_EPS) * g + b


def _inproj_kernel(h_ref, w_ref, o_ref):
    w = w_ref[...].astype(BF16)
    o_ref[...] = jnp.dot(h_ref[...], w, preferred_element_type=F32).astype(BF16)


def _ln_inproj_kernel(x_ref, g_ref, b_ref, w_ref, o_ref, h_ref, hb_ref):
    @pl.when(pl.program_id(2) == 0)
    def _():
        h = _layer_norm(x_ref[...], g_ref[...], b_ref[...])
        h_ref[...] = h
        hb_ref[...] = h.astype(BF16)

    w = w_ref[...].astype(BF16)
    o_ref[...] = jnp.dot(hb_ref[...], w, preferred_element_type=F32).astype(BF16)


def ln_in_proj(x, g, b, w, layer, ts=2048):
    bsz, s, d = x.shape
    n = w.shape[-1]
    nb = n // PROJ_BLK
    return pl.pallas_call(
        _ln_inproj_kernel,
        grid=(bsz, s // ts, nb),
        in_specs=[pl.BlockSpec((None, ts, d), lambda bi, i, j: (bi, i, 0)),
                  pl.BlockSpec((1, d), lambda bi, i, j: (0, 0)),
                  pl.BlockSpec((1, d), lambda bi, i, j: (0, 0)),
                  pl.BlockSpec((None, d, PROJ_BLK), lambda bi, i, j: (layer, 0, j))],
        out_specs=[pl.BlockSpec((ts, PROJ_BLK), lambda bi, i, j: (i, bi * nb + j)),
                   pl.BlockSpec((ts, d), lambda bi, i, j: (i, bi))],
        out_shape=[jax.ShapeDtypeStruct((s, bsz * n), BF16),
                   jax.ShapeDtypeStruct((s, bsz * d), F32)],
        scratch_shapes=[pltpu.VMEM((ts, d), BF16)],
        compiler_params=_cparams(("parallel", "parallel", "arbitrary")),
        name="ln_in_proj",
    )(x, g.reshape(1, d), b.reshape(1, d), w)


def in_proj(hb_sb, w, layer, bsz, ts=2048):
    s = hb_sb.shape[0]
    d, n = w.shape[-2:]
    nb = n // PROJ_BLK
    return pl.pallas_call(
        _inproj_kernel,
        grid=(bsz, s // ts, nb),
        in_specs=[pl.BlockSpec((ts, d), lambda b, i, j: (i, b)),
                  pl.BlockSpec((None, d, PROJ_BLK), lambda b, i, j: (layer, 0, j))],
        out_specs=pl.BlockSpec((ts, PROJ_BLK), lambda b, i, j: (i, b * nb + j)),
        out_shape=jax.ShapeDtypeStruct((s, bsz * n), BF16),
        compiler_params=_cparams(("parallel", "parallel", "arbitrary")),
        name="in_proj",
    )(hb_sb, w)


def _swap_halves(x):
    lane = lax.broadcasted_iota(jnp.int32, x.shape, 1)
    first = (lane % RET_QK_DIM) < (RET_QK_DIM // 2)
    n = x.shape[1]
    return jnp.where(first, pltpu.roll(x, n - RET_QK_DIM // 2, 1), pltpu.roll(x, RET_QK_DIM // 2, 1))


def _retention_kernel(q_ref, k_ref, v0_ref, v1_ref, g0_ref, g1_ref, cos_ref, sin_ref,
                      qd_ref, kd_ref, mask_ref, cd_ref, o_ref, state_ref):
    @pl.when(pl.program_id(1) == 0)
    def _():
        state_ref[...] = jnp.zeros_like(state_ref)

    cos = cos_ref[...]
    sin = sin_ref[...]
    q = q_ref[...].astype(F32)
    k = k_ref[...].astype(F32)
    q = q * cos + _swap_halves(q) * sin
    k = k * cos + _swap_halves(k) * sin
    qb = q.astype(BF16)
    kb = k.astype(BF16)
    qdb = (q * qd_ref[...]).astype(BF16)
    kdb = (k * kd_ref[...]).astype(BF16)
    for hd in range(RET_HEADS):
        qs = slice(hd * RET_QK_DIM, (hd + 1) * RET_QK_DIM)
        half, off = divmod(hd * RET_V_DIM, PROJ_BLK)
        vs = slice(off, off + RET_V_DIM)
        vh = (v0_ref, v1_ref)[half][:, vs]
        gh = (g0_ref, g1_ref)[half][:, vs].astype(F32)
        sc = lax.dot_general(qb[:, qs], kb[:, qs], (((1,), (1,)), ((), ())),
                             preferred_element_type=F32)
        sc = (sc * mask_ref[hd]).astype(BF16)
        st = state_ref[hd]
        o = jnp.dot(sc, vh, preferred_element_type=F32)
        o = o + jnp.dot(qdb[:, qs], st.astype(BF16), preferred_element_type=F32)
        kv = lax.dot_general(kdb[:, qs], vh, (((0,), (0,)), ((), ())),
                             preferred_element_type=F32)
        state_ref[hd] = st * cd_ref[hd] + kv
        mu = jnp.mean(o, axis=-1, keepdims=True)
        oc = o - mu
        var = jnp.mean(oc * oc, axis=-1, keepdims=True)
        on = oc * lax.rsqrt(var + HEAD_NORM_EPS)
        o_ref[:, hd * RET_V_DIM:(hd + 1) * RET_V_DIM] = (gh * jax.nn.sigmoid(gh) * on).astype(BF16)


def retention(proj_sb, tabs, bsz, s):
    cos_t, sin_t, qd_t, kd_t, mask, cd = tabs
    L = RET_SUPER
    nb = N_PROJ_BLK

    def pspec(col):
        return pl.BlockSpec((L, PROJ_BLK), lambda b, i, col=col: (i, b * nb + col))

    full2 = pl.BlockSpec((L, RET_QK), lambda b, i: (0, 0))
    return pl.pallas_call(
        _retention_kernel,
        grid=(bsz, s // L),
        in_specs=[pspec(0), pspec(1), pspec(2), pspec(3), pspec(4), pspec(5),
                  pl.BlockSpec((L, RET_QK), lambda b, i: (i, 0)),
                  pl.BlockSpec((L, RET_QK), lambda b, i: (i, 0)),
                  full2, full2,
                  pl.BlockSpec((RET_HEADS, L, L), lambda b, i: (0, 0, 0)),
                  pl.BlockSpec((RET_HEADS, 1, RET_V_DIM), lambda b, i: (0, 0, 0))],
        out_specs=pl.BlockSpec((L, RET_V), lambda b, i: (i, b)),
        out_shape=jax.ShapeDtypeStruct((s, bsz * RET_V), BF16),
        scratch_shapes=[pltpu.VMEM((RET_HEADS, RET_QK_DIM, RET_V_DIM), F32)],
        compiler_params=_cparams(("parallel", "arbitrary")),
        name="retention",
    )(proj_sb, proj_sb, proj_sb, proj_sb, proj_sb, proj_sb, cos_t, sin_t, qd_t, kd_t, mask, cd)


def retention_tables(s):
    L = RET_SUPER
    half = RET_QK_DIM // 2
    inv_freq = ROPE_BASE ** (-jnp.arange(half, dtype=F32) / half)
    ang = jnp.arange(s, dtype=F32)[:, None] * inv_freq[None, :]
    cos, sin = jnp.cos(ang), jnp.sin(ang)
    cos_t = jnp.tile(jnp.concatenate([cos, cos], -1), (1, RET_HEADS))
    sin_t = jnp.tile(jnp.concatenate([-sin, sin], -1), (1, RET_HEADS))
    log_gamma = jnp.log1p(-(2.0 ** (-5.0 - jnp.arange(RET_HEADS, dtype=F32))))
    pos = jnp.arange(L, dtype=F32)
    qd = jnp.exp(log_gamma[None, :] * (pos + 1.0)[:, None])
    k_scale = RET_QK_DIM ** -0.5
    kd = jnp.exp(log_gamma[None, :] * (L - 1.0 - pos)[:, None]) * k_scale
    qd_t = jnp.repeat(qd, RET_QK_DIM, axis=1)
    kd_t = jnp.repeat(kd, RET_QK_DIM, axis=1)
    chunk_id = jnp.arange(L) // CHUNK
    visible = (chunk_id[None, :] <= chunk_id[:, None]).astype(F32)
    mask = jnp.exp(log_gamma[:, None, None] * jnp.abs(pos[:, None] - pos[None, :])) * visible[None] * k_scale
    cd = jnp.broadcast_to(jnp.exp(log_gamma * L)[:, None, None], (RET_HEADS, 1, RET_V_DIM))
    return cos_t, sin_t, qd_t, kd_t, mask, cd


def _s5_param_kernel(lre_ref, lim_ref, ls_ref, bre_ref, bim_ref, are_ref, aim_ref, bbre_ref, bbim_ref):
    lam_re = jnp.minimum(lre_ref[...], -1e-4)
    lam_im = lim_ref[...]
    step = jnp.exp(ls_ref[...])
    mag = jnp.exp(lam_re * step)
    ang = lam_im * step
    ab_re = mag * jnp.cos(ang)
    ab_im = mag * jnp.sin(ang)
    den = lam_re * lam_re + lam_im * lam_im
    n_re = ab_re - 1.0
    zc_re = (n_re * lam_re + ab_im * lam_im) / den
    zc_im = (ab_im * lam_re - n_re * lam_im) / den
    are_ref[...] = ab_re
    aim_ref[...] = ab_im
    b_re = bre_ref[...]
    b_im = bim_ref[...]
    bbre_ref[...] = zc_re * b_re - zc_im * b_im
    bbim_ref[...] = zc_re * b_im + zc_im * b_re


def s5_params(lam_re, lam_im, log_step, b_re, b_im):
    g, n = lam_re.shape
    c = b_re.shape[-1]
    outs = pl.pallas_call(
        _s5_param_kernel,
        out_shape=[jax.ShapeDtypeStruct((g, 1, n), F32), jax.ShapeDtypeStruct((g, 1, n), F32),
                   jax.ShapeDtypeStruct((g, c, n), F32), jax.ShapeDtypeStruct((g, c, n), F32)],
        name="s5_params",
    )(lam_re.reshape(g, 1, n), lam_im.reshape(g, 1, n), log_step.reshape(g, 1, 1),
      jnp.swapaxes(b_re, 1, 2), jnp.swapaxes(b_im, 1, 2))
    return outs


def _block_diag(x):
    ns, gl, r, c = x.shape
    eye = jnp.eye(gl, dtype=x.dtype)
    return jnp.einsum('sgrc,gh->sgrhc', x, eye).reshape(ns, gl * r, gl * c)


def s5_matrices(a_re, a_im, bb_re, bb_im, c_re, c_im):
    gl = S5_COLS // S5_GROUP_CH
    ns = S5_NSLICE
    bre = _block_diag(bb_re.reshape(ns, gl, S5_GROUP_CH, S5_STATE))
    bim = _block_diag(bb_im.reshape(ns, gl, S5_GROUP_CH, S5_STATE))
    bq = jnp.concatenate([bre, bim], axis=-1).astype(BF16)
    cre = _block_diag(jnp.swapaxes(c_re, 1, 2).reshape(ns, gl, S5_STATE, S5_GROUP_CH))
    cim = _block_diag(jnp.swapaxes(c_im, 1, 2).reshape(ns, gl, S5_STATE, S5_GROUP_CH))
    cq = jnp.concatenate([cre, -cim], axis=1).astype(BF16)
    are = a_re.reshape(ns, 1, S5_SLICE_STATE)
    aim = a_im.reshape(ns, 1, S5_SLICE_STATE)
    return bq, cq, are, aim


def _s5_kernel(*refs, bsz, tt):
    u_refs = refs[:bsz]
    bq_ref, cq_ref, are_ref, aim_ref, d_ref, y_ref, us_ref, ys_ref, bu_ref, st_ref = refs[bsz:]

    @pl.when(pl.program_id(0) == 0)
    def _():
        st_ref[...] = jnp.zeros_like(st_ref)

    for b in range(bsz):
        ub = u_refs[b][...].astype(F32)
        for cs in range(S5_NSLICE):
            us_ref[cs, pl.ds(b, tt, stride=bsz), :] = ub[:, cs * S5_COLS:(cs + 1) * S5_COLS]

    ns2 = S5_SLICE_STATE
    for cs in range(S5_NSLICE):
        cols = slice(cs * S5_COLS, (cs + 1) * S5_COLS)
        uf = us_ref[cs]
        bu_ref[cs] = jnp.dot(uf.astype(BF16), bq_ref[cs], preferred_element_type=F32)
        a_re = jnp.broadcast_to(are_ref[cs], (bsz, ns2))
        a_im = jnp.broadcast_to(aim_ref[cs], (bsz, ns2))

        def step(t, carry):
            h_re, h_im = carry
            rows = pl.ds(pl.multiple_of(t * bsz, bsz), bsz)
            n_re = a_re * h_re - a_im * h_im + bu_ref[cs, rows, 0:ns2]
            n_im = a_re * h_im + a_im * h_re + bu_ref[cs, rows, ns2:2 * ns2]
            bu_ref[cs, rows, 0:ns2] = n_re
            bu_ref[cs, rows, ns2:2 * ns2] = n_im
            return n_re, n_im

        h_re, h_im = lax.fori_loop(0, tt, step, (st_ref[cs, 0], st_ref[cs, 1]), unroll=True)
        st_ref[cs, 0] = h_re
        st_ref[cs, 1] = h_im
        y = jnp.dot(bu_ref[cs].astype(BF16), cq_ref[cs], preferred_element_type=F32)
        ys_ref[cs] = y + d_ref[:, cols] * uf

    for b in range(bsz):
        for cs in range(S5_NSLICE):
            lo = b * S5_WIDTH + cs * S5_COLS
            y_ref[:, lo:lo + S5_COLS] = ys_ref[cs, pl.ds(b, tt, stride=bsz), :]


def s5_scan(proj_sb, mats, d_skip, bsz, s):
    bq, cq, are, aim = mats
    tt = S5_TT
    rows = tt * bsz
    nb = N_PROJ_BLK
    kern = functools.partial(_s5_kernel, bsz=bsz, tt=tt)
    u_specs = [pl.BlockSpec((tt, PROJ_BLK), lambda i, b=b: (i, b * nb + 6)) for b in range(bsz)]
    return pl.pallas_call(
        kern,
        grid=(s // tt,),
        in_specs=u_specs + [
                  pl.BlockSpec(bq.shape, lambda i: (0, 0, 0)),
                  pl.BlockSpec(cq.shape, lambda i: (0, 0, 0)),
                  pl.BlockSpec(are.shape, lambda i: (0, 0, 0)),
                  pl.BlockSpec(aim.shape, lambda i: (0, 0, 0)),
                  pl.BlockSpec((1, S5_WIDTH), lambda i: (0, 0))],
        out_specs=pl.BlockSpec((tt, bsz * S5_WIDTH), lambda i: (i, 0)),
        out_shape=jax.ShapeDtypeStruct((s, bsz * S5_WIDTH), F32),
        scratch_shapes=[pltpu.VMEM((S5_NSLICE, rows, S5_COLS), F32),
                        pltpu.VMEM((S5_NSLICE, rows, S5_COLS), F32),
                        pltpu.VMEM((S5_NSLICE, rows, 2 * S5_SLICE_STATE), F32),
                        pltpu.VMEM((S5_NSLICE, 2, bsz, S5_SLICE_STATE), F32)],
        compiler_params=_cparams(("arbitrary",)),
        name="s5_scan",
    )(*([proj_sb] * bsz), bq, cq, are, aim, d_skip.reshape(1, S5_WIDTH))


def _gelu_tanh(x):
    c = math.sqrt(2.0 / math.pi)
    return 0.5 * x * (1.0 + jnp.tanh(c * (x + 0.044715 * (x * x * x))))


def _route(logits):
    lane = lax.broadcasted_iota(jnp.int32, logits.shape, 1)
    neg = jnp.float32(-jnp.inf)
    big = jnp.int32(1 << 20)
    is_g = lane < MOE_GROUPS
    lg = jnp.where(is_g, logits, neg)
    mg = jnp.max(lg, axis=-1, keepdims=True)
    sg = jnp.sum(jnp.where(is_g, jnp.exp(lg - mg), 0.0), axis=-1, keepdims=True)
    g_top = 1.0 / sg
    g_idx = jnp.min(jnp.where(lg == mg, lane, big), axis=-1, keepdims=True)
    lo = MOE_GROUPS + g_idx * EXPERTS_PER_GROUP
    in_grp = (lane >= lo) & (lane < lo + EXPERTS_PER_GROUP)
    le = jnp.where(in_grp, logits, neg)
    m1 = jnp.max(le, axis=-1, keepdims=True)
    se = jnp.sum(jnp.where(in_grp, jnp.exp(le - m1), 0.0), axis=-1, keepdims=True)
    i1 = jnp.min(jnp.where(le == m1, lane, big), axis=-1, keepdims=True)
    le2 = jnp.where(lane == i1, neg, le)
    m2 = jnp.max(le2, axis=-1, keepdims=True)
    i2 = jnp.min(jnp.where(le2 == m2, lane, big), axis=-1, keepdims=True)
    p1 = 1.0 / se
    p2 = jnp.exp(m2 - m1) / se
    tot = p1 + p2
    w1 = g_top * (p1 / tot)
    w2 = g_top * (p2 / tot)
    return lane, i1, i2, w1, w2


def _mix_kernel(ro_ref, y_ref, gr0_ref, gr1_ref, gs0_ref, gs1_ref, h_ref,
                wglu_ref, wbr_ref, wbs_ref, wout_ref, g_ref, b_ref, wr_ref, br_ref, tri_ref,
                h1_ref, route_ref, cnt_ref, blk_ref, run_ref):
    @pl.when((pl.program_id(0) == 0) & (pl.program_id(1) == 0))
    def _():
        run_ref[...] = jnp.zeros_like(run_ref)

    z = _gelu_tanh(y_ref[...])
    zg = jnp.dot(z.astype(BF16), wglu_ref[...], preferred_element_type=F32)
    zz = (z * jax.nn.sigmoid(zg)).astype(BF16)
    s5b = jnp.dot(zz, wbs_ref[...], preferred_element_type=F32)
    rb = jnp.dot(ro_ref[...], wbr_ref[...], preferred_element_type=F32)
    gr = jnp.concatenate([gr0_ref[...], gr1_ref[...]], axis=-1).astype(F32)
    gs = jnp.concatenate([gs0_ref[...], gs1_ref[...]], axis=-1).astype(F32)
    merged = jax.nn.sigmoid(gr) * rb + jax.nn.sigmoid(gs) * s5b
    mix = jnp.dot(merged.astype(BF16), wout_ref[...], preferred_element_type=F32)
    h1 = _layer_norm(DN_ALPHA * h_ref[...] + mix, g_ref[...], b_ref[...])
    h1_ref[...] = h1
    h_hi = h1.astype(BF16)
    h_lo = (h1 - h_hi.astype(F32)).astype(BF16)
    both = jnp.dot(h_hi, wr_ref[...], preferred_element_type=F32)
    logits = (both[:, :ROUTE_LANES] + both[:, ROUTE_LANES:]
              + jnp.dot(h_lo, wr_ref[:, :ROUTE_LANES], preferred_element_type=F32)) + br_ref[...]
    lane, i1, i2, w1, w2 = _route(logits)
    oh1 = lane == i1
    oh2 = lane == i2
    oh = jnp.where(oh1 | oh2, 1.0, 0.0)
    rank_in_blk = jnp.dot(tri_ref[...], oh.astype(BF16), preferred_element_type=F32)
    blk_cnt = jnp.broadcast_to(jnp.sum(oh, axis=0, keepdims=True), run_ref.shape)
    lane8 = lax.broadcasted_iota(jnp.int32, run_ref.shape, 1)
    incl = blk_cnt
    shift = 1
    while shift < ROUTE_LANES:
        incl = incl + jnp.where(lane8 >= shift, pltpu.roll(incl, shift, 1), 0.0)
        shift *= 2
    lstart = incl - blk_cnt
    pos = lstart[0:1, :] + rank_in_blk
    l1 = jnp.sum(jnp.where(oh1, pos, 0.0), axis=-1, keepdims=True)
    l2 = jnp.sum(jnp.where(oh2, pos, 0.0), axis=-1, keepdims=True)
    run_before = run_ref[...]
    row8 = lax.broadcasted_iota(jnp.int32, run_ref.shape, 0)
    blk_ref[...] = jnp.where(row8 == 0, run_before,
                             jnp.where(row8 == 1, blk_cnt, jnp.where(row8 == 2, lstart, 0.0)))
    cnt = run_before + blk_cnt
    run_ref[...] = cnt
    cnt_ref[...] = cnt
    vals = (i1.astype(F32) - MOE_GROUPS, i2.astype(F32) - MOE_GROUPS, w1, w2, l1, l2)
    route = jnp.zeros(logits.shape, F32)
    for k, v in enumerate(vals):
        route = jnp.where(lane == k, v, route)
    route_ref[...] = route


def mix_out(ro, y, proj, h, wglu, wbr, wbs, wout, g, b, wr, br, bsz, ts=MOE_TS):
    s = h.shape[0]
    d = D_MODEL
    nb = N_PROJ_BLK

    def const(a):
        return pl.BlockSpec(a.shape, lambda bi, i: (0,) * a.ndim)

    def pspec(col):
        return pl.BlockSpec((ts, PROJ_BLK), lambda bi, i, col=col: (i, bi * nb + col))

    def tok(w):
        return pl.BlockSpec((ts, w), lambda bi, i: (i, bi))

    g2, b2 = g.reshape(1, d), b.reshape(1, d)
    idx = jnp.arange(ts)
    tri = (idx[None, :] < idx[:, None]).astype(BF16)
    return pl.pallas_call(
        _mix_kernel,
        grid=(bsz, s // ts),
        in_specs=[tok(RET_V), tok(S5_WIDTH), pspec(7), pspec(8), pspec(9), pspec(10), tok(d),
                  const(wglu), const(wbr), const(wbs), const(wout), const(g2), const(b2),
                  const(wr), const(br), const(tri)],
        out_specs=[tok(d), tok(ROUTE_LANES), pl.BlockSpec((8, ROUTE_LANES), lambda bi, i: (0, 0)),
                   pl.BlockSpec((None, 8, ROUTE_LANES), lambda bi, i: (bi * (s // ts) + i, 0, 0))],
        out_shape=[jax.ShapeDtypeStruct((s, bsz * d), F32),
                   jax.ShapeDtypeStruct((s, bsz * ROUTE_LANES), F32),
                   jax.ShapeDtypeStruct((8, ROUTE_LANES), F32),
                   jax.ShapeDtypeStruct((bsz * (s // ts), 8, ROUTE_LANES), F32)],
        scratch_shapes=[pltpu.VMEM((8, ROUTE_LANES), F32)],
        compiler_params=_cparams(("arbitrary", "arbitrary")),
        name="mix_out",
    )(ro, y, proj, proj, proj, proj, h, wglu, wbr, wbs, wout, g2, b2, wr, br, tri)


def moe_rows(t):
    return TOP_K * t + (N_EXPERTS + 1) * MOE_TM


def moe_plan(route, cnt, blk, bsz, s):
    t = bsz * s
    rec = route.reshape(s, bsz, ROUTE_LANES)[:, :, 4:6]
    rec = jnp.transpose(rec, (1, 0, 2)).reshape(t, 2).astype(jnp.int32)
    experts = slice(MOE_GROUPS, MOE_GROUPS + N_EXPERTS)
    counts = cnt[0, experts].astype(jnp.int32)
    padded = ((counts + MOE_TM - 1) // MOE_TM) * MOE_TM
    ends = jnp.cumsum(padded)
    off = ends - padded
    run_start = (off[None, :] + blk[:, 0, experts].astype(jnp.int32)).reshape(-1)
    run_len = blk[:, 1, experts].astype(jnp.int32).reshape(-1)
    run_local = blk[:, 2, experts].astype(jnp.int32).reshape(-1)
    pads = (off + counts, padded - counts)
    n_tiles = moe_rows(t) // MOE_TM
    n_used = (ends[-1:] // MOE_TM).astype(jnp.int32)
    tile_start = jnp.arange(n_tiles, dtype=jnp.int32) * MOE_TM
    tile_expert = jnp.sum((ends[None, :] <= tile_start[:, None]).astype(jnp.int32), axis=1)
    tile_expert = jnp.minimum(tile_expert, N_EXPERTS - 1)
    return (rec[:, 0], rec[:, 1], run_start, run_len, run_local), pads, tile_expert, n_used


def _token_rows(start, n):
    return pl.ds(pl.multiple_of(start * ROW_TILE, ROW_TILE), n * ROW_TILE)


def _for_each_piece(n, nbits, fn):
    for bit in reversed(range(nbits)):
        size = 1 << bit
        done = (n >> (bit + 1)) << (bit + 1)

        @pl.when((n & size) != 0)
        def _(size=size, done=done):
            fn(done, size)


def _run_copies(blk, run_refs, make_copy):
    run_start_ref, run_len_ref, run_local_ref = run_refs
    for e in range(N_EXPERTS):
        idx = blk * N_EXPERTS + e
        g0 = run_start_ref[idx]
        l0 = run_local_ref[idx]
        _for_each_piece(run_len_ref[idx], RUN_BITS,
                        lambda off, size, g0=g0, l0=l0: make_copy(l0 + off, g0 + off, size).start())


def _to_token_tiles(ref, x):
    n = x.shape[0]
    for j in range(ROW_TILE):
        ref[pl.ds(j, n, stride=ROW_TILE), :] = x[:, j * LANES:(j + 1) * LANES]


def _from_token_tiles(ref, n):
    return jnp.concatenate([ref[pl.ds(j, n, stride=ROW_TILE), :] for j in range(ROW_TILE)], axis=-1)


def _dispatch_kernel(lp1_ref, lp2_ref, run_start_ref, run_len_ref, run_local_ref,
                     pad_ref, pad_len_ref, nu_ref,
                     x_ref, xs_ref, xt_ref, srt_ref, zero_ref, sem, zsem, *, ts):
    run_refs = (run_start_ref, run_len_ref, run_local_ref)
    nsteps = pl.num_programs(0) * pl.num_programs(1)
    step = pl.program_id(0) * pl.num_programs(1) + pl.program_id(1)

    def zero_copy(start, size):
        return pltpu.make_async_copy(zero_ref.at[_token_rows(0, size)],
                                     xs_ref.at[_token_rows(start, size)], zsem)

    def zero_fill(action):
        for e in range(N_EXPERTS):
            p0 = pad_ref[e]
            _for_each_piece(pad_len_ref[e], PAD_BITS,
                            lambda off, size, p0=p0: action(zero_copy(p0 + off, size)))

        def tail_tile(r, c):
            action(zero_copy(r * MOE_TM, MOE_TM))
            return c

        lax.fori_loop(nu_ref[0], xs_ref.shape[0] // (MOE_TM * ROW_TILE), tail_tile, 0)

    @pl.when(step == 0)
    def _():
        zero_ref[...] = jnp.zeros_like(zero_ref)
        zero_fill(lambda cp: cp.start())

    @pl.when(step == nsteps - 1)
    def _():
        zero_fill(lambda cp: cp.wait())

    slot = step % 2
    _to_token_tiles(xt_ref, x_ref[...])
    base = step * ts

    def place(i, c):
        tile = xt_ref[_token_rows(i, 1), :]
        srt_ref[slot, _token_rows(lp1_ref[base + i], 1), :] = tile
        srt_ref[slot, _token_rows(lp2_ref[base + i], 1), :] = tile
        return c

    lax.fori_loop(0, ts, place, 0, unroll=8)

    def run_copy(local_start, sorted_start, size):
        return pltpu.make_async_copy(srt_ref.at[slot, _token_rows(local_start, size)],
                                     xs_ref.at[_token_rows(sorted_start, size)], sem.at[slot])

    _run_copies(step, run_refs, run_copy)

    def drain(sl):
        pltpu.make_async_copy(srt_ref.at[sl], xs_ref.at[_token_rows(0, TOP_K * ts)], sem.at[sl]).wait()

    @pl.when(step > 0)
    def _():
        drain(1 - slot)

    @pl.when(step == nsteps - 1)
    def _():
        drain(slot)


def moe_dispatch(h1, plan, pads, n_used, bsz, ts=MOE_TS):
    s = h1.shape[0]
    d = D_MODEL
    kern = functools.partial(_dispatch_kernel, ts=ts)
    grid_spec = pltpu.PrefetchScalarGridSpec(
        num_scalar_prefetch=len(plan) + len(pads) + 1,
        grid=(bsz, s // ts),
        in_specs=[pl.BlockSpec((ts, d), lambda bi, i, *_: (i, bi))],
        out_specs=pl.BlockSpec(memory_space=pl.ANY),
        scratch_shapes=[pltpu.VMEM((ts * ROW_TILE, LANES), F32),
                        pltpu.VMEM((2, TOP_K * ts * ROW_TILE, LANES), F32),
                        pltpu.VMEM((MOE_TM * ROW_TILE, LANES), F32),
                        pltpu.SemaphoreType.DMA((2,)), pltpu.SemaphoreType.DMA(())],
    )
    return pl.pallas_call(
        kern,
        grid_spec=grid_spec,
        out_shape=jax.ShapeDtypeStruct((moe_rows(bsz * s) * ROW_TILE, LANES), F32),
        compiler_params=_cparams(("arbitrary", "arbitrary")),
        name="moe_dispatch",
    )(*plan, *pads, n_used, h1)


def _experts_kernel(te_ref, nu_ref, xs_ref, wg_ref, wu_ref, wd_ref, ys_ref, wgb_ref, wub_ref, wdb_ref):
    r = pl.program_id(0)

    @pl.when(r < nu_ref[0])
    def _():
        changed = (r == 0) | (te_ref[r] != te_ref[jnp.maximum(r - 1, 0)])

        @pl.when(changed)
        def _():
            wgb_ref[...] = wg_ref[...].astype(BF16)
            wub_ref[...] = wu_ref[...].astype(BF16)
            wdb_ref[...] = wd_ref[...].astype(BF16)

        x = _from_token_tiles(xs_ref, MOE_TM).astype(BF16)
        gate = jnp.dot(x, wgb_ref[...], preferred_element_type=F32)
        up = jnp.dot(x, wub_ref[...], preferred_element_type=F32)
        act = (gate * jax.nn.sigmoid(gate) * up).astype(BF16)
        _to_token_tiles(ys_ref, jnp.dot(act, wdb_ref[...], preferred_element_type=F32))

    @pl.when(r >= nu_ref[0])
    def _():
        ys_ref[...] = jnp.zeros_like(ys_ref)


def moe_experts(xs, tile_expert, n_used, wg, wu, wd, layer):
    d, f = wg.shape[-2:]
    rows = xs.shape[0] // ROW_TILE
    blk = MOE_TM * ROW_TILE

    def tile(r, te, nu):
        return jnp.minimum(r, nu[0] - 1)

    def wspec(a, b):
        return pl.BlockSpec((None, None, a, b), lambda r, te, nu: (layer, te[tile(r, te, nu)], 0, 0))

    grid_spec = pltpu.PrefetchScalarGridSpec(
        num_scalar_prefetch=2,
        grid=(rows // MOE_TM,),
        in_specs=[pl.BlockSpec((blk, LANES), lambda r, te, nu: (tile(r, te, nu), 0)),
                  wspec(d, f), wspec(d, f), wspec(f, d)],
        out_specs=pl.BlockSpec((blk, LANES), lambda r, te, nu: (r, 0)),
        scratch_shapes=[pltpu.VMEM((d, f), BF16), pltpu.VMEM((d, f), BF16), pltpu.VMEM((f, d), BF16)],
    )
    return pl.pallas_call(
        _experts_kernel,
        grid_spec=grid_spec,
        out_shape=jax.ShapeDtypeStruct(xs.shape, F32),
        compiler_params=_cparams(("arbitrary",)),
        name="moe_experts",
    )(tile_expert, n_used, xs, wg, wu, wd)


def _combine_kernel(lp1_ref, lp2_ref, run_start_ref, run_len_ref, run_local_ref,
                    ys_ref, route_ref, h_ref, g_ref, b_ref, *refs, ts):
    o_ref, maybe_ob_ref = refs[0], refs[1:-4]
    srt_ref, t1_ref, t2_ref, sem = refs[-4:]
    run_refs = (run_start_ref, run_len_ref, run_local_ref)
    nsteps = pl.num_programs(0) * pl.num_programs(1)
    step = pl.program_id(0) * pl.num_programs(1) + pl.program_id(1)

    def issue(blk, sl):
        def run_copy(local_start, sorted_start, size):
            return pltpu.make_async_copy(ys_ref.at[_token_rows(sorted_start, size)],
                                         srt_ref.at[sl, _token_rows(local_start, size)], sem.at[sl])
        _run_copies(blk, run_refs, run_copy)

    @pl.when(step == 0)
    def _():
        issue(0, 0)

    @pl.when(step + 1 < nsteps)
    def _():
        issue(step + 1, (step + 1) % 2)

    slot = step % 2
    pltpu.make_async_copy(ys_ref.at[_token_rows(0, TOP_K * ts)], srt_ref.at[slot], sem.at[slot]).wait()

    base = step * ts

    def pick(i, c):
        t1_ref[_token_rows(i, 1), :] = srt_ref[slot, _token_rows(lp1_ref[base + i], 1), :]
        t2_ref[_token_rows(i, 1), :] = srt_ref[slot, _token_rows(lp2_ref[base + i], 1), :]
        return c

    lax.fori_loop(0, ts, pick, 0, unroll=8)

    route = route_ref[...]
    w1 = route[:, 2:3]
    w2 = route[:, 3:4]
    ffn = w1 * _from_token_tiles(t1_ref, ts) + w2 * _from_token_tiles(t2_ref, ts)
    h2 = _layer_norm(DN_ALPHA * h_ref[...] + ffn, g_ref[...], b_ref[...])
    o_ref[...] = h2
    for ob_ref in maybe_ob_ref:
        ob_ref[...] = h2.astype(BF16)


def moe_combine(ys, plan, route, h1, g, b, bsz, last, ts=MOE_TS):
    s = h1.shape[0]
    d = D_MODEL
    g2, b2 = g.reshape(1, d), b.reshape(1, d)
    if last:
        o_specs = [pl.BlockSpec((None, ts, d), lambda bi, i, *_: (bi, i, 0))]
        o_shapes = [jax.ShapeDtypeStruct((bsz, s, d), F32)]
    else:
        o_specs = [pl.BlockSpec((ts, d), lambda bi, i, *_: (i, bi))] * 2
        o_shapes = [jax.ShapeDtypeStruct((s, bsz * d), F32), jax.ShapeDtypeStruct((s, bsz * d), BF16)]
    kern = functools.partial(_combine_kernel, ts=ts)
    grid_spec = pltpu.PrefetchScalarGridSpec(
        num_scalar_prefetch=len(plan),
        grid=(bsz, s // ts),
        in_specs=[pl.BlockSpec(memory_space=pl.ANY),
                  pl.BlockSpec((ts, ROUTE_LANES), lambda bi, i, *_: (i, bi)),
                  pl.BlockSpec((ts, d), lambda bi, i, *_: (i, bi)),
                  pl.BlockSpec((1, d), lambda bi, i, *_: (0, 0)),
                  pl.BlockSpec((1, d), lambda bi, i, *_: (0, 0))],
        out_specs=o_specs,
        scratch_shapes=[pltpu.VMEM((2, TOP_K * ts * ROW_TILE, LANES), F32),
                        pltpu.VMEM((ts * ROW_TILE, LANES), F32), pltpu.VMEM((ts * ROW_TILE, LANES), F32),
                        pltpu.SemaphoreType.DMA((2,))],
    )
    return pl.pallas_call(
        kern,
        grid_spec=grid_spec,
        out_shape=o_shapes,
        compiler_params=_cparams(("arbitrary", "arbitrary")),
        name="moe_combine",
    )(*plan, ys, route, h1, g2, b2)


def kernel(x, ln_in_g, ln_in_b, w_in, s5_lambda_re, s5_lambda_im, s5_log_step, s5_b_re, s5_b_im,
           s5_c_re, s5_c_im, s5_d, w_glu, w_branch_ret, w_branch_s5, w_out, ln_mix_g, ln_mix_b,
           w_router_group, b_router_group, w_router_expert, b_router_expert, w_exp_gate, w_exp_up,
           w_exp_down, ln_ffn_g, ln_ffn_b):
    bsz, s, d = x.shape
    t = bsz * s
    depth = w_in.shape[0]
    tabs = retention_tables(s)
    out = None
    for l in range(depth):
        if l == 0:
            proj, h_sb = ln_in_proj(x, ln_in_g, ln_in_b, w_in, l)
        else:
            proj = in_proj(hb_sb, w_in, l, bsz)
        ro = retention(proj, tabs, bsz, s)
        a_re, a_im, bb_re, bb_im = s5_params(s5_lambda_re[l], s5_lambda_im[l], s5_log_step[l],
                                             s5_b_re[l], s5_b_im[l])
        mats = s5_matrices(a_re, a_im, bb_re, bb_im, s5_c_re[l], s5_c_im[l])
        y = s5_scan(proj, mats, s5_d[l], bsz, s)
        pad = ROUTE_LANES - MOE_GROUPS - N_EXPERTS
        wr = jnp.concatenate([w_router_group[l], w_router_expert[l], jnp.zeros((d, pad), F32)], axis=1)
        wr_hi = wr.astype(BF16)
        wr = jnp.concatenate([wr_hi, (wr - wr_hi.astype(F32)).astype(BF16)], axis=1)
        br = jnp.concatenate([b_router_group[l], b_router_expert[l], jnp.zeros((pad,), F32)]).reshape(1, -1)
        h1, route, cnt, blk = mix_out(ro, y, proj, h_sb,
                                 w_glu[l].astype(BF16), w_branch_ret[l].astype(BF16),
                                 w_branch_s5[l].astype(BF16), w_out[l].astype(BF16),
                                 ln_mix_g[l], ln_mix_b[l], wr, br, bsz)
        last = l == depth - 1
        plan, pads, tile_expert, n_used = moe_plan(route, cnt, blk, bsz, s)
        xs = moe_dispatch(h1, plan, pads, n_used, bsz)
        ys = moe_experts(xs, tile_expert, n_used, w_exp_gate, w_exp_up, w_exp_down, l)
        outs = moe_combine(ys, plan, route, h1, ln_ffn_g[l], ln_ffn_b[l], bsz, last)
        if last:
            out = outs[0]
        else:
            h_sb, hb_sb = outs
    return out
```

```python
import functools
import math

import jax
import jax.numpy as jnp
import numpy as np
from jax import lax
from jax.experimental import pallas as pl
from jax.experimental.pallas import tpu as pltpu

D_MODEL = 1024
CHUNK = 64
RET_HEADS = 8
RET_QK = 512
RET_V = 1024
RET_QK_DIM = 64
RET_V_DIM = 128
ROPE_BASE = 10000.0
S5_WIDTH = 512
S5_GROUP_CH = 16
S5_GROUPS = 32
S5_STATE = 64
MOE_GROUPS = 4
EXPERTS_PER_GROUP = 8
N_EXPERTS = 32
EXPERT_FF = 256
LN_EPS = 1e-5
HEAD_NORM_EPS = 1e-6
DEPTH = 2
DN_ALPHA = (2 * DEPTH) ** 0.25
IN_WIDTH = 2 * RET_QK + 2 * RET_V + S5_WIDTH + 2 * D_MODEL
PROJ_BLK = 512
N_PROJ_BLK = IN_WIDTH // PROJ_BLK

RET_SUPER = 256
S5_TT = 64
S5_COLS = 128
S5_NSLICE = S5_WIDTH // S5_COLS
S5_SLICE_STATE = (S5_COLS // S5_GROUP_CH) * S5_STATE
ROUTE_LANES = 128
TOP_K = 2
MOE_TM = 512
MOE_TS = 512
RUN_BITS = MOE_TS.bit_length()
PAD_BITS = (MOE_TM - 1).bit_length()
LANES = 128
ROW_TILE = D_MODEL // LANES
VMEM_LIMIT = 56 * 1024 * 1024

F32 = jnp.float32
BF16 = jnp.bfloat16


def _cparams(sem):
    return pltpu.CompilerParams(dimension_semantics=sem, vmem_limit_bytes=VMEM_LIMIT)


def _layer_norm(x, g, b):
    mu = jnp.mean(x, axis=-1, keepdims=True)
    xc = x - mu
    var = jnp.mean(xc * xc, axis=-1, keepdims=True)
    return xc * lax.rsqrt(var + LN_EPS) * g + b


def _inproj_kernel(h_ref, w_ref, o_ref):
    w = w_ref[...].astype(BF16)
    o_ref[...] = jnp.dot(h_ref[...], w, preferred_element_type=F32).astype(BF16)


def _ln_inproj_kernel(x_ref, g_ref, b_ref, w_ref, o_ref, h_ref, hb_ref):
    @pl.when(pl.program_id(2) == 0)
    def _():
        h = _layer_norm(x_ref[...], g_ref[...], b_ref[...])
        h_ref[...] = h
        hb_ref[...] = h.astype(BF16)

    w = w_ref[...].astype(BF16)
    o_ref[...] = jnp.dot(hb_ref[...], w, preferred_element_type=F32).astype(BF16)


def ln_in_proj(x, g, b, w, layer, ts=2048):
    bsz, s, d = x.shape
    n = w.shape[-1]
    nb = n // PROJ_BLK
    return pl.pallas_call(
        _ln_inproj_kernel,
        grid=(bsz, s // ts, nb),
        in_specs=[pl.BlockSpec((None, ts, d), lambda bi, i, j: (bi, i, 0)),
                  pl.BlockSpec((1, d), lambda bi, i, j: (0, 0)),
                  pl.BlockSpec((1, d), lambda bi, i, j: (0, 0)),
                  pl.BlockSpec((None, d, PROJ_BLK), lambda bi, i, j: (layer, 0, j))],
        out_specs=[pl.BlockSpec((ts, PROJ_BLK), lambda bi, i, j: (i, bi * nb + j)),
                   pl.BlockSpec((ts, d), lambda bi, i, j: (i, bi))],
        out_shape=[jax.ShapeDtypeStruct((s, bsz * n), BF16),
                   jax.ShapeDtypeStruct((s, bsz * d), F32)],
        scratch_shapes=[pltpu.VMEM((ts, d), BF16)],
        compiler_params=_cparams(("parallel", "parallel", "arbitrary")),
        name="ln_in_proj",
    )(x, g.reshape(1, d), b.reshape(1, d), w)


def in_proj(hb_sb, w, layer, bsz, ts=2048):
    s = hb_sb.shape[0]
    d, n = w.shape[-2:]
    nb = n // PROJ_BLK
    return pl.pallas_call(
        _inproj_kernel,
        grid=(bsz, s // ts, nb),
        in_specs=[pl.BlockSpec((ts, d), lambda b, i, j: (i, b)),
                  pl.BlockSpec((None, d, PROJ_BLK), lambda b, i, j: (layer, 0, j))],
        out_specs=pl.BlockSpec((ts, PROJ_BLK), lambda b, i, j: (i, b * nb + j)),
        out_shape=jax.ShapeDtypeStruct((s, bsz * n), BF16),
        compiler_params=_cparams(("parallel", "parallel", "arbitrary")),
        name="in_proj",
    )(hb_sb, w)


def _swap_halves(x):
    lane = lax.broadcasted_iota(jnp.int32, x.shape, 1)
    first = (lane % RET_QK_DIM) < (RET_QK_DIM // 2)
    n = x.shape[1]
    return jnp.where(first, pltpu.roll(x, n - RET_QK_DIM // 2, 1), pltpu.roll(x, RET_QK_DIM // 2, 1))


def _retention_kernel(q_ref, k_ref, v0_ref, v1_ref, g0_ref, g1_ref, cos_ref, sin_ref,
                      qd_ref, kd_ref, mask_ref, cd_ref, o_ref, state_ref):
    @pl.when(pl.program_id(1) == 0)
    def _():
        state_ref[...] = jnp.zeros_like(state_ref)

    cos = cos_ref[...]
    sin = sin_ref[...]
    q = q_ref[...].astype(F32)
    k = k_ref[...].astype(F32)
    q = q * cos + _swap_halves(q) * sin
    k = k * cos + _swap_halves(k) * sin
    qb = q.astype(BF16)
    kb = k.astype(BF16)
    qdb = (q * qd_ref[...]).astype(BF16)
    kdb = (k * kd_ref[...]).astype(BF16)
    for hd in range(RET_HEADS):
        qs = slice(hd * RET_QK_DIM, (hd + 1) * RET_QK_DIM)
        half, off = divmod(hd * RET_V_DIM, PROJ_BLK)
        vs = slice(off, off + RET_V_DIM)
        vh = (v0_ref, v1_ref)[half][:, vs]
        gh = (g0_ref, g1_ref)[half][:, vs].astype(F32)
        sc = lax.dot_general(qb[:, qs], kb[:, qs], (((1,), (1,)), ((), ())),
                             preferred_element_type=F32)
        sc = (sc * mask_ref[hd]).astype(BF16)
        st = state_ref[hd]
        o = jnp.dot(sc, vh, preferred_element_type=F32)
        o = o + jnp.dot(qdb[:, qs], st.astype(BF16), preferred_element_type=F32)
        kv = lax.dot_general(kdb[:, qs], vh, (((0,), (0,)), ((), ())),
                             preferred_element_type=F32)
        state_ref[hd] = st * cd_ref[hd] + kv
        mu = jnp.mean(o, axis=-1, keepdims=True)
        oc = o - mu
        var = jnp.mean(oc * oc, axis=-1, keepdims=True)
        on = oc * lax.rsqrt(var + HEAD_NORM_EPS)
        o_ref[:, hd * RET_V_DIM:(hd + 1) * RET_V_DIM] = (gh * jax.nn.sigmoid(gh) * on).astype(BF16)


def retention(proj_sb, tabs, bsz, s):
    cos_t, sin_t, qd_t, kd_t, mask, cd = tabs
    L = RET_SUPER
    nb = N_PROJ_BLK

    def pspec(col):
        return pl.BlockSpec((L, PROJ_BLK), lambda b, i, col=col: (i, b * nb + col))

    full2 = pl.BlockSpec((L, RET_QK), lambda b, i: (0, 0))
    return pl.pallas_call(
        _retention_kernel,
        grid=(bsz, s // L),
        in_specs=[pspec(0), pspec(1), pspec(2), pspec(3), pspec(4), pspec(5),
                  pl.BlockSpec((L, RET_QK), lambda b, i: (i, 0)),
                  pl.BlockSpec((L, RET_QK), lambda b, i: (i, 0)),
                  full2, full2,
                  pl.BlockSpec((RET_HEADS, L, L), lambda b, i: (0, 0, 0)),
                  pl.BlockSpec((RET_HEADS, 1, RET_V_DIM), lambda b, i: (0, 0, 0))],
        out_specs=pl.BlockSpec((L, RET_V), lambda b, i: (i, b)),
        out_shape=jax.ShapeDtypeStruct((s, bsz * RET_V), BF16),
        scratch_shapes=[pltpu.VMEM((RET_HEADS, RET_QK_DIM, RET_V_DIM), F32)],
        compiler_params=_cparams(("parallel", "arbitrary")),
        name="retention",
    )(proj_sb, proj_sb, proj_sb, proj_sb, proj_sb, proj_sb, cos_t, sin_t, qd_t, kd_t, mask, cd)


def retention_tables(s):
    L = RET_SUPER
    half = RET_QK_DIM // 2
    inv_freq = ROPE_BASE ** (-np.arange(half, dtype=np.float64) / half)
    ang = np.arange(s, dtype=np.float64)[:, None] * inv_freq[None, :]
    cos, sin = np.cos(ang), np.sin(ang)
    cos_t = np.tile(np.concatenate([cos, cos], -1), (1, RET_HEADS))
    sin_t = np.tile(np.concatenate([-sin, sin], -1), (1, RET_HEADS))
    log_gamma = np.log1p(-(2.0 ** (-5.0 - np.arange(RET_HEADS, dtype=np.float64))))
    pos = np.arange(L, dtype=np.float64)
    qd = np.exp(log_gamma[None, :] * (pos + 1.0)[:, None])
    k_scale = RET_QK_DIM ** -0.5
    kd = np.exp(log_gamma[None, :] * (L - 1.0 - pos)[:, None]) * k_scale
    qd_t = np.repeat(qd, RET_QK_DIM, axis=1)
    kd_t = np.repeat(kd, RET_QK_DIM, axis=1)
    chunk_id = np.arange(L) // CHUNK
    visible = (chunk_id[None, :] <= chunk_id[:, None]).astype(np.float64)
    mask = np.exp(log_gamma[:, None, None] * np.abs(pos[:, None] - pos[None, :])) * visible[None] * k_scale
    cd = np.broadcast_to(np.exp(log_gamma * L)[:, None, None], (RET_HEADS, 1, RET_V_DIM))
    return tuple(jnp.asarray(a, dtype=F32) for a in (cos_t, sin_t, qd_t, kd_t, mask, cd))


def _s5_param_kernel(lre_ref, lim_ref, ls_ref, bre_ref, bim_ref, are_ref, aim_ref, bbre_ref, bbim_ref):
    lam_re = jnp.minimum(lre_ref[...], -1e-4)
    lam_im = lim_ref[...]
    step = jnp.exp(ls_ref[...])
    mag = jnp.exp(lam_re * step)
    ang = lam_im * step
    ab_re = mag * jnp.cos(ang)
    ab_im = mag * jnp.sin(ang)
    den = lam_re * lam_re + lam_im * lam_im
    n_re = ab_re - 1.0
    zc_re = (n_re * lam_re + ab_im * lam_im) / den
    zc_im = (ab_im * lam_re - n_re * lam_im) / den
    are_ref[...] = ab_re
    aim_ref[...] = ab_im
    b_re = bre_ref[...]
    b_im = bim_ref[...]
    bbre_ref[...] = zc_re * b_re - zc_im * b_im
    bbim_ref[...] = zc_re * b_im + zc_im * b_re


def s5_params(lam_re, lam_im, log_step, b_re, b_im):
    g, n = lam_re.shape
    c = b_re.shape[-1]
    outs = pl.pallas_call(
        _s5_param_kernel,
        out_shape=[jax.ShapeDtypeStruct((g, 1, n), F32), jax.ShapeDtypeStruct((g, 1, n), F32),
                   jax.ShapeDtypeStruct((g, c, n), F32), jax.ShapeDtypeStruct((g, c, n), F32)],
        name="s5_params",
    )(lam_re.reshape(g, 1, n), lam_im.reshape(g, 1, n), log_step.reshape(g, 1, 1),
      jnp.swapaxes(b_re, 1, 2), jnp.swapaxes(b_im, 1, 2))
    return outs


def _block_diag(x):
    ns, gl, r, c = x.shape
    eye = jnp.eye(gl, dtype=x.dtype)
    return jnp.einsum('sgrc,gh->sgrhc', x, eye).reshape(ns, gl * r, gl * c)


def s5_matrices(a_re, a_im, bb_re, bb_im, c_re, c_im):
    gl = S5_COLS // S5_GROUP_CH
    ns = S5_NSLICE
    bre = _block_diag(bb_re.reshape(ns, gl, S5_GROUP_CH, S5_STATE))
    bim = _block_diag(bb_im.reshape(ns, gl, S5_GROUP_CH, S5_STATE))
    bq = jnp.concatenate([bre, bim], axis=-1).astype(BF16)
    cre = _block_diag(jnp.swapaxes(c_re, 1, 2).reshape(ns, gl, S5_STATE, S5_GROUP_CH))
    cim = _block_diag(jnp.swapaxes(c_im, 1, 2).reshape(ns, gl, S5_STATE, S5_GROUP_CH))
    cq = jnp.concatenate([cre, -cim], axis=1).astype(BF16)
    are = a_re.reshape(ns, 1, S5_SLICE_STATE)
    aim = a_im.reshape(ns, 1, S5_SLICE_STATE)
    return bq, cq, are, aim


def _s5_kernel(*refs, bsz, tt):
    u_refs = refs[:bsz]
    bq_ref, cq_ref, are_ref, aim_ref, d_ref, y_ref, us_ref, ys_ref, bu_ref, st_ref = refs[bsz:]

    @pl.when(pl.program_id(0) == 0)
    def _():
        st_ref[...] = jnp.zeros_like(st_ref)

    for b in range(bsz):
        ub = u_refs[b][...].astype(F32)
        for cs in range(S5_NSLICE):
            us_ref[cs, pl.ds(b, tt, stride=bsz), :] = ub[:, cs * S5_COLS:(cs + 1) * S5_COLS]

    ns2 = S5_SLICE_STATE
    for cs in range(S5_NSLICE):
        cols = slice(cs * S5_COLS, (cs + 1) * S5_COLS)
        uf = us_ref[cs]
        bu_ref[cs] = jnp.dot(uf.astype(BF16), bq_ref[cs], preferred_element_type=F32)
        a_re = jnp.broadcast_to(are_ref[cs], (bsz, ns2))
        a_im = jnp.broadcast_to(aim_ref[cs], (bsz, ns2))

        def step(t, carry):
            h_re, h_im = carry
            rows = pl.ds(pl.multiple_of(t * bsz, bsz), bsz)
            n_re = a_re * h_re - a_im * h_im + bu_ref[cs, rows, 0:ns2]
            n_im = a_re * h_im + a_im * h_re + bu_ref[cs, rows, ns2:2 * ns2]
            bu_ref[cs, rows, 0:ns2] = n_re
            bu_ref[cs, rows, ns2:2 * ns2] = n_im
            return n_re, n_im

        h_re, h_im = lax.fori_loop(0, tt, step, (st_ref[cs, 0], st_ref[cs, 1]), unroll=True)
        st_ref[cs, 0] = h_re
        st_ref[cs, 1] = h_im
        y = jnp.dot(bu_ref[cs].astype(BF16), cq_ref[cs], preferred_element_type=F32)
        ys_ref[cs] = y + d_ref[:, cols] * uf

    for b in range(bsz):
        for cs in range(S5_NSLICE):
            lo = b * S5_WIDTH + cs * S5_COLS
            y_ref[:, lo:lo + S5_COLS] = ys_ref[cs, pl.ds(b, tt, stride=bsz), :]


def s5_scan(proj_sb, mats, d_skip, bsz, s):
    bq, cq, are, aim = mats
    tt = S5_TT
    rows = tt * bsz
    nb = N_PROJ_BLK
    kern = functools.partial(_s5_kernel, bsz=bsz, tt=tt)
    u_specs = [pl.BlockSpec((tt, PROJ_BLK), lambda i, b=b: (i, b * nb + 6)) for b in range(bsz)]
    return pl.pallas_call(
        kern,
        grid=(s // tt,),
        in_specs=u_specs + [
                  pl.BlockSpec(bq.shape, lambda i: (0, 0, 0)),
                  pl.BlockSpec(cq.shape, lambda i: (0, 0, 0)),
                  pl.BlockSpec(are.shape, lambda i: (0, 0, 0)),
                  pl.BlockSpec(aim.shape, lambda i: (0, 0, 0)),
                  pl.BlockSpec((1, S5_WIDTH), lambda i: (0, 0))],
        out_specs=pl.BlockSpec((tt, bsz * S5_WIDTH), lambda i: (i, 0)),
        out_shape=jax.ShapeDtypeStruct((s, bsz * S5_WIDTH), F32),
        scratch_shapes=[pltpu.VMEM((S5_NSLICE, rows, S5_COLS), F32),
                        pltpu.VMEM((S5_NSLICE, rows, S5_COLS), F32),
                        pltpu.VMEM((S5_NSLICE, rows, 2 * S5_SLICE_STATE), F32),
                        pltpu.VMEM((S5_NSLICE, 2, bsz, S5_SLICE_STATE), F32)],
        compiler_params=_cparams(("arbitrary",)),
        name="s5_scan",
    )(*([proj_sb] * bsz), bq, cq, are, aim, d_skip.reshape(1, S5_WIDTH))


def _gelu_tanh(x):
    c = math.sqrt(2.0 / math.pi)
    return 0.5 * x * (1.0 + jnp.tanh(c * (x + 0.044715 * (x * x * x))))


def _route(logits):
    lane = lax.broadcasted_iota(jnp.int32, logits.shape, 1)
    neg = jnp.float32(-jnp.inf)
    big = jnp.int32(1 << 20)
    is_g = lane < MOE_GROUPS
    lg = jnp.where(is_g, logits, neg)
    mg = jnp.max(lg, axis=-1, keepdims=True)
    sg = jnp.sum(jnp.where(is_g, jnp.exp(lg - mg), 0.0), axis=-1, keepdims=True)
    g_top = 1.0 / sg
    g_idx = jnp.min(jnp.where(lg == mg, lane, big), axis=-1, keepdims=True)
    lo = MOE_GROUPS + g_idx * EXPERTS_PER_GROUP
    in_grp = (lane >= lo) & (lane < lo + EXPERTS_PER_GROUP)
    le = jnp.where(in_grp, logits, neg)
    m1 = jnp.max(le, axis=-1, keepdims=True)
    se = jnp.sum(jnp.where(in_grp, jnp.exp(le - m1), 0.0), axis=-1, keepdims=True)
    i1 = jnp.min(jnp.where(le == m1, lane, big), axis=-1, keepdims=True)
    le2 = jnp.where(lane == i1, neg, le)
    m2 = jnp.max(le2, axis=-1, keepdims=True)
    i2 = jnp.min(jnp.where(le2 == m2, lane, big), axis=-1, keepdims=True)
    p1 = 1.0 / se
    p2 = jnp.exp(m2 - m1) / se
    tot = p1 + p2
    w1 = g_top * (p1 / tot)
    w2 = g_top * (p2 / tot)
    return lane, i1, i2, w1, w2


def _mix_kernel(ro_ref, y_ref, gr0_ref, gr1_ref, gs0_ref, gs1_ref, h_ref,
                wglu_ref, wbr_ref, wbs_ref, wout_ref, g_ref, b_ref, wr_ref, br_ref, tri_ref,
                h1_ref, route_ref, cnt_ref, blk_ref, lp_ref, run_ref):
    @pl.when((pl.program_id(0) == 0) & (pl.program_id(1) == 0))
    def _():
        run_ref[...] = jnp.zeros_like(run_ref)

    z = _gelu_tanh(y_ref[...])
    zg = jnp.dot(z.astype(BF16), wglu_ref[...], preferred_element_type=F32)
    zz = (z * jax.nn.sigmoid(zg)).astype(BF16)
    s5b = jnp.dot(zz, wbs_ref[...], preferred_element_type=F32)
    rb = jnp.dot(ro_ref[...], wbr_ref[...], preferred_element_type=F32)
    gr = jnp.concatenate([gr0_ref[...], gr1_ref[...]], axis=-1).astype(F32)
    gs = jnp.concatenate([gs0_ref[...], gs1_ref[...]], axis=-1).astype(F32)
    merged = jax.nn.sigmoid(gr) * rb + jax.nn.sigmoid(gs) * s5b
    mix = jnp.dot(merged.astype(BF16), wout_ref[...], preferred_element_type=F32)
    h1 = _layer_norm(DN_ALPHA * h_ref[...] + mix, g_ref[...], b_ref[...])
    h1_ref[...] = h1
    h_hi = h1.astype(BF16)
    h_lo = (h1 - h_hi.astype(F32)).astype(BF16)
    both = jnp.dot(h_hi, wr_ref[...], preferred_element_type=F32)
    logits = (both[:, :ROUTE_LANES] + both[:, ROUTE_LANES:]
              + jnp.dot(h_lo, wr_ref[:, :ROUTE_LANES], preferred_element_type=F32)) + br_ref[...]
    lane, i1, i2, w1, w2 = _route(logits)
    oh1 = lane == i1
    oh2 = lane == i2
    oh = jnp.where(oh1 | oh2, 1.0, 0.0)
    rank_in_blk = jnp.dot(tri_ref[...], oh.astype(BF16), preferred_element_type=F32)
    blk_cnt = jnp.broadcast_to(jnp.sum(oh, axis=0, keepdims=True), run_ref.shape)
    lane8 = lax.broadcasted_iota(jnp.int32, run_ref.shape, 1)
    incl = blk_cnt
    shift = 1
    while shift < ROUTE_LANES:
        incl = incl + jnp.where(lane8 >= shift, pltpu.roll(incl, shift, 1), 0.0)
        shift *= 2
    lstart = incl - blk_cnt
    pos = lstart[0:1, :] + rank_in_blk
    l1 = jnp.sum(jnp.where(oh1, pos, 0.0), axis=-1, keepdims=True)
    l2 = jnp.sum(jnp.where(oh2, pos, 0.0), axis=-1, keepdims=True)
    run_before = run_ref[...]
    row8 = lax.broadcasted_iota(jnp.int32, run_ref.shape, 0)
    blk_ref[...] = jnp.where(row8 == 0, run_before,
                             jnp.where(row8 == 1, blk_cnt, jnp.where(row8 == 2, lstart, 0.0)))
    cnt = run_before + blk_cnt
    run_ref[...] = cnt
    cnt_ref[...] = cnt
    vals = (i1.astype(F32) - MOE_GROUPS, i2.astype(F32) - MOE_GROUPS, w1, w2)
    route = jnp.zeros(logits.shape, F32)
    for k, v in enumerate(vals):
        route = jnp.where(lane == k, v, route)
    route_ref[...] = route
    pos_t = jnp.transpose(jnp.where(lane == 0, l1, jnp.where(lane == 1, l2, 0.0)))
    lp_ref[...] = pos_t[0:8, :].astype(jnp.int32)


def mix_out(ro, y, proj, h, wglu, wbr, wbs, wout, g, b, wr, br, bsz, ts=MOE_TS):
    s = h.shape[0]
    d = D_MODEL
    nb = N_PROJ_BLK

    def const(a):
        return pl.BlockSpec(a.shape, lambda bi, i: (0,) * a.ndim)

    def pspec(col):
        return pl.BlockSpec((ts, PROJ_BLK), lambda bi, i, col=col: (i, bi * nb + col))

    def tok(w):
        return pl.BlockSpec((ts, w), lambda bi, i: (i, bi))

    g2, b2 = g.reshape(1, d), b.reshape(1, d)
    tri = jnp.asarray(np.tril(np.ones((ts, ts)), -1), dtype=BF16)
    return pl.pallas_call(
        _mix_kernel,
        grid=(bsz, s // ts),
        in_specs=[tok(RET_V), tok(S5_WIDTH), pspec(7), pspec(8), pspec(9), pspec(10), tok(d),
                  const(wglu), const(wbr), const(wbs), const(wout), const(g2), const(b2),
                  const(wr), const(br), const(tri)],
        out_specs=[tok(d), tok(ROUTE_LANES), pl.BlockSpec((8, ROUTE_LANES), lambda bi, i: (0, 0)),
                   pl.BlockSpec((None, 8, ROUTE_LANES), lambda bi, i: (bi * (s // ts) + i, 0, 0)),
                   pl.BlockSpec((None, 8, ts), lambda bi, i: (bi * (s // ts) + i, 0, 0))],
        out_shape=[jax.ShapeDtypeStruct((s, bsz * d), F32),
                   jax.ShapeDtypeStruct((s, bsz * ROUTE_LANES), F32),
                   jax.ShapeDtypeStruct((8, ROUTE_LANES), F32),
                   jax.ShapeDtypeStruct((bsz * (s // ts), 8, ROUTE_LANES), F32),
                   jax.ShapeDtypeStruct((bsz * (s // ts), 8, ts), jnp.int32)],
        scratch_shapes=[pltpu.VMEM((8, ROUTE_LANES), F32)],
        compiler_params=_cparams(("arbitrary", "arbitrary")),
        name="mix_out",
    )(ro, y, proj, proj, proj, proj, h, wglu, wbr, wbs, wout, g2, b2, wr, br, tri)


def moe_rows(t):
    return TOP_K * t + (N_EXPERTS + 1) * MOE_TM


def moe_plan(lp, cnt, blk, bsz, s):
    t = bsz * s
    experts = slice(MOE_GROUPS, MOE_GROUPS + N_EXPERTS)
    counts = cnt[0, experts].astype(jnp.int32)
    padded = ((counts + MOE_TM - 1) // MOE_TM) * MOE_TM
    ends = jnp.cumsum(padded)
    off = ends - padded
    run_start = (off[None, :] + blk[:, 0, experts].astype(jnp.int32)).reshape(-1)
    run_len = blk[:, 1, experts].astype(jnp.int32).reshape(-1)
    run_local = blk[:, 2, experts].astype(jnp.int32).reshape(-1)
    pads = (off + counts, padded - counts)
    n_tiles = moe_rows(t) // MOE_TM
    n_used = (ends[-1:] // MOE_TM).astype(jnp.int32)
    tile_start = jnp.arange(n_tiles, dtype=jnp.int32) * MOE_TM
    tile_expert = jnp.sum((ends[None, :] <= tile_start[:, None]).astype(jnp.int32), axis=1)
    tile_expert = jnp.minimum(tile_expert, N_EXPERTS - 1)
    return (lp[:, 0, :].reshape(t), lp[:, 1, :].reshape(t), run_start, run_len, run_local), pads, tile_expert, n_used


def _token_rows(start, n):
    return pl.ds(pl.multiple_of(start * ROW_TILE, ROW_TILE), n * ROW_TILE)


def _for_each_piece(n, nbits, fn):
    for bit in reversed(range(nbits)):
        size = 1 << bit
        done = (n >> (bit + 1)) << (bit + 1)

        @pl.when((n & size) != 0)
        def _(size=size, done=done):
            fn(done, size)


def _run_copies(blk, run_refs, make_copy):
    run_start_ref, run_len_ref, run_local_ref = run_refs
    for e in range(N_EXPERTS):
        idx = blk * N_EXPERTS + e
        g0 = run_start_ref[idx]
        l0 = run_local_ref[idx]
        _for_each_piece(run_len_ref[idx], RUN_BITS,
                        lambda off, size, g0=g0, l0=l0: make_copy(l0 + off, g0 + off, size).start())


def _to_token_tiles(ref, x):
    n = x.shape[0]
    for j in range(ROW_TILE):
        ref[pl.ds(j, n, stride=ROW_TILE), :] = x[:, j * LANES:(j + 1) * LANES]


def _from_token_tiles(ref, n):
    return jnp.concatenate([ref[pl.ds(j, n, stride=ROW_TILE), :] for j in range(ROW_TILE)], axis=-1)


def _dispatch_kernel(lp1_ref, lp2_ref, run_start_ref, run_len_ref, run_local_ref,
                     pad_ref, pad_len_ref, nu_ref,
                     x_ref, xs_ref, xt_ref, srt_ref, zero_ref, sem, zsem, *, ts):
    run_refs = (run_start_ref, run_len_ref, run_local_ref)
    nsteps = pl.num_programs(0) * pl.num_programs(1)
    step = pl.program_id(0) * pl.num_programs(1) + pl.program_id(1)

    def zero_copy(start, size):
        return pltpu.make_async_copy(zero_ref.at[_token_rows(0, size)],
                                     xs_ref.at[_token_rows(start, size)], zsem)

    def zero_fill(action):
        for e in range(N_EXPERTS):
            p0 = pad_ref[e]
            _for_each_piece(pad_len_ref[e], PAD_BITS,
                            lambda off, size, p0=p0: action(zero_copy(p0 + off, size)))

        def tail_tile(r, c):
            action(zero_copy(r * MOE_TM, MOE_TM))
            return c

        lax.fori_loop(nu_ref[0], xs_ref.shape[0] // (MOE_TM * ROW_TILE), tail_tile, 0)

    @pl.when(step == 0)
    def _():
        zero_ref[...] = jnp.zeros_like(zero_ref)
        zero_fill(lambda cp: cp.start())

    @pl.when(step == nsteps - 1)
    def _():
        zero_fill(lambda cp: cp.wait())

    slot = step % 2
    _to_token_tiles(xt_ref, x_ref[...])
    base = step * ts

    def place(i, c):
        tile = xt_ref[_token_rows(i, 1), :]
        srt_ref[slot, _token_rows(lp1_ref[base + i], 1), :] = tile
        srt_ref[slot, _token_rows(lp2_ref[base + i], 1), :] = tile
        return c

    lax.fori_loop(0, ts, place, 0, unroll=8)

    def run_copy(local_start, sorted_start, size):
        return pltpu.make_async_copy(srt_ref.at[slot, _token_rows(local_start, size)],
                                     xs_ref.at[_token_rows(sorted_start, size)], sem.at[slot])

    _run_copies(step, run_refs, run_copy)

    def drain(sl):
        pltpu.make_async_copy(srt_ref.at[sl], xs_ref.at[_token_rows(0, TOP_K * ts)], sem.at[sl]).wait()

    @pl.when(step > 0)
    def _():
        drain(1 - slot)

    @pl.when(step == nsteps - 1)
    def _():
        drain(slot)


def moe_dispatch(h1, plan, pads, n_used, bsz, ts=MOE_TS):
    s = h1.shape[0]
    d = D_MODEL
    kern = functools.partial(_dispatch_kernel, ts=ts)
    grid_spec = pltpu.PrefetchScalarGridSpec(
        num_scalar_prefetch=len(plan) + len(pads) + 1,
        grid=(bsz, s // ts),
        in_specs=[pl.BlockSpec((ts, d), lambda bi, i, *_: (i, bi))],
        out_specs=pl.BlockSpec(memory_space=pl.ANY),
        scratch_shapes=[pltpu.VMEM((ts * ROW_TILE, LANES), F32),
                        pltpu.VMEM((2, TOP_K * ts * ROW_TILE, LANES), F32),
                        pltpu.VMEM((MOE_TM * ROW_TILE, LANES), F32),
                        pltpu.SemaphoreType.DMA((2,)), pltpu.SemaphoreType.DMA(())],
    )
    return pl.pallas_call(
        kern,
        grid_spec=grid_spec,
        out_shape=jax.ShapeDtypeStruct((moe_rows(bsz * s) * ROW_TILE, LANES), F32),
        compiler_params=_cparams(("arbitrary", "arbitrary")),
        name="moe_dispatch",
    )(*plan, *pads, n_used, h1)


def _experts_kernel(te_ref, nu_ref, xs_ref, wg_ref, wu_ref, wd_ref, ys_ref, wgb_ref, wub_ref, wdb_ref):
    r = pl.program_id(0)

    @pl.when(r < nu_ref[0])
    def _():
        changed = (r == 0) | (te_ref[r] != te_ref[jnp.maximum(r - 1, 0)])

        @pl.when(changed)
        def _():
            wgb_ref[...] = wg_ref[...].astype(BF16)
            wub_ref[...] = wu_ref[...].astype(BF16)
            wdb_ref[...] = wd_ref[...].astype(BF16)

        x = _from_token_tiles(xs_ref, MOE_TM).astype(BF16)
        gate = jnp.dot(x, wgb_ref[...], preferred_element_type=F32)
        up = jnp.dot(x, wub_ref[...], preferred_element_type=F32)
        act = (gate * jax.nn.sigmoid(gate) * up).astype(BF16)
        _to_token_tiles(ys_ref, jnp.dot(act, wdb_ref[...], preferred_element_type=F32))

    @pl.when(r >= nu_ref[0])
    def _():
        ys_ref[...] = jnp.zeros_like(ys_ref)


def moe_experts(xs, tile_expert, n_used, wg, wu, wd, layer):
    d, f = wg.shape[-2:]
    rows = xs.shape[0] // ROW_TILE
    blk = MOE_TM * ROW_TILE

    def tile(r, te, nu):
        return jnp.minimum(r, nu[0] - 1)

    def wspec(a, b):
        return pl.BlockSpec((None, None, a, b), lambda r, te, nu: (layer, te[tile(r, te, nu)], 0, 0))

    grid_spec = pltpu.PrefetchScalarGridSpec(
        num_scalar_prefetch=2,
        grid=(rows // MOE_TM,),
        in_specs=[pl.BlockSpec((blk, LANES), lambda r, te, nu: (tile(r, te, nu), 0)),
                  wspec(d, f), wspec(d, f), wspec(f, d)],
        out_specs=pl.BlockSpec((blk, LANES), lambda r, te, nu: (r, 0)),
        scratch_shapes=[pltpu.VMEM((d, f), BF16), pltpu.VMEM((d, f), BF16), pltpu.VMEM((f, d), BF16)],
    )
    return pl.pallas_call(
        _experts_kernel,
        grid_spec=grid_spec,
        out_shape=jax.ShapeDtypeStruct(xs.shape, F32),
        compiler_params=_cparams(("arbitrary",)),
        name="moe_experts",
    )(tile_expert, n_used, xs, wg, wu, wd)


def _combine_kernel(lp1_ref, lp2_ref, run_start_ref, run_len_ref, run_local_ref,
                    ys_ref, route_ref, h_ref, g_ref, b_ref, *refs, ts):
    o_ref, maybe_ob_ref = refs[0], refs[1:-4]
    srt_ref, t1_ref, t2_ref, sem = refs[-4:]
    run_refs = (run_start_ref, run_len_ref, run_local_ref)
    nsteps = pl.num_programs(0) * pl.num_programs(1)
    step = pl.program_id(0) * pl.num_programs(1) + pl.program_id(1)

    def issue(blk, sl):
        def run_copy(local_start, sorted_start, size):
            return pltpu.make_async_copy(ys_ref.at[_token_rows(sorted_start, size)],
                                         srt_ref.at[sl, _token_rows(local_start, size)], sem.at[sl])
        _run_copies(blk, run_refs, run_copy)

    @pl.when(step == 0)
    def _():
        issue(0, 0)

    @pl.when(step + 1 < nsteps)
    def _():
        issue(step + 1, (step + 1) % 2)

    slot = step % 2
    pltpu.make_async_copy(ys_ref.at[_token_rows(0, TOP_K * ts)], srt_ref.at[slot], sem.at[slot]).wait()

    base = step * ts

    def pick(i, c):
        t1_ref[_token_rows(i, 1), :] = srt_ref[slot, _token_rows(lp1_ref[base + i], 1), :]
        t2_ref[_token_rows(i, 1), :] = srt_ref[slot, _token_rows(lp2_ref[base + i], 1), :]
        return c

    lax.fori_loop(0, ts, pick, 0, unroll=8)

    route = route_ref[...]
    w1 = route[:, 2:3]
    w2 = route[:, 3:4]
    ffn = w1 * _from_token_tiles(t1_ref, ts) + w2 * _from_token_tiles(t2_ref, ts)
    h2 = _layer_norm(DN_ALPHA * h_ref[...] + ffn, g_ref[...], b_ref[...])
    o_ref[...] = h2
    for ob_ref in maybe_ob_ref:
        ob_ref[...] = h2.astype(BF16)


def moe_combine(ys, plan, route, h1, g, b, bsz, last, ts=MOE_TS):
    s = h1.shape[0]
    d = D_MODEL
    g2, b2 = g.reshape(1, d), b.reshape(1, d)
    if last:
        o_specs = [pl.BlockSpec((None, ts, d), lambda bi, i, *_: (bi, i, 0))]
        o_shapes = [jax.ShapeDtypeStruct((bsz, s, d), F32)]
    else:
        o_specs = [pl.BlockSpec((ts, d), lambda bi, i, *_: (i, bi))] * 2
        o_shapes = [jax.ShapeDtypeStruct((s, bsz * d), F32), jax.ShapeDtypeStruct((s, bsz * d), BF16)]
    kern = functools.partial(_combine_kernel, ts=ts)
    grid_spec = pltpu.PrefetchScalarGridSpec(
        num_scalar_prefetch=len(plan),
        grid=(bsz, s // ts),
        in_specs=[pl.BlockSpec(memory_space=pl.ANY),
                  pl.BlockSpec((ts, ROUTE_LANES), lambda bi, i, *_: (i, bi)),
                  pl.BlockSpec((ts, d), lambda bi, i, *_: (i, bi)),
                  pl.BlockSpec((1, d), lambda bi, i, *_: (0, 0)),
                  pl.BlockSpec((1, d), lambda bi, i, *_: (0, 0))],
        out_specs=o_specs,
        scratch_shapes=[pltpu.VMEM((2, TOP_K * ts * ROW_TILE, LANES), F32),
                        pltpu.VMEM((ts * ROW_TILE, LANES), F32), pltpu.VMEM((ts * ROW_TILE, LANES), F32),
                        pltpu.SemaphoreType.DMA((2,))],
    )
    return pl.pallas_call(
        kern,
        grid_spec=grid_spec,
        out_shape=o_shapes,
        compiler_params=_cparams(("arbitrary", "arbitrary")),
        name="moe_combine",
    )(*plan, ys, route, h1, g2, b2)


def kernel(x, ln_in_g, ln_in_b, w_in, s5_lambda_re, s5_lambda_im, s5_log_step, s5_b_re, s5_b_im,
           s5_c_re, s5_c_im, s5_d, w_glu, w_branch_ret, w_branch_s5, w_out, ln_mix_g, ln_mix_b,
           w_router_group, b_router_group, w_router_expert, b_router_expert, w_exp_gate, w_exp_up,
           w_exp_down, ln_ffn_g, ln_ffn_b):
    bsz, s, d = x.shape
    t = bsz * s
    depth = w_in.shape[0]
    tabs = retention_tables(s)
    out = None
    for l in range(depth):
        if l == 0:
            proj, h_sb = ln_in_proj(x, ln_in_g, ln_in_b, w_in, l)
        else:
            proj = in_proj(hb_sb, w_in, l, bsz)
        ro = retention(proj, tabs, bsz, s)
        a_re, a_im, bb_re, bb_im = s5_params(s5_lambda_re[l], s5_lambda_im[l], s5_log_step[l],
                                             s5_b_re[l], s5_b_im[l])
        mats = s5_matrices(a_re, a_im, bb_re, bb_im, s5_c_re[l], s5_c_im[l])
        y = s5_scan(proj, mats, s5_d[l], bsz, s)
        pad = ROUTE_LANES - MOE_GROUPS - N_EXPERTS
        wr = jnp.concatenate([w_router_group[l], w_router_expert[l], jnp.zeros((d, pad), F32)], axis=1)
        wr_hi = wr.astype(BF16)
        wr = jnp.concatenate([wr_hi, (wr - wr_hi.astype(F32)).astype(BF16)], axis=1)
        br = jnp.concatenate([b_router_group[l], b_router_expert[l], jnp.zeros((pad,), F32)]).reshape(1, -1)
        h1, route, cnt, blk, lp = mix_out(ro, y, proj, h_sb,
                                 w_glu[l].astype(BF16), w_branch_ret[l].astype(BF16),
                                 w_branch_s5[l].astype(BF16), w_out[l].astype(BF16),
                                 ln_mix_g[l], ln_mix_b[l], wr, br, bsz)
        last = l == depth - 1
        plan, pads, tile_expert, n_used = moe_plan(lp, cnt, blk, bsz, s)
        xs = moe_dispatch(h1, plan, pads, n_used, bsz)
        ys = moe_experts(xs, tile_expert, n_used, w_exp_gate, w_exp_up, w_exp_down, l)
        outs = moe_combine(ys, plan, route, h1, ln_ffn_g[l], ln_ffn_b[l], bsz, last)
        if last:
            out = outs[0]
        else:
            h_sb, hb_sb = outs
    return out
```

```python
import functools
import math

import jax
import jax.numpy as jnp
import numpy as np
from jax import lax
from jax.experimental import pallas as pl
from jax.experimental.pallas import tpu as pltpu

D_MODEL = 1024
CHUNK = 64
RET_HEADS = 8
RET_QK = 512
RET_V = 1024
RET_QK_DIM = 64
RET_V_DIM = 128
ROPE_BASE = 10000.0
S5_WIDTH = 512
S5_GROUP_CH = 16
S5_GROUPS = 32
S5_STATE = 64
MOE_GROUPS = 4
EXPERTS_PER_GROUP = 8
N_EXPERTS = 32
EXPERT_FF = 256
LN_EPS = 1e-5
HEAD_NORM_EPS = 1e-6
DEPTH = 2
DN_ALPHA = (2 * DEPTH) ** 0.25
IN_WIDTH = 2 * RET_QK + 2 * RET_V + S5_WIDTH + 2 * D_MODEL
PROJ_BLK = 512
N_PROJ_BLK = IN_WIDTH // PROJ_BLK
INPROJ_BATCHES = 2

RET_SUPER = 256
S5_TT = 64
S5_COLS = 128
S5_NSLICE = S5_WIDTH // S5_COLS
S5_SLICE_STATE = (S5_COLS // S5_GROUP_CH) * S5_STATE
ROUTE_LANES = 128
TOP_K = 2
MOE_TM = 512
MOE_TS = 512
RUN_BITS = MOE_TS.bit_length()
PAD_BITS = (MOE_TM - 1).bit_length()
LANES = 128
ROW_TILE = D_MODEL // LANES
VMEM_LIMIT = 56 * 1024 * 1024

F32 = jnp.float32
BF16 = jnp.bfloat16


def _cparams(sem):
    return pltpu.CompilerParams(dimension_semantics=sem, vmem_limit_bytes=VMEM_LIMIT)


def _layer_norm(x, g, b):
    mu = jnp.mean(x, axis=-1, keepdims=True)
    xc = x - mu
    var = jnp.mean(xc * xc, axis=-1, keepdims=True)
    return xc * lax.rsqrt(var + LN_EPS) * g + b


def _inproj_kernel(h_ref, w_ref, o_ref):
    w = w_ref[...].astype(BF16)
    d = w.shape[0]
    for k in range(INPROJ_BATCHES):
        hk = h_ref[:, k * d:(k + 1) * d]
        o_ref[:, k * PROJ_BLK:(k + 1) * PROJ_BLK] = jnp.dot(hk, w, preferred_element_type=F32).astype(BF16)


def _ln_inproj_kernel(x_ref, g_ref, b_ref, w_ref, o_ref, h_ref, hb_ref):
    @pl.when(pl.program_id(2) == 0)
    def _():
        h = _layer_norm(x_ref[...], g_ref[...], b_ref[...])
        h_ref[...] = h
        hb_ref[...] = h.astype(BF16)

    w = w_ref[...].astype(BF16)
    o_ref[...] = jnp.dot(hb_ref[...], w, preferred_element_type=F32).astype(BF16)


def ln_in_proj(x, g, b, w, layer, ts=2048):
    bsz, s, d = x.shape
    n = w.shape[-1]
    nb = n // PROJ_BLK
    return pl.pallas_call(
        _ln_inproj_kernel,
        grid=(bsz, s // ts, nb),
        in_specs=[pl.BlockSpec((None, ts, d), lambda bi, i, j: (bi, i, 0)),
                  pl.BlockSpec((1, d), lambda bi, i, j: (0, 0)),
                  pl.BlockSpec((1, d), lambda bi, i, j: (0, 0)),
                  pl.BlockSpec((None, d, PROJ_BLK), lambda bi, i, j: (layer, 0, j))],
        out_specs=[pl.BlockSpec((ts, PROJ_BLK), lambda bi, i, j: (i, j * bsz + bi)),
                   pl.BlockSpec((ts, d), lambda bi, i, j: (i, bi))],
        out_shape=[jax.ShapeDtypeStruct((s, bsz * n), BF16),
                   jax.ShapeDtypeStruct((s, bsz * d), F32)],
        scratch_shapes=[pltpu.VMEM((ts, d), BF16)],
        compiler_params=_cparams(("parallel", "parallel", "arbitrary")),
        name="ln_in_proj",
    )(x, g.reshape(1, d), b.reshape(1, d), w)


def in_proj(hb_sb, w, layer, bsz, ts=2048):
    s = hb_sb.shape[0]
    d, n = w.shape[-2:]
    nb = n // PROJ_BLK
    kb = INPROJ_BATCHES
    return pl.pallas_call(
        _inproj_kernel,
        grid=(bsz // kb, s // ts, nb),
        in_specs=[pl.BlockSpec((ts, kb * d), lambda b, i, j: (i, b)),
                  pl.BlockSpec((None, d, PROJ_BLK), lambda b, i, j: (layer, 0, j))],
        out_specs=pl.BlockSpec((ts, kb * PROJ_BLK), lambda b, i, j: (i, (j * bsz) // kb + b)),
        out_shape=jax.ShapeDtypeStruct((s, bsz * n), BF16),
        compiler_params=_cparams(("parallel", "parallel", "arbitrary")),
        name="in_proj",
    )(hb_sb, w)


def _swap_halves(x):
    lane = lax.broadcasted_iota(jnp.int32, x.shape, 1)
    first = (lane % RET_QK_DIM) < (RET_QK_DIM // 2)
    n = x.shape[1]
    return jnp.where(first, pltpu.roll(x, n - RET_QK_DIM // 2, 1), pltpu.roll(x, RET_QK_DIM // 2, 1))


def _retention_kernel(q_ref, k_ref, v0_ref, v1_ref, g0_ref, g1_ref, cos_ref, sin_ref,
                      qd_ref, kd_ref, mask_ref, cd_ref, o_ref, state_ref):
    @pl.when(pl.program_id(1) == 0)
    def _():
        state_ref[...] = jnp.zeros_like(state_ref)

    cos = cos_ref[...]
    sin = sin_ref[...]
    q = q_ref[...].astype(F32)
    k = k_ref[...].astype(F32)
    q = q * cos + _swap_halves(q) * sin
    k = k * cos + _swap_halves(k) * sin
    qb = q.astype(BF16)
    kb = k.astype(BF16)
    qdb = (q * qd_ref[...]).astype(BF16)
    kdb = (k * kd_ref[...]).astype(BF16)
    for hd in range(RET_HEADS):
        qs = slice(hd * RET_QK_DIM, (hd + 1) * RET_QK_DIM)
        half, off = divmod(hd * RET_V_DIM, PROJ_BLK)
        vs = slice(off, off + RET_V_DIM)
        vh = (v0_ref, v1_ref)[half][:, vs]
        gh = (g0_ref, g1_ref)[half][:, vs].astype(F32)
        sc = lax.dot_general(qb[:, qs], kb[:, qs], (((1,), (1,)), ((), ())),
                             preferred_element_type=F32)
        sc = (sc * mask_ref[hd]).astype(BF16)
        st = state_ref[hd]
        o = jnp.dot(sc, vh, preferred_element_type=F32)
        o = o + jnp.dot(qdb[:, qs], st.astype(BF16), preferred_element_type=F32)
        kv = lax.dot_general(kdb[:, qs], vh, (((0,), (0,)), ((), ())),
                             preferred_element_type=F32)
        state_ref[hd] = st * cd_ref[hd] + kv
        mu = jnp.mean(o, axis=-1, keepdims=True)
        oc = o - mu
        var = jnp.mean(oc * oc, axis=-1, keepdims=True)
        on = oc * lax.rsqrt(var + HEAD_NORM_EPS)
        o_ref[:, hd * RET_V_DIM:(hd + 1) * RET_V_DIM] = (gh * jax.nn.sigmoid(gh) * on).astype(BF16)


def retention(proj_sb, tabs, bsz, s):
    cos_t, sin_t, qd_t, kd_t, mask, cd = tabs
    L = RET_SUPER

    def pspec(col):
        return pl.BlockSpec((L, PROJ_BLK), lambda b, i, col=col: (i, col * bsz + b))

    full2 = pl.BlockSpec((L, RET_QK), lambda b, i: (0, 0))
    return pl.pallas_call(
        _retention_kernel,
        grid=(bsz, s // L),
        in_specs=[pspec(0), pspec(1), pspec(2), pspec(3), pspec(4), pspec(5),
                  pl.BlockSpec((L, RET_QK), lambda b, i: (i, 0)),
                  pl.BlockSpec((L, RET_QK), lambda b, i: (i, 0)),
                  full2, full2,
                  pl.BlockSpec((RET_HEADS, L, L), lambda b, i: (0, 0, 0)),
                  pl.BlockSpec((RET_HEADS, 1, RET_V_DIM), lambda b, i: (0, 0, 0))],
        out_specs=pl.BlockSpec((L, RET_V), lambda b, i: (i, b)),
        out_shape=jax.ShapeDtypeStruct((s, bsz * RET_V), BF16),
        scratch_shapes=[pltpu.VMEM((RET_HEADS, RET_QK_DIM, RET_V_DIM), F32)],
        compiler_params=_cparams(("parallel", "arbitrary")),
        name="retention",
    )(proj_sb, proj_sb, proj_sb, proj_sb, proj_sb, proj_sb, cos_t, sin_t, qd_t, kd_t, mask, cd)


def retention_tables(s):
    L = RET_SUPER
    half = RET_QK_DIM // 2
    inv_freq = ROPE_BASE ** (-np.arange(half, dtype=np.float64) / half)
    ang = np.arange(s, dtype=np.float64)[:, None] * inv_freq[None, :]
    cos, sin = np.cos(ang), np.sin(ang)
    cos_t = np.tile(np.concatenate([cos, cos], -1), (1, RET_HEADS))
    sin_t = np.tile(np.concatenate([-sin, sin], -1), (1, RET_HEADS))
    log_gamma = np.log1p(-(2.0 ** (-5.0 - np.arange(RET_HEADS, dtype=np.float64))))
    pos = np.arange(L, dtype=np.float64)
    qd = np.exp(log_gamma[None, :] * (pos + 1.0)[:, None])
    k_scale = RET_QK_DIM ** -0.5
    kd = np.exp(log_gamma[None, :] * (L - 1.0 - pos)[:, None]) * k_scale
    qd_t = np.repeat(qd, RET_QK_DIM, axis=1)
    kd_t = np.repeat(kd, RET_QK_DIM, axis=1)
    chunk_id = np.arange(L) // CHUNK
    visible = (chunk_id[None, :] <= chunk_id[:, None]).astype(np.float64)
    mask = np.exp(log_gamma[:, None, None] * np.abs(pos[:, None] - pos[None, :])) * visible[None] * k_scale
    cd = np.broadcast_to(np.exp(log_gamma * L)[:, None, None], (RET_HEADS, 1, RET_V_DIM))
    return tuple(jnp.asarray(a, dtype=F32) for a in (cos_t, sin_t, qd_t, kd_t, mask, cd))


def _s5_param_kernel(lre_ref, lim_ref, ls_ref, bre_ref, bim_ref, are_ref, aim_ref, bbre_ref, bbim_ref):
    lam_re = jnp.minimum(lre_ref[...], -1e-4)
    lam_im = lim_ref[...]
    step = jnp.exp(ls_ref[...])
    mag = jnp.exp(lam_re * step)
    ang = lam_im * step
    ab_re = mag * jnp.cos(ang)
    ab_im = mag * jnp.sin(ang)
    den = lam_re * lam_re + lam_im * lam_im
    n_re = ab_re - 1.0
    zc_re = (n_re * lam_re + ab_im * lam_im) / den
    zc_im = (ab_im * lam_re - n_re * lam_im) / den
    are_ref[...] = ab_re
    aim_ref[...] = ab_im
    b_re = bre_ref[...]
    b_im = bim_ref[...]
    bbre_ref[...] = zc_re * b_re - zc_im * b_im
    bbim_ref[...] = zc_re * b_im + zc_im * b_re


def s5_params(lam_re, lam_im, log_step, b_re, b_im):
    g, n = lam_re.shape
    c = b_re.shape[-1]
    outs = pl.pallas_call(
        _s5_param_kernel,
        out_shape=[jax.ShapeDtypeStruct((g, 1, n), F32), jax.ShapeDtypeStruct((g, 1, n), F32),
                   jax.ShapeDtypeStruct((g, c, n), F32), jax.ShapeDtypeStruct((g, c, n), F32)],
        name="s5_params",
    )(lam_re.reshape(g, 1, n), lam_im.reshape(g, 1, n), log_step.reshape(g, 1, 1),
      jnp.swapaxes(b_re, 1, 2), jnp.swapaxes(b_im, 1, 2))
    return outs


def _block_diag(x):
    ns, gl, r, c = x.shape
    eye = jnp.eye(gl, dtype=x.dtype)
    return jnp.einsum('sgrc,gh->sgrhc', x, eye).reshape(ns, gl * r, gl * c)


def s5_matrices(a_re, a_im, bb_re, bb_im, c_re, c_im):
    gl = S5_COLS // S5_GROUP_CH
    ns = S5_NSLICE
    bre = _block_diag(bb_re.reshape(ns, gl, S5_GROUP_CH, S5_STATE))
    bim = _block_diag(bb_im.reshape(ns, gl, S5_GROUP_CH, S5_STATE))
    bq = jnp.concatenate([bre, bim], axis=-1).astype(BF16)
    cre = _block_diag(jnp.swapaxes(c_re, 1, 2).reshape(ns, gl, S5_STATE, S5_GROUP_CH))
    cim = _block_diag(jnp.swapaxes(c_im, 1, 2).reshape(ns, gl, S5_STATE, S5_GROUP_CH))
    cq = jnp.concatenate([cre, -cim], axis=1).astype(BF16)
    are = a_re.reshape(ns, 1, S5_SLICE_STATE)
    aim = a_im.reshape(ns, 1, S5_SLICE_STATE)
    return bq, cq, are, aim


def _s5_kernel(*refs, bsz, tt):
    u_refs = refs[:bsz]
    bq_ref, cq_ref, are_ref, aim_ref, d_ref, y_ref, us_ref, ys_ref, bu_ref, st_ref = refs[bsz:]

    @pl.when(pl.program_id(0) == 0)
    def _():
        st_ref[...] = jnp.zeros_like(st_ref)

    for b in range(bsz):
        ub = u_refs[b][...].astype(F32)
        for cs in range(S5_NSLICE):
            us_ref[cs, pl.ds(b, tt, stride=bsz), :] = ub[:, cs * S5_COLS:(cs + 1) * S5_COLS]

    ns2 = S5_SLICE_STATE
    for cs in range(S5_NSLICE):
        cols = slice(cs * S5_COLS, (cs + 1) * S5_COLS)
        uf = us_ref[cs]
        bu_ref[cs] = jnp.dot(uf.astype(BF16), bq_ref[cs], preferred_element_type=F32)
        a_re = jnp.broadcast_to(are_ref[cs], (bsz, ns2))
        a_im = jnp.broadcast_to(aim_ref[cs], (bsz, ns2))

        def step(t, carry):
            h_re, h_im = carry
            rows = pl.ds(pl.multiple_of(t * bsz, bsz), bsz)
            n_re = a_re * h_re - a_im * h_im + bu_ref[cs, rows, 0:ns2]
            n_im = a_re * h_im + a_im * h_re + bu_ref[cs, rows, ns2:2 * ns2]
            bu_ref[cs, rows, 0:ns2] = n_re
            bu_ref[cs, rows, ns2:2 * ns2] = n_im
            return n_re, n_im

        h_re, h_im = lax.fori_loop(0, tt, step, (st_ref[cs, 0], st_ref[cs, 1]), unroll=True)
        st_ref[cs, 0] = h_re
        st_ref[cs, 1] = h_im
        y = jnp.dot(bu_ref[cs].astype(BF16), cq_ref[cs], preferred_element_type=F32)
        ys_ref[cs] = y + d_ref[:, cols] * uf

    for b in range(bsz):
        for cs in range(S5_NSLICE):
            lo = b * S5_WIDTH + cs * S5_COLS
            y_ref[:, lo:lo + S5_COLS] = ys_ref[cs, pl.ds(b, tt, stride=bsz), :]


def s5_scan(proj_sb, mats, d_skip, bsz, s):
    bq, cq, are, aim = mats
    tt = S5_TT
    rows = tt * bsz
    kern = functools.partial(_s5_kernel, bsz=bsz, tt=tt)
    u_specs = [pl.BlockSpec((tt, PROJ_BLK), lambda i, b=b: (i, 6 * bsz + b)) for b in range(bsz)]
    return pl.pallas_call(
        kern,
        grid=(s // tt,),
        in_specs=u_specs + [
                  pl.BlockSpec(bq.shape, lambda i: (0, 0, 0)),
                  pl.BlockSpec(cq.shape, lambda i: (0, 0, 0)),
                  pl.BlockSpec(are.shape, lambda i: (0, 0, 0)),
                  pl.BlockSpec(aim.shape, lambda i: (0, 0, 0)),
                  pl.BlockSpec((1, S5_WIDTH), lambda i: (0, 0))],
        out_specs=pl.BlockSpec((tt, bsz * S5_WIDTH), lambda i: (i, 0)),
        out_shape=jax.ShapeDtypeStruct((s, bsz * S5_WIDTH), F32),
        scratch_shapes=[pltpu.VMEM((S5_NSLICE, rows, S5_COLS), F32),
                        pltpu.VMEM((S5_NSLICE, rows, S5_COLS), F32),
                        pltpu.VMEM((S5_NSLICE, rows, 2 * S5_SLICE_STATE), F32),
                        pltpu.VMEM((S5_NSLICE, 2, bsz, S5_SLICE_STATE), F32)],
        compiler_params=_cparams(("arbitrary",)),
        name="s5_scan",
    )(*([proj_sb] * bsz), bq, cq, are, aim, d_skip.reshape(1, S5_WIDTH))


def _gelu_tanh(x):
    c = math.sqrt(2.0 / math.pi)
    return 0.5 * x * (1.0 + jnp.tanh(c * (x + 0.044715 * (x * x * x))))


def _route(logits):
    lane = lax.broadcasted_iota(jnp.int32, logits.shape, 1)
    neg = jnp.float32(-jnp.inf)
    big = jnp.int32(1 << 20)
    is_g = lane < MOE_GROUPS
    lg = jnp.where(is_g, logits, neg)
    mg = jnp.max(lg, axis=-1, keepdims=True)
    sg = jnp.sum(jnp.where(is_g, jnp.exp(lg - mg), 0.0), axis=-1, keepdims=True)
    g_top = 1.0 / sg
    g_idx = jnp.min(jnp.where(lg == mg, lane, big), axis=-1, keepdims=True)
    lo = MOE_GROUPS + g_idx * EXPERTS_PER_GROUP
    in_grp = (lane >= lo) & (lane < lo + EXPERTS_PER_GROUP)
    le = jnp.where(in_grp, logits, neg)
    m1 = jnp.max(le, axis=-1, keepdims=True)
    se = jnp.sum(jnp.where(in_grp, jnp.exp(le - m1), 0.0), axis=-1, keepdims=True)
    i1 = jnp.min(jnp.where(le == m1, lane, big), axis=-1, keepdims=True)
    le2 = jnp.where(lane == i1, neg, le)
    m2 = jnp.max(le2, axis=-1, keepdims=True)
    i2 = jnp.min(jnp.where(le2 == m2, lane, big), axis=-1, keepdims=True)
    p1 = 1.0 / se
    p2 = jnp.exp(m2 - m1) / se
    tot = p1 + p2
    w1 = g_top * (p1 / tot)
    w2 = g_top * (p2 / tot)
    return lane, i1, i2, w1, w2


def _mix_kernel(ro_ref, y_ref, gr0_ref, gr1_ref, gs0_ref, gs1_ref, h_ref,
                wglu_ref, wbr_ref, wbs_ref, wout_ref, g_ref, b_ref, wr_ref, br_ref, tri_ref,
                h1_ref, route_ref, cnt_ref, blk_ref, lp_ref, run_ref):
    @pl.when((pl.program_id(0) == 0) & (pl.program_id(1) == 0))
    def _():
        run_ref[...] = jnp.zeros_like(run_ref)

    z = _gelu_tanh(y_ref[...])
    zg = jnp.dot(z.astype(BF16), wglu_ref[...], preferred_element_type=F32)
    zz = (z * jax.nn.sigmoid(zg)).astype(BF16)
    s5b = jnp.dot(zz, wbs_ref[...], preferred_element_type=F32)
    rb = jnp.dot(ro_ref[...], wbr_ref[...], preferred_element_type=F32)
    gr = jnp.concatenate([gr0_ref[...], gr1_ref[...]], axis=-1).astype(F32)
    gs = jnp.concatenate([gs0_ref[...], gs1_ref[...]], axis=-1).astype(F32)
    merged = jax.nn.sigmoid(gr) * rb + jax.nn.sigmoid(gs) * s5b
    mix = jnp.dot(merged.astype(BF16), wout_ref[...], preferred_element_type=F32)
    h1 = _layer_norm(DN_ALPHA * h_ref[...] + mix, g_ref[...], b_ref[...])
    h1_ref[...] = h1
    h_hi = h1.astype(BF16)
    h_lo = (h1 - h_hi.astype(F32)).astype(BF16)
    both = jnp.dot(h_hi, wr_ref[...], preferred_element_type=F32)
    logits = (both[:, :ROUTE_LANES] + both[:, ROUTE_LANES:]
              + jnp.dot(h_lo, wr_ref[:, :ROUTE_LANES], preferred_element_type=F32)) + br_ref[...]
    lane, i1, i2, w1, w2 = _route(logits)
    oh1 = lane == i1
    oh2 = lane == i2
    oh = jnp.where(oh1 | oh2, 1.0, 0.0)
    rank_in_blk = jnp.dot(tri_ref[...], oh.astype(BF16), preferred_element_type=F32)
    blk_cnt = jnp.broadcast_to(jnp.sum(oh, axis=0, keepdims=True), run_ref.shape)
    lane8 = lax.broadcasted_iota(jnp.int32, run_ref.shape, 1)
    incl = blk_cnt
    shift = 1
    while shift < ROUTE_LANES:
        incl = incl + jnp.where(lane8 >= shift, pltpu.roll(incl, shift, 1), 0.0)
        shift *= 2
    lstart = incl - blk_cnt
    pos = lstart[0:1, :] + rank_in_blk
    l1 = jnp.sum(jnp.where(oh1, pos, 0.0), axis=-1, keepdims=True)
    l2 = jnp.sum(jnp.where(oh2, pos, 0.0), axis=-1, keepdims=True)
    run_before = run_ref[...]
    row8 = lax.broadcasted_iota(jnp.int32, run_ref.shape, 0)
    blk_ref[...] = jnp.where(row8 == 0, run_before,
                             jnp.where(row8 == 1, blk_cnt, jnp.where(row8 == 2, lstart, 0.0)))
    cnt = run_before + blk_cnt
    run_ref[...] = cnt
    cnt_ref[...] = cnt
    vals = (i1.astype(F32) - MOE_GROUPS, i2.astype(F32) - MOE_GROUPS, w1, w2)
    route = jnp.zeros(logits.shape, F32)
    for k, v in enumerate(vals):
        route = jnp.where(lane == k, v, route)
    route_ref[...] = route
    pos_t = jnp.transpose(jnp.where(lane == 0, l1, jnp.where(lane == 1, l2, 0.0)))
    lp_ref[...] = pos_t[0:8, :].astype(jnp.int32)


def mix_out(ro, y, proj, h, wglu, wbr, wbs, wout, g, b, wr, br, bsz, ts=MOE_TS):
    s = h.shape[0]
    d = D_MODEL

    def const(a):
        return pl.BlockSpec(a.shape, lambda bi, i: (0,) * a.ndim)

    def pspec(col):
        return pl.BlockSpec((ts, PROJ_BLK), lambda bi, i, col=col: (i, col * bsz + bi))

    def tok(w):
        return pl.BlockSpec((ts, w), lambda bi, i: (i, bi))

    g2, b2 = g.reshape(1, d), b.reshape(1, d)
    tri = jnp.asarray(np.tril(np.ones((ts, ts)), -1), dtype=BF16)
    return pl.pallas_call(
        _mix_kernel,
        grid=(bsz, s // ts),
        in_specs=[tok(RET_V), tok(S5_WIDTH), pspec(7), pspec(8), pspec(9), pspec(10), tok(d),
                  const(wglu), const(wbr), const(wbs), const(wout), const(g2), const(b2),
                  const(wr), const(br), const(tri)],
        out_specs=[tok(d), tok(ROUTE_LANES), pl.BlockSpec((8, ROUTE_LANES), lambda bi, i: (0, 0)),
                   pl.BlockSpec((None, 8, ROUTE_LANES), lambda bi, i: (bi * (s // ts) + i, 0, 0)),
                   pl.BlockSpec((None, 8, ts), lambda bi, i: (bi * (s // ts) + i, 0, 0))],
        out_shape=[jax.ShapeDtypeStruct((s, bsz * d), F32),
                   jax.ShapeDtypeStruct((s, bsz * ROUTE_LANES), F32),
                   jax.ShapeDtypeStruct((8, ROUTE_LANES), F32),
                   jax.ShapeDtypeStruct((bsz * (s // ts), 8, ROUTE_LANES), F32),
                   jax.ShapeDtypeStruct((bsz * (s // ts), 8, ts), jnp.int32)],
        scratch_shapes=[pltpu.VMEM((8, ROUTE_LANES), F32)],
        compiler_params=_cparams(("arbitrary", "arbitrary")),
        name="mix_out",
    )(ro, y, proj, proj, proj, proj, h, wglu, wbr, wbs, wout, g2, b2, wr, br, tri)


def moe_rows(t):
    return TOP_K * t + (N_EXPERTS + 1) * MOE_TM


def moe_plan(lp, cnt, blk, bsz, s):
    t = bsz * s
    experts = slice(MOE_GROUPS, MOE_GROUPS + N_EXPERTS)
    counts = cnt[0, experts].astype(jnp.int32)
    padded = ((counts + MOE_TM - 1) // MOE_TM) * MOE_TM
    ends = jnp.cumsum(padded)
    off = ends - padded
    run_start = (off[None, :] + blk[:, 0, experts].astype(jnp.int32)).reshape(-1)
    run_len = blk[:, 1, experts].astype(jnp.int32).reshape(-1)
    run_local = blk[:, 2, experts].astype(jnp.int32).reshape(-1)
    pads = (off + counts, padded - counts)
    n_tiles = moe_rows(t) // MOE_TM
    n_used = (ends[-1:] // MOE_TM).astype(jnp.int32)
    tile_start = jnp.arange(n_tiles, dtype=jnp.int32) * MOE_TM
    tile_expert = jnp.sum((ends[None, :] <= tile_start[:, None]).astype(jnp.int32), axis=1)
    tile_expert = jnp.minimum(tile_expert, N_EXPERTS - 1)
    return (lp[:, 0, :].reshape(t), lp[:, 1, :].reshape(t), run_start, run_len, run_local), pads, tile_expert, n_used


def _token_rows(start, n):
    return pl.ds(pl.multiple_of(start * ROW_TILE, ROW_TILE), n * ROW_TILE)


def _for_each_piece(n, nbits, fn):
    for bit in reversed(range(nbits)):
        size = 1 << bit
        done = (n >> (bit + 1)) << (bit + 1)

        @pl.when((n & size) != 0)
        def _(size=size, done=done):
            fn(done, size)


def _run_copies(blk, run_refs, make_copy):
    run_start_ref, run_len_ref, run_local_ref = run_refs
    for e in range(N_EXPERTS):
        idx = blk * N_EXPERTS + e
        g0 = run_start_ref[idx]
        l0 = run_local_ref[idx]
        _for_each_piece(run_len_ref[idx], RUN_BITS,
                        lambda off, size, g0=g0, l0=l0: make_copy(l0 + off, g0 + off, size).start())


def _to_token_tiles(ref, x):
    n = x.shape[0]
    for j in range(ROW_TILE):
        ref[pl.ds(j, n, stride=ROW_TILE), :] = x[:, j * LANES:(j + 1) * LANES]


def _from_token_tiles(ref, n):
    return jnp.concatenate([ref[pl.ds(j, n, stride=ROW_TILE), :] for j in range(ROW_TILE)], axis=-1)


def _dispatch_kernel(lp1_ref, lp2_ref, run_start_ref, run_len_ref, run_local_ref,
                     pad_ref, pad_len_ref, nu_ref,
                     x_ref, xs_ref, xt_ref, srt_ref, zero_ref, sem, zsem, *, ts):
    run_refs = (run_start_ref, run_len_ref, run_local_ref)
    nsteps = pl.num_programs(0) * pl.num_programs(1)
    step = pl.program_id(0) * pl.num_programs(1) + pl.program_id(1)

    def zero_copy(start, size):
        return pltpu.make_async_copy(zero_ref.at[_token_rows(0, size)],
                                     xs_ref.at[_token_rows(start, size)], zsem)

    def zero_fill(action):
        for e in range(N_EXPERTS):
            p0 = pad_ref[e]
            _for_each_piece(pad_len_ref[e], PAD_BITS,
                            lambda off, size, p0=p0: action(zero_copy(p0 + off, size)))

        def tail_tile(r, c):
            action(zero_copy(r * MOE_TM, MOE_TM))
            return c

        lax.fori_loop(nu_ref[0], xs_ref.shape[0] // (MOE_TM * ROW_TILE), tail_tile, 0)

    @pl.when(step == 0)
    def _():
        zero_ref[...] = jnp.zeros_like(zero_ref)
        zero_fill(lambda cp: cp.start())

    @pl.when(step == nsteps - 1)
    def _():
        zero_fill(lambda cp: cp.wait())

    slot = step % 2
    _to_token_tiles(xt_ref, x_ref[...])
    base = step * ts

    def place(i, c):
        tile = xt_ref[_token_rows(i, 1), :]
        srt_ref[slot, _token_rows(lp1_ref[base + i], 1), :] = tile
        srt_ref[slot, _token_rows(lp2_ref[base + i], 1), :] = tile
        return c

    lax.fori_loop(0, ts, place, 0, unroll=8)

    def run_copy(local_start, sorted_start, size):
        return pltpu.make_async_copy(srt_ref.at[slot, _token_rows(local_start, size)],
                                     xs_ref.at[_token_rows(sorted_start, size)], sem.at[slot])

    _run_copies(step, run_refs, run_copy)

    def drain(sl):
        pltpu.make_async_copy(srt_ref.at[sl], xs_ref.at[_token_rows(0, TOP_K * ts)], sem.at[sl]).wait()

    @pl.when(step > 0)
    def _():
        drain(1 - slot)

    @pl.when(step == nsteps - 1)
    def _():
        drain(slot)


def moe_dispatch(h1, plan, pads, n_used, bsz, ts=MOE_TS):
    s = h1.shape[0]
    d = D_MODEL
    kern = functools.partial(_dispatch_kernel, ts=ts)
    grid_spec = pltpu.PrefetchScalarGridSpec(
        num_scalar_prefetch=len(plan) + len(pads) + 1,
        grid=(bsz, s // ts),
        in_specs=[pl.BlockSpec((ts, d), lambda bi, i, *_: (i, bi))],
        out_specs=pl.BlockSpec(memory_space=pl.ANY),
        scratch_shapes=[pltpu.VMEM((ts * ROW_TILE, LANES), F32),
                        pltpu.VMEM((2, TOP_K * ts * ROW_TILE, LANES), F32),
                        pltpu.VMEM((MOE_TM * ROW_TILE, LANES), F32),
                        pltpu.SemaphoreType.DMA((2,)), pltpu.SemaphoreType.DMA(())],
    )
    return pl.pallas_call(
        kern,
        grid_spec=grid_spec,
        out_shape=jax.ShapeDtypeStruct((moe_rows(bsz * s) * ROW_TILE, LANES), F32),
        compiler_params=_cparams(("arbitrary", "arbitrary")),
        name="moe_dispatch",
    )(*plan, *pads, n_used, h1)


def _experts_kernel(te_ref, nu_ref, xs_ref, wg_ref, wu_ref, wd_ref, ys_ref, wgb_ref, wub_ref, wdb_ref):
    r = pl.program_id(0)

    @pl.when(r < nu_ref[0])
    def _():
        changed = (r == 0) | (te_ref[r] != te_ref[jnp.maximum(r - 1, 0)])

        @pl.when(changed)
        def _():
            wgb_ref[...] = wg_ref[...].astype(BF16)
            wub_ref[...] = wu_ref[...].astype(BF16)
            wdb_ref[...] = wd_ref[...].astype(BF16)

        x = _from_token_tiles(xs_ref, MOE_TM).astype(BF16)
        gate = jnp.dot(x, wgb_ref[...], preferred_element_type=F32)
        up = jnp.dot(x, wub_ref[...], preferred_element_type=F32)
        act = (gate * jax.nn.sigmoid(gate) * up).astype(BF16)
        _to_token_tiles(ys_ref, jnp.dot(act, wdb_ref[...], preferred_element_type=F32))

    @pl.when(r >= nu_ref[0])
    def _():
        ys_ref[...] = jnp.zeros_like(ys_ref)


def moe_experts(xs, tile_expert, n_used, wg, wu, wd, layer):
    d, f = wg.shape[-2:]
    rows = xs.shape[0] // ROW_TILE
    blk = MOE_TM * ROW_TILE

    def tile(r, te, nu):
        return jnp.minimum(r, nu[0] - 1)

    def wspec(a, b):
        return pl.BlockSpec((None, None, a, b), lambda r, te, nu: (layer, te[tile(r, te, nu)], 0, 0))

    grid_spec = pltpu.PrefetchScalarGridSpec(
        num_scalar_prefetch=2,
        grid=(rows // MOE_TM,),
        in_specs=[pl.BlockSpec((blk, LANES), lambda r, te, nu: (tile(r, te, nu), 0)),
                  wspec(d, f), wspec(d, f), wspec(f, d)],
        out_specs=pl.BlockSpec((blk, LANES), lambda r, te, nu: (r, 0)),
        scratch_shapes=[pltpu.VMEM((d, f), BF16), pltpu.VMEM((d, f), BF16), pltpu.VMEM((f, d), BF16)],
    )
    return pl.pallas_call(
        _experts_kernel,
        grid_spec=grid_spec,
        out_shape=jax.ShapeDtypeStruct(xs.shape, F32),
        compiler_params=_cparams(("arbitrary",)),
        name="moe_experts",
    )(tile_expert, n_used, xs, wg, wu, wd)


def _combine_kernel(lp1_ref, lp2_ref, run_start_ref, run_len_ref, run_local_ref,
                    ys_ref, route_ref, h_ref, g_ref, b_ref, *refs, ts):
    o_ref, maybe_ob_ref = refs[0], refs[1:-4]
    srt_ref, t1_ref, t2_ref, sem = refs[-4:]
    run_refs = (run_start_ref, run_len_ref, run_local_ref)
    nsteps = pl.num_programs(0) * pl.num_programs(1)
    step = pl.program_id(0) * pl.num_programs(1) + pl.program_id(1)

    def issue(blk, sl):
        def run_copy(local_start, sorted_start, size):
            return pltpu.make_async_copy(ys_ref.at[_token_rows(sorted_start, size)],
                                         srt_ref.at[sl, _token_rows(local_start, size)], sem.at[sl])
        _run_copies(blk, run_refs, run_copy)

    @pl.when(step == 0)
    def _():
        issue(0, 0)

    @pl.when(step + 1 < nsteps)
    def _():
        issue(step + 1, (step + 1) % 2)

    slot = step % 2
    pltpu.make_async_copy(ys_ref.at[_token_rows(0, TOP_K * ts)], srt_ref.at[slot], sem.at[slot]).wait()

    base = step * ts

    def pick(i, c):
        t1_ref[_token_rows(i, 1), :] = srt_ref[slot, _token_rows(lp1_ref[base + i], 1), :]
        t2_ref[_token_rows(i, 1), :] = srt_ref[slot, _token_rows(lp2_ref[base + i], 1), :]
        return c

    lax.fori_loop(0, ts, pick, 0, unroll=8)

    route = route_ref[...]
    w1 = route[:, 2:3]
    w2 = route[:, 3:4]
    ffn = w1 * _from_token_tiles(t1_ref, ts) + w2 * _from_token_tiles(t2_ref, ts)
    h2 = _layer_norm(DN_ALPHA * h_ref[...] + ffn, g_ref[...], b_ref[...])
    o_ref[...] = h2
    for ob_ref in maybe_ob_ref:
        ob_ref[...] = h2.astype(BF16)


def moe_combine(ys, plan, route, h1, g, b, bsz, last, ts=MOE_TS):
    s = h1.shape[0]
    d = D_MODEL
    g2, b2 = g.reshape(1, d), b.reshape(1, d)
    if last:
        o_specs = [pl.BlockSpec((None, ts, d), lambda bi, i, *_: (bi, i, 0))]
        o_shapes = [jax.ShapeDtypeStruct((bsz, s, d), F32)]
    else:
        o_specs = [pl.BlockSpec((ts, d), lambda bi, i, *_: (i, bi))] * 2
        o_shapes = [jax.ShapeDtypeStruct((s, bsz * d), F32), jax.ShapeDtypeStruct((s, bsz * d), BF16)]
    kern = functools.partial(_combine_kernel, ts=ts)
    grid_spec = pltpu.PrefetchScalarGridSpec(
        num_scalar_prefetch=len(plan),
        grid=(bsz, s // ts),
        in_specs=[pl.BlockSpec(memory_space=pl.ANY),
                  pl.BlockSpec((ts, ROUTE_LANES), lambda bi, i, *_: (i, bi)),
                  pl.BlockSpec((ts, d), lambda bi, i, *_: (i, bi)),
                  pl.BlockSpec((1, d), lambda bi, i, *_: (0, 0)),
                  pl.BlockSpec((1, d), lambda bi, i, *_: (0, 0))],
        out_specs=o_specs,
        scratch_shapes=[pltpu.VMEM((2, TOP_K * ts * ROW_TILE, LANES), F32),
                        pltpu.VMEM((ts * ROW_TILE, LANES), F32), pltpu.VMEM((ts * ROW_TILE, LANES), F32),
                        pltpu.SemaphoreType.DMA((2,))],
    )
    return pl.pallas_call(
        kern,
        grid_spec=grid_spec,
        out_shape=o_shapes,
        compiler_params=_cparams(("arbitrary", "arbitrary")),
        name="moe_combine",
    )(*plan, ys, route, h1, g2, b2)


def kernel(x, ln_in_g, ln_in_b, w_in, s5_lambda_re, s5_lambda_im, s5_log_step, s5_b_re, s5_b_im,
           s5_c_re, s5_c_im, s5_d, w_glu, w_branch_ret, w_branch_s5, w_out, ln_mix_g, ln_mix_b,
           w_router_group, b_router_group, w_router_expert, b_router_expert, w_exp_gate, w_exp_up,
           w_exp_down, ln_ffn_g, ln_ffn_b):
    bsz, s, d = x.shape
    t = bsz * s
    depth = w_in.shape[0]
    tabs = retention_tables(s)
    out = None
    for l in range(depth):
        if l == 0:
            proj, h_sb = ln_in_proj(x, ln_in_g, ln_in_b, w_in, l)
        else:
            proj = in_proj(hb_sb, w_in, l, bsz)
        ro = retention(proj, tabs, bsz, s)
        a_re, a_im, bb_re, bb_im = s5_params(s5_lambda_re[l], s5_lambda_im[l], s5_log_step[l],
                                             s5_b_re[l], s5_b_im[l])
        mats = s5_matrices(a_re, a_im, bb_re, bb_im, s5_c_re[l], s5_c_im[l])
        y = s5_scan(proj, mats, s5_d[l], bsz, s)
        pad = ROUTE_LANES - MOE_GROUPS - N_EXPERTS
        wr = jnp.concatenate([w_router_group[l], w_router_expert[l], jnp.zeros((d, pad), F32)], axis=1)
        wr_hi = wr.astype(BF16)
        wr = jnp.concatenate([wr_hi, (wr - wr_hi.astype(F32)).astype(BF16)], axis=1)
        br = jnp.concatenate([b_router_group[l], b_router_expert[l], jnp.zeros((pad,), F32)]).reshape(1, -1)
        h1, route, cnt, blk, lp = mix_out(ro, y, proj, h_sb,
                                 w_glu[l].astype(BF16), w_branch_ret[l].astype(BF16),
                                 w_branch_s5[l].astype(BF16), w_out[l].astype(BF16),
                                 ln_mix_g[l], ln_mix_b[l], wr, br, bsz)
        last = l == depth - 1
        plan, pads, tile_expert, n_used = moe_plan(lp, cnt, blk, bsz, s)
        xs = moe_dispatch(h1, plan, pads, n_used, bsz)
        ys = moe_experts(xs, tile_expert, n_used, w_exp_gate, w_exp_up, w_exp_down, l)
        outs = moe_combine(ys, plan, route, h1, ln_ffn_g[l], ln_ffn_b[l], bsz, last)
        if last:
            out = outs[0]
        else:
            h_sb, hb_sb = outs
    return out
```

```python
import functools
import math

import jax
import jax.numpy as jnp
import numpy as np
from jax import lax
from jax.experimental import pallas as pl
from jax.experimental.pallas import tpu as pltpu

D_MODEL = 1024
CHUNK = 64
RET_HEADS = 8
RET_QK = 512
RET_V = 1024
RET_QK_DIM = 64
RET_V_DIM = 128
ROPE_BASE = 10000.0
S5_WIDTH = 512
S5_GROUP_CH = 16
S5_GROUPS = 32
S5_STATE = 64
MOE_GROUPS = 4
EXPERTS_PER_GROUP = 8
N_EXPERTS = 32
EXPERT_FF = 256
LN_EPS = 1e-5
HEAD_NORM_EPS = 1e-6
DEPTH = 2
DN_ALPHA = (2 * DEPTH) ** 0.25
IN_WIDTH = 2 * RET_QK + 2 * RET_V + S5_WIDTH + 2 * D_MODEL
PROJ_BLK = 512
N_PROJ_BLK = IN_WIDTH // PROJ_BLK
INPROJ_BATCHES = 2

RET_SUPER = 512
S5_TT = 128
S5_COLS = 128
S5_NSLICE = S5_WIDTH // S5_COLS
S5_SLICE_STATE = (S5_COLS // S5_GROUP_CH) * S5_STATE
ROUTE_LANES = 128
TOP_K = 2
MOE_TM = 512
MOE_TS = 512
RUN_BITS = MOE_TS.bit_length()
PAD_BITS = (MOE_TM - 1).bit_length()
LANES = 128
ROW_TILE = D_MODEL // LANES
VMEM_LIMIT = 56 * 1024 * 1024

F32 = jnp.float32
BF16 = jnp.bfloat16


def _cparams(sem):
    return pltpu.CompilerParams(dimension_semantics=sem, vmem_limit_bytes=VMEM_LIMIT)


def _layer_norm(x, g, b):
    mu = jnp.mean(x, axis=-1, keepdims=True)
    xc = x - mu
    var = jnp.mean(xc * xc, axis=-1, keepdims=True)
    return xc * lax.rsqrt(var + LN_EPS) * g + b


def _inproj_kernel(h_ref, w_ref, o_ref):
    w = w_ref[...].astype(BF16)
    d = w.shape[0]
    for k in range(INPROJ_BATCHES):
        hk = h_ref[:, k * d:(k + 1) * d]
        o_ref[:, k * PROJ_BLK:(k + 1) * PROJ_BLK] = jnp.dot(hk, w, preferred_element_type=F32).astype(BF16)


def _ln_inproj_kernel(x_ref, g_ref, b_ref, w_ref, o_ref, h_ref, hb_ref):
    @pl.when(pl.program_id(2) == 0)
    def _():
        h = _layer_norm(x_ref[...], g_ref[...], b_ref[...])
        h_ref[...] = h
        hb_ref[...] = h.astype(BF16)

    w = w_ref[...].astype(BF16)
    o_ref[...] = jnp.dot(hb_ref[...], w, preferred_element_type=F32).astype(BF16)


def ln_in_proj(x, g, b, w, layer, ts=2048):
    bsz, s, d = x.shape
    n = w.shape[-1]
    nb = n // PROJ_BLK
    return pl.pallas_call(
        _ln_inproj_kernel,
        grid=(bsz, s // ts, nb),
        in_specs=[pl.BlockSpec((None, ts, d), lambda bi, i, j: (bi, i, 0)),
                  pl.BlockSpec((1, d), lambda bi, i, j: (0, 0)),
                  pl.BlockSpec((1, d), lambda bi, i, j: (0, 0)),
                  pl.BlockSpec((None, d, PROJ_BLK), lambda bi, i, j: (layer, 0, j))],
        out_specs=[pl.BlockSpec((ts, PROJ_BLK), lambda bi, i, j: (i, j * bsz + bi)),
                   pl.BlockSpec((ts, d), lambda bi, i, j: (i, bi))],
        out_shape=[jax.ShapeDtypeStruct((s, bsz * n), BF16),
                   jax.ShapeDtypeStruct((s, bsz * d), F32)],
        scratch_shapes=[pltpu.VMEM((ts, d), BF16)],
        compiler_params=_cparams(("parallel", "parallel", "arbitrary")),
        name="ln_in_proj",
    )(x, g.reshape(1, d), b.reshape(1, d), w)


def in_proj(hb_sb, w, layer, bsz, ts=2048):
    s = hb_sb.shape[0]
    d, n = w.shape[-2:]
    nb = n // PROJ_BLK
    kb = INPROJ_BATCHES
    return pl.pallas_call(
        _inproj_kernel,
        grid=(bsz // kb, s // ts, nb),
        in_specs=[pl.BlockSpec((ts, kb * d), lambda b, i, j: (i, b)),
                  pl.BlockSpec((None, d, PROJ_BLK), lambda b, i, j: (layer, 0, j))],
        out_specs=pl.BlockSpec((ts, kb * PROJ_BLK), lambda b, i, j: (i, (j * bsz) // kb + b)),
        out_shape=jax.ShapeDtypeStruct((s, bsz * n), BF16),
        compiler_params=_cparams(("parallel", "parallel", "arbitrary")),
        name="in_proj",
    )(hb_sb, w)


def _swap_halves(x):
    lane = lax.broadcasted_iota(jnp.int32, x.shape, 1)
    first = (lane % RET_QK_DIM) < (RET_QK_DIM // 2)
    n = x.shape[1]
    return jnp.where(first, pltpu.roll(x, n - RET_QK_DIM // 2, 1), pltpu.roll(x, RET_QK_DIM // 2, 1))


def _retention_kernel(q_ref, k_ref, v0_ref, v1_ref, g0_ref, g1_ref, cos_ref, sin_ref,
                      qd_ref, kd_ref, mask_ref, cd_ref, o_ref, state_ref):
    @pl.when(pl.program_id(1) == 0)
    def _():
        state_ref[...] = jnp.zeros_like(state_ref)

    cos = cos_ref[...]
    sin = sin_ref[...]
    q = q_ref[...].astype(F32)
    k = k_ref[...].astype(F32)
    q = q * cos + _swap_halves(q) * sin
    k = k * cos + _swap_halves(k) * sin
    qb = q.astype(BF16)
    kb = k.astype(BF16)
    qdb = (q * qd_ref[...]).astype(BF16)
    kdb = (k * kd_ref[...]).astype(BF16)
    for hd in range(RET_HEADS):
        qs = slice(hd * RET_QK_DIM, (hd + 1) * RET_QK_DIM)
        half, off = divmod(hd * RET_V_DIM, PROJ_BLK)
        vs = slice(off, off + RET_V_DIM)
        vh = (v0_ref, v1_ref)[half][:, vs]
        gh = (g0_ref, g1_ref)[half][:, vs].astype(F32)
        sc = lax.dot_general(qb[:, qs], kb[:, qs], (((1,), (1,)), ((), ())),
                             preferred_element_type=F32)
        sc = (sc * mask_ref[hd]).astype(BF16)
        st = state_ref[hd]
        o = jnp.dot(sc, vh, preferred_element_type=F32)
        o = o + jnp.dot(qdb[:, qs], st.astype(BF16), preferred_element_type=F32)
        kv = lax.dot_general(kdb[:, qs], vh, (((0,), (0,)), ((), ())),
                             preferred_element_type=F32)
        state_ref[hd] = st * cd_ref[hd] + kv
        mu = jnp.mean(o, axis=-1, keepdims=True)
        oc = o - mu
        var = jnp.mean(oc * oc, axis=-1, keepdims=True)
        on = oc * lax.rsqrt(var + HEAD_NORM_EPS)
        o_ref[:, hd * RET_V_DIM:(hd + 1) * RET_V_DIM] = (gh * jax.nn.sigmoid(gh) * on).astype(BF16)


def retention(proj_sb, tabs, bsz, s):
    cos_t, sin_t, qd_t, kd_t, mask, cd = tabs
    L = RET_SUPER

    def pspec(col):
        return pl.BlockSpec((L, PROJ_BLK), lambda b, i, col=col: (i, col * bsz + b))

    full2 = pl.BlockSpec((L, RET_QK), lambda b, i: (0, 0))
    return pl.pallas_call(
        _retention_kernel,
        grid=(bsz, s // L),
        in_specs=[pspec(0), pspec(1), pspec(2), pspec(3), pspec(4), pspec(5),
                  pl.BlockSpec((L, RET_QK), lambda b, i: (i, 0)),
                  pl.BlockSpec((L, RET_QK), lambda b, i: (i, 0)),
                  full2, full2,
                  pl.BlockSpec((RET_HEADS, L, L), lambda b, i: (0, 0, 0)),
                  pl.BlockSpec((RET_HEADS, 1, RET_V_DIM), lambda b, i: (0, 0, 0))],
        out_specs=pl.BlockSpec((L, RET_V), lambda b, i: (i, b)),
        out_shape=jax.ShapeDtypeStruct((s, bsz * RET_V), BF16),
        scratch_shapes=[pltpu.VMEM((RET_HEADS, RET_QK_DIM, RET_V_DIM), F32)],
        compiler_params=_cparams(("parallel", "arbitrary")),
        name="retention",
    )(proj_sb, proj_sb, proj_sb, proj_sb, proj_sb, proj_sb, cos_t, sin_t, qd_t, kd_t, mask, cd)


def retention_tables(s):
    L = RET_SUPER
    half = RET_QK_DIM // 2
    inv_freq = ROPE_BASE ** (-np.arange(half, dtype=np.float64) / half)
    ang = np.arange(s, dtype=np.float64)[:, None] * inv_freq[None, :]
    cos, sin = np.cos(ang), np.sin(ang)
    cos_t = np.tile(np.concatenate([cos, cos], -1), (1, RET_HEADS))
    sin_t = np.tile(np.concatenate([-sin, sin], -1), (1, RET_HEADS))
    log_gamma = np.log1p(-(2.0 ** (-5.0 - np.arange(RET_HEADS, dtype=np.float64))))
    pos = np.arange(L, dtype=np.float64)
    qd = np.exp(log_gamma[None, :] * (pos + 1.0)[:, None])
    k_scale = RET_QK_DIM ** -0.5
    kd = np.exp(log_gamma[None, :] * (L - 1.0 - pos)[:, None]) * k_scale
    qd_t = np.repeat(qd, RET_QK_DIM, axis=1)
    kd_t = np.repeat(kd, RET_QK_DIM, axis=1)
    chunk_id = np.arange(L) // CHUNK
    visible = (chunk_id[None, :] <= chunk_id[:, None]).astype(np.float64)
    mask = np.exp(log_gamma[:, None, None] * np.abs(pos[:, None] - pos[None, :])) * visible[None] * k_scale
    cd = np.broadcast_to(np.exp(log_gamma * L)[:, None, None], (RET_HEADS, 1, RET_V_DIM))
    return tuple(jnp.asarray(a, dtype=F32) for a in (cos_t, sin_t, qd_t, kd_t, mask, cd))


def _s5_param_kernel(lre_ref, lim_ref, ls_ref, bre_ref, bim_ref, are_ref, aim_ref, bbre_ref, bbim_ref):
    lam_re = jnp.minimum(lre_ref[...], -1e-4)
    lam_im = lim_ref[...]
    step = jnp.exp(ls_ref[...])
    mag = jnp.exp(lam_re * step)
    ang = lam_im * step
    ab_re = mag * jnp.cos(ang)
    ab_im = mag * jnp.sin(ang)
    den = lam_re * lam_re + lam_im * lam_im
    n_re = ab_re - 1.0
    zc_re = (n_re * lam_re + ab_im * lam_im) / den
    zc_im = (ab_im * lam_re - n_re * lam_im) / den
    are_ref[...] = ab_re
    aim_ref[...] = ab_im
    b_re = bre_ref[...]
    b_im = bim_ref[...]
    bbre_ref[...] = zc_re * b_re - zc_im * b_im
    bbim_ref[...] = zc_re * b_im + zc_im * b_re


def s5_params(lam_re, lam_im, log_step, b_re, b_im):
    g, n = lam_re.shape
    c = b_re.shape[-1]
    outs = pl.pallas_call(
        _s5_param_kernel,
        out_shape=[jax.ShapeDtypeStruct((g, 1, n), F32), jax.ShapeDtypeStruct((g, 1, n), F32),
                   jax.ShapeDtypeStruct((g, c, n), F32), jax.ShapeDtypeStruct((g, c, n), F32)],
        name="s5_params",
    )(lam_re.reshape(g, 1, n), lam_im.reshape(g, 1, n), log_step.reshape(g, 1, 1),
      jnp.swapaxes(b_re, 1, 2), jnp.swapaxes(b_im, 1, 2))
    return outs


def _block_diag(x):
    ns, gl, r, c = x.shape
    eye = jnp.eye(gl, dtype=x.dtype)
    return jnp.einsum('sgrc,gh->sgrhc', x, eye).reshape(ns, gl * r, gl * c)


def s5_matrices(a_re, a_im, bb_re, bb_im, c_re, c_im):
    gl = S5_COLS // S5_GROUP_CH
    ns = S5_NSLICE
    bre = _block_diag(bb_re.reshape(ns, gl, S5_GROUP_CH, S5_STATE))
    bim = _block_diag(bb_im.reshape(ns, gl, S5_GROUP_CH, S5_STATE))
    bq = jnp.concatenate([bre, bim], axis=-1).astype(BF16)
    cre = _block_diag(jnp.swapaxes(c_re, 1, 2).reshape(ns, gl, S5_STATE, S5_GROUP_CH))
    cim = _block_diag(jnp.swapaxes(c_im, 1, 2).reshape(ns, gl, S5_STATE, S5_GROUP_CH))
    cq = jnp.concatenate([cre, -cim], axis=1).astype(BF16)
    are = a_re.reshape(ns, 1, S5_SLICE_STATE)
    aim = a_im.reshape(ns, 1, S5_SLICE_STATE)
    return bq, cq, are, aim


def _s5_kernel(*refs, bsz, tt):
    u_refs = refs[:bsz]
    bq_ref, cq_ref, are_ref, aim_ref, d_ref, y_ref, us_ref, ys_ref, bu_ref, st_ref = refs[bsz:]

    @pl.when(pl.program_id(0) == 0)
    def _():
        st_ref[...] = jnp.zeros_like(st_ref)

    for b in range(bsz):
        ub = u_refs[b][...].astype(F32)
        for cs in range(S5_NSLICE):
            us_ref[cs, pl.ds(b, tt, stride=bsz), :] = ub[:, cs * S5_COLS:(cs + 1) * S5_COLS]

    ns2 = S5_SLICE_STATE
    for cs in range(S5_NSLICE):
        cols = slice(cs * S5_COLS, (cs + 1) * S5_COLS)
        uf = us_ref[cs]
        bu_ref[cs] = jnp.dot(uf.astype(BF16), bq_ref[cs], preferred_element_type=F32)
        a_re = jnp.broadcast_to(are_ref[cs], (bsz, ns2))
        a_im = jnp.broadcast_to(aim_ref[cs], (bsz, ns2))

        def step(t, carry):
            h_re, h_im = carry
            rows = pl.ds(pl.multiple_of(t * bsz, bsz), bsz)
            n_re = a_re * h_re - a_im * h_im + bu_ref[cs, rows, 0:ns2]
            n_im = a_re * h_im + a_im * h_re + bu_ref[cs, rows, ns2:2 * ns2]
            bu_ref[cs, rows, 0:ns2] = n_re
            bu_ref[cs, rows, ns2:2 * ns2] = n_im
            return n_re, n_im

        h_re, h_im = lax.fori_loop(0, tt, step, (st_ref[cs, 0], st_ref[cs, 1]), unroll=True)
        st_ref[cs, 0] = h_re
        st_ref[cs, 1] = h_im
        y = jnp.dot(bu_ref[cs].astype(BF16), cq_ref[cs], preferred_element_type=F32)
        ys_ref[cs] = y + d_ref[:, cols] * uf

    for b in range(bsz):
        for cs in range(S5_NSLICE):
            lo = b * S5_WIDTH + cs * S5_COLS
            y_ref[:, lo:lo + S5_COLS] = ys_ref[cs, pl.ds(b, tt, stride=bsz), :]


def s5_scan(proj_sb, mats, d_skip, bsz, s):
    bq, cq, are, aim = mats
    tt = S5_TT
    rows = tt * bsz
    kern = functools.partial(_s5_kernel, bsz=bsz, tt=tt)
    u_specs = [pl.BlockSpec((tt, PROJ_BLK), lambda i, b=b: (i, 6 * bsz + b)) for b in range(bsz)]
    return pl.pallas_call(
        kern,
        grid=(s // tt,),
        in_specs=u_specs + [
                  pl.BlockSpec(bq.shape, lambda i: (0, 0, 0)),
                  pl.BlockSpec(cq.shape, lambda i: (0, 0, 0)),
                  pl.BlockSpec(are.shape, lambda i: (0, 0, 0)),
                  pl.BlockSpec(aim.shape, lambda i: (0, 0, 0)),
                  pl.BlockSpec((1, S5_WIDTH), lambda i: (0, 0))],
        out_specs=pl.BlockSpec((tt, bsz * S5_WIDTH), lambda i: (i, 0)),
        out_shape=jax.ShapeDtypeStruct((s, bsz * S5_WIDTH), F32),
        scratch_shapes=[pltpu.VMEM((S5_NSLICE, rows, S5_COLS), F32),
                        pltpu.VMEM((S5_NSLICE, rows, S5_COLS), F32),
                        pltpu.VMEM((S5_NSLICE, rows, 2 * S5_SLICE_STATE), F32),
                        pltpu.VMEM((S5_NSLICE, 2, bsz, S5_SLICE_STATE), F32)],
        compiler_params=_cparams(("arbitrary",)),
        name="s5_scan",
    )(*([proj_sb] * bsz), bq, cq, are, aim, d_skip.reshape(1, S5_WIDTH))


def _gelu_tanh(x):
    c = math.sqrt(2.0 / math.pi)
    return 0.5 * x * (1.0 + jnp.tanh(c * (x + 0.044715 * (x * x * x))))


def _route(logits):
    lane = lax.broadcasted_iota(jnp.int32, logits.shape, 1)
    neg = jnp.float32(-jnp.inf)
    big = jnp.int32(1 << 20)
    is_g = lane < MOE_GROUPS
    lg = jnp.where(is_g, logits, neg)
    mg = jnp.max(lg, axis=-1, keepdims=True)
    sg = jnp.sum(jnp.where(is_g, jnp.exp(lg - mg), 0.0), axis=-1, keepdims=True)
    g_top = 1.0 / sg
    g_idx = jnp.min(jnp.where(lg == mg, lane, big), axis=-1, keepdims=True)
    lo = MOE_GROUPS + g_idx * EXPERTS_PER_GROUP
    in_grp = (lane >= lo) & (lane < lo + EXPERTS_PER_GROUP)
    le = jnp.where(in_grp, logits, neg)
    m1 = jnp.max(le, axis=-1, keepdims=True)
    se = jnp.sum(jnp.where(in_grp, jnp.exp(le - m1), 0.0), axis=-1, keepdims=True)
    i1 = jnp.min(jnp.where(le == m1, lane, big), axis=-1, keepdims=True)
    le2 = jnp.where(lane == i1, neg, le)
    m2 = jnp.max(le2, axis=-1, keepdims=True)
    i2 = jnp.min(jnp.where(le2 == m2, lane, big), axis=-1, keepdims=True)
    p1 = 1.0 / se
    p2 = jnp.exp(m2 - m1) / se
    tot = p1 + p2
    w1 = g_top * (p1 / tot)
    w2 = g_top * (p2 / tot)
    return lane, i1, i2, w1, w2


def _mix_kernel(ro_ref, y_ref, gr0_ref, gr1_ref, gs0_ref, gs1_ref, h_ref,
                wglu_ref, wbr_ref, wbs_ref, wout_ref, g_ref, b_ref, wr_ref, br_ref, tri_ref,
                h1_ref, route_ref, cnt_ref, blk_ref, lp_ref, run_ref):
    @pl.when((pl.program_id(0) == 0) & (pl.program_id(1) == 0))
    def _():
        run_ref[...] = jnp.zeros_like(run_ref)

    z = _gelu_tanh(y_ref[...])
    zg = jnp.dot(z.astype(BF16), wglu_ref[...], preferred_element_type=F32)
    zz = (z * jax.nn.sigmoid(zg)).astype(BF16)
    s5b = jnp.dot(zz, wbs_ref[...], preferred_element_type=F32)
    rb = jnp.dot(ro_ref[...], wbr_ref[...], preferred_element_type=F32)
    gr = jnp.concatenate([gr0_ref[...], gr1_ref[...]], axis=-1).astype(F32)
    gs = jnp.concatenate([gs0_ref[...], gs1_ref[...]], axis=-1).astype(F32)
    merged = jax.nn.sigmoid(gr) * rb + jax.nn.sigmoid(gs) * s5b
    mix = jnp.dot(merged.astype(BF16), wout_ref[...], preferred_element_type=F32)
    h1 = _layer_norm(DN_ALPHA * h_ref[...] + mix, g_ref[...], b_ref[...])
    h1_ref[...] = h1
    h_hi = h1.astype(BF16)
    h_lo = (h1 - h_hi.astype(F32)).astype(BF16)
    both = jnp.dot(h_hi, wr_ref[...], preferred_element_type=F32)
    logits = (both[:, :ROUTE_LANES] + both[:, ROUTE_LANES:]
              + jnp.dot(h_lo, wr_ref[:, :ROUTE_LANES], preferred_element_type=F32)) + br_ref[...]
    lane, i1, i2, w1, w2 = _route(logits)
    oh1 = lane == i1
    oh2 = lane == i2
    oh = jnp.where(oh1 | oh2, 1.0, 0.0)
    rank_in_blk = jnp.dot(tri_ref[...], oh.astype(BF16), preferred_element_type=F32)
    blk_cnt = jnp.broadcast_to(jnp.sum(oh, axis=0, keepdims=True), run_ref.shape)
    lane8 = lax.broadcasted_iota(jnp.int32, run_ref.shape, 1)
    incl = blk_cnt
    shift = 1
    while shift < ROUTE_LANES:
        incl = incl + jnp.where(lane8 >= shift, pltpu.roll(incl, shift, 1), 0.0)
        shift *= 2
    lstart = incl - blk_cnt
    pos = lstart[0:1, :] + rank_in_blk
    l1 = jnp.sum(jnp.where(oh1, pos, 0.0), axis=-1, keepdims=True)
    l2 = jnp.sum(jnp.where(oh2, pos, 0.0), axis=-1, keepdims=True)
    run_before = run_ref[...]
    row8 = lax.broadcasted_iota(jnp.int32, run_ref.shape, 0)
    blk_ref[...] = jnp.where(row8 == 0, run_before,
                             jnp.where(row8 == 1, blk_cnt, jnp.where(row8 == 2, lstart, 0.0)))
    cnt = run_before + blk_cnt
    run_ref[...] = cnt
    cnt_ref[...] = cnt
    vals = (i1.astype(F32) - MOE_GROUPS, i2.astype(F32) - MOE_GROUPS, w1, w2)
    route = jnp.zeros(logits.shape, F32)
    for k, v in enumerate(vals):
        route = jnp.where(lane == k, v, route)
    route_ref[...] = route
    pos_t = jnp.transpose(jnp.where(lane == 0, l1, jnp.where(lane == 1, l2, 0.0)))
    lp_ref[...] = pos_t[0:8, :].astype(jnp.int32)


def mix_out(ro, y, proj, h, wglu, wbr, wbs, wout, g, b, wr, br, bsz, ts=MOE_TS):
    s = h.shape[0]
    d = D_MODEL

    def const(a):
        return pl.BlockSpec(a.shape, lambda bi, i: (0,) * a.ndim)

    def pspec(col):
        return pl.BlockSpec((ts, PROJ_BLK), lambda bi, i, col=col: (i, col * bsz + bi))

    def tok(w):
        return pl.BlockSpec((ts, w), lambda bi, i: (i, bi))

    g2, b2 = g.reshape(1, d), b.reshape(1, d)
    tri = jnp.asarray(np.tril(np.ones((ts, ts)), -1), dtype=BF16)
    return pl.pallas_call(
        _mix_kernel,
        grid=(bsz, s // ts),
        in_specs=[tok(RET_V), tok(S5_WIDTH), pspec(7), pspec(8), pspec(9), pspec(10), tok(d),
                  const(wglu), const(wbr), const(wbs), const(wout), const(g2), const(b2),
                  const(wr), const(br), const(tri)],
        out_specs=[tok(d), tok(ROUTE_LANES), pl.BlockSpec((8, ROUTE_LANES), lambda bi, i: (0, 0)),
                   pl.BlockSpec((None, 8, ROUTE_LANES), lambda bi, i: (bi * (s // ts) + i, 0, 0)),
                   pl.BlockSpec((None, 8, ts), lambda bi, i: (bi * (s // ts) + i, 0, 0))],
        out_shape=[jax.ShapeDtypeStruct((s, bsz * d), F32),
                   jax.ShapeDtypeStruct((s, bsz * ROUTE_LANES), F32),
                   jax.ShapeDtypeStruct((8, ROUTE_LANES), F32),
                   jax.ShapeDtypeStruct((bsz * (s // ts), 8, ROUTE_LANES), F32),
                   jax.ShapeDtypeStruct((bsz * (s // ts), 8, ts), jnp.int32)],
        scratch_shapes=[pltpu.VMEM((8, ROUTE_LANES), F32)],
        compiler_params=_cparams(("arbitrary", "arbitrary")),
        name="mix_out",
    )(ro, y, proj, proj, proj, proj, h, wglu, wbr, wbs, wout, g2, b2, wr, br, tri)


def moe_rows(t):
    return TOP_K * t + (N_EXPERTS + 1) * MOE_TM


def moe_plan(lp, cnt, blk, bsz, s):
    t = bsz * s
    experts = slice(MOE_GROUPS, MOE_GROUPS + N_EXPERTS)
    counts = cnt[0, experts].astype(jnp.int32)
    padded = ((counts + MOE_TM - 1) // MOE_TM) * MOE_TM
    ends = jnp.cumsum(padded)
    off = ends - padded
    run_start = (off[None, :] + blk[:, 0, experts].astype(jnp.int32)).reshape(-1)
    run_len = blk[:, 1, experts].astype(jnp.int32).reshape(-1)
    run_local = blk[:, 2, experts].astype(jnp.int32).reshape(-1)
    pads = (off + counts, padded - counts)
    n_tiles = moe_rows(t) // MOE_TM
    n_used = (ends[-1:] // MOE_TM).astype(jnp.int32)
    tile_start = jnp.arange(n_tiles, dtype=jnp.int32) * MOE_TM
    tile_expert = jnp.sum((ends[None, :] <= tile_start[:, None]).astype(jnp.int32), axis=1)
    tile_expert = jnp.minimum(tile_expert, N_EXPERTS - 1)
    return (lp[:, 0, :].reshape(t), lp[:, 1, :].reshape(t), run_start, run_len, run_local), pads, tile_expert, n_used


def _token_rows(start, n):
    return pl.ds(pl.multiple_of(start * ROW_TILE, ROW_TILE), n * ROW_TILE)


def _for_each_piece(n, nbits, fn):
    for bit in reversed(range(nbits)):
        size = 1 << bit
        done = (n >> (bit + 1)) << (bit + 1)

        @pl.when((n & size) != 0)
        def _(size=size, done=done):
            fn(done, size)


def _run_copies(blk, run_refs, make_copy):
    run_start_ref, run_len_ref, run_local_ref = run_refs
    for e in range(N_EXPERTS):
        idx = blk * N_EXPERTS + e
        g0 = run_start_ref[idx]
        l0 = run_local_ref[idx]
        _for_each_piece(run_len_ref[idx], RUN_BITS,
                        lambda off, size, g0=g0, l0=l0: make_copy(l0 + off, g0 + off, size).start())


def _to_token_tiles(ref, x):
    n = x.shape[0]
    for j in range(ROW_TILE):
        ref[pl.ds(j, n, stride=ROW_TILE), :] = x[:, j * LANES:(j + 1) * LANES]


def _from_token_tiles(ref, n):
    return jnp.concatenate([ref[pl.ds(j, n, stride=ROW_TILE), :] for j in range(ROW_TILE)], axis=-1)


def _dispatch_kernel(lp1_ref, lp2_ref, run_start_ref, run_len_ref, run_local_ref,
                     pad_ref, pad_len_ref, nu_ref,
                     x_ref, xs_ref, xt_ref, srt_ref, zero_ref, sem, zsem, *, ts):
    run_refs = (run_start_ref, run_len_ref, run_local_ref)
    nsteps = pl.num_programs(0) * pl.num_programs(1)
    step = pl.program_id(0) * pl.num_programs(1) + pl.program_id(1)

    def zero_copy(start, size):
        return pltpu.make_async_copy(zero_ref.at[_token_rows(0, size)],
                                     xs_ref.at[_token_rows(start, size)], zsem)

    def zero_fill(action):
        for e in range(N_EXPERTS):
            p0 = pad_ref[e]
            _for_each_piece(pad_len_ref[e], PAD_BITS,
                            lambda off, size, p0=p0: action(zero_copy(p0 + off, size)))

        def tail_tile(r, c):
            action(zero_copy(r * MOE_TM, MOE_TM))
            return c

        lax.fori_loop(nu_ref[0], xs_ref.shape[0] // (MOE_TM * ROW_TILE), tail_tile, 0)

    @pl.when(step == 0)
    def _():
        zero_ref[...] = jnp.zeros_like(zero_ref)
        zero_fill(lambda cp: cp.start())

    @pl.when(step == nsteps - 1)
    def _():
        zero_fill(lambda cp: cp.wait())

    slot = step % 2
    _to_token_tiles(xt_ref, x_ref[...])
    base = step * ts

    def place(i, c):
        tile = xt_ref[_token_rows(i, 1), :]
        srt_ref[slot, _token_rows(lp1_ref[base + i], 1), :] = tile
        srt_ref[slot, _token_rows(lp2_ref[base + i], 1), :] = tile
        return c

    lax.fori_loop(0, ts, place, 0, unroll=8)

    def run_copy(local_start, sorted_start, size):
        return pltpu.make_async_copy(srt_ref.at[slot, _token_rows(local_start, size)],
                                     xs_ref.at[_token_rows(sorted_start, size)], sem.at[slot])

    _run_copies(step, run_refs, run_copy)

    def drain(sl):
        pltpu.make_async_copy(srt_ref.at[sl], xs_ref.at[_token_rows(0, TOP_K * ts)], sem.at[sl]).wait()

    @pl.when(step > 0)
    def _():
        drain(1 - slot)

    @pl.when(step == nsteps - 1)
    def _():
        drain(slot)


def moe_dispatch(h1, plan, pads, n_used, bsz, ts=MOE_TS):
    s = h1.shape[0]
    d = D_MODEL
    kern = functools.partial(_dispatch_kernel, ts=ts)
    grid_spec = pltpu.PrefetchScalarGridSpec(
        num_scalar_prefetch=len(plan) + len(pads) + 1,
        grid=(bsz, s // ts),
        in_specs=[pl.BlockSpec((ts, d), lambda bi, i, *_: (i, bi))],
        out_specs=pl.BlockSpec(memory_space=pl.ANY),
        scratch_shapes=[pltpu.VMEM((ts * ROW_TILE, LANES), F32),
                        pltpu.VMEM((2, TOP_K * ts * ROW_TILE, LANES), F32),
                        pltpu.VMEM((MOE_TM * ROW_TILE, LANES), F32),
                        pltpu.SemaphoreType.DMA((2,)), pltpu.SemaphoreType.DMA(())],
    )
    return pl.pallas_call(
        kern,
        grid_spec=grid_spec,
        out_shape=jax.ShapeDtypeStruct((moe_rows(bsz * s) * ROW_TILE, LANES), F32),
        compiler_params=_cparams(("arbitrary", "arbitrary")),
        name="moe_dispatch",
    )(*plan, *pads, n_used, h1)


def _experts_kernel(te_ref, nu_ref, xs_ref, wg_ref, wu_ref, wd_ref, ys_ref, wgb_ref, wub_ref, wdb_ref):
    r = pl.program_id(0)

    @pl.when(r < nu_ref[0])
    def _():
        changed = (r == 0) | (te_ref[r] != te_ref[jnp.maximum(r - 1, 0)])

        @pl.when(changed)
        def _():
            wgb_ref[...] = wg_ref[...].astype(BF16)
            wub_ref[...] = wu_ref[...].astype(BF16)
            wdb_ref[...] = wd_ref[...].astype(BF16)

        x = _from_token_tiles(xs_ref, MOE_TM).astype(BF16)
        gate = jnp.dot(x, wgb_ref[...], preferred_element_type=F32)
        up = jnp.dot(x, wub_ref[...], preferred_element_type=F32)
        act = (gate * jax.nn.sigmoid(gate) * up).astype(BF16)
        _to_token_tiles(ys_ref, jnp.dot(act, wdb_ref[...], preferred_element_type=F32))

    @pl.when(r >= nu_ref[0])
    def _():
        ys_ref[...] = jnp.zeros_like(ys_ref)


def moe_experts(xs, tile_expert, n_used, wg, wu, wd, layer):
    d, f = wg.shape[-2:]
    rows = xs.shape[0] // ROW_TILE
    blk = MOE_TM * ROW_TILE

    def tile(r, te, nu):
        return jnp.minimum(r, nu[0] - 1)

    def wspec(a, b):
        return pl.BlockSpec((None, None, a, b), lambda r, te, nu: (layer, te[tile(r, te, nu)], 0, 0))

    grid_spec = pltpu.PrefetchScalarGridSpec(
        num_scalar_prefetch=2,
        grid=(rows // MOE_TM,),
        in_specs=[pl.BlockSpec((blk, LANES), lambda r, te, nu: (tile(r, te, nu), 0)),
                  wspec(d, f), wspec(d, f), wspec(f, d)],
        out_specs=pl.BlockSpec((blk, LANES), lambda r, te, nu: (r, 0)),
        scratch_shapes=[pltpu.VMEM((d, f), BF16), pltpu.VMEM((d, f), BF16), pltpu.VMEM((f, d), BF16)],
    )
    return pl.pallas_call(
        _experts_kernel,
        grid_spec=grid_spec,
        out_shape=jax.ShapeDtypeStruct(xs.shape, F32),
        compiler_params=_cparams(("arbitrary",)),
        name="moe_experts",
    )(tile_expert, n_used, xs, wg, wu, wd)


def _combine_kernel(lp1_ref, lp2_ref, run_start_ref, run_len_ref, run_local_ref,
                    ys_ref, route_ref, h_ref, g_ref, b_ref, *refs, ts):
    o_ref, maybe_ob_ref = refs[0], refs[1:-4]
    srt_ref, t1_ref, t2_ref, sem = refs[-4:]
    run_refs = (run_start_ref, run_len_ref, run_local_ref)
    nsteps = pl.num_programs(0) * pl.num_programs(1)
    step = pl.program_id(0) * pl.num_programs(1) + pl.program_id(1)

    def issue(blk, sl):
        def run_copy(local_start, sorted_start, size):
            return pltpu.make_async_copy(ys_ref.at[_token_rows(sorted_start, size)],
                                         srt_ref.at[sl, _token_rows(local_start, size)], sem.at[sl])
        _run_copies(blk, run_refs, run_copy)

    @pl.when(step == 0)
    def _():
        issue(0, 0)

    @pl.when(step + 1 < nsteps)
    def _():
        issue(step + 1, (step + 1) % 2)

    slot = step % 2
    pltpu.make_async_copy(ys_ref.at[_token_rows(0, TOP_K * ts)], srt_ref.at[slot], sem.at[slot]).wait()

    base = step * ts

    def pick(i, c):
        t1_ref[_token_rows(i, 1), :] = srt_ref[slot, _token_rows(lp1_ref[base + i], 1), :]
        t2_ref[_token_rows(i, 1), :] = srt_ref[slot, _token_rows(lp2_ref[base + i], 1), :]
        return c

    lax.fori_loop(0, ts, pick, 0, unroll=8)

    route = route_ref[...]
    w1 = route[:, 2:3]
    w2 = route[:, 3:4]
    ffn = w1 * _from_token_tiles(t1_ref, ts) + w2 * _from_token_tiles(t2_ref, ts)
    h2 = _layer_norm(DN_ALPHA * h_ref[...] + ffn, g_ref[...], b_ref[...])
    o_ref[...] = h2
    for ob_ref in maybe_ob_ref:
        ob_ref[...] = h2.astype(BF16)


def moe_combine(ys, plan, route, h1, g, b, bsz, last, ts=MOE_TS):
    s = h1.shape[0]
    d = D_MODEL
    g2, b2 = g.reshape(1, d), b.reshape(1, d)
    if last:
        o_specs = [pl.BlockSpec((None, ts, d), lambda bi, i, *_: (bi, i, 0))]
        o_shapes = [jax.ShapeDtypeStruct((bsz, s, d), F32)]
    else:
        o_specs = [pl.BlockSpec((ts, d), lambda bi, i, *_: (i, bi))] * 2
        o_shapes = [jax.ShapeDtypeStruct((s, bsz * d), F32), jax.ShapeDtypeStruct((s, bsz * d), BF16)]
    kern = functools.partial(_combine_kernel, ts=ts)
    grid_spec = pltpu.PrefetchScalarGridSpec(
        num_scalar_prefetch=len(plan),
        grid=(bsz, s // ts),
        in_specs=[pl.BlockSpec(memory_space=pl.ANY),
                  pl.BlockSpec((ts, ROUTE_LANES), lambda bi, i, *_: (i, bi)),
                  pl.BlockSpec((ts, d), lambda bi, i, *_: (i, bi)),
                  pl.BlockSpec((1, d), lambda bi, i, *_: (0, 0)),
                  pl.BlockSpec((1, d), lambda bi, i, *_: (0, 0))],
        out_specs=o_specs,
        scratch_shapes=[pltpu.VMEM((2, TOP_K * ts * ROW_TILE, LANES), F32),
                        pltpu.VMEM((ts * ROW_TILE, LANES), F32), pltpu.VMEM((ts * ROW_TILE, LANES), F32),
                        pltpu.SemaphoreType.DMA((2,))],
    )
    return pl.pallas_call(
        kern,
        grid_spec=grid_spec,
        out_shape=o_shapes,
        compiler_params=_cparams(("arbitrary", "arbitrary")),
        name="moe_combine",
    )(*plan, ys, route, h1, g2, b2)


def kernel(x, ln_in_g, ln_in_b, w_in, s5_lambda_re, s5_lambda_im, s5_log_step, s5_b_re, s5_b_im,
           s5_c_re, s5_c_im, s5_d, w_glu, w_branch_ret, w_branch_s5, w_out, ln_mix_g, ln_mix_b,
           w_router_group, b_router_group, w_router_expert, b_router_expert, w_exp_gate, w_exp_up,
           w_exp_down, ln_ffn_g, ln_ffn_b):
    bsz, s, d = x.shape
    t = bsz * s
    depth = w_in.shape[0]
    tabs = retention_tables(s)
    out = None
    for l in range(depth):
        if l == 0:
            proj, h_sb = ln_in_proj(x, ln_in_g, ln_in_b, w_in, l)
        else:
            proj = in_proj(hb_sb, w_in, l, bsz)
        ro = retention(proj, tabs, bsz, s)
        a_re, a_im, bb_re, bb_im = s5_params(s5_lambda_re[l], s5_lambda_im[l], s5_log_step[l],
                                             s5_b_re[l], s5_b_im[l])
        mats = s5_matrices(a_re, a_im, bb_re, bb_im, s5_c_re[l], s5_c_im[l])
        y = s5_scan(proj, mats, s5_d[l], bsz, s)
        pad = ROUTE_LANES - MOE_GROUPS - N_EXPERTS
        wr = jnp.concatenate([w_router_group[l], w_router_expert[l], jnp.zeros((d, pad), F32)], axis=1)
        wr_hi = wr.astype(BF16)
        wr = jnp.concatenate([wr_hi, (wr - wr_hi.astype(F32)).astype(BF16)], axis=1)
        br = jnp.concatenate([b_router_group[l], b_router_expert[l], jnp.zeros((pad,), F32)]).reshape(1, -1)
        h1, route, cnt, blk, lp = mix_out(ro, y, proj, h_sb,
                                 w_glu[l].astype(BF16), w_branch_ret[l].astype(BF16),
                                 w_branch_s5[l].astype(BF16), w_out[l].astype(BF16),
                                 ln_mix_g[l], ln_mix_b[l], wr, br, bsz)
        last = l == depth - 1
        plan, pads, tile_expert, n_used = moe_plan(lp, cnt, blk, bsz, s)
        xs = moe_dispatch(h1, plan, pads, n_used, bsz)
        ys = moe_experts(xs, tile_expert, n_used, w_exp_gate, w_exp_up, w_exp_down, l)
        outs = moe_combine(ys, plan, route, h1, ln_ffn_g[l], ln_ffn_b[l], bsz, last)
        if last:
            out = outs[0]
        else:
            h_sb, hb_sb = outs
    return out
```

```python
import functools
import math

import jax
import jax.numpy as jnp
import numpy as np
from jax import lax
from jax.experimental import pallas as pl
from jax.experimental.pallas import tpu as pltpu

D_MODEL = 1024
CHUNK = 64
RET_HEADS = 8
RET_QK = 512
RET_V = 1024
RET_QK_DIM = 64
RET_V_DIM = 128
ROPE_BASE = 10000.0
S5_WIDTH = 512
S5_GROUP_CH = 16
S5_GROUPS = 32
S5_STATE = 64
MOE_GROUPS = 4
EXPERTS_PER_GROUP = 8
N_EXPERTS = 32
EXPERT_FF = 256
LN_EPS = 1e-5
HEAD_NORM_EPS = 1e-6
DEPTH = 2
DN_ALPHA = (2 * DEPTH) ** 0.25
IN_WIDTH = 2 * RET_QK + 2 * RET_V + S5_WIDTH + 2 * D_MODEL
PROJ_BLK = 512
N_PROJ_BLK = IN_WIDTH // PROJ_BLK
INPROJ_BATCHES = 2

RET_SUPER = 512
S5_TT = 128
S5_COLS = 128
S5_NSLICE = S5_WIDTH // S5_COLS
S5_SLICE_STATE = (S5_COLS // S5_GROUP_CH) * S5_STATE
ROUTE_LANES = 128
TOP_K = 2
MOE_TM = 512
MOE_TS = 512
RUN_BITS = MOE_TS.bit_length()
PAD_BITS = (MOE_TM - 1).bit_length()
LANES = 128
ROW_TILE = D_MODEL // LANES
VMEM_LIMIT = 56 * 1024 * 1024

F32 = jnp.float32
BF16 = jnp.bfloat16


def _cparams(sem):
    return pltpu.CompilerParams(dimension_semantics=sem, vmem_limit_bytes=VMEM_LIMIT)


def _layer_norm(x, g, b):
    mu = jnp.mean(x, axis=-1, keepdims=True)
    xc = x - mu
    var = jnp.mean(xc * xc, axis=-1, keepdims=True)
    return xc * lax.rsqrt(var + LN_EPS) * g + b


def _inproj_kernel(h_ref, w_ref, o_ref):
    w = w_ref[...].astype(BF16)
    d = w.shape[0]
    for k in range(INPROJ_BATCHES):
        hk = h_ref[:, k * d:(k + 1) * d]
        o_ref[:, k * PROJ_BLK:(k + 1) * PROJ_BLK] = jnp.dot(hk, w, preferred_element_type=F32).astype(BF16)


def _ln_inproj_kernel(x_ref, g_ref, b_ref, w_ref, o_ref, h_ref, hb_ref):
    @pl.when(pl.program_id(2) == 0)
    def _():
        h = _layer_norm(x_ref[...], g_ref[...], b_ref[...])
        h_ref[...] = h
        hb_ref[...] = h.astype(BF16)

    w = w_ref[...].astype(BF16)
    o_ref[...] = jnp.dot(hb_ref[...], w, preferred_element_type=F32).astype(BF16)


def ln_in_proj(x, g, b, w, layer, ts=2048):
    bsz, s, d = x.shape
    n = w.shape[-1]
    nb = n // PROJ_BLK
    return pl.pallas_call(
        _ln_inproj_kernel,
        grid=(bsz, s // ts, nb),
        in_specs=[pl.BlockSpec((None, ts, d), lambda bi, i, j: (bi, i, 0)),
                  pl.BlockSpec((1, d), lambda bi, i, j: (0, 0)),
                  pl.BlockSpec((1, d), lambda bi, i, j: (0, 0)),
                  pl.BlockSpec((None, d, PROJ_BLK), lambda bi, i, j: (layer, 0, j))],
        out_specs=[pl.BlockSpec((ts, PROJ_BLK), lambda bi, i, j: (i, j * bsz + bi)),
                   pl.BlockSpec((ts, d), lambda bi, i, j: (i, bi))],
        out_shape=[jax.ShapeDtypeStruct((s, bsz * n), BF16),
                   jax.ShapeDtypeStruct((s, bsz * d), F32)],
        scratch_shapes=[pltpu.VMEM((ts, d), BF16)],
        compiler_params=_cparams(("parallel", "parallel", "arbitrary")),
        name="ln_in_proj",
    )(x, g.reshape(1, d), b.reshape(1, d), w)


def in_proj(hb_sb, w, layer, bsz, ts=2048):
    s = hb_sb.shape[0]
    d, n = w.shape[-2:]
    nb = n // PROJ_BLK
    kb = INPROJ_BATCHES
    return pl.pallas_call(
        _inproj_kernel,
        grid=(bsz // kb, s // ts, nb),
        in_specs=[pl.BlockSpec((ts, kb * d), lambda b, i, j: (i, b)),
                  pl.BlockSpec((None, d, PROJ_BLK), lambda b, i, j: (layer, 0, j))],
        out_specs=pl.BlockSpec((ts, kb * PROJ_BLK), lambda b, i, j: (i, (j * bsz) // kb + b)),
        out_shape=jax.ShapeDtypeStruct((s, bsz * n), BF16),
        compiler_params=_cparams(("parallel", "parallel", "arbitrary")),
        name="in_proj",
    )(hb_sb, w)


def _swap_halves(x):
    lane = lax.broadcasted_iota(jnp.int32, x.shape, 1)
    first = (lane % RET_QK_DIM) < (RET_QK_DIM // 2)
    n = x.shape[1]
    return jnp.where(first, pltpu.roll(x, n - RET_QK_DIM // 2, 1), pltpu.roll(x, RET_QK_DIM // 2, 1))


def _retention_kernel(q_ref, k_ref, v0_ref, v1_ref, g0_ref, g1_ref, cos_ref, sin_ref,
                      qd_ref, kd_ref, mask_ref, cd_ref, o_ref, state_ref):
    @pl.when(pl.program_id(1) == 0)
    def _():
        state_ref[...] = jnp.zeros_like(state_ref)

    cos = cos_ref[...]
    sin = sin_ref[...]
    q = q_ref[...].astype(F32)
    k = k_ref[...].astype(F32)
    q = q * cos + _swap_halves(q) * sin
    k = k * cos + _swap_halves(k) * sin
    qb = q.astype(BF16)
    kb = k.astype(BF16)
    qdb = (q * qd_ref[...]).astype(BF16)
    kdb = (k * kd_ref[...]).astype(BF16)
    for hd in range(RET_HEADS):
        qs = slice(hd * RET_QK_DIM, (hd + 1) * RET_QK_DIM)
        half, off = divmod(hd * RET_V_DIM, PROJ_BLK)
        vs = slice(off, off + RET_V_DIM)
        vh = (v0_ref, v1_ref)[half][:, vs]
        gh = (g0_ref, g1_ref)[half][:, vs].astype(F32)
        sc = lax.dot_general(qb[:, qs], kb[:, qs], (((1,), (1,)), ((), ())),
                             preferred_element_type=F32)
        sc = (sc * mask_ref[hd]).astype(BF16)
        st = state_ref[hd]
        o = jnp.dot(sc, vh, preferred_element_type=F32)
        o = o + jnp.dot(qdb[:, qs], st.astype(BF16), preferred_element_type=F32)
        kv = lax.dot_general(kdb[:, qs], vh, (((0,), (0,)), ((), ())),
                             preferred_element_type=F32)
        state_ref[hd] = st * cd_ref[hd] + kv
        mu = jnp.mean(o, axis=-1, keepdims=True)
        oc = o - mu
        var = jnp.mean(oc * oc, axis=-1, keepdims=True)
        on = oc * lax.rsqrt(var + HEAD_NORM_EPS)
        o_ref[:, hd * RET_V_DIM:(hd + 1) * RET_V_DIM] = (gh * jax.nn.sigmoid(gh) * on).astype(BF16)


def retention(proj_sb, tabs, bsz, s):
    cos_t, sin_t, qd_t, kd_t, mask, cd = tabs
    L = RET_SUPER

    def pspec(col):
        return pl.BlockSpec((L, PROJ_BLK), lambda b, i, col=col: (i, col * bsz + b))

    full2 = pl.BlockSpec((L, RET_QK), lambda b, i: (0, 0))
    return pl.pallas_call(
        _retention_kernel,
        grid=(bsz, s // L),
        in_specs=[pspec(0), pspec(1), pspec(2), pspec(3), pspec(4), pspec(5),
                  pl.BlockSpec((L, RET_QK), lambda b, i: (i, 0)),
                  pl.BlockSpec((L, RET_QK), lambda b, i: (i, 0)),
                  full2, full2,
                  pl.BlockSpec((RET_HEADS, L, L), lambda b, i: (0, 0, 0)),
                  pl.BlockSpec((RET_HEADS, 1, RET_V_DIM), lambda b, i: (0, 0, 0))],
        out_specs=pl.BlockSpec((L, RET_V), lambda b, i: (i, b)),
        out_shape=jax.ShapeDtypeStruct((s, bsz * RET_V), BF16),
        scratch_shapes=[pltpu.VMEM((RET_HEADS, RET_QK_DIM, RET_V_DIM), F32)],
        compiler_params=_cparams(("parallel", "arbitrary")),
        name="retention",
    )(proj_sb, proj_sb, proj_sb, proj_sb, proj_sb, proj_sb, cos_t, sin_t, qd_t, kd_t, mask, cd)


def retention_tables(s):
    L = RET_SUPER
    half = RET_QK_DIM // 2
    inv_freq = ROPE_BASE ** (-np.arange(half, dtype=np.float64) / half)
    ang = np.arange(s, dtype=np.float64)[:, None] * inv_freq[None, :]
    cos, sin = np.cos(ang), np.sin(ang)
    cos_t = np.tile(np.concatenate([cos, cos], -1), (1, RET_HEADS))
    sin_t = np.tile(np.concatenate([-sin, sin], -1), (1, RET_HEADS))
    log_gamma = np.log1p(-(2.0 ** (-5.0 - np.arange(RET_HEADS, dtype=np.float64))))
    pos = np.arange(L, dtype=np.float64)
    qd = np.exp(log_gamma[None, :] * (pos + 1.0)[:, None])
    k_scale = RET_QK_DIM ** -0.5
    kd = np.exp(log_gamma[None, :] * (L - 1.0 - pos)[:, None]) * k_scale
    qd_t = np.repeat(qd, RET_QK_DIM, axis=1)
    kd_t = np.repeat(kd, RET_QK_DIM, axis=1)
    chunk_id = np.arange(L) // CHUNK
    visible = (chunk_id[None, :] <= chunk_id[:, None]).astype(np.float64)
    mask = np.exp(log_gamma[:, None, None] * np.abs(pos[:, None] - pos[None, :])) * visible[None] * k_scale
    cd = np.broadcast_to(np.exp(log_gamma * L)[:, None, None], (RET_HEADS, 1, RET_V_DIM))
    return tuple(jnp.asarray(a, dtype=F32) for a in (cos_t, sin_t, qd_t, kd_t, mask, cd))


def _s5_param_kernel(lre_ref, lim_ref, ls_ref, bre_ref, bim_ref, are_ref, aim_ref, bbre_ref, bbim_ref):
    lam_re = jnp.minimum(lre_ref[...], -1e-4)
    lam_im = lim_ref[...]
    step = jnp.exp(ls_ref[...])
    mag = jnp.exp(lam_re * step)
    ang = lam_im * step
    ab_re = mag * jnp.cos(ang)
    ab_im = mag * jnp.sin(ang)
    den = lam_re * lam_re + lam_im * lam_im
    n_re = ab_re - 1.0
    zc_re = (n_re * lam_re + ab_im * lam_im) / den
    zc_im = (ab_im * lam_re - n_re * lam_im) / den
    are_ref[...] = ab_re
    aim_ref[...] = ab_im
    b_re = bre_ref[...]
    b_im = bim_ref[...]
    bbre_ref[...] = zc_re * b_re - zc_im * b_im
    bbim_ref[...] = zc_re * b_im + zc_im * b_re


def s5_params(lam_re, lam_im, log_step, b_re, b_im):
    g, n = lam_re.shape
    c = b_re.shape[-1]
    outs = pl.pallas_call(
        _s5_param_kernel,
        out_shape=[jax.ShapeDtypeStruct((g, 1, n), F32), jax.ShapeDtypeStruct((g, 1, n), F32),
                   jax.ShapeDtypeStruct((g, c, n), F32), jax.ShapeDtypeStruct((g, c, n), F32)],
        name="s5_params",
    )(lam_re.reshape(g, 1, n), lam_im.reshape(g, 1, n), log_step.reshape(g, 1, 1),
      jnp.swapaxes(b_re, 1, 2), jnp.swapaxes(b_im, 1, 2))
    return outs


def _block_diag(x):
    ns, gl, r, c = x.shape
    eye = jnp.eye(gl, dtype=x.dtype)
    return jnp.einsum('sgrc,gh->sgrhc', x, eye).reshape(ns, gl * r, gl * c)


def s5_matrices(a_re, a_im, bb_re, bb_im, c_re, c_im):
    gl = S5_COLS // S5_GROUP_CH
    ns = S5_NSLICE
    bre = _block_diag(bb_re.reshape(ns, gl, S5_GROUP_CH, S5_STATE))
    bim = _block_diag(bb_im.reshape(ns, gl, S5_GROUP_CH, S5_STATE))
    bq = jnp.concatenate([bre, bim], axis=-1).astype(BF16)
    cre = _block_diag(jnp.swapaxes(c_re, 1, 2).reshape(ns, gl, S5_STATE, S5_GROUP_CH))
    cim = _block_diag(jnp.swapaxes(c_im, 1, 2).reshape(ns, gl, S5_STATE, S5_GROUP_CH))
    cq = jnp.concatenate([cre, -cim], axis=1).astype(BF16)
    are = a_re.reshape(ns, 1, S5_SLICE_STATE)
    aim = a_im.reshape(ns, 1, S5_SLICE_STATE)
    return bq, cq, are, aim


def _s5_kernel(*refs, bsz, tt):
    u_refs = refs[:bsz]
    bq_ref, cq_ref, are_ref, aim_ref, d_ref, y_ref, us_ref, ys_ref, bu_ref, st_ref = refs[bsz:]

    @pl.when(pl.program_id(0) == 0)
    def _():
        st_ref[...] = jnp.zeros_like(st_ref)

    for b in range(bsz):
        ub = u_refs[b][...].astype(F32)
        for cs in range(S5_NSLICE):
            us_ref[cs, pl.ds(b, tt, stride=bsz), :] = ub[:, cs * S5_COLS:(cs + 1) * S5_COLS]

    ns2 = S5_SLICE_STATE
    for cs in range(S5_NSLICE):
        cols = slice(cs * S5_COLS, (cs + 1) * S5_COLS)
        uf = us_ref[cs]
        bu_ref[cs] = jnp.dot(uf.astype(BF16), bq_ref[cs], preferred_element_type=F32)
        a_re = jnp.broadcast_to(are_ref[cs], (bsz, ns2))
        a_im = jnp.broadcast_to(aim_ref[cs], (bsz, ns2))

        def step(t, carry):
            h_re, h_im = carry
            rows = pl.ds(pl.multiple_of(t * bsz, bsz), bsz)
            n_re = a_re * h_re - a_im * h_im + bu_ref[cs, rows, 0:ns2]
            n_im = a_re * h_im + a_im * h_re + bu_ref[cs, rows, ns2:2 * ns2]
            bu_ref[cs, rows, 0:ns2] = n_re
            bu_ref[cs, rows, ns2:2 * ns2] = n_im
            return n_re, n_im

        h_re, h_im = lax.fori_loop(0, tt, step, (st_ref[cs, 0], st_ref[cs, 1]), unroll=True)
        st_ref[cs, 0] = h_re
        st_ref[cs, 1] = h_im
        y = jnp.dot(bu_ref[cs].astype(BF16), cq_ref[cs], preferred_element_type=F32)
        ys_ref[cs] = y + d_ref[:, cols] * uf

    for b in range(bsz):
        for cs in range(S5_NSLICE):
            lo = b * S5_WIDTH + cs * S5_COLS
            y_ref[:, lo:lo + S5_COLS] = ys_ref[cs, pl.ds(b, tt, stride=bsz), :]


def s5_scan(proj_sb, mats, d_skip, bsz, s):
    bq, cq, are, aim = mats
    tt = S5_TT
    rows = tt * bsz
    kern = functools.partial(_s5_kernel, bsz=bsz, tt=tt)
    u_specs = [pl.BlockSpec((tt, PROJ_BLK), lambda i, b=b: (i, 6 * bsz + b)) for b in range(bsz)]
    return pl.pallas_call(
        kern,
        grid=(s // tt,),
        in_specs=u_specs + [
                  pl.BlockSpec(bq.shape, lambda i: (0, 0, 0)),
                  pl.BlockSpec(cq.shape, lambda i: (0, 0, 0)),
                  pl.BlockSpec(are.shape, lambda i: (0, 0, 0)),
                  pl.BlockSpec(aim.shape, lambda i: (0, 0, 0)),
                  pl.BlockSpec((1, S5_WIDTH), lambda i: (0, 0))],
        out_specs=pl.BlockSpec((tt, bsz * S5_WIDTH), lambda i: (i, 0)),
        out_shape=jax.ShapeDtypeStruct((s, bsz * S5_WIDTH), F32),
        scratch_shapes=[pltpu.VMEM((S5_NSLICE, rows, S5_COLS), F32),
                        pltpu.VMEM((S5_NSLICE, rows, S5_COLS), F32),
                        pltpu.VMEM((S5_NSLICE, rows, 2 * S5_SLICE_STATE), F32),
                        pltpu.VMEM((S5_NSLICE, 2, bsz, S5_SLICE_STATE), F32)],
        compiler_params=_cparams(("arbitrary",)),
        name="s5_scan",
    )(*([proj_sb] * bsz), bq, cq, are, aim, d_skip.reshape(1, S5_WIDTH))


def _gelu_tanh(x):
    c = math.sqrt(2.0 / math.pi)
    return 0.5 * x * (1.0 + jnp.tanh(c * (x + 0.044715 * (x * x * x))))


def _route(logits):
    lane = lax.broadcasted_iota(jnp.int32, logits.shape, 1)
    neg = jnp.float32(-jnp.inf)
    big = jnp.int32(1 << 20)
    is_g = lane < MOE_GROUPS
    lg = jnp.where(is_g, logits, neg)
    mg = jnp.max(lg, axis=-1, keepdims=True)
    sg = jnp.sum(jnp.where(is_g, jnp.exp(lg - mg), 0.0), axis=-1, keepdims=True)
    g_top = 1.0 / sg
    g_idx = jnp.min(jnp.where(lg == mg, lane, big), axis=-1, keepdims=True)
    lo = MOE_GROUPS + g_idx * EXPERTS_PER_GROUP
    in_grp = (lane >= lo) & (lane < lo + EXPERTS_PER_GROUP)
    le = jnp.where(in_grp, logits, neg)
    m1 = jnp.max(le, axis=-1, keepdims=True)
    se = jnp.sum(jnp.where(in_grp, jnp.exp(le - m1), 0.0), axis=-1, keepdims=True)
    i1 = jnp.min(jnp.where(le == m1, lane, big), axis=-1, keepdims=True)
    le2 = jnp.where(lane == i1, neg, le)
    m2 = jnp.max(le2, axis=-1, keepdims=True)
    i2 = jnp.min(jnp.where(le2 == m2, lane, big), axis=-1, keepdims=True)
    p1 = 1.0 / se
    p2 = jnp.exp(m2 - m1) / se
    tot = p1 + p2
    w1 = g_top * (p1 / tot)
    w2 = g_top * (p2 / tot)
    return lane, i1, i2, w1, w2


def _mix_kernel(ro_ref, y_ref, gr0_ref, gr1_ref, gs0_ref, gs1_ref, h_ref,
                wglu_ref, wbr_ref, wbs_ref, wout_ref, g_ref, b_ref, wr_ref, br_ref, tri_ref,
                h1_ref, route_ref, cnt_ref, blk_ref, lp_ref,
                run_ref, wglu_b, wbr_b, wbs_b, wout_b):
    @pl.when((pl.program_id(0) == 0) & (pl.program_id(1) == 0))
    def _():
        run_ref[...] = jnp.zeros_like(run_ref)
        wglu_b[...] = wglu_ref[...].astype(BF16)
        wbr_b[...] = wbr_ref[...].astype(BF16)
        wbs_b[...] = wbs_ref[...].astype(BF16)
        wout_b[...] = wout_ref[...].astype(BF16)

    z = _gelu_tanh(y_ref[...])
    zg = jnp.dot(z.astype(BF16), wglu_b[...], preferred_element_type=F32)
    zz = (z * jax.nn.sigmoid(zg)).astype(BF16)
    s5b = jnp.dot(zz, wbs_b[...], preferred_element_type=F32)
    rb = jnp.dot(ro_ref[...], wbr_b[...], preferred_element_type=F32)
    gr = jnp.concatenate([gr0_ref[...], gr1_ref[...]], axis=-1).astype(F32)
    gs = jnp.concatenate([gs0_ref[...], gs1_ref[...]], axis=-1).astype(F32)
    merged = jax.nn.sigmoid(gr) * rb + jax.nn.sigmoid(gs) * s5b
    mix = jnp.dot(merged.astype(BF16), wout_b[...], preferred_element_type=F32)
    h1 = _layer_norm(DN_ALPHA * h_ref[...] + mix, g_ref[...], b_ref[...])
    h1_ref[...] = h1
    h_hi = h1.astype(BF16)
    h_lo = (h1 - h_hi.astype(F32)).astype(BF16)
    both = jnp.dot(h_hi, wr_ref[...], preferred_element_type=F32)
    logits = (both[:, :ROUTE_LANES] + both[:, ROUTE_LANES:]
              + jnp.dot(h_lo, wr_ref[:, :ROUTE_LANES], preferred_element_type=F32)) + br_ref[...]
    lane, i1, i2, w1, w2 = _route(logits)
    oh1 = lane == i1
    oh2 = lane == i2
    oh = jnp.where(oh1 | oh2, 1.0, 0.0)
    rank_in_blk = jnp.dot(tri_ref[...], oh.astype(BF16), preferred_element_type=F32)
    blk_cnt = jnp.broadcast_to(jnp.sum(oh, axis=0, keepdims=True), run_ref.shape)
    lane8 = lax.broadcasted_iota(jnp.int32, run_ref.shape, 1)
    incl = blk_cnt
    shift = 1
    while shift < ROUTE_LANES:
        incl = incl + jnp.where(lane8 >= shift, pltpu.roll(incl, shift, 1), 0.0)
        shift *= 2
    lstart = incl - blk_cnt
    pos = lstart[0:1, :] + rank_in_blk
    l1 = jnp.sum(jnp.where(oh1, pos, 0.0), axis=-1, keepdims=True)
    l2 = jnp.sum(jnp.where(oh2, pos, 0.0), axis=-1, keepdims=True)
    run_before = run_ref[...]
    row8 = lax.broadcasted_iota(jnp.int32, run_ref.shape, 0)
    blk_ref[...] = jnp.where(row8 == 0, run_before,
                             jnp.where(row8 == 1, blk_cnt, jnp.where(row8 == 2, lstart, 0.0)))
    cnt = run_before + blk_cnt
    run_ref[...] = cnt
    cnt_ref[...] = cnt
    vals = (i1.astype(F32) - MOE_GROUPS, i2.astype(F32) - MOE_GROUPS, w1, w2)
    route = jnp.zeros(logits.shape, F32)
    for k, v in enumerate(vals):
        route = jnp.where(lane == k, v, route)
    route_ref[...] = route
    pos_t = jnp.transpose(jnp.where(lane == 0, l1, jnp.where(lane == 1, l2, 0.0)))
    lp_ref[...] = pos_t[0:8, :].astype(jnp.int32)


def mix_out(ro, y, proj, h, wglu, wbr, wbs, wout, layer, g, b, wr, br, bsz, ts=MOE_TS):
    s = h.shape[0]
    d = D_MODEL

    def const(a):
        return pl.BlockSpec(a.shape, lambda bi, i: (0,) * a.ndim)

    def stacked(w):
        return pl.BlockSpec((None,) + w.shape[1:], lambda bi, i: (layer, 0, 0))

    def pspec(col):
        return pl.BlockSpec((ts, PROJ_BLK), lambda bi, i, col=col: (i, col * bsz + bi))

    def tok(w):
        return pl.BlockSpec((ts, w), lambda bi, i: (i, bi))

    g2, b2 = g.reshape(1, d), b.reshape(1, d)
    tri = jnp.asarray(np.tril(np.ones((ts, ts)), -1), dtype=BF16)
    return pl.pallas_call(
        _mix_kernel,
        grid=(bsz, s // ts),
        in_specs=[tok(RET_V), tok(S5_WIDTH), pspec(7), pspec(8), pspec(9), pspec(10), tok(d),
                  stacked(wglu), stacked(wbr), stacked(wbs), stacked(wout), const(g2), const(b2),
                  const(wr), const(br), const(tri)],
        out_specs=[tok(d), tok(ROUTE_LANES), pl.BlockSpec((8, ROUTE_LANES), lambda bi, i: (0, 0)),
                   pl.BlockSpec((None, 8, ROUTE_LANES), lambda bi, i: (bi * (s // ts) + i, 0, 0)),
                   pl.BlockSpec((None, 8, ts), lambda bi, i: (bi * (s // ts) + i, 0, 0))],
        out_shape=[jax.ShapeDtypeStruct((s, bsz * d), F32),
                   jax.ShapeDtypeStruct((s, bsz * ROUTE_LANES), F32),
                   jax.ShapeDtypeStruct((8, ROUTE_LANES), F32),
                   jax.ShapeDtypeStruct((bsz * (s // ts), 8, ROUTE_LANES), F32),
                   jax.ShapeDtypeStruct((bsz * (s // ts), 8, ts), jnp.int32)],
        scratch_shapes=[pltpu.VMEM((8, ROUTE_LANES), F32)]
        + [pltpu.VMEM(w.shape[1:], BF16) for w in (wglu, wbr, wbs, wout)],
        compiler_params=_cparams(("arbitrary", "arbitrary")),
        name="mix_out",
    )(ro, y, proj, proj, proj, proj, h, wglu, wbr, wbs, wout, g2, b2, wr, br, tri)


def moe_rows(t):
    return TOP_K * t + (N_EXPERTS + 1) * MOE_TM


def moe_plan(lp, cnt, blk, bsz, s):
    t = bsz * s
    experts = slice(MOE_GROUPS, MOE_GROUPS + N_EXPERTS)
    counts = cnt[0, experts].astype(jnp.int32)
    padded = ((counts + MOE_TM - 1) // MOE_TM) * MOE_TM
    ends = jnp.cumsum(padded)
    off = ends - padded
    run_start = (off[None, :] + blk[:, 0, experts].astype(jnp.int32)).reshape(-1)
    run_len = blk[:, 1, experts].astype(jnp.int32).reshape(-1)
    run_local = blk[:, 2, experts].astype(jnp.int32).reshape(-1)
    pads = (off + counts, padded - counts)
    n_tiles = moe_rows(t) // MOE_TM
    n_used = (ends[-1:] // MOE_TM).astype(jnp.int32)
    tile_start = jnp.arange(n_tiles, dtype=jnp.int32) * MOE_TM
    tile_expert = jnp.sum((ends[None, :] <= tile_start[:, None]).astype(jnp.int32), axis=1)
    tile_expert = jnp.minimum(tile_expert, N_EXPERTS - 1)
    return (lp[:, 0, :].reshape(t), lp[:, 1, :].reshape(t), run_start, run_len, run_local), pads, tile_expert, n_used


def _token_rows(start, n):
    return pl.ds(pl.multiple_of(start * ROW_TILE, ROW_TILE), n * ROW_TILE)


def _for_each_piece(n, nbits, fn):
    for bit in reversed(range(nbits)):
        size = 1 << bit
        done = (n >> (bit + 1)) << (bit + 1)

        @pl.when((n & size) != 0)
        def _(size=size, done=done):
            fn(done, size)


def _run_copies(blk, run_refs, make_copy):
    run_start_ref, run_len_ref, run_local_ref = run_refs
    for e in range(N_EXPERTS):
        idx = blk * N_EXPERTS + e
        g0 = run_start_ref[idx]
        l0 = run_local_ref[idx]
        _for_each_piece(run_len_ref[idx], RUN_BITS,
                        lambda off, size, g0=g0, l0=l0: make_copy(l0 + off, g0 + off, size).start())


def _to_token_tiles(ref, x):
    n = x.shape[0]
    for j in range(ROW_TILE):
        ref[pl.ds(j, n, stride=ROW_TILE), :] = x[:, j * LANES:(j + 1) * LANES]


def _from_token_tiles(ref, n):
    return jnp.concatenate([ref[pl.ds(j, n, stride=ROW_TILE), :] for j in range(ROW_TILE)], axis=-1)


def _dispatch_kernel(lp1_ref, lp2_ref, run_start_ref, run_len_ref, run_local_ref,
                     pad_ref, pad_len_ref, nu_ref,
                     x_ref, xs_ref, xt_ref, srt_ref, zero_ref, sem, zsem, *, ts):
    run_refs = (run_start_ref, run_len_ref, run_local_ref)
    nsteps = pl.num_programs(0) * pl.num_programs(1)
    step = pl.program_id(0) * pl.num_programs(1) + pl.program_id(1)

    def zero_copy(start, size):
        return pltpu.make_async_copy(zero_ref.at[_token_rows(0, size)],
                                     xs_ref.at[_token_rows(start, size)], zsem)

    def zero_fill(action):
        for e in range(N_EXPERTS):
            p0 = pad_ref[e]
            _for_each_piece(pad_len_ref[e], PAD_BITS,
                            lambda off, size, p0=p0: action(zero_copy(p0 + off, size)))

        def tail_tile(r, c):
            action(zero_copy(r * MOE_TM, MOE_TM))
            return c

        lax.fori_loop(nu_ref[0], xs_ref.shape[0] // (MOE_TM * ROW_TILE), tail_tile, 0)

    @pl.when(step == 0)
    def _():
        zero_ref[...] = jnp.zeros_like(zero_ref)
        zero_fill(lambda cp: cp.start())

    @pl.when(step == nsteps - 1)
    def _():
        zero_fill(lambda cp: cp.wait())

    slot = step % 2
    _to_token_tiles(xt_ref, x_ref[...])
    base = step * ts

    def place(i, c):
        tile = xt_ref[_token_rows(i, 1), :]
        srt_ref[slot, _token_rows(lp1_ref[base + i], 1), :] = tile
        srt_ref[slot, _token_rows(lp2_ref[base + i], 1), :] = tile
        return c

    lax.fori_loop(0, ts, place, 0, unroll=8)

    def run_copy(local_start, sorted_start, size):
        return pltpu.make_async_copy(srt_ref.at[slot, _token_rows(local_start, size)],
                                     xs_ref.at[_token_rows(sorted_start, size)], sem.at[slot])

    _run_copies(step, run_refs, run_copy)

    def drain(sl):
        pltpu.make_async_copy(srt_ref.at[sl], xs_ref.at[_token_rows(0, TOP_K * ts)], sem.at[sl]).wait()

    @pl.when(step > 0)
    def _():
        drain(1 - slot)

    @pl.when(step == nsteps - 1)
    def _():
        drain(slot)


def moe_dispatch(h1, plan, pads, n_used, bsz, ts=MOE_TS):
    s = h1.shape[0]
    d = D_MODEL
    kern = functools.partial(_dispatch_kernel, ts=ts)
    grid_spec = pltpu.PrefetchScalarGridSpec(
        num_scalar_prefetch=len(plan) + len(pads) + 1,
        grid=(bsz, s // ts),
        in_specs=[pl.BlockSpec((ts, d), lambda bi, i, *_: (i, bi))],
        out_specs=pl.BlockSpec(memory_space=pl.ANY),
        scratch_shapes=[pltpu.VMEM((ts * ROW_TILE, LANES), F32),
                        pltpu.VMEM((2, TOP_K * ts * ROW_TILE, LANES), F32),
                        pltpu.VMEM((MOE_TM * ROW_TILE, LANES), F32),
                        pltpu.SemaphoreType.DMA((2,)), pltpu.SemaphoreType.DMA(())],
    )
    return pl.pallas_call(
        kern,
        grid_spec=grid_spec,
        out_shape=jax.ShapeDtypeStruct((moe_rows(bsz * s) * ROW_TILE, LANES), F32),
        compiler_params=_cparams(("arbitrary", "arbitrary")),
        name="moe_dispatch",
    )(*plan, *pads, n_used, h1)


def _experts_kernel(te_ref, nu_ref, xs_ref, wg_ref, wu_ref, wd_ref, ys_ref, wgb_ref, wub_ref, wdb_ref):
    r = pl.program_id(0)

    @pl.when(r < nu_ref[0])
    def _():
        changed = (r == 0) | (te_ref[r] != te_ref[jnp.maximum(r - 1, 0)])

        @pl.when(changed)
        def _():
            wgb_ref[...] = wg_ref[...].astype(BF16)
            wub_ref[...] = wu_ref[...].astype(BF16)
            wdb_ref[...] = wd_ref[...].astype(BF16)

        x = _from_token_tiles(xs_ref, MOE_TM).astype(BF16)
        gate = jnp.dot(x, wgb_ref[...], preferred_element_type=F32)
        up = jnp.dot(x, wub_ref[...], preferred_element_type=F32)
        act = (gate * jax.nn.sigmoid(gate) * up).astype(BF16)
        _to_token_tiles(ys_ref, jnp.dot(act, wdb_ref[...], preferred_element_type=F32))

    @pl.when(r >= nu_ref[0])
    def _():
        ys_ref[...] = jnp.zeros_like(ys_ref)


def moe_experts(xs, tile_expert, n_used, wg, wu, wd, layer):
    d, f = wg.shape[-2:]
    rows = xs.shape[0] // ROW_TILE
    blk = MOE_TM * ROW_TILE

    def tile(r, te, nu):
        return jnp.minimum(r, nu[0] - 1)

    def wspec(a, b):
        return pl.BlockSpec((None, None, a, b), lambda r, te, nu: (layer, te[tile(r, te, nu)], 0, 0))

    grid_spec = pltpu.PrefetchScalarGridSpec(
        num_scalar_prefetch=2,
        grid=(rows // MOE_TM,),
        in_specs=[pl.BlockSpec((blk, LANES), lambda r, te, nu: (tile(r, te, nu), 0)),
                  wspec(d, f), wspec(d, f), wspec(f, d)],
        out_specs=pl.BlockSpec((blk, LANES), lambda r, te, nu: (r, 0)),
        scratch_shapes=[pltpu.VMEM((d, f), BF16), pltpu.VMEM((d, f), BF16), pltpu.VMEM((f, d), BF16)],
    )
    return pl.pallas_call(
        _experts_kernel,
        grid_spec=grid_spec,
        out_shape=jax.ShapeDtypeStruct(xs.shape, F32),
        compiler_params=_cparams(("arbitrary",)),
        name="moe_experts",
    )(tile_expert, n_used, xs, wg, wu, wd)


def _combine_kernel(lp1_ref, lp2_ref, run_start_ref, run_len_ref, run_local_ref,
                    ys_ref, route_ref, h_ref, g_ref, b_ref, *refs, ts):
    o_ref, maybe_ob_ref = refs[0], refs[1:-4]
    srt_ref, t1_ref, t2_ref, sem = refs[-4:]
    run_refs = (run_start_ref, run_len_ref, run_local_ref)
    nsteps = pl.num_programs(0) * pl.num_programs(1)
    step = pl.program_id(0) * pl.num_programs(1) + pl.program_id(1)

    def issue(blk, sl):
        def run_copy(local_start, sorted_start, size):
            return pltpu.make_async_copy(ys_ref.at[_token_rows(sorted_start, size)],
                                         srt_ref.at[sl, _token_rows(local_start, size)], sem.at[sl])
        _run_copies(blk, run_refs, run_copy)

    @pl.when(step == 0)
    def _():
        issue(0, 0)

    @pl.when(step + 1 < nsteps)
    def _():
        issue(step + 1, (step + 1) % 2)

    slot = step % 2
    pltpu.make_async_copy(ys_ref.at[_token_rows(0, TOP_K * ts)], srt_ref.at[slot], sem.at[slot]).wait()

    base = step * ts

    def pick(i, c):
        t1_ref[_token_rows(i, 1), :] = srt_ref[slot, _token_rows(lp1_ref[base + i], 1), :]
        t2_ref[_token_rows(i, 1), :] = srt_ref[slot, _token_rows(lp2_ref[base + i], 1), :]
        return c

    lax.fori_loop(0, ts, pick, 0, unroll=8)

    route = route_ref[...]
    w1 = route[:, 2:3]
    w2 = route[:, 3:4]
    ffn = w1 * _from_token_tiles(t1_ref, ts) + w2 * _from_token_tiles(t2_ref, ts)
    h2 = _layer_norm(DN_ALPHA * h_ref[...] + ffn, g_ref[...], b_ref[...])
    o_ref[...] = h2
    for ob_ref in maybe_ob_ref:
        ob_ref[...] = h2.astype(BF16)


def moe_combine(ys, plan, route, h1, g, b, bsz, last, ts=MOE_TS):
    s = h1.shape[0]
    d = D_MODEL
    g2, b2 = g.reshape(1, d), b.reshape(1, d)
    if last:
        o_specs = [pl.BlockSpec((None, ts, d), lambda bi, i, *_: (bi, i, 0))]
        o_shapes = [jax.ShapeDtypeStruct((bsz, s, d), F32)]
    else:
        o_specs = [pl.BlockSpec((ts, d), lambda bi, i, *_: (i, bi))] * 2
        o_shapes = [jax.ShapeDtypeStruct((s, bsz * d), F32), jax.ShapeDtypeStruct((s, bsz * d), BF16)]
    kern = functools.partial(_combine_kernel, ts=ts)
    grid_spec = pltpu.PrefetchScalarGridSpec(
        num_scalar_prefetch=len(plan),
        grid=(bsz, s // ts),
        in_specs=[pl.BlockSpec(memory_space=pl.ANY),
                  pl.BlockSpec((ts, ROUTE_LANES), lambda bi, i, *_: (i, bi)),
                  pl.BlockSpec((ts, d), lambda bi, i, *_: (i, bi)),
                  pl.BlockSpec((1, d), lambda bi, i, *_: (0, 0)),
                  pl.BlockSpec((1, d), lambda bi, i, *_: (0, 0))],
        out_specs=o_specs,
        scratch_shapes=[pltpu.VMEM((2, TOP_K * ts * ROW_TILE, LANES), F32),
                        pltpu.VMEM((ts * ROW_TILE, LANES), F32), pltpu.VMEM((ts * ROW_TILE, LANES), F32),
                        pltpu.SemaphoreType.DMA((2,))],
    )
    return pl.pallas_call(
        kern,
        grid_spec=grid_spec,
        out_shape=o_shapes,
        compiler_params=_cparams(("arbitrary", "arbitrary")),
        name="moe_combine",
    )(*plan, ys, route, h1, g2, b2)


def kernel(x, ln_in_g, ln_in_b, w_in, s5_lambda_re, s5_lambda_im, s5_log_step, s5_b_re, s5_b_im,
           s5_c_re, s5_c_im, s5_d, w_glu, w_branch_ret, w_branch_s5, w_out, ln_mix_g, ln_mix_b,
           w_router_group, b_router_group, w_router_expert, b_router_expert, w_exp_gate, w_exp_up,
           w_exp_down, ln_ffn_g, ln_ffn_b):
    bsz, s, d = x.shape
    t = bsz * s
    depth = w_in.shape[0]
    tabs = retention_tables(s)
    out = None
    for l in range(depth):
        if l == 0:
            proj, h_sb = ln_in_proj(x, ln_in_g, ln_in_b, w_in, l)
        else:
            proj = in_proj(hb_sb, w_in, l, bsz)
        ro = retention(proj, tabs, bsz, s)
        a_re, a_im, bb_re, bb_im = s5_params(s5_lambda_re[l], s5_lambda_im[l], s5_log_step[l],
                                             s5_b_re[l], s5_b_im[l])
        mats = s5_matrices(a_re, a_im, bb_re, bb_im, s5_c_re[l], s5_c_im[l])
        y = s5_scan(proj, mats, s5_d[l], bsz, s)
        pad = ROUTE_LANES - MOE_GROUPS - N_EXPERTS
        wr = jnp.concatenate([w_router_group[l], w_router_expert[l], jnp.zeros((d, pad), F32)], axis=1)
        wr_hi = wr.astype(BF16)
        wr = jnp.concatenate([wr_hi, (wr - wr_hi.astype(F32)).astype(BF16)], axis=1)
        br = jnp.concatenate([b_router_group[l], b_router_expert[l], jnp.zeros((pad,), F32)]).reshape(1, -1)
        h1, route, cnt, blk, lp = mix_out(ro, y, proj, h_sb, w_glu, w_branch_ret, w_branch_s5, w_out, l,
                                          ln_mix_g[l], ln_mix_b[l], wr, br, bsz)
        last = l == depth - 1
        plan, pads, tile_expert, n_used = moe_plan(lp, cnt, blk, bsz, s)
        xs = moe_dispatch(h1, plan, pads, n_used, bsz)
        ys = moe_experts(xs, tile_expert, n_used, w_exp_gate, w_exp_up, w_exp_down, l)
        outs = moe_combine(ys, plan, route, h1, ln_ffn_g[l], ln_ffn_b[l], bsz, last)
        if last:
            out = outs[0]
        else:
            h_sb, hb_sb = outs
    return out
```

```python
import functools
import math

import jax
import jax.numpy as jnp
import numpy as np
from jax import lax
from jax.experimental import pallas as pl
from jax.experimental.pallas import tpu as pltpu

D_MODEL = 1024
CHUNK = 64
RET_HEADS = 8
RET_QK = 512
RET_V = 1024
RET_QK_DIM = 64
RET_V_DIM = 128
ROPE_BASE = 10000.0
S5_WIDTH = 512
S5_GROUP_CH = 16
S5_STATE = 64
MOE_GROUPS = 4
EXPERTS_PER_GROUP = 8
N_EXPERTS = 32
LN_EPS = 1e-5
HEAD_NORM_EPS = 1e-6
DEPTH = 2
DN_ALPHA = (2 * DEPTH) ** 0.25
IN_WIDTH = 2 * RET_QK + 2 * RET_V + S5_WIDTH + 2 * D_MODEL
PROJ_BLK = 512
N_PROJ_BLK = IN_WIDTH // PROJ_BLK
INPROJ_BATCHES = 2

RET_SUPER = 512
S5_TT = 128
S5_COLS = 128
S5_NSLICE = S5_WIDTH // S5_COLS
S5_SLICE_STATE = (S5_COLS // S5_GROUP_CH) * S5_STATE
ROUTE_LANES = 128
TOP_K = 2
MOE_TM = 512
MOE_TS = 512
RUN_BITS = MOE_TS.bit_length()
PAD_BITS = (MOE_TM - 1).bit_length()
LANES = 128
SUBLANES = 8
ROW_TILE = D_MODEL // LANES
VMEM_LIMIT = 56 * 1024 * 1024

F32 = jnp.float32
BF16 = jnp.bfloat16


def _cparams(sem):
    return pltpu.CompilerParams(dimension_semantics=sem, vmem_limit_bytes=VMEM_LIMIT)


def _layer_norm(x, g, b):
    mu = jnp.mean(x, axis=-1, keepdims=True)
    xc = x - mu
    var = jnp.mean(xc * xc, axis=-1, keepdims=True)
    return xc * lax.rsqrt(var + LN_EPS) * g + b


def _inproj_kernel(h_ref, w_ref, o_ref):
    w = w_ref[...].astype(BF16)
    d = w.shape[0]
    for k in range(INPROJ_BATCHES):
        hk = h_ref[:, k * d:(k + 1) * d]
        o_ref[:, k * PROJ_BLK:(k + 1) * PROJ_BLK] = jnp.dot(hk, w, preferred_element_type=F32).astype(BF16)


def _ln_inproj_kernel(x_ref, g_ref, b_ref, w_ref, o_ref, h_ref, hb_ref):
    @pl.when(pl.program_id(2) == 0)
    def _():
        h = _layer_norm(x_ref[...], g_ref[...], b_ref[...])
        h_ref[...] = h
        hb_ref[...] = h.astype(BF16)

    w = w_ref[...].astype(BF16)
    o_ref[...] = jnp.dot(hb_ref[...], w, preferred_element_type=F32).astype(BF16)


def ln_in_proj(x, g, b, w, layer, ts=2048):
    bsz, s, d = x.shape
    n = w.shape[-1]
    nb = n // PROJ_BLK
    return pl.pallas_call(
        _ln_inproj_kernel,
        grid=(bsz, s // ts, nb),
        in_specs=[pl.BlockSpec((None, ts, d), lambda bi, i, j: (bi, i, 0)),
                  pl.BlockSpec((1, d), lambda bi, i, j: (0, 0)),
                  pl.BlockSpec((1, d), lambda bi, i, j: (0, 0)),
                  pl.BlockSpec((None, d, PROJ_BLK), lambda bi, i, j: (layer, 0, j))],
        out_specs=[pl.BlockSpec((ts, PROJ_BLK), lambda bi, i, j: (i, j * bsz + bi)),
                   pl.BlockSpec((ts, d), lambda bi, i, j: (i, bi))],
        out_shape=[jax.ShapeDtypeStruct((s, bsz * n), BF16),
                   jax.ShapeDtypeStruct((s, bsz * d), F32)],
        scratch_shapes=[pltpu.VMEM((ts, d), BF16)],
        compiler_params=_cparams(("parallel", "parallel", "arbitrary")),
        name="ln_in_proj",
    )(x, g.reshape(1, d), b.reshape(1, d), w)


def in_proj(hb_sb, w, layer, bsz, ts=2048):
    s = hb_sb.shape[0]
    d, n = w.shape[-2:]
    nb = n // PROJ_BLK
    kb = INPROJ_BATCHES
    return pl.pallas_call(
        _inproj_kernel,
        grid=(bsz // kb, s // ts, nb),
        in_specs=[pl.BlockSpec((ts, kb * d), lambda b, i, j: (i, b)),
                  pl.BlockSpec((None, d, PROJ_BLK), lambda b, i, j: (layer, 0, j))],
        out_specs=pl.BlockSpec((ts, kb * PROJ_BLK), lambda b, i, j: (i, (j * bsz) // kb + b)),
        out_shape=jax.ShapeDtypeStruct((s, bsz * n), BF16),
        compiler_params=_cparams(("parallel", "parallel", "arbitrary")),
        name="in_proj",
    )(hb_sb, w)


def _swap_halves(x):
    lane = lax.broadcasted_iota(jnp.int32, x.shape, 1)
    first = (lane % RET_QK_DIM) < (RET_QK_DIM // 2)
    n = x.shape[1]
    return jnp.where(first, pltpu.roll(x, n - RET_QK_DIM // 2, 1), pltpu.roll(x, RET_QK_DIM // 2, 1))


def _retention_kernel(q_ref, k_ref, v0_ref, v1_ref, g0_ref, g1_ref, cos_ref, sin_ref,
                      qd_ref, kd_ref, mask_ref, cd_ref, o_ref, state_ref):
    @pl.when(pl.program_id(1) == 0)
    def _():
        state_ref[...] = jnp.zeros_like(state_ref)

    cos = cos_ref[...]
    sin = sin_ref[...]
    q = q_ref[...].astype(F32)
    k = k_ref[...].astype(F32)
    q = q * cos + _swap_halves(q) * sin
    k = k * cos + _swap_halves(k) * sin
    qb = q.astype(BF16)
    kb = k.astype(BF16)
    qdb = (q * qd_ref[...]).astype(BF16)
    kdb = (k * kd_ref[...]).astype(BF16)
    for hd in range(RET_HEADS):
        qs = slice(hd * RET_QK_DIM, (hd + 1) * RET_QK_DIM)
        half, off = divmod(hd * RET_V_DIM, PROJ_BLK)
        vs = slice(off, off + RET_V_DIM)
        vh = (v0_ref, v1_ref)[half][:, vs]
        gh = (g0_ref, g1_ref)[half][:, vs].astype(F32)
        sc = lax.dot_general(qb[:, qs], kb[:, qs], (((1,), (1,)), ((), ())),
                             preferred_element_type=F32)
        sc = (sc * mask_ref[hd]).astype(BF16)
        st = state_ref[hd]
        o = jnp.dot(sc, vh, preferred_element_type=F32)
        o = o + jnp.dot(qdb[:, qs], st.astype(BF16), preferred_element_type=F32)
        kv = lax.dot_general(kdb[:, qs], vh, (((0,), (0,)), ((), ())),
                             preferred_element_type=F32)
        state_ref[hd] = st * cd_ref[hd] + kv
        mu = jnp.mean(o, axis=-1, keepdims=True)
        oc = o - mu
        var = jnp.mean(oc * oc, axis=-1, keepdims=True)
        on = oc * lax.rsqrt(var + HEAD_NORM_EPS)
        o_ref[:, hd * RET_V_DIM:(hd + 1) * RET_V_DIM] = (gh * jax.nn.sigmoid(gh) * on).astype(BF16)


def retention(proj_sb, tabs, bsz, s):
    cos_t, sin_t, qd_t, kd_t, mask, cd = tabs
    L = RET_SUPER

    def pspec(col):
        return pl.BlockSpec((L, PROJ_BLK), lambda b, i, col=col: (i, col * bsz + b))

    full2 = pl.BlockSpec((L, RET_QK), lambda b, i: (0, 0))
    return pl.pallas_call(
        _retention_kernel,
        grid=(bsz, s // L),
        in_specs=[pspec(0), pspec(1), pspec(2), pspec(3), pspec(4), pspec(5),
                  pl.BlockSpec((L, RET_QK), lambda b, i: (i, 0)),
                  pl.BlockSpec((L, RET_QK), lambda b, i: (i, 0)),
                  full2, full2,
                  pl.BlockSpec((RET_HEADS, L, L), lambda b, i: (0, 0, 0)),
                  pl.BlockSpec((RET_HEADS, 1, RET_V_DIM), lambda b, i: (0, 0, 0))],
        out_specs=pl.BlockSpec((L, RET_V), lambda b, i: (i, b)),
        out_shape=jax.ShapeDtypeStruct((s, bsz * RET_V), BF16),
        scratch_shapes=[pltpu.VMEM((RET_HEADS, RET_QK_DIM, RET_V_DIM), F32)],
        compiler_params=_cparams(("parallel", "arbitrary")),
        name="retention",
    )(proj_sb, proj_sb, proj_sb, proj_sb, proj_sb, proj_sb, cos_t, sin_t, qd_t, kd_t, mask, cd)


def retention_tables(s):
    L = RET_SUPER
    half = RET_QK_DIM // 2
    inv_freq = ROPE_BASE ** (-np.arange(half, dtype=np.float64) / half)
    ang = np.arange(s, dtype=np.float64)[:, None] * inv_freq[None, :]
    cos, sin = np.cos(ang), np.sin(ang)
    cos_t = np.tile(np.concatenate([cos, cos], -1), (1, RET_HEADS))
    sin_t = np.tile(np.concatenate([-sin, sin], -1), (1, RET_HEADS))
    log_gamma = np.log1p(-(2.0 ** (-5.0 - np.arange(RET_HEADS, dtype=np.float64))))
    pos = np.arange(L, dtype=np.float64)
    qd = np.exp(log_gamma[None, :] * (pos + 1.0)[:, None])
    k_scale = RET_QK_DIM ** -0.5
    kd = np.exp(log_gamma[None, :] * (L - 1.0 - pos)[:, None]) * k_scale
    qd_t = np.repeat(qd, RET_QK_DIM, axis=1)
    kd_t = np.repeat(kd, RET_QK_DIM, axis=1)
    chunk_id = np.arange(L) // CHUNK
    visible = (chunk_id[None, :] <= chunk_id[:, None]).astype(np.float64)
    mask = np.exp(log_gamma[:, None, None] * np.abs(pos[:, None] - pos[None, :])) * visible[None] * k_scale
    cd = np.broadcast_to(np.exp(log_gamma * L)[:, None, None], (RET_HEADS, 1, RET_V_DIM))
    return tuple(jnp.asarray(a, dtype=F32) for a in (cos_t, sin_t, qd_t, kd_t, mask, cd))


def _s5_param_kernel(lre_ref, lim_ref, ls_ref, bre_ref, bim_ref, are_ref, aim_ref, bbre_ref, bbim_ref):
    lam_re = jnp.minimum(lre_ref[...], -1e-4)
    lam_im = lim_ref[...]
    step = jnp.exp(ls_ref[...])
    mag = jnp.exp(lam_re * step)
    ang = lam_im * step
    ab_re = mag * jnp.cos(ang)
    ab_im = mag * jnp.sin(ang)
    den = lam_re * lam_re + lam_im * lam_im
    n_re = ab_re - 1.0
    zc_re = (n_re * lam_re + ab_im * lam_im) / den
    zc_im = (ab_im * lam_re - n_re * lam_im) / den
    are_ref[...] = ab_re
    aim_ref[...] = ab_im
    b_re = bre_ref[...]
    b_im = bim_ref[...]
    bbre_ref[...] = zc_re * b_re - zc_im * b_im
    bbim_ref[...] = zc_re * b_im + zc_im * b_re


def s5_params(lam_re, lam_im, log_step, b_re, b_im):
    g, n = lam_re.shape
    c = b_re.shape[-1]
    outs = pl.pallas_call(
        _s5_param_kernel,
        out_shape=[jax.ShapeDtypeStruct((g, 1, n), F32), jax.ShapeDtypeStruct((g, 1, n), F32),
                   jax.ShapeDtypeStruct((g, c, n), F32), jax.ShapeDtypeStruct((g, c, n), F32)],
        name="s5_params",
    )(lam_re.reshape(g, 1, n), lam_im.reshape(g, 1, n), log_step.reshape(g, 1, 1),
      jnp.swapaxes(b_re, 1, 2), jnp.swapaxes(b_im, 1, 2))
    return outs


def _block_diag(x):
    ns, gl, r, c = x.shape
    eye = jnp.eye(gl, dtype=x.dtype)
    return jnp.einsum('sgrc,gh->sgrhc', x, eye).reshape(ns, gl * r, gl * c)


def s5_matrices(a_re, a_im, bb_re, bb_im, c_re, c_im):
    gl = S5_COLS // S5_GROUP_CH
    ns = S5_NSLICE
    bre = _block_diag(bb_re.reshape(ns, gl, S5_GROUP_CH, S5_STATE))
    bim = _block_diag(bb_im.reshape(ns, gl, S5_GROUP_CH, S5_STATE))
    bq = jnp.concatenate([bre, bim], axis=-1).astype(BF16)
    cre = _block_diag(jnp.swapaxes(c_re, 1, 2).reshape(ns, gl, S5_STATE, S5_GROUP_CH))
    cim = _block_diag(jnp.swapaxes(c_im, 1, 2).reshape(ns, gl, S5_STATE, S5_GROUP_CH))
    cq = jnp.concatenate([cre, -cim], axis=1).astype(BF16)
    are = a_re.reshape(ns, 1, S5_SLICE_STATE)
    aim = a_im.reshape(ns, 1, S5_SLICE_STATE)
    return bq, cq, are, aim


def _s5_kernel(*refs, bsz, tt):
    u_refs = refs[:bsz]
    bq_ref, cq_ref, are_ref, aim_ref, d_ref, y_ref, us_ref, ys_ref, bu_ref, st_ref = refs[bsz:]

    @pl.when(pl.program_id(0) == 0)
    def _():
        st_ref[...] = jnp.zeros_like(st_ref)

    for b in range(bsz):
        ub = u_refs[b][...].astype(F32)
        for cs in range(S5_NSLICE):
            us_ref[cs, pl.ds(b, tt, stride=bsz), :] = ub[:, cs * S5_COLS:(cs + 1) * S5_COLS]

    ns2 = S5_SLICE_STATE
    for cs in range(S5_NSLICE):
        cols = slice(cs * S5_COLS, (cs + 1) * S5_COLS)
        uf = us_ref[cs]
        bu_ref[cs] = jnp.dot(uf.astype(BF16), bq_ref[cs], preferred_element_type=F32)
        a_re = jnp.broadcast_to(are_ref[cs], (bsz, ns2))
        a_im = jnp.broadcast_to(aim_ref[cs], (bsz, ns2))

        def step(t, carry):
            h_re, h_im = carry
            rows = pl.ds(pl.multiple_of(t * bsz, bsz), bsz)
            n_re = a_re * h_re - a_im * h_im + bu_ref[cs, rows, 0:ns2]
            n_im = a_re * h_im + a_im * h_re + bu_ref[cs, rows, ns2:2 * ns2]
            bu_ref[cs, rows, 0:ns2] = n_re
            bu_ref[cs, rows, ns2:2 * ns2] = n_im
            return n_re, n_im

        h_re, h_im = lax.fori_loop(0, tt, step, (st_ref[cs, 0], st_ref[cs, 1]), unroll=True)
        st_ref[cs, 0] = h_re
        st_ref[cs, 1] = h_im
        y = jnp.dot(bu_ref[cs].astype(BF16), cq_ref[cs], preferred_element_type=F32)
        ys_ref[cs] = y + d_ref[:, cols] * uf

    for b in range(bsz):
        for cs in range(S5_NSLICE):
            lo = b * S5_WIDTH + cs * S5_COLS
            y_ref[:, lo:lo + S5_COLS] = ys_ref[cs, pl.ds(b, tt, stride=bsz), :]


def s5_scan(proj_sb, mats, d_skip, bsz, s):
    bq, cq, are, aim = mats
    tt = S5_TT
    rows = tt * bsz
    kern = functools.partial(_s5_kernel, bsz=bsz, tt=tt)
    u_specs = [pl.BlockSpec((tt, PROJ_BLK), lambda i, b=b: (i, 6 * bsz + b)) for b in range(bsz)]
    return pl.pallas_call(
        kern,
        grid=(s // tt,),
        in_specs=u_specs + [
                  pl.BlockSpec(bq.shape, lambda i: (0, 0, 0)),
                  pl.BlockSpec(cq.shape, lambda i: (0, 0, 0)),
                  pl.BlockSpec(are.shape, lambda i: (0, 0, 0)),
                  pl.BlockSpec(aim.shape, lambda i: (0, 0, 0)),
                  pl.BlockSpec((1, S5_WIDTH), lambda i: (0, 0))],
        out_specs=pl.BlockSpec((tt, bsz * S5_WIDTH), lambda i: (i, 0)),
        out_shape=jax.ShapeDtypeStruct((s, bsz * S5_WIDTH), F32),
        scratch_shapes=[pltpu.VMEM((S5_NSLICE, rows, S5_COLS), F32),
                        pltpu.VMEM((S5_NSLICE, rows, S5_COLS), F32),
                        pltpu.VMEM((S5_NSLICE, rows, 2 * S5_SLICE_STATE), F32),
                        pltpu.VMEM((S5_NSLICE, 2, bsz, S5_SLICE_STATE), F32)],
        compiler_params=_cparams(("arbitrary",)),
        name="s5_scan",
    )(*([proj_sb] * bsz), bq, cq, are, aim, d_skip.reshape(1, S5_WIDTH))


def _gelu_tanh(x):
    c = math.sqrt(2.0 / math.pi)
    return 0.5 * x * (1.0 + jnp.tanh(c * (x + 0.044715 * (x * x * x))))


def _route(logits):
    lane = lax.broadcasted_iota(jnp.int32, logits.shape, 1)
    neg = jnp.float32(-jnp.inf)
    big = jnp.int32(1 << 20)
    is_g = lane < MOE_GROUPS
    lg = jnp.where(is_g, logits, neg)
    mg = jnp.max(lg, axis=-1, keepdims=True)
    sg = jnp.sum(jnp.where(is_g, jnp.exp(lg - mg), 0.0), axis=-1, keepdims=True)
    g_top = 1.0 / sg
    g_idx = jnp.min(jnp.where(lg == mg, lane, big), axis=-1, keepdims=True)
    lo = MOE_GROUPS + g_idx * EXPERTS_PER_GROUP
    in_grp = (lane >= lo) & (lane < lo + EXPERTS_PER_GROUP)
    le = jnp.where(in_grp, logits, neg)
    m1 = jnp.max(le, axis=-1, keepdims=True)
    se = jnp.sum(jnp.where(in_grp, jnp.exp(le - m1), 0.0), axis=-1, keepdims=True)
    i1 = jnp.min(jnp.where(le == m1, lane, big), axis=-1, keepdims=True)
    le2 = jnp.where(lane == i1, neg, le)
    m2 = jnp.max(le2, axis=-1, keepdims=True)
    i2 = jnp.min(jnp.where(le2 == m2, lane, big), axis=-1, keepdims=True)
    p1 = 1.0 / se
    p2 = jnp.exp(m2 - m1) / se
    tot = p1 + p2
    w1 = g_top * (p1 / tot)
    w2 = g_top * (p2 / tot)
    return lane, i1, i2, w1, w2


def _mix_kernel(ro_ref, y_ref, gr0_ref, gr1_ref, gs0_ref, gs1_ref, h_ref,
                wglu_ref, wbr_ref, wbs_ref, wout_ref, g_ref, b_ref, wr_ref, br_ref, tri_ref,
                h1_ref, route_ref, cnt_ref, blk_ref, lp_ref,
                run_ref, wglu_b, wbr_b, wbs_b, wout_b):
    @pl.when((pl.program_id(0) == 0) & (pl.program_id(1) == 0))
    def _():
        run_ref[...] = jnp.zeros_like(run_ref)
        wglu_b[...] = wglu_ref[...].astype(BF16)
        wbr_b[...] = wbr_ref[...].astype(BF16)
        wbs_b[...] = wbs_ref[...].astype(BF16)
        wout_b[...] = wout_ref[...].astype(BF16)

    z = _gelu_tanh(y_ref[...])
    zg = jnp.dot(z.astype(BF16), wglu_b[...], preferred_element_type=F32)
    zz = (z * jax.nn.sigmoid(zg)).astype(BF16)
    s5b = jnp.dot(zz, wbs_b[...], preferred_element_type=F32)
    rb = jnp.dot(ro_ref[...], wbr_b[...], preferred_element_type=F32)
    gr = jnp.concatenate([gr0_ref[...], gr1_ref[...]], axis=-1).astype(F32)
    gs = jnp.concatenate([gs0_ref[...], gs1_ref[...]], axis=-1).astype(F32)
    merged = jax.nn.sigmoid(gr) * rb + jax.nn.sigmoid(gs) * s5b
    mix = jnp.dot(merged.astype(BF16), wout_b[...], preferred_element_type=F32)
    h1 = _layer_norm(DN_ALPHA * h_ref[...] + mix, g_ref[...], b_ref[...])
    h1_ref[...] = h1
    h_hi = h1.astype(BF16)
    h_lo = (h1 - h_hi.astype(F32)).astype(BF16)
    both = jnp.dot(h_hi, wr_ref[...], preferred_element_type=F32)
    logits = (both[:, :ROUTE_LANES] + both[:, ROUTE_LANES:]
              + jnp.dot(h_lo, wr_ref[:, :ROUTE_LANES], preferred_element_type=F32)) + br_ref[...]
    lane, i1, i2, w1, w2 = _route(logits)
    oh1 = lane == i1
    oh2 = lane == i2
    oh = jnp.where(oh1 | oh2, 1.0, 0.0)
    rank_in_blk = jnp.dot(tri_ref[...], oh.astype(BF16), preferred_element_type=F32)
    blk_cnt = jnp.broadcast_to(jnp.sum(oh, axis=0, keepdims=True), run_ref.shape)
    lane8 = lax.broadcasted_iota(jnp.int32, run_ref.shape, 1)
    incl = blk_cnt
    shift = 1
    while shift < ROUTE_LANES:
        incl = incl + jnp.where(lane8 >= shift, pltpu.roll(incl, shift, 1), 0.0)
        shift *= 2
    lstart = incl - blk_cnt
    pos = lstart[0:1, :] + rank_in_blk
    l1 = jnp.sum(jnp.where(oh1, pos, 0.0), axis=-1, keepdims=True)
    l2 = jnp.sum(jnp.where(oh2, pos, 0.0), axis=-1, keepdims=True)
    run_before = run_ref[...]
    row8 = lax.broadcasted_iota(jnp.int32, run_ref.shape, 0)
    blk_ref[...] = jnp.where(row8 == 0, run_before,
                             jnp.where(row8 == 1, blk_cnt, jnp.where(row8 == 2, lstart, 0.0)))
    cnt = run_before + blk_cnt
    run_ref[...] = cnt
    cnt_ref[...] = cnt
    vals = (i1.astype(F32) - MOE_GROUPS, i2.astype(F32) - MOE_GROUPS, w1, w2)
    route = jnp.zeros(logits.shape, F32)
    for k, v in enumerate(vals):
        route = jnp.where(lane == k, v, route)
    route_ref[...] = route
    pos_t = jnp.transpose(jnp.where(lane == 0, l1, jnp.where(lane == 1, l2, 0.0)))
    lp_ref[...] = pos_t[0:SUBLANES, :].astype(jnp.int32)


def mix_out(ro, y, proj, h, wglu, wbr, wbs, wout, layer, g, b, wr, br, bsz, ts=MOE_TS):
    s = h.shape[0]
    d = D_MODEL

    def const(a):
        return pl.BlockSpec(a.shape, lambda bi, i: (0,) * a.ndim)

    def stacked(w):
        return pl.BlockSpec((None,) + w.shape[1:], lambda bi, i: (layer, 0, 0))

    def pspec(col):
        return pl.BlockSpec((ts, PROJ_BLK), lambda bi, i, col=col: (i, col * bsz + bi))

    def tok(w):
        return pl.BlockSpec((ts, w), lambda bi, i: (i, bi))

    g2, b2 = g.reshape(1, d), b.reshape(1, d)
    tri = jnp.asarray(np.tril(np.ones((ts, ts)), -1), dtype=BF16)
    return pl.pallas_call(
        _mix_kernel,
        grid=(bsz, s // ts),
        in_specs=[tok(RET_V), tok(S5_WIDTH), pspec(7), pspec(8), pspec(9), pspec(10), tok(d),
                  stacked(wglu), stacked(wbr), stacked(wbs), stacked(wout), const(g2), const(b2),
                  const(wr), const(br), const(tri)],
        out_specs=[tok(d), tok(ROUTE_LANES), pl.BlockSpec((SUBLANES, ROUTE_LANES), lambda bi, i: (0, 0)),
                   pl.BlockSpec((None, SUBLANES, ROUTE_LANES), lambda bi, i: (bi * (s // ts) + i, 0, 0)),
                   pl.BlockSpec((None, SUBLANES, ts), lambda bi, i: (bi * (s // ts) + i, 0, 0))],
        out_shape=[jax.ShapeDtypeStruct((s, bsz * d), F32),
                   jax.ShapeDtypeStruct((s, bsz * ROUTE_LANES), F32),
                   jax.ShapeDtypeStruct((SUBLANES, ROUTE_LANES), F32),
                   jax.ShapeDtypeStruct((bsz * (s // ts), SUBLANES, ROUTE_LANES), F32),
                   jax.ShapeDtypeStruct((bsz * (s // ts), SUBLANES, ts), jnp.int32)],
        scratch_shapes=[pltpu.VMEM((SUBLANES, ROUTE_LANES), F32)]
        + [pltpu.VMEM(w.shape[1:], BF16) for w in (wglu, wbr, wbs, wout)],
        compiler_params=_cparams(("arbitrary", "arbitrary")),
        name="mix_out",
    )(ro, y, proj, proj, proj, proj, h, wglu, wbr, wbs, wout, g2, b2, wr, br, tri)


def moe_rows(t):
    return TOP_K * t + N_EXPERTS * MOE_TM


def moe_plan(lp, cnt, blk, bsz, s):
    t = bsz * s
    experts = slice(MOE_GROUPS, MOE_GROUPS + N_EXPERTS)
    counts = cnt[0, experts].astype(jnp.int32)
    padded = ((counts + MOE_TM - 1) // MOE_TM) * MOE_TM
    ends = jnp.cumsum(padded)
    off = ends - padded
    run_start = (off[None, :] + blk[:, 0, experts].astype(jnp.int32)).reshape(-1)
    run_len = blk[:, 1, experts].astype(jnp.int32).reshape(-1)
    run_local = blk[:, 2, experts].astype(jnp.int32).reshape(-1)
    pads = (off + counts, padded - counts)
    n_tiles = moe_rows(t) // MOE_TM
    n_used = (ends[-1:] // MOE_TM).astype(jnp.int32)
    tile_start = jnp.arange(n_tiles, dtype=jnp.int32) * MOE_TM
    tile_expert = jnp.sum((ends[None, :] <= tile_start[:, None]).astype(jnp.int32), axis=1)
    tile_expert = jnp.minimum(tile_expert, N_EXPERTS - 1)
    return (lp[:, 0, :].reshape(t), lp[:, 1, :].reshape(t), run_start, run_len, run_local), pads, tile_expert, n_used


def _token_rows(start, n):
    return pl.ds(pl.multiple_of(start * ROW_TILE, ROW_TILE), n * ROW_TILE)


def _for_each_piece(n, nbits, fn):
    for bit in reversed(range(nbits)):
        size = 1 << bit
        done = (n >> (bit + 1)) << (bit + 1)

        @pl.when((n & size) != 0)
        def _(size=size, done=done):
            fn(done, size)


def _run_copies(blk, run_refs, make_copy):
    run_start_ref, run_len_ref, run_local_ref = run_refs
    for e in range(N_EXPERTS):
        idx = blk * N_EXPERTS + e
        g0 = run_start_ref[idx]
        l0 = run_local_ref[idx]
        _for_each_piece(run_len_ref[idx], RUN_BITS,
                        lambda off, size, g0=g0, l0=l0: make_copy(l0 + off, g0 + off, size).start())


def _to_token_tiles(ref, x):
    n = x.shape[0]
    for j in range(ROW_TILE):
        ref[pl.ds(j, n, stride=ROW_TILE), :] = x[:, j * LANES:(j + 1) * LANES]


def _from_token_tiles(ref, n):
    return jnp.concatenate([ref[pl.ds(j, n, stride=ROW_TILE), :] for j in range(ROW_TILE)], axis=-1)


def _dispatch_kernel(lp1_ref, lp2_ref, run_start_ref, run_len_ref, run_local_ref,
                     pad_ref, pad_len_ref, nu_ref,
                     x_ref, xs_ref, xt_ref, srt_ref, zero_ref, sem, zsem, *, ts):
    run_refs = (run_start_ref, run_len_ref, run_local_ref)
    nsteps = pl.num_programs(0) * pl.num_programs(1)
    step = pl.program_id(0) * pl.num_programs(1) + pl.program_id(1)

    def zero_copy(start, size):
        return pltpu.make_async_copy(zero_ref.at[_token_rows(0, size)],
                                     xs_ref.at[_token_rows(start, size)], zsem)

    def zero_fill(action):
        for e in range(N_EXPERTS):
            p0 = pad_ref[e]
            _for_each_piece(pad_len_ref[e], PAD_BITS,
                            lambda off, size, p0=p0: action(zero_copy(p0 + off, size)))

        def tail_tile(r, c):
            action(zero_copy(r * MOE_TM, MOE_TM))
            return c

        lax.fori_loop(nu_ref[0], xs_ref.shape[0] // (MOE_TM * ROW_TILE), tail_tile, 0)

    @pl.when(step == 0)
    def _():
        zero_ref[...] = jnp.zeros_like(zero_ref)
        zero_fill(lambda cp: cp.start())

    @pl.when(step == nsteps - 1)
    def _():
        zero_fill(lambda cp: cp.wait())

    slot = step % 2
    _to_token_tiles(xt_ref, x_ref[...])
    base = step * ts

    def place(i, c):
        tile = xt_ref[_token_rows(i, 1), :]
        srt_ref[slot, _token_rows(lp1_ref[base + i], 1), :] = tile
        srt_ref[slot, _token_rows(lp2_ref[base + i], 1), :] = tile
        return c

    lax.fori_loop(0, ts, place, 0, unroll=8)

    def run_copy(local_start, sorted_start, size):
        return pltpu.make_async_copy(srt_ref.at[slot, _token_rows(local_start, size)],
                                     xs_ref.at[_token_rows(sorted_start, size)], sem.at[slot])

    _run_copies(step, run_refs, run_copy)

    def drain(sl):
        pltpu.make_async_copy(srt_ref.at[sl], xs_ref.at[_token_rows(0, TOP_K * ts)], sem.at[sl]).wait()

    @pl.when(step > 0)
    def _():
        drain(1 - slot)

    @pl.when(step == nsteps - 1)
    def _():
        drain(slot)


def moe_dispatch(h1, plan, pads, n_used, bsz, ts=MOE_TS):
    s = h1.shape[0]
    d = D_MODEL
    kern = functools.partial(_dispatch_kernel, ts=ts)
    grid_spec = pltpu.PrefetchScalarGridSpec(
        num_scalar_prefetch=len(plan) + len(pads) + 1,
        grid=(bsz, s // ts),
        in_specs=[pl.BlockSpec((ts, d), lambda bi, i, *_: (i, bi))],
        out_specs=pl.BlockSpec(memory_space=pl.ANY),
        scratch_shapes=[pltpu.VMEM((ts * ROW_TILE, LANES), F32),
                        pltpu.VMEM((2, TOP_K * ts * ROW_TILE, LANES), F32),
                        pltpu.VMEM((MOE_TM * ROW_TILE, LANES), F32),
                        pltpu.SemaphoreType.DMA((2,)), pltpu.SemaphoreType.DMA(())],
    )
    return pl.pallas_call(
        kern,
        grid_spec=grid_spec,
        out_shape=jax.ShapeDtypeStruct((moe_rows(bsz * s) * ROW_TILE, LANES), F32),
        compiler_params=_cparams(("arbitrary", "arbitrary")),
        name="moe_dispatch",
    )(*plan, *pads, n_used, h1)


def _experts_kernel(te_ref, nu_ref, xs_ref, wg_ref, wu_ref, wd_ref, ys_ref, wgb_ref, wub_ref, wdb_ref):
    r = pl.program_id(0)

    @pl.when(r < nu_ref[0])
    def _():
        changed = (r == 0) | (te_ref[r] != te_ref[jnp.maximum(r - 1, 0)])

        @pl.when(changed)
        def _():
            wgb_ref[...] = wg_ref[...].astype(BF16)
            wub_ref[...] = wu_ref[...].astype(BF16)
            wdb_ref[...] = wd_ref[...].astype(BF16)

        x = _from_token_tiles(xs_ref, MOE_TM).astype(BF16)
        gate = jnp.dot(x, wgb_ref[...], preferred_element_type=F32)
        up = jnp.dot(x, wub_ref[...], preferred_element_type=F32)
        act = (gate * jax.nn.sigmoid(gate) * up).astype(BF16)
        _to_token_tiles(ys_ref, jnp.dot(act, wdb_ref[...], preferred_element_type=F32))

    @pl.when(r >= nu_ref[0])
    def _():
        ys_ref[...] = jnp.zeros_like(ys_ref)


def moe_experts(xs, tile_expert, n_used, wg, wu, wd, layer):
    d, f = wg.shape[-2:]
    rows = xs.shape[0] // ROW_TILE
    blk = MOE_TM * ROW_TILE

    def tile(r, te, nu):
        return jnp.minimum(r, nu[0] - 1)

    def wspec(a, b):
        return pl.BlockSpec((None, None, a, b), lambda r, te, nu: (layer, te[tile(r, te, nu)], 0, 0))

    grid_spec = pltpu.PrefetchScalarGridSpec(
        num_scalar_prefetch=2,
        grid=(rows // MOE_TM,),
        in_specs=[pl.BlockSpec((blk, LANES), lambda r, te, nu: (tile(r, te, nu), 0)),
                  wspec(d, f), wspec(d, f), wspec(f, d)],
        out_specs=pl.BlockSpec((blk, LANES), lambda r, te, nu: (r, 0)),
        scratch_shapes=[pltpu.VMEM((d, f), BF16), pltpu.VMEM((d, f), BF16), pltpu.VMEM((f, d), BF16)],
    )
    return pl.pallas_call(
        _experts_kernel,
        grid_spec=grid_spec,
        out_shape=jax.ShapeDtypeStruct(xs.shape, F32),
        compiler_params=_cparams(("arbitrary",)),
        name="moe_experts",
    )(tile_expert, n_used, xs, wg, wu, wd)


def _combine_kernel(lp1_ref, lp2_ref, run_start_ref, run_len_ref, run_local_ref,
                    ys_ref, route_ref, h_ref, g_ref, b_ref, *refs, ts):
    o_ref, maybe_ob_ref = refs[0], refs[1:-4]
    srt_ref, t1_ref, t2_ref, sem = refs[-4:]
    run_refs = (run_start_ref, run_len_ref, run_local_ref)
    nsteps = pl.num_programs(0) * pl.num_programs(1)
    step = pl.program_id(0) * pl.num_programs(1) + pl.program_id(1)

    def issue(blk, sl):
        def run_copy(local_start, sorted_start, size):
            return pltpu.make_async_copy(ys_ref.at[_token_rows(sorted_start, size)],
                                         srt_ref.at[sl, _token_rows(local_start, size)], sem.at[sl])
        _run_copies(blk, run_refs, run_copy)

    @pl.when(step == 0)
    def _():
        issue(0, 0)

    @pl.when(step + 1 < nsteps)
    def _():
        issue(step + 1, (step + 1) % 2)

    slot = step % 2
    pltpu.make_async_copy(ys_ref.at[_token_rows(0, TOP_K * ts)], srt_ref.at[slot], sem.at[slot]).wait()

    base = step * ts

    def pick(i, c):
        t1_ref[_token_rows(i, 1), :] = srt_ref[slot, _token_rows(lp1_ref[base + i], 1), :]
        t2_ref[_token_rows(i, 1), :] = srt_ref[slot, _token_rows(lp2_ref[base + i], 1), :]
        return c

    lax.fori_loop(0, ts, pick, 0, unroll=8)

    route = route_ref[...]
    w1 = route[:, 2:3]
    w2 = route[:, 3:4]
    ffn = w1 * _from_token_tiles(t1_ref, ts) + w2 * _from_token_tiles(t2_ref, ts)
    h2 = _layer_norm(DN_ALPHA * h_ref[...] + ffn, g_ref[...], b_ref[...])
    o_ref[...] = h2
    for ob_ref in maybe_ob_ref:
        ob_ref[...] = h2.astype(BF16)


def moe_combine(ys, plan, route, h1, g, b, bsz, last, ts=MOE_TS):
    s = h1.shape[0]
    d = D_MODEL
    g2, b2 = g.reshape(1, d), b.reshape(1, d)
    if last:
        o_specs = [pl.BlockSpec((None, ts, d), lambda bi, i, *_: (bi, i, 0))]
        o_shapes = [jax.ShapeDtypeStruct((bsz, s, d), F32)]
    else:
        o_specs = [pl.BlockSpec((ts, d), lambda bi, i, *_: (i, bi))] * 2
        o_shapes = [jax.ShapeDtypeStruct((s, bsz * d), F32), jax.ShapeDtypeStruct((s, bsz * d), BF16)]
    kern = functools.partial(_combine_kernel, ts=ts)
    grid_spec = pltpu.PrefetchScalarGridSpec(
        num_scalar_prefetch=len(plan),
        grid=(bsz, s // ts),
        in_specs=[pl.BlockSpec(memory_space=pl.ANY),
                  pl.BlockSpec((ts, ROUTE_LANES), lambda bi, i, *_: (i, bi)),
                  pl.BlockSpec((ts, d), lambda bi, i, *_: (i, bi)),
                  pl.BlockSpec((1, d), lambda bi, i, *_: (0, 0)),
                  pl.BlockSpec((1, d), lambda bi, i, *_: (0, 0))],
        out_specs=o_specs,
        scratch_shapes=[pltpu.VMEM((2, TOP_K * ts * ROW_TILE, LANES), F32),
                        pltpu.VMEM((ts * ROW_TILE, LANES), F32), pltpu.VMEM((ts * ROW_TILE, LANES), F32),
                        pltpu.SemaphoreType.DMA((2,))],
    )
    return pl.pallas_call(
        kern,
        grid_spec=grid_spec,
        out_shape=o_shapes,
        compiler_params=_cparams(("arbitrary", "arbitrary")),
        name="moe_combine",
    )(*plan, ys, route, h1, g2, b2)


def kernel(x, ln_in_g, ln_in_b, w_in, s5_lambda_re, s5_lambda_im, s5_log_step, s5_b_re, s5_b_im,
           s5_c_re, s5_c_im, s5_d, w_glu, w_branch_ret, w_branch_s5, w_out, ln_mix_g, ln_mix_b,
           w_router_group, b_router_group, w_router_expert, b_router_expert, w_exp_gate, w_exp_up,
           w_exp_down, ln_ffn_g, ln_ffn_b):
    bsz, s, d = x.shape
    depth = w_in.shape[0]
    tabs = retention_tables(s)
    out = None
    for l in range(depth):
        if l == 0:
            proj, h_sb = ln_in_proj(x, ln_in_g, ln_in_b, w_in, l)
        else:
            proj = in_proj(hb_sb, w_in, l, bsz)
        ro = retention(proj, tabs, bsz, s)
        a_re, a_im, bb_re, bb_im = s5_params(s5_lambda_re[l], s5_lambda_im[l], s5_log_step[l],
                                             s5_b_re[l], s5_b_im[l])
        mats = s5_matrices(a_re, a_im, bb_re, bb_im, s5_c_re[l], s5_c_im[l])
        y = s5_scan(proj, mats, s5_d[l], bsz, s)
        pad = ROUTE_LANES - MOE_GROUPS - N_EXPERTS
        wr = jnp.concatenate([w_router_group[l], w_router_expert[l], jnp.zeros((d, pad), F32)], axis=1)
        wr_hi = wr.astype(BF16)
        wr = jnp.concatenate([wr_hi, (wr - wr_hi.astype(F32)).astype(BF16)], axis=1)
        br = jnp.concatenate([b_router_group[l], b_router_expert[l], jnp.zeros((pad,), F32)]).reshape(1, -1)
        h1, route, cnt, blk, lp = mix_out(ro, y, proj, h_sb, w_glu, w_branch_ret, w_branch_s5, w_out, l,
                                          ln_mix_g[l], ln_mix_b[l], wr, br, bsz)
        last = l == depth - 1
        plan, pads, tile_expert, n_used = moe_plan(lp, cnt, blk, bsz, s)
        xs = moe_dispatch(h1, plan, pads, n_used, bsz)
        ys = moe_experts(xs, tile_expert, n_used, w_exp_gate, w_exp_up, w_exp_down, l)
        outs = moe_combine(ys, plan, route, h1, ln_ffn_g[l], ln_ffn_b[l], bsz, last)
        if last:
            out = outs[0]
        else:
            h_sb, hb_sb = outs
    return out
```

```python
import functools
import math

import jax
import jax.numpy as jnp
import numpy as np
from jax import lax
from jax.experimental import pallas as pl
from jax.experimental.pallas import tpu as pltpu

D_MODEL = 1024
CHUNK = 64
RET_HEADS = 8
RET_QK = 512
RET_V = 1024
RET_QK_DIM = 64
RET_V_DIM = 128
ROPE_BASE = 10000.0
S5_WIDTH = 512
S5_GROUP_CH = 16
S5_STATE = 64
MOE_GROUPS = 4
EXPERTS_PER_GROUP = 8
N_EXPERTS = 32
LN_EPS = 1e-5
HEAD_NORM_EPS = 1e-6
DEPTH = 2
DN_ALPHA = (2 * DEPTH) ** 0.25
IN_WIDTH = 2 * RET_QK + 2 * RET_V + S5_WIDTH + 2 * D_MODEL
PROJ_BLK = 512
N_PROJ_BLK = IN_WIDTH // PROJ_BLK
INPROJ_BATCHES = 2

RET_SUPER = 512
S5_TT = 128
S5_COLS = 128
S5_NSLICE = S5_WIDTH // S5_COLS
S5_SLICE_STATE = (S5_COLS // S5_GROUP_CH) * S5_STATE
ROUTE_LANES = 128
TOP_K = 2
MOE_TM = 512
MOE_TS = 512
RUN_BITS = MOE_TS.bit_length()
PAD_BITS = (MOE_TM - 1).bit_length()
LANES = 128
SUBLANES = 8
ROW_TILE = D_MODEL // LANES
VMEM_LIMIT = 56 * 1024 * 1024

F32 = jnp.float32
BF16 = jnp.bfloat16


def _cparams(sem):
    return pltpu.CompilerParams(dimension_semantics=sem, vmem_limit_bytes=VMEM_LIMIT)


def _layer_norm(x, g, b):
    mu = jnp.mean(x, axis=-1, keepdims=True)
    xc = x - mu
    var = jnp.mean(xc * xc, axis=-1, keepdims=True)
    return xc * lax.rsqrt(var + LN_EPS) * g + b


def _inproj_kernel(h_ref, w_ref, o_ref):
    w = w_ref[...].astype(BF16)
    d = w.shape[0]
    for k in range(INPROJ_BATCHES):
        hk = h_ref[:, k * d:(k + 1) * d]
        o_ref[:, k * PROJ_BLK:(k + 1) * PROJ_BLK] = jnp.dot(hk, w, preferred_element_type=F32).astype(BF16)


def _ln_inproj_kernel(x_ref, g_ref, b_ref, w_ref, o_ref, h_ref, hb_ref):
    @pl.when(pl.program_id(2) == 0)
    def _():
        h = _layer_norm(x_ref[...], g_ref[...], b_ref[...])
        h_ref[...] = h
        hb_ref[...] = h.astype(BF16)

    w = w_ref[...].astype(BF16)
    o_ref[...] = jnp.dot(hb_ref[...], w, preferred_element_type=F32).astype(BF16)


def ln_in_proj(x, g, b, w, layer, ts=2048):
    bsz, s, d = x.shape
    n = w.shape[-1]
    nb = n // PROJ_BLK
    return pl.pallas_call(
        _ln_inproj_kernel,
        grid=(bsz, s // ts, nb),
        in_specs=[pl.BlockSpec((None, ts, d), lambda bi, i, j: (bi, i, 0)),
                  pl.BlockSpec((1, d), lambda bi, i, j: (0, 0)),
                  pl.BlockSpec((1, d), lambda bi, i, j: (0, 0)),
                  pl.BlockSpec((None, d, PROJ_BLK), lambda bi, i, j: (layer, 0, j))],
        out_specs=[pl.BlockSpec((ts, PROJ_BLK), lambda bi, i, j: (i, j * bsz + bi)),
                   pl.BlockSpec((ts, d), lambda bi, i, j: (i, bi))],
        out_shape=[jax.ShapeDtypeStruct((s, bsz * n), BF16),
                   jax.ShapeDtypeStruct((s, bsz * d), F32)],
        scratch_shapes=[pltpu.VMEM((ts, d), BF16)],
        compiler_params=_cparams(("parallel", "parallel", "arbitrary")),
        name="ln_in_proj",
    )(x, g.reshape(1, d), b.reshape(1, d), w)


def in_proj(hb_sb, w, layer, bsz, ts=2048):
    s = hb_sb.shape[0]
    d, n = w.shape[-2:]
    nb = n // PROJ_BLK
    kb = INPROJ_BATCHES
    return pl.pallas_call(
        _inproj_kernel,
        grid=(bsz // kb, s // ts, nb),
        in_specs=[pl.BlockSpec((ts, kb * d), lambda b, i, j: (i, b)),
                  pl.BlockSpec((None, d, PROJ_BLK), lambda b, i, j: (layer, 0, j))],
        out_specs=pl.BlockSpec((ts, kb * PROJ_BLK), lambda b, i, j: (i, (j * bsz) // kb + b)),
        out_shape=jax.ShapeDtypeStruct((s, bsz * n), BF16),
        compiler_params=_cparams(("parallel", "parallel", "arbitrary")),
        name="in_proj",
    )(hb_sb, w)


def _swap_halves(x):
    lane = lax.broadcasted_iota(jnp.int32, x.shape, 1)
    first = (lane % RET_QK_DIM) < (RET_QK_DIM // 2)
    n = x.shape[1]
    return jnp.where(first, pltpu.roll(x, n - RET_QK_DIM // 2, 1), pltpu.roll(x, RET_QK_DIM // 2, 1))


def _retention_kernel(q_ref, k_ref, v0_ref, v1_ref, g0_ref, g1_ref, cos_ref, sin_ref,
                      qd_ref, kd_ref, mask_ref, cd_ref, o_ref, state_ref):
    @pl.when(pl.program_id(1) == 0)
    def _():
        state_ref[...] = jnp.zeros_like(state_ref)

    cos = cos_ref[...]
    sin = sin_ref[...]
    q = q_ref[...].astype(F32)
    k = k_ref[...].astype(F32)
    q = q * cos + _swap_halves(q) * sin
    k = k * cos + _swap_halves(k) * sin
    qb = q.astype(BF16)
    kb = k.astype(BF16)
    qdb = (q * qd_ref[...]).astype(BF16)
    kdb = (k * kd_ref[...]).astype(BF16)
    for hd in range(RET_HEADS):
        qs = slice(hd * RET_QK_DIM, (hd + 1) * RET_QK_DIM)
        half, off = divmod(hd * RET_V_DIM, PROJ_BLK)
        vs = slice(off, off + RET_V_DIM)
        vh = (v0_ref, v1_ref)[half][:, vs]
        gh = (g0_ref, g1_ref)[half][:, vs].astype(F32)
        sc = lax.dot_general(qb[:, qs], kb[:, qs], (((1,), (1,)), ((), ())),
                             preferred_element_type=F32)
        sc = (sc * mask_ref[hd]).astype(BF16)
        st = state_ref[hd]
        o = jnp.dot(sc, vh, preferred_element_type=F32)
        o = o + jnp.dot(qdb[:, qs], st.astype(BF16), preferred_element_type=F32)
        kv = lax.dot_general(kdb[:, qs], vh, (((0,), (0,)), ((), ())),
                             preferred_element_type=F32)
        state_ref[hd] = st * cd_ref[hd] + kv
        mu = jnp.mean(o, axis=-1, keepdims=True)
        oc = o - mu
        var = jnp.mean(oc * oc, axis=-1, keepdims=True)
        on = oc * lax.rsqrt(var + HEAD_NORM_EPS)
        o_ref[:, hd * RET_V_DIM:(hd + 1) * RET_V_DIM] = (gh * jax.nn.sigmoid(gh) * on).astype(BF16)


def retention(proj_sb, tabs, bsz, s):
    cos_t, sin_t, qd_t, kd_t, mask, cd = tabs
    L = RET_SUPER

    def pspec(col):
        return pl.BlockSpec((L, PROJ_BLK), lambda b, i, col=col: (i, col * bsz + b))

    full2 = pl.BlockSpec((L, RET_QK), lambda b, i: (0, 0))
    return pl.pallas_call(
        _retention_kernel,
        grid=(bsz, s // L),
        in_specs=[pspec(0), pspec(1), pspec(2), pspec(3), pspec(4), pspec(5),
                  pl.BlockSpec((L, RET_QK), lambda b, i: (i, 0)),
                  pl.BlockSpec((L, RET_QK), lambda b, i: (i, 0)),
                  full2, full2,
                  pl.BlockSpec((RET_HEADS, L, L), lambda b, i: (0, 0, 0)),
                  pl.BlockSpec((RET_HEADS, 1, RET_V_DIM), lambda b, i: (0, 0, 0))],
        out_specs=pl.BlockSpec((L, RET_V), lambda b, i: (i, b)),
        out_shape=jax.ShapeDtypeStruct((s, bsz * RET_V), BF16),
        scratch_shapes=[pltpu.VMEM((RET_HEADS, RET_QK_DIM, RET_V_DIM), F32)],
        compiler_params=_cparams(("parallel", "arbitrary")),
        name="retention",
    )(proj_sb, proj_sb, proj_sb, proj_sb, proj_sb, proj_sb, cos_t, sin_t, qd_t, kd_t, mask, cd)


def retention_tables(s):
    L = RET_SUPER
    half = RET_QK_DIM // 2
    inv_freq = ROPE_BASE ** (-np.arange(half, dtype=np.float64) / half)
    ang = np.arange(s, dtype=np.float64)[:, None] * inv_freq[None, :]
    cos, sin = np.cos(ang), np.sin(ang)
    cos_t = np.tile(np.concatenate([cos, cos], -1), (1, RET_HEADS))
    sin_t = np.tile(np.concatenate([-sin, sin], -1), (1, RET_HEADS))
    log_gamma = np.log1p(-(2.0 ** (-5.0 - np.arange(RET_HEADS, dtype=np.float64))))
    pos = np.arange(L, dtype=np.float64)
    qd = np.exp(log_gamma[None, :] * (pos + 1.0)[:, None])
    k_scale = RET_QK_DIM ** -0.5
    kd = np.exp(log_gamma[None, :] * (L - 1.0 - pos)[:, None]) * k_scale
    qd_t = np.repeat(qd, RET_QK_DIM, axis=1)
    kd_t = np.repeat(kd, RET_QK_DIM, axis=1)
    chunk_id = np.arange(L) // CHUNK
    visible = (chunk_id[None, :] <= chunk_id[:, None]).astype(np.float64)
    mask = np.exp(log_gamma[:, None, None] * np.abs(pos[:, None] - pos[None, :])) * visible[None] * k_scale
    cd = np.broadcast_to(np.exp(log_gamma * L)[:, None, None], (RET_HEADS, 1, RET_V_DIM))
    return tuple(jnp.asarray(a, dtype=F32) for a in (cos_t, sin_t, qd_t, kd_t, mask, cd))


def _s5_param_kernel(lre_ref, lim_ref, ls_ref, bre_ref, bim_ref, are_ref, aim_ref, bbre_ref, bbim_ref):
    lam_re = jnp.minimum(lre_ref[...], -1e-4)
    lam_im = lim_ref[...]
    step = jnp.exp(ls_ref[...])
    mag = jnp.exp(lam_re * step)
    ang = lam_im * step
    ab_re = mag * jnp.cos(ang)
    ab_im = mag * jnp.sin(ang)
    den = lam_re * lam_re + lam_im * lam_im
    n_re = ab_re - 1.0
    zc_re = (n_re * lam_re + ab_im * lam_im) / den
    zc_im = (ab_im * lam_re - n_re * lam_im) / den
    are_ref[...] = ab_re
    aim_ref[...] = ab_im
    b_re = bre_ref[...]
    b_im = bim_ref[...]
    bbre_ref[...] = zc_re * b_re - zc_im * b_im
    bbim_ref[...] = zc_re * b_im + zc_im * b_re


def s5_params(lam_re, lam_im, log_step, b_re, b_im):
    g, n = lam_re.shape
    c = b_re.shape[-1]
    outs = pl.pallas_call(
        _s5_param_kernel,
        out_shape=[jax.ShapeDtypeStruct((g, 1, n), F32), jax.ShapeDtypeStruct((g, 1, n), F32),
                   jax.ShapeDtypeStruct((g, c, n), F32), jax.ShapeDtypeStruct((g, c, n), F32)],
        name="s5_params",
    )(lam_re.reshape(g, 1, n), lam_im.reshape(g, 1, n), log_step.reshape(g, 1, 1),
      jnp.swapaxes(b_re, 1, 2), jnp.swapaxes(b_im, 1, 2))
    return outs


def _block_diag(x):
    ns, gl, r, c = x.shape
    eye = jnp.eye(gl, dtype=x.dtype)
    return jnp.einsum('sgrc,gh->sgrhc', x, eye).reshape(ns, gl * r, gl * c)


def s5_matrices(a_re, a_im, bb_re, bb_im, c_re, c_im):
    gl = S5_COLS // S5_GROUP_CH
    ns = S5_NSLICE
    bre = _block_diag(bb_re.reshape(ns, gl, S5_GROUP_CH, S5_STATE))
    bim = _block_diag(bb_im.reshape(ns, gl, S5_GROUP_CH, S5_STATE))
    bq = jnp.concatenate([bre, bim], axis=-1).astype(BF16)
    cre = _block_diag(jnp.swapaxes(c_re, 1, 2).reshape(ns, gl, S5_STATE, S5_GROUP_CH))
    cim = _block_diag(jnp.swapaxes(c_im, 1, 2).reshape(ns, gl, S5_STATE, S5_GROUP_CH))
    cq = jnp.concatenate([cre, -cim], axis=1).astype(BF16)
    are = a_re.reshape(ns, 1, S5_SLICE_STATE)
    aim = a_im.reshape(ns, 1, S5_SLICE_STATE)
    return bq, cq, are, aim


def _s5_kernel(*refs, bsz, tt):
    u_refs = refs[:bsz]
    bq_ref, cq_ref, are_ref, aim_ref, d_ref, y_ref, us_ref, ys_ref, bu_ref, st_ref = refs[bsz:]

    @pl.when(pl.program_id(0) == 0)
    def _():
        st_ref[...] = jnp.zeros_like(st_ref)

    for b in range(bsz):
        ub = u_refs[b][...].astype(F32)
        for cs in range(S5_NSLICE):
            us_ref[cs, pl.ds(b, tt, stride=bsz), :] = ub[:, cs * S5_COLS:(cs + 1) * S5_COLS]

    ns2 = S5_SLICE_STATE
    for cs in range(S5_NSLICE):
        cols = slice(cs * S5_COLS, (cs + 1) * S5_COLS)
        uf = us_ref[cs]
        bu_ref[cs] = jnp.dot(uf.astype(BF16), bq_ref[cs], preferred_element_type=F32)
        a_re = jnp.broadcast_to(are_ref[cs], (bsz, ns2))
        a_im = jnp.broadcast_to(aim_ref[cs], (bsz, ns2))

        def step(t, carry):
            h_re, h_im = carry
            rows = pl.ds(pl.multiple_of(t * bsz, bsz), bsz)
            n_re = a_re * h_re - a_im * h_im + bu_ref[cs, rows, 0:ns2]
            n_im = a_re * h_im + a_im * h_re + bu_ref[cs, rows, ns2:2 * ns2]
            bu_ref[cs, rows, 0:ns2] = n_re
            bu_ref[cs, rows, ns2:2 * ns2] = n_im
            return n_re, n_im

        h_re, h_im = lax.fori_loop(0, tt, step, (st_ref[cs, 0], st_ref[cs, 1]), unroll=True)
        st_ref[cs, 0] = h_re
        st_ref[cs, 1] = h_im
        y = jnp.dot(bu_ref[cs].astype(BF16), cq_ref[cs], preferred_element_type=F32)
        ys_ref[cs] = y + d_ref[:, cols] * uf

    for b in range(bsz):
        for cs in range(S5_NSLICE):
            lo = b * S5_WIDTH + cs * S5_COLS
            y_ref[:, lo:lo + S5_COLS] = ys_ref[cs, pl.ds(b, tt, stride=bsz), :]


def s5_scan(proj_sb, mats, d_skip, bsz, s):
    bq, cq, are, aim = mats
    tt = S5_TT
    rows = tt * bsz
    kern = functools.partial(_s5_kernel, bsz=bsz, tt=tt)
    u_specs = [pl.BlockSpec((tt, PROJ_BLK), lambda i, b=b: (i, 6 * bsz + b)) for b in range(bsz)]
    return pl.pallas_call(
        kern,
        grid=(s // tt,),
        in_specs=u_specs + [
                  pl.BlockSpec(bq.shape, lambda i: (0, 0, 0)),
                  pl.BlockSpec(cq.shape, lambda i: (0, 0, 0)),
                  pl.BlockSpec(are.shape, lambda i: (0, 0, 0)),
                  pl.BlockSpec(aim.shape, lambda i: (0, 0, 0)),
                  pl.BlockSpec((1, S5_WIDTH), lambda i: (0, 0))],
        out_specs=pl.BlockSpec((tt, bsz * S5_WIDTH), lambda i: (i, 0)),
        out_shape=jax.ShapeDtypeStruct((s, bsz * S5_WIDTH), F32),
        scratch_shapes=[pltpu.VMEM((S5_NSLICE, rows, S5_COLS), F32),
                        pltpu.VMEM((S5_NSLICE, rows, S5_COLS), F32),
                        pltpu.VMEM((S5_NSLICE, rows, 2 * S5_SLICE_STATE), F32),
                        pltpu.VMEM((S5_NSLICE, 2, bsz, S5_SLICE_STATE), F32)],
        compiler_params=_cparams(("arbitrary",)),
        name="s5_scan",
    )(*([proj_sb] * bsz), bq, cq, are, aim, d_skip.reshape(1, S5_WIDTH))


def _gelu_tanh(x):
    c = math.sqrt(2.0 / math.pi)
    return 0.5 * x * (1.0 + jnp.tanh(c * (x + 0.044715 * (x * x * x))))


def _route(logits):
    lane = lax.broadcasted_iota(jnp.int32, logits.shape, 1)
    neg = jnp.float32(-jnp.inf)
    big = jnp.int32(1 << 20)
    is_g = lane < MOE_GROUPS
    lg = jnp.where(is_g, logits, neg)
    mg = jnp.max(lg, axis=-1, keepdims=True)
    sg = jnp.sum(jnp.where(is_g, jnp.exp(lg - mg), 0.0), axis=-1, keepdims=True)
    g_top = 1.0 / sg
    g_idx = jnp.min(jnp.where(lg == mg, lane, big), axis=-1, keepdims=True)
    lo = MOE_GROUPS + g_idx * EXPERTS_PER_GROUP
    in_grp = (lane >= lo) & (lane < lo + EXPERTS_PER_GROUP)
    le = jnp.where(in_grp, logits, neg)
    m1 = jnp.max(le, axis=-1, keepdims=True)
    se = jnp.sum(jnp.where(in_grp, jnp.exp(le - m1), 0.0), axis=-1, keepdims=True)
    i1 = jnp.min(jnp.where(le == m1, lane, big), axis=-1, keepdims=True)
    le2 = jnp.where(lane == i1, neg, le)
    m2 = jnp.max(le2, axis=-1, keepdims=True)
    i2 = jnp.min(jnp.where(le2 == m2, lane, big), axis=-1, keepdims=True)
    p1 = 1.0 / se
    p2 = jnp.exp(m2 - m1) / se
    tot = p1 + p2
    w1 = g_top * (p1 / tot)
    w2 = g_top * (p2 / tot)
    return lane, i1, i2, w1, w2


def _mix_kernel(ro_ref, y_ref, gr0_ref, gr1_ref, gs0_ref, gs1_ref, h_ref,
                wglu_ref, wbr_ref, wbs_ref, wout_ref, g_ref, b_ref, wr_ref, br_ref, tri_ref,
                h1_ref, route_ref, cnt_ref, blk_ref, lp_ref,
                run_ref, wglu_b, wbr_b, wbs_b, wout_b):
    @pl.when((pl.program_id(0) == 0) & (pl.program_id(1) == 0))
    def _():
        run_ref[...] = jnp.zeros_like(run_ref)
        wglu_b[...] = wglu_ref[...].astype(BF16)
        wbr_b[...] = wbr_ref[...].astype(BF16)
        wbs_b[...] = wbs_ref[...].astype(BF16)
        wout_b[...] = wout_ref[...].astype(BF16)

    z = _gelu_tanh(y_ref[...])
    zg = jnp.dot(z.astype(BF16), wglu_b[...], preferred_element_type=F32)
    zz = (z * jax.nn.sigmoid(zg)).astype(BF16)
    s5b = jnp.dot(zz, wbs_b[...], preferred_element_type=F32)
    rb = jnp.dot(ro_ref[...], wbr_b[...], preferred_element_type=F32)
    gr = jnp.concatenate([gr0_ref[...], gr1_ref[...]], axis=-1).astype(F32)
    gs = jnp.concatenate([gs0_ref[...], gs1_ref[...]], axis=-1).astype(F32)
    merged = jax.nn.sigmoid(gr) * rb + jax.nn.sigmoid(gs) * s5b
    mix = jnp.dot(merged.astype(BF16), wout_b[...], preferred_element_type=F32)
    h1 = _layer_norm(DN_ALPHA * h_ref[...] + mix, g_ref[...], b_ref[...])
    h1_ref[...] = h1
    h_hi = h1.astype(BF16)
    h_lo = (h1 - h_hi.astype(F32)).astype(BF16)
    both = jnp.dot(h_hi, wr_ref[...], preferred_element_type=F32)
    logits = (both[:, :ROUTE_LANES] + both[:, ROUTE_LANES:]
              + jnp.dot(h_lo, wr_ref[:, :ROUTE_LANES], preferred_element_type=F32)) + br_ref[...]
    lane, i1, i2, w1, w2 = _route(logits)
    oh1 = lane == i1
    oh2 = lane == i2
    oh = jnp.where(oh1 | oh2, 1.0, 0.0)
    rank_in_blk = jnp.dot(tri_ref[...], oh.astype(BF16), preferred_element_type=F32)
    blk_cnt = jnp.broadcast_to(jnp.sum(oh, axis=0, keepdims=True), run_ref.shape)
    lane8 = lax.broadcasted_iota(jnp.int32, run_ref.shape, 1)
    incl = blk_cnt
    shift = 1
    while shift < ROUTE_LANES:
        incl = incl + jnp.where(lane8 >= shift, pltpu.roll(incl, shift, 1), 0.0)
        shift *= 2
    lstart = incl - blk_cnt
    pos = lstart[0:1, :] + rank_in_blk
    l1 = jnp.sum(jnp.where(oh1, pos, 0.0), axis=-1, keepdims=True)
    l2 = jnp.sum(jnp.where(oh2, pos, 0.0), axis=-1, keepdims=True)
    run_before = run_ref[...]
    row8 = lax.broadcasted_iota(jnp.int32, run_ref.shape, 0)
    blk_ref[...] = jnp.where(row8 == 0, run_before,
                             jnp.where(row8 == 1, blk_cnt, jnp.where(row8 == 2, lstart, 0.0)))
    cnt = run_before + blk_cnt
    run_ref[...] = cnt
    cnt_ref[...] = cnt
    vals = (i1.astype(F32) - MOE_GROUPS, i2.astype(F32) - MOE_GROUPS, w1, w2)
    route = jnp.zeros(logits.shape, F32)
    for k, v in enumerate(vals):
        route = jnp.where(lane == k, v, route)
    route_ref[...] = route
    pos_t = jnp.transpose(jnp.where(lane == 0, l1, jnp.where(lane == 1, l2, 0.0)))
    lp_ref[...] = (pos_t[0:SUBLANES, :] * ROW_TILE).astype(jnp.int32)


def mix_out(ro, y, proj, h, wglu, wbr, wbs, wout, layer, g, b, wr, br, bsz, ts=MOE_TS):
    s = h.shape[0]
    d = D_MODEL

    def const(a):
        return pl.BlockSpec(a.shape, lambda bi, i: (0,) * a.ndim)

    def stacked(w):
        return pl.BlockSpec((None,) + w.shape[1:], lambda bi, i: (layer, 0, 0))

    def pspec(col):
        return pl.BlockSpec((ts, PROJ_BLK), lambda bi, i, col=col: (i, col * bsz + bi))

    def tok(w):
        return pl.BlockSpec((ts, w), lambda bi, i: (i, bi))

    g2, b2 = g.reshape(1, d), b.reshape(1, d)
    tri = jnp.asarray(np.tril(np.ones((ts, ts)), -1), dtype=BF16)
    return pl.pallas_call(
        _mix_kernel,
        grid=(bsz, s // ts),
        in_specs=[tok(RET_V), tok(S5_WIDTH), pspec(7), pspec(8), pspec(9), pspec(10), tok(d),
                  stacked(wglu), stacked(wbr), stacked(wbs), stacked(wout), const(g2), const(b2),
                  const(wr), const(br), const(tri)],
        out_specs=[tok(d), tok(ROUTE_LANES), pl.BlockSpec((SUBLANES, ROUTE_LANES), lambda bi, i: (0, 0)),
                   pl.BlockSpec((None, SUBLANES, ROUTE_LANES), lambda bi, i: (bi * (s // ts) + i, 0, 0)),
                   pl.BlockSpec((None, SUBLANES, ts), lambda bi, i: (bi * (s // ts) + i, 0, 0))],
        out_shape=[jax.ShapeDtypeStruct((s, bsz * d), F32),
                   jax.ShapeDtypeStruct((s, bsz * ROUTE_LANES), F32),
                   jax.ShapeDtypeStruct((SUBLANES, ROUTE_LANES), F32),
                   jax.ShapeDtypeStruct((bsz * (s // ts), SUBLANES, ROUTE_LANES), F32),
                   jax.ShapeDtypeStruct((bsz * (s // ts), SUBLANES, ts), jnp.int32)],
        scratch_shapes=[pltpu.VMEM((SUBLANES, ROUTE_LANES), F32)]
        + [pltpu.VMEM(w.shape[1:], BF16) for w in (wglu, wbr, wbs, wout)],
        compiler_params=_cparams(("arbitrary", "arbitrary")),
        name="mix_out",
    )(ro, y, proj, proj, proj, proj, h, wglu, wbr, wbs, wout, g2, b2, wr, br, tri)


def moe_rows(t):
    return TOP_K * t + N_EXPERTS * MOE_TM


def moe_plan(lp, cnt, blk, bsz, s):
    t = bsz * s
    experts = slice(MOE_GROUPS, MOE_GROUPS + N_EXPERTS)
    counts = cnt[0, experts].astype(jnp.int32)
    padded = ((counts + MOE_TM - 1) // MOE_TM) * MOE_TM
    ends = jnp.cumsum(padded)
    off = ends - padded
    run_start = (off[None, :] + blk[:, 0, experts].astype(jnp.int32)).reshape(-1)
    run_len = blk[:, 1, experts].astype(jnp.int32).reshape(-1)
    run_local = blk[:, 2, experts].astype(jnp.int32).reshape(-1)
    pads = (off + counts, padded - counts)
    n_tiles = moe_rows(t) // MOE_TM
    n_used = (ends[-1:] // MOE_TM).astype(jnp.int32)
    tile_start = jnp.arange(n_tiles, dtype=jnp.int32) * MOE_TM
    tile_expert = jnp.sum((ends[None, :] <= tile_start[:, None]).astype(jnp.int32), axis=1)
    tile_expert = jnp.minimum(tile_expert, N_EXPERTS - 1)
    return (lp[:, 0, :].reshape(t), lp[:, 1, :].reshape(t), run_start, run_len, run_local), pads, tile_expert, n_used


def _token_rows(start, n):
    return pl.ds(pl.multiple_of(start * ROW_TILE, ROW_TILE), n * ROW_TILE)


def _tile_at(row):
    return pl.ds(pl.multiple_of(row, ROW_TILE), ROW_TILE)


def _for_each_piece(n, nbits, fn):
    for bit in reversed(range(nbits)):
        size = 1 << bit
        done = (n >> (bit + 1)) << (bit + 1)

        @pl.when((n & size) != 0)
        def _(size=size, done=done):
            fn(done, size)


def _run_copies(blk, run_refs, make_copy):
    run_start_ref, run_len_ref, run_local_ref = run_refs
    for e in range(N_EXPERTS):
        idx = blk * N_EXPERTS + e
        g0 = run_start_ref[idx]
        l0 = run_local_ref[idx]
        _for_each_piece(run_len_ref[idx], RUN_BITS,
                        lambda off, size, g0=g0, l0=l0: make_copy(l0 + off, g0 + off, size).start())


def _to_token_tiles(ref, x):
    n = x.shape[0]
    for j in range(ROW_TILE):
        ref[pl.ds(j, n, stride=ROW_TILE), :] = x[:, j * LANES:(j + 1) * LANES]


def _from_token_tiles(ref, n):
    return jnp.concatenate([ref[pl.ds(j, n, stride=ROW_TILE), :] for j in range(ROW_TILE)], axis=-1)


def _dispatch_kernel(lp1_ref, lp2_ref, run_start_ref, run_len_ref, run_local_ref,
                     pad_ref, pad_len_ref, nu_ref,
                     x_ref, xs_ref, xt_ref, srt_ref, zero_ref, sem, zsem, *, ts):
    run_refs = (run_start_ref, run_len_ref, run_local_ref)
    nsteps = pl.num_programs(0) * pl.num_programs(1)
    step = pl.program_id(0) * pl.num_programs(1) + pl.program_id(1)

    def zero_copy(start, size):
        return pltpu.make_async_copy(zero_ref.at[_token_rows(0, size)],
                                     xs_ref.at[_token_rows(start, size)], zsem)

    def zero_fill(action):
        for e in range(N_EXPERTS):
            p0 = pad_ref[e]
            _for_each_piece(pad_len_ref[e], PAD_BITS,
                            lambda off, size, p0=p0: action(zero_copy(p0 + off, size)))

        def tail_tile(r, c):
            action(zero_copy(r * MOE_TM, MOE_TM))
            return c

        lax.fori_loop(nu_ref[0], xs_ref.shape[0] // (MOE_TM * ROW_TILE), tail_tile, 0)

    @pl.when(step == 0)
    def _():
        zero_ref[...] = jnp.zeros_like(zero_ref)
        zero_fill(lambda cp: cp.start())

    @pl.when(step == nsteps - 1)
    def _():
        zero_fill(lambda cp: cp.wait())

    slot = step % 2
    _to_token_tiles(xt_ref, x_ref[...])
    base = step * ts

    def place(i, c):
        tile = xt_ref[_token_rows(i, 1), :]
        srt_ref[slot, _tile_at(lp1_ref[base + i]), :] = tile
        srt_ref[slot, _tile_at(lp2_ref[base + i]), :] = tile
        return c

    lax.fori_loop(0, ts, place, 0, unroll=8)

    def run_copy(local_start, sorted_start, size):
        return pltpu.make_async_copy(srt_ref.at[slot, _token_rows(local_start, size)],
                                     xs_ref.at[_token_rows(sorted_start, size)], sem.at[slot])

    _run_copies(step, run_refs, run_copy)

    def drain(sl):
        pltpu.make_async_copy(srt_ref.at[sl], xs_ref.at[_token_rows(0, TOP_K * ts)], sem.at[sl]).wait()

    @pl.when(step > 0)
    def _():
        drain(1 - slot)

    @pl.when(step == nsteps - 1)
    def _():
        drain(slot)


def moe_dispatch(h1, plan, pads, n_used, bsz, ts=MOE_TS):
    s = h1.shape[0]
    d = D_MODEL
    kern = functools.partial(_dispatch_kernel, ts=ts)
    grid_spec = pltpu.PrefetchScalarGridSpec(
        num_scalar_prefetch=len(plan) + len(pads) + 1,
        grid=(bsz, s // ts),
        in_specs=[pl.BlockSpec((ts, d), lambda bi, i, *_: (i, bi))],
        out_specs=pl.BlockSpec(memory_space=pl.ANY),
        scratch_shapes=[pltpu.VMEM((ts * ROW_TILE, LANES), F32),
                        pltpu.VMEM((2, TOP_K * ts * ROW_TILE, LANES), F32),
                        pltpu.VMEM((MOE_TM * ROW_TILE, LANES), F32),
                        pltpu.SemaphoreType.DMA((2,)), pltpu.SemaphoreType.DMA(())],
    )
    return pl.pallas_call(
        kern,
        grid_spec=grid_spec,
        out_shape=jax.ShapeDtypeStruct((moe_rows(bsz * s) * ROW_TILE, LANES), F32),
        compiler_params=_cparams(("arbitrary", "arbitrary")),
        name="moe_dispatch",
    )(*plan, *pads, n_used, h1)


def _experts_kernel(te_ref, nu_ref, xs_ref, wg_ref, wu_ref, wd_ref, ys_ref, wgb_ref, wub_ref, wdb_ref):
    r = pl.program_id(0)

    @pl.when(r < nu_ref[0])
    def _():
        changed = (r == 0) | (te_ref[r] != te_ref[jnp.maximum(r - 1, 0)])

        @pl.when(changed)
        def _():
            wgb_ref[...] = wg_ref[...].astype(BF16)
            wub_ref[...] = wu_ref[...].astype(BF16)
            wdb_ref[...] = wd_ref[...].astype(BF16)

        x = _from_token_tiles(xs_ref, MOE_TM).astype(BF16)
        gate = jnp.dot(x, wgb_ref[...], preferred_element_type=F32)
        up = jnp.dot(x, wub_ref[...], preferred_element_type=F32)
        act = (gate * jax.nn.sigmoid(gate) * up).astype(BF16)
        _to_token_tiles(ys_ref, jnp.dot(act, wdb_ref[...], preferred_element_type=F32))

    @pl.when(r >= nu_ref[0])
    def _():
        ys_ref[...] = jnp.zeros_like(ys_ref)


def moe_experts(xs, tile_expert, n_used, wg, wu, wd, layer):
    d, f = wg.shape[-2:]
    rows = xs.shape[0] // ROW_TILE
    blk = MOE_TM * ROW_TILE

    def tile(r, te, nu):
        return jnp.minimum(r, nu[0] - 1)

    def wspec(a, b):
        return pl.BlockSpec((None, None, a, b), lambda r, te, nu: (layer, te[tile(r, te, nu)], 0, 0))

    grid_spec = pltpu.PrefetchScalarGridSpec(
        num_scalar_prefetch=2,
        grid=(rows // MOE_TM,),
        in_specs=[pl.BlockSpec((blk, LANES), lambda r, te, nu: (tile(r, te, nu), 0)),
                  wspec(d, f), wspec(d, f), wspec(f, d)],
        out_specs=pl.BlockSpec((blk, LANES), lambda r, te, nu: (r, 0)),
        scratch_shapes=[pltpu.VMEM((d, f), BF16), pltpu.VMEM((d, f), BF16), pltpu.VMEM((f, d), BF16)],
    )
    return pl.pallas_call(
        _experts_kernel,
        grid_spec=grid_spec,
        out_shape=jax.ShapeDtypeStruct(xs.shape, F32),
        compiler_params=_cparams(("arbitrary",)),
        name="moe_experts",
    )(tile_expert, n_used, xs, wg, wu, wd)


def _combine_kernel(lp1_ref, lp2_ref, run_start_ref, run_len_ref, run_local_ref,
                    ys_ref, route_ref, h_ref, g_ref, b_ref, *refs, ts):
    o_ref, maybe_ob_ref = refs[0], refs[1:-4]
    srt_ref, t1_ref, t2_ref, sem = refs[-4:]
    run_refs = (run_start_ref, run_len_ref, run_local_ref)
    nsteps = pl.num_programs(0) * pl.num_programs(1)
    step = pl.program_id(0) * pl.num_programs(1) + pl.program_id(1)

    def issue(blk, sl):
        def run_copy(local_start, sorted_start, size):
            return pltpu.make_async_copy(ys_ref.at[_token_rows(sorted_start, size)],
                                         srt_ref.at[sl, _token_rows(local_start, size)], sem.at[sl])
        _run_copies(blk, run_refs, run_copy)

    @pl.when(step == 0)
    def _():
        issue(0, 0)

    @pl.when(step + 1 < nsteps)
    def _():
        issue(step + 1, (step + 1) % 2)

    slot = step % 2
    pltpu.make_async_copy(ys_ref.at[_token_rows(0, TOP_K * ts)], srt_ref.at[slot], sem.at[slot]).wait()

    base = step * ts

    def pick(i, c):
        t1_ref[_token_rows(i, 1), :] = srt_ref[slot, _tile_at(lp1_ref[base + i]), :]
        t2_ref[_token_rows(i, 1), :] = srt_ref[slot, _tile_at(lp2_ref[base + i]), :]
        return c

    lax.fori_loop(0, ts, pick, 0, unroll=8)

    route = route_ref[...]
    w1 = route[:, 2:3]
    w2 = route[:, 3:4]
    ffn = w1 * _from_token_tiles(t1_ref, ts) + w2 * _from_token_tiles(t2_ref, ts)
    h2 = _layer_norm(DN_ALPHA * h_ref[...] + ffn, g_ref[...], b_ref[...])
    o_ref[...] = h2
    for ob_ref in maybe_ob_ref:
        ob_ref[...] = h2.astype(BF16)


def moe_combine(ys, plan, route, h1, g, b, bsz, last, ts=MOE_TS):
    s = h1.shape[0]
    d = D_MODEL
    g2, b2 = g.reshape(1, d), b.reshape(1, d)
    if last:
        o_specs = [pl.BlockSpec((None, ts, d), lambda bi, i, *_: (bi, i, 0))]
        o_shapes = [jax.ShapeDtypeStruct((bsz, s, d), F32)]
    else:
        o_specs = [pl.BlockSpec((ts, d), lambda bi, i, *_: (i, bi))] * 2
        o_shapes = [jax.ShapeDtypeStruct((s, bsz * d), F32), jax.ShapeDtypeStruct((s, bsz * d), BF16)]
    kern = functools.partial(_combine_kernel, ts=ts)
    grid_spec = pltpu.PrefetchScalarGridSpec(
        num_scalar_prefetch=len(plan),
        grid=(bsz, s // ts),
        in_specs=[pl.BlockSpec(memory_space=pl.ANY),
                  pl.BlockSpec((ts, ROUTE_LANES), lambda bi, i, *_: (i, bi)),
                  pl.BlockSpec((ts, d), lambda bi, i, *_: (i, bi)),
                  pl.BlockSpec((1, d), lambda bi, i, *_: (0, 0)),
                  pl.BlockSpec((1, d), lambda bi, i, *_: (0, 0))],
        out_specs=o_specs,
        scratch_shapes=[pltpu.VMEM((2, TOP_K * ts * ROW_TILE, LANES), F32),
                        pltpu.VMEM((ts * ROW_TILE, LANES), F32), pltpu.VMEM((ts * ROW_TILE, LANES), F32),
                        pltpu.SemaphoreType.DMA((2,))],
    )
    return pl.pallas_call(
        kern,
        grid_spec=grid_spec,
        out_shape=o_shapes,
        compiler_params=_cparams(("arbitrary", "arbitrary")),
        name="moe_combine",
    )(*plan, ys, route, h1, g2, b2)


def kernel(x, ln_in_g, ln_in_b, w_in, s5_lambda_re, s5_lambda_im, s5_log_step, s5_b_re, s5_b_im,
           s5_c_re, s5_c_im, s5_d, w_glu, w_branch_ret, w_branch_s5, w_out, ln_mix_g, ln_mix_b,
           w_router_group, b_router_group, w_router_expert, b_router_expert, w_exp_gate, w_exp_up,
           w_exp_down, ln_ffn_g, ln_ffn_b):
    bsz, s, d = x.shape
    depth = w_in.shape[0]
    tabs = retention_tables(s)
    out = None
    for l in range(depth):
        if l == 0:
            proj, h_sb = ln_in_proj(x, ln_in_g, ln_in_b, w_in, l)
        else:
            proj = in_proj(hb_sb, w_in, l, bsz)
        ro = retention(proj, tabs, bsz, s)
        a_re, a_im, bb_re, bb_im = s5_params(s5_lambda_re[l], s5_lambda_im[l], s5_log_step[l],
                                             s5_b_re[l], s5_b_im[l])
        mats = s5_matrices(a_re, a_im, bb_re, bb_im, s5_c_re[l], s5_c_im[l])
        y = s5_scan(proj, mats, s5_d[l], bsz, s)
        pad = ROUTE_LANES - MOE_GROUPS - N_EXPERTS
        wr = jnp.concatenate([w_router_group[l], w_router_expert[l], jnp.zeros((d, pad), F32)], axis=1)
        wr_hi = wr.astype(BF16)
        wr = jnp.concatenate([wr_hi, (wr - wr_hi.astype(F32)).astype(BF16)], axis=1)
        br = jnp.concatenate([b_router_group[l], b_router_expert[l], jnp.zeros((pad,), F32)]).reshape(1, -1)
        h1, route, cnt, blk, lp = mix_out(ro, y, proj, h_sb, w_glu, w_branch_ret, w_branch_s5, w_out, l,
                                          ln_mix_g[l], ln_mix_b[l], wr, br, bsz)
        last = l == depth - 1
        plan, pads, tile_expert, n_used = moe_plan(lp, cnt, blk, bsz, s)
        xs = moe_dispatch(h1, plan, pads, n_used, bsz)
        ys = moe_experts(xs, tile_expert, n_used, w_exp_gate, w_exp_up, w_exp_down, l)
        outs = moe_combine(ys, plan, route, h1, ln_ffn_g[l], ln_ffn_b[l], bsz, last)
        if last:
            out = outs[0]
        else:
            h_sb, hb_sb = outs
    return out
```

```python
import functools
import math

import jax
import jax.numpy as jnp
import numpy as np
from jax import lax
from jax.experimental import pallas as pl
from jax.experimental.pallas import tpu as pltpu

D_MODEL = 1024
CHUNK = 64
RET_HEADS = 8
RET_QK = 512
RET_V = 1024
RET_QK_DIM = 64
RET_V_DIM = 128
ROPE_BASE = 10000.0
S5_WIDTH = 512
S5_GROUP_CH = 16
S5_STATE = 64
MOE_GROUPS = 4
EXPERTS_PER_GROUP = 8
N_EXPERTS = 32
LN_EPS = 1e-5
HEAD_NORM_EPS = 1e-6
DEPTH = 2
DN_ALPHA = (2 * DEPTH) ** 0.25
IN_WIDTH = 2 * RET_QK + 2 * RET_V + S5_WIDTH + 2 * D_MODEL
PROJ_BLK = 512
N_PROJ_BLK = IN_WIDTH // PROJ_BLK
INPROJ_BATCHES = 2

RET_SUPER = 512
S5_TT = 128
S5_COLS = 128
S5_NSLICE = S5_WIDTH // S5_COLS
S5_SLICE_STATE = (S5_COLS // S5_GROUP_CH) * S5_STATE
ROUTE_LANES = 128
TOP_K = 2
MOE_TM = 512
MOE_TS = 512
RUN_BITS = MOE_TS.bit_length()
PAD_BITS = (MOE_TM - 1).bit_length()
LONG_RUN = 64
LANES = 128
SUBLANES = 8
ROW_TILE = D_MODEL // LANES
VMEM_LIMIT = 56 * 1024 * 1024

F32 = jnp.float32
BF16 = jnp.bfloat16


def _cparams(sem):
    return pltpu.CompilerParams(dimension_semantics=sem, vmem_limit_bytes=VMEM_LIMIT)


def _layer_norm(x, g, b):
    mu = jnp.mean(x, axis=-1, keepdims=True)
    xc = x - mu
    var = jnp.mean(xc * xc, axis=-1, keepdims=True)
    return xc * lax.rsqrt(var + LN_EPS) * g + b


def _inproj_kernel(h_ref, w_ref, o_ref):
    w = w_ref[...].astype(BF16)
    d = w.shape[0]
    for k in range(INPROJ_BATCHES):
        hk = h_ref[:, k * d:(k + 1) * d]
        o_ref[:, k * PROJ_BLK:(k + 1) * PROJ_BLK] = jnp.dot(hk, w, preferred_element_type=F32).astype(BF16)


def _ln_inproj_kernel(x_ref, g_ref, b_ref, w_ref, o_ref, h_ref, hb_ref):
    @pl.when(pl.program_id(2) == 0)
    def _():
        h = _layer_norm(x_ref[...], g_ref[...], b_ref[...])
        h_ref[...] = h
        hb_ref[...] = h.astype(BF16)

    w = w_ref[...].astype(BF16)
    o_ref[...] = jnp.dot(hb_ref[...], w, preferred_element_type=F32).astype(BF16)


def ln_in_proj(x, g, b, w, layer, ts=2048):
    bsz, s, d = x.shape
    n = w.shape[-1]
    nb = n // PROJ_BLK
    return pl.pallas_call(
        _ln_inproj_kernel,
        grid=(bsz, s // ts, nb),
        in_specs=[pl.BlockSpec((None, ts, d), lambda bi, i, j: (bi, i, 0)),
                  pl.BlockSpec((1, d), lambda bi, i, j: (0, 0)),
                  pl.BlockSpec((1, d), lambda bi, i, j: (0, 0)),
                  pl.BlockSpec((None, d, PROJ_BLK), lambda bi, i, j: (layer, 0, j))],
        out_specs=[pl.BlockSpec((ts, PROJ_BLK), lambda bi, i, j: (i, j * bsz + bi)),
                   pl.BlockSpec((ts, d), lambda bi, i, j: (i, bi))],
        out_shape=[jax.ShapeDtypeStruct((s, bsz * n), BF16),
                   jax.ShapeDtypeStruct((s, bsz * d), F32)],
        scratch_shapes=[pltpu.VMEM((ts, d), BF16)],
        compiler_params=_cparams(("parallel", "parallel", "arbitrary")),
        name="ln_in_proj",
    )(x, g.reshape(1, d), b.reshape(1, d), w)


def in_proj(hb_sb, w, layer, bsz, ts=2048):
    s = hb_sb.shape[0]
    d, n = w.shape[-2:]
    nb = n // PROJ_BLK
    kb = INPROJ_BATCHES
    return pl.pallas_call(
        _inproj_kernel,
        grid=(bsz // kb, s // ts, nb),
        in_specs=[pl.BlockSpec((ts, kb * d), lambda b, i, j: (i, b)),
                  pl.BlockSpec((None, d, PROJ_BLK), lambda b, i, j: (layer, 0, j))],
        out_specs=pl.BlockSpec((ts, kb * PROJ_BLK), lambda b, i, j: (i, (j * bsz) // kb + b)),
        out_shape=jax.ShapeDtypeStruct((s, bsz * n), BF16),
        compiler_params=_cparams(("parallel", "parallel", "arbitrary")),
        name="in_proj",
    )(hb_sb, w)


def _swap_halves(x):
    lane = lax.broadcasted_iota(jnp.int32, x.shape, 1)
    first = (lane % RET_QK_DIM) < (RET_QK_DIM // 2)
    n = x.shape[1]
    return jnp.where(first, pltpu.roll(x, n - RET_QK_DIM // 2, 1), pltpu.roll(x, RET_QK_DIM // 2, 1))


def _retention_kernel(q_ref, k_ref, v0_ref, v1_ref, g0_ref, g1_ref, cos_ref, sin_ref,
                      qd_ref, kd_ref, mask_ref, cd_ref, o_ref, state_ref):
    @pl.when(pl.program_id(1) == 0)
    def _():
        state_ref[...] = jnp.zeros_like(state_ref)

    cos = cos_ref[...]
    sin = sin_ref[...]
    q = q_ref[...].astype(F32)
    k = k_ref[...].astype(F32)
    q = q * cos + _swap_halves(q) * sin
    k = k * cos + _swap_halves(k) * sin
    qb = q.astype(BF16)
    kb = k.astype(BF16)
    qdb = (q * qd_ref[...]).astype(BF16)
    kdb = (k * kd_ref[...]).astype(BF16)
    for hd in range(RET_HEADS):
        qs = slice(hd * RET_QK_DIM, (hd + 1) * RET_QK_DIM)
        half, off = divmod(hd * RET_V_DIM, PROJ_BLK)
        vs = slice(off, off + RET_V_DIM)
        vh = (v0_ref, v1_ref)[half][:, vs]
        gh = (g0_ref, g1_ref)[half][:, vs].astype(F32)
        sc = lax.dot_general(qb[:, qs], kb[:, qs], (((1,), (1,)), ((), ())),
                             preferred_element_type=F32)
        sc = (sc * mask_ref[hd]).astype(BF16)
        st = state_ref[hd]
        o = jnp.dot(sc, vh, preferred_element_type=F32)
        o = o + jnp.dot(qdb[:, qs], st.astype(BF16), preferred_element_type=F32)
        kv = lax.dot_general(kdb[:, qs], vh, (((0,), (0,)), ((), ())),
                             preferred_element_type=F32)
        state_ref[hd] = st * cd_ref[hd] + kv
        mu = jnp.mean(o, axis=-1, keepdims=True)
        oc = o - mu
        var = jnp.mean(oc * oc, axis=-1, keepdims=True)
        on = oc * lax.rsqrt(var + HEAD_NORM_EPS)
        o_ref[:, hd * RET_V_DIM:(hd + 1) * RET_V_DIM] = (gh * jax.nn.sigmoid(gh) * on).astype(BF16)


def retention(proj_sb, tabs, bsz, s):
    cos_t, sin_t, qd_t, kd_t, mask, cd = tabs
    L = RET_SUPER

    def pspec(col):
        return pl.BlockSpec((L, PROJ_BLK), lambda b, i, col=col: (i, col * bsz + b))

    full2 = pl.BlockSpec((L, RET_QK), lambda b, i: (0, 0))
    return pl.pallas_call(
        _retention_kernel,
        grid=(bsz, s // L),
        in_specs=[pspec(0), pspec(1), pspec(2), pspec(3), pspec(4), pspec(5),
                  pl.BlockSpec((L, RET_QK), lambda b, i: (i, 0)),
                  pl.BlockSpec((L, RET_QK), lambda b, i: (i, 0)),
                  full2, full2,
                  pl.BlockSpec((RET_HEADS, L, L), lambda b, i: (0, 0, 0)),
                  pl.BlockSpec((RET_HEADS, 1, RET_V_DIM), lambda b, i: (0, 0, 0))],
        out_specs=pl.BlockSpec((L, RET_V), lambda b, i: (i, b)),
        out_shape=jax.ShapeDtypeStruct((s, bsz * RET_V), BF16),
        scratch_shapes=[pltpu.VMEM((RET_HEADS, RET_QK_DIM, RET_V_DIM), F32)],
        compiler_params=_cparams(("parallel", "arbitrary")),
        name="retention",
    )(proj_sb, proj_sb, proj_sb, proj_sb, proj_sb, proj_sb, cos_t, sin_t, qd_t, kd_t, mask, cd)


def retention_tables(s):
    L = RET_SUPER
    half = RET_QK_DIM // 2
    inv_freq = ROPE_BASE ** (-np.arange(half, dtype=np.float64) / half)
    ang = np.arange(s, dtype=np.float64)[:, None] * inv_freq[None, :]
    cos, sin = np.cos(ang), np.sin(ang)
    cos_t = np.tile(np.concatenate([cos, cos], -1), (1, RET_HEADS))
    sin_t = np.tile(np.concatenate([-sin, sin], -1), (1, RET_HEADS))
    log_gamma = np.log1p(-(2.0 ** (-5.0 - np.arange(RET_HEADS, dtype=np.float64))))
    pos = np.arange(L, dtype=np.float64)
    qd = np.exp(log_gamma[None, :] * (pos + 1.0)[:, None])
    k_scale = RET_QK_DIM ** -0.5
    kd = np.exp(log_gamma[None, :] * (L - 1.0 - pos)[:, None]) * k_scale
    qd_t = np.repeat(qd, RET_QK_DIM, axis=1)
    kd_t = np.repeat(kd, RET_QK_DIM, axis=1)
    chunk_id = np.arange(L) // CHUNK
    visible = (chunk_id[None, :] <= chunk_id[:, None]).astype(np.float64)
    mask = np.exp(log_gamma[:, None, None] * np.abs(pos[:, None] - pos[None, :])) * visible[None] * k_scale
    cd = np.broadcast_to(np.exp(log_gamma * L)[:, None, None], (RET_HEADS, 1, RET_V_DIM))
    return tuple(jnp.asarray(a, dtype=F32) for a in (cos_t, sin_t, qd_t, kd_t, mask, cd))


def _s5_param_kernel(lre_ref, lim_ref, ls_ref, bre_ref, bim_ref, are_ref, aim_ref, bbre_ref, bbim_ref):
    lam_re = jnp.minimum(lre_ref[...], -1e-4)
    lam_im = lim_ref[...]
    step = jnp.exp(ls_ref[...])
    mag = jnp.exp(lam_re * step)
    ang = lam_im * step
    ab_re = mag * jnp.cos(ang)
    ab_im = mag * jnp.sin(ang)
    den = lam_re * lam_re + lam_im * lam_im
    n_re = ab_re - 1.0
    zc_re = (n_re * lam_re + ab_im * lam_im) / den
    zc_im = (ab_im * lam_re - n_re * lam_im) / den
    are_ref[...] = ab_re
    aim_ref[...] = ab_im
    b_re = bre_ref[...]
    b_im = bim_ref[...]
    bbre_ref[...] = zc_re * b_re - zc_im * b_im
    bbim_ref[...] = zc_re * b_im + zc_im * b_re


def s5_params(lam_re, lam_im, log_step, b_re, b_im):
    g, n = lam_re.shape
    c = b_re.shape[-1]
    outs = pl.pallas_call(
        _s5_param_kernel,
        out_shape=[jax.ShapeDtypeStruct((g, 1, n), F32), jax.ShapeDtypeStruct((g, 1, n), F32),
                   jax.ShapeDtypeStruct((g, c, n), F32), jax.ShapeDtypeStruct((g, c, n), F32)],
        name="s5_params",
    )(lam_re.reshape(g, 1, n), lam_im.reshape(g, 1, n), log_step.reshape(g, 1, 1),
      jnp.swapaxes(b_re, 1, 2), jnp.swapaxes(b_im, 1, 2))
    return outs


def _block_diag(x):
    ns, gl, r, c = x.shape
    eye = jnp.eye(gl, dtype=x.dtype)
    return jnp.einsum('sgrc,gh->sgrhc', x, eye).reshape(ns, gl * r, gl * c)


def s5_matrices(a_re, a_im, bb_re, bb_im, c_re, c_im):
    gl = S5_COLS // S5_GROUP_CH
    ns = S5_NSLICE
    bre = _block_diag(bb_re.reshape(ns, gl, S5_GROUP_CH, S5_STATE))
    bim = _block_diag(bb_im.reshape(ns, gl, S5_GROUP_CH, S5_STATE))
    bq = jnp.concatenate([bre, bim], axis=-1).astype(BF16)
    cre = _block_diag(jnp.swapaxes(c_re, 1, 2).reshape(ns, gl, S5_STATE, S5_GROUP_CH))
    cim = _block_diag(jnp.swapaxes(c_im, 1, 2).reshape(ns, gl, S5_STATE, S5_GROUP_CH))
    cq = jnp.concatenate([cre, -cim], axis=1).astype(BF16)
    are = a_re.reshape(ns, 1, S5_SLICE_STATE)
    aim = a_im.reshape(ns, 1, S5_SLICE_STATE)
    return bq, cq, are, aim


def _s5_kernel(*refs, bsz, tt):
    u_refs = refs[:bsz]
    bq_ref, cq_ref, are_ref, aim_ref, d_ref, y_ref, us_ref, ys_ref, bu_ref, st_ref = refs[bsz:]

    @pl.when(pl.program_id(0) == 0)
    def _():
        st_ref[...] = jnp.zeros_like(st_ref)

    for b in range(bsz):
        ub = u_refs[b][...].astype(F32)
        for cs in range(S5_NSLICE):
            us_ref[cs, pl.ds(b, tt, stride=bsz), :] = ub[:, cs * S5_COLS:(cs + 1) * S5_COLS]

    ns2 = S5_SLICE_STATE
    for cs in range(S5_NSLICE):
        cols = slice(cs * S5_COLS, (cs + 1) * S5_COLS)
        uf = us_ref[cs]
        bu_ref[cs] = jnp.dot(uf.astype(BF16), bq_ref[cs], preferred_element_type=F32)
        a_re = jnp.broadcast_to(are_ref[cs], (bsz, ns2))
        a_im = jnp.broadcast_to(aim_ref[cs], (bsz, ns2))

        def step(t, carry):
            h_re, h_im = carry
            rows = pl.ds(pl.multiple_of(t * bsz, bsz), bsz)
            n_re = a_re * h_re - a_im * h_im + bu_ref[cs, rows, 0:ns2]
            n_im = a_re * h_im + a_im * h_re + bu_ref[cs, rows, ns2:2 * ns2]
            bu_ref[cs, rows, 0:ns2] = n_re
            bu_ref[cs, rows, ns2:2 * ns2] = n_im
            return n_re, n_im

        h_re, h_im = lax.fori_loop(0, tt, step, (st_ref[cs, 0], st_ref[cs, 1]), unroll=True)
        st_ref[cs, 0] = h_re
        st_ref[cs, 1] = h_im
        y = jnp.dot(bu_ref[cs].astype(BF16), cq_ref[cs], preferred_element_type=F32)
        ys_ref[cs] = y + d_ref[:, cols] * uf

    for b in range(bsz):
        for cs in range(S5_NSLICE):
            lo = b * S5_WIDTH + cs * S5_COLS
            y_ref[:, lo:lo + S5_COLS] = ys_ref[cs, pl.ds(b, tt, stride=bsz), :]


def s5_scan(proj_sb, mats, d_skip, bsz, s):
    bq, cq, are, aim = mats
    tt = S5_TT
    rows = tt * bsz
    kern = functools.partial(_s5_kernel, bsz=bsz, tt=tt)
    u_specs = [pl.BlockSpec((tt, PROJ_BLK), lambda i, b=b: (i, 6 * bsz + b)) for b in range(bsz)]
    return pl.pallas_call(
        kern,
        grid=(s // tt,),
        in_specs=u_specs + [
                  pl.BlockSpec(bq.shape, lambda i: (0, 0, 0)),
                  pl.BlockSpec(cq.shape, lambda i: (0, 0, 0)),
                  pl.BlockSpec(are.shape, lambda i: (0, 0, 0)),
                  pl.BlockSpec(aim.shape, lambda i: (0, 0, 0)),
                  pl.BlockSpec((1, S5_WIDTH), lambda i: (0, 0))],
        out_specs=pl.BlockSpec((tt, bsz * S5_WIDTH), lambda i: (i, 0)),
        out_shape=jax.ShapeDtypeStruct((s, bsz * S5_WIDTH), F32),
        scratch_shapes=[pltpu.VMEM((S5_NSLICE, rows, S5_COLS), F32),
                        pltpu.VMEM((S5_NSLICE, rows, S5_COLS), F32),
                        pltpu.VMEM((S5_NSLICE, rows, 2 * S5_SLICE_STATE), F32),
                        pltpu.VMEM((S5_NSLICE, 2, bsz, S5_SLICE_STATE), F32)],
        compiler_params=_cparams(("arbitrary",)),
        name="s5_scan",
    )(*([proj_sb] * bsz), bq, cq, are, aim, d_skip.reshape(1, S5_WIDTH))


def _gelu_tanh(x):
    c = math.sqrt(2.0 / math.pi)
    return 0.5 * x * (1.0 + jnp.tanh(c * (x + 0.044715 * (x * x * x))))


def _route(logits):
    lane = lax.broadcasted_iota(jnp.int32, logits.shape, 1)
    neg = jnp.float32(-jnp.inf)
    big = jnp.int32(1 << 20)
    is_g = lane < MOE_GROUPS
    lg = jnp.where(is_g, logits, neg)
    mg = jnp.max(lg, axis=-1, keepdims=True)
    sg = jnp.sum(jnp.where(is_g, jnp.exp(lg - mg), 0.0), axis=-1, keepdims=True)
    g_top = 1.0 / sg
    g_idx = jnp.min(jnp.where(lg == mg, lane, big), axis=-1, keepdims=True)
    lo = MOE_GROUPS + g_idx * EXPERTS_PER_GROUP
    in_grp = (lane >= lo) & (lane < lo + EXPERTS_PER_GROUP)
    le = jnp.where(in_grp, logits, neg)
    m1 = jnp.max(le, axis=-1, keepdims=True)
    se = jnp.sum(jnp.where(in_grp, jnp.exp(le - m1), 0.0), axis=-1, keepdims=True)
    i1 = jnp.min(jnp.where(le == m1, lane, big), axis=-1, keepdims=True)
    le2 = jnp.where(lane == i1, neg, le)
    m2 = jnp.max(le2, axis=-1, keepdims=True)
    i2 = jnp.min(jnp.where(le2 == m2, lane, big), axis=-1, keepdims=True)
    p1 = 1.0 / se
    p2 = jnp.exp(m2 - m1) / se
    tot = p1 + p2
    w1 = g_top * (p1 / tot)
    w2 = g_top * (p2 / tot)
    return lane, i1, i2, w1, w2


def _mix_kernel(ro_ref, y_ref, gr0_ref, gr1_ref, gs0_ref, gs1_ref, h_ref,
                wglu_ref, wbr_ref, wbs_ref, wout_ref, g_ref, b_ref, wr_ref, br_ref, tri_ref,
                h1_ref, route_ref, cnt_ref, blk_ref, lp_ref,
                run_ref, wglu_b, wbr_b, wbs_b, wout_b):
    @pl.when((pl.program_id(0) == 0) & (pl.program_id(1) == 0))
    def _():
        run_ref[...] = jnp.zeros_like(run_ref)
        wglu_b[...] = wglu_ref[...].astype(BF16)
        wbr_b[...] = wbr_ref[...].astype(BF16)
        wbs_b[...] = wbs_ref[...].astype(BF16)
        wout_b[...] = wout_ref[...].astype(BF16)

    z = _gelu_tanh(y_ref[...])
    zg = jnp.dot(z.astype(BF16), wglu_b[...], preferred_element_type=F32)
    zz = (z * jax.nn.sigmoid(zg)).astype(BF16)
    s5b = jnp.dot(zz, wbs_b[...], preferred_element_type=F32)
    rb = jnp.dot(ro_ref[...], wbr_b[...], preferred_element_type=F32)
    gr = jnp.concatenate([gr0_ref[...], gr1_ref[...]], axis=-1).astype(F32)
    gs = jnp.concatenate([gs0_ref[...], gs1_ref[...]], axis=-1).astype(F32)
    merged = jax.nn.sigmoid(gr) * rb + jax.nn.sigmoid(gs) * s5b
    mix = jnp.dot(merged.astype(BF16), wout_b[...], preferred_element_type=F32)
    h1 = _layer_norm(DN_ALPHA * h_ref[...] + mix, g_ref[...], b_ref[...])
    h1_ref[...] = h1
    h_hi = h1.astype(BF16)
    h_lo = (h1 - h_hi.astype(F32)).astype(BF16)
    both = jnp.dot(h_hi, wr_ref[...], preferred_element_type=F32)
    logits = (both[:, :ROUTE_LANES] + both[:, ROUTE_LANES:]
              + jnp.dot(h_lo, wr_ref[:, :ROUTE_LANES], preferred_element_type=F32)) + br_ref[...]
    lane, i1, i2, w1, w2 = _route(logits)
    oh1 = lane == i1
    oh2 = lane == i2
    oh = jnp.where(oh1 | oh2, 1.0, 0.0)
    rank_in_blk = jnp.dot(tri_ref[...], oh.astype(BF16), preferred_element_type=F32)
    blk_cnt = jnp.broadcast_to(jnp.sum(oh, axis=0, keepdims=True), run_ref.shape)
    lane8 = lax.broadcasted_iota(jnp.int32, run_ref.shape, 1)
    incl = blk_cnt
    shift = 1
    while shift < ROUTE_LANES:
        incl = incl + jnp.where(lane8 >= shift, pltpu.roll(incl, shift, 1), 0.0)
        shift *= 2
    lstart = incl - blk_cnt
    pos = lstart[0:1, :] + rank_in_blk
    l1 = jnp.sum(jnp.where(oh1, pos, 0.0), axis=-1, keepdims=True)
    l2 = jnp.sum(jnp.where(oh2, pos, 0.0), axis=-1, keepdims=True)
    run_before = run_ref[...]
    row8 = lax.broadcasted_iota(jnp.int32, run_ref.shape, 0)
    blk_ref[...] = jnp.where(row8 == 0, run_before,
                             jnp.where(row8 == 1, blk_cnt, jnp.where(row8 == 2, lstart, 0.0)))
    cnt = run_before + blk_cnt
    run_ref[...] = cnt
    cnt_ref[...] = cnt
    vals = (i1.astype(F32) - MOE_GROUPS, i2.astype(F32) - MOE_GROUPS, w1, w2)
    route = jnp.zeros(logits.shape, F32)
    for k, v in enumerate(vals):
        route = jnp.where(lane == k, v, route)
    route_ref[...] = route
    pos_t = jnp.transpose(jnp.where(lane == 0, l1, jnp.where(lane == 1, l2, 0.0)))
    lp_ref[...] = (pos_t[0:SUBLANES, :] * ROW_TILE).astype(jnp.int32)


def mix_out(ro, y, proj, h, wglu, wbr, wbs, wout, layer, g, b, wr, br, bsz, ts=MOE_TS):
    s = h.shape[0]
    d = D_MODEL

    def const(a):
        return pl.BlockSpec(a.shape, lambda bi, i: (0,) * a.ndim)

    def stacked(w):
        return pl.BlockSpec((None,) + w.shape[1:], lambda bi, i: (layer, 0, 0))

    def pspec(col):
        return pl.BlockSpec((ts, PROJ_BLK), lambda bi, i, col=col: (i, col * bsz + bi))

    def tok(w):
        return pl.BlockSpec((ts, w), lambda bi, i: (i, bi))

    g2, b2 = g.reshape(1, d), b.reshape(1, d)
    tri = jnp.asarray(np.tril(np.ones((ts, ts)), -1), dtype=BF16)
    return pl.pallas_call(
        _mix_kernel,
        grid=(bsz, s // ts),
        in_specs=[tok(RET_V), tok(S5_WIDTH), pspec(7), pspec(8), pspec(9), pspec(10), tok(d),
                  stacked(wglu), stacked(wbr), stacked(wbs), stacked(wout), const(g2), const(b2),
                  const(wr), const(br), const(tri)],
        out_specs=[tok(d), tok(ROUTE_LANES), pl.BlockSpec((SUBLANES, ROUTE_LANES), lambda bi, i: (0, 0)),
                   pl.BlockSpec((None, SUBLANES, ROUTE_LANES), lambda bi, i: (bi * (s // ts) + i, 0, 0)),
                   pl.BlockSpec((None, SUBLANES, ts), lambda bi, i: (bi * (s // ts) + i, 0, 0))],
        out_shape=[jax.ShapeDtypeStruct((s, bsz * d), F32),
                   jax.ShapeDtypeStruct((s, bsz * ROUTE_LANES), F32),
                   jax.ShapeDtypeStruct((SUBLANES, ROUTE_LANES), F32),
                   jax.ShapeDtypeStruct((bsz * (s // ts), SUBLANES, ROUTE_LANES), F32),
                   jax.ShapeDtypeStruct((bsz * (s // ts), SUBLANES, ts), jnp.int32)],
        scratch_shapes=[pltpu.VMEM((SUBLANES, ROUTE_LANES), F32)]
        + [pltpu.VMEM(w.shape[1:], BF16) for w in (wglu, wbr, wbs, wout)],
        compiler_params=_cparams(("arbitrary", "arbitrary")),
        name="mix_out",
    )(ro, y, proj, proj, proj, proj, h, wglu, wbr, wbs, wout, g2, b2, wr, br, tri)


def moe_rows(t):
    return TOP_K * t + N_EXPERTS * MOE_TM


def moe_plan(lp, cnt, blk, bsz, s):
    t = bsz * s
    experts = slice(MOE_GROUPS, MOE_GROUPS + N_EXPERTS)
    counts = cnt[0, experts].astype(jnp.int32)
    padded = ((counts + MOE_TM - 1) // MOE_TM) * MOE_TM
    ends = jnp.cumsum(padded)
    off = ends - padded
    run_start = (off[None, :] + blk[:, 0, experts].astype(jnp.int32)).reshape(-1)
    run_len = blk[:, 1, experts].astype(jnp.int32).reshape(-1)
    run_local = blk[:, 2, experts].astype(jnp.int32).reshape(-1)
    pads = (off + counts, padded - counts)
    n_tiles = moe_rows(t) // MOE_TM
    n_used = (ends[-1:] // MOE_TM).astype(jnp.int32)
    tile_start = jnp.arange(n_tiles, dtype=jnp.int32) * MOE_TM
    tile_expert = jnp.sum((ends[None, :] <= tile_start[:, None]).astype(jnp.int32), axis=1)
    tile_expert = jnp.minimum(tile_expert, N_EXPERTS - 1)
    return (lp[:, 0, :].reshape(t), lp[:, 1, :].reshape(t), run_start, run_len, run_local), pads, tile_expert, n_used


def _token_rows(start, n):
    return pl.ds(pl.multiple_of(start * ROW_TILE, ROW_TILE), n * ROW_TILE)


def _tile_at(row):
    return pl.ds(pl.multiple_of(row, ROW_TILE), ROW_TILE)


def _for_each_piece(n, nbits, fn):
    def pieces(bits):
        for bit in bits:
            size = 1 << bit
            done = (n >> (bit + 1)) << (bit + 1)

            @pl.when((n & size) != 0)
            def _(size=size, done=done):
                fn(done, size)

    long_bit = LONG_RUN.bit_length() - 1
    if nbits > long_bit:
        @pl.when(n >= LONG_RUN)
        def _():
            pieces(reversed(range(long_bit, nbits)))
    pieces(reversed(range(min(nbits, long_bit))))


def _run_copies(blk, run_refs, make_copy):
    run_start_ref, run_len_ref, run_local_ref = run_refs
    for e in range(N_EXPERTS):
        idx = blk * N_EXPERTS + e
        g0 = run_start_ref[idx]
        l0 = run_local_ref[idx]
        _for_each_piece(run_len_ref[idx], RUN_BITS,
                        lambda off, size, g0=g0, l0=l0: make_copy(l0 + off, g0 + off, size).start())


def _to_token_tiles(ref, x):
    n = x.shape[0]
    for j in range(ROW_TILE):
        ref[pl.ds(j, n, stride=ROW_TILE), :] = x[:, j * LANES:(j + 1) * LANES]


def _from_token_tiles(ref, n):
    return jnp.concatenate([ref[pl.ds(j, n, stride=ROW_TILE), :] for j in range(ROW_TILE)], axis=-1)


def _dispatch_kernel(lp1_ref, lp2_ref, run_start_ref, run_len_ref, run_local_ref,
                     pad_ref, pad_len_ref, nu_ref,
                     x_ref, xs_ref, xt_ref, srt_ref, zero_ref, sem, zsem, *, ts):
    run_refs = (run_start_ref, run_len_ref, run_local_ref)
    nsteps = pl.num_programs(0) * pl.num_programs(1)
    step = pl.program_id(0) * pl.num_programs(1) + pl.program_id(1)

    def zero_copy(start, size):
        return pltpu.make_async_copy(zero_ref.at[_token_rows(0, size)],
                                     xs_ref.at[_token_rows(start, size)], zsem)

    def zero_fill(action):
        for e in range(N_EXPERTS):
            p0 = pad_ref[e]
            _for_each_piece(pad_len_ref[e], PAD_BITS,
                            lambda off, size, p0=p0: action(zero_copy(p0 + off, size)))

        def tail_tile(r, c):
            action(zero_copy(r * MOE_TM, MOE_TM))
            return c

        lax.fori_loop(nu_ref[0], xs_ref.shape[0] // (MOE_TM * ROW_TILE), tail_tile, 0)

    @pl.when(step == 0)
    def _():
        zero_ref[...] = jnp.zeros_like(zero_ref)
        zero_fill(lambda cp: cp.start())

    @pl.when(step == nsteps - 1)
    def _():
        zero_fill(lambda cp: cp.wait())

    slot = step % 2
    _to_token_tiles(xt_ref, x_ref[...])
    base = step * ts

    def place(i, c):
        tile = xt_ref[_token_rows(i, 1), :]
        srt_ref[slot, _tile_at(lp1_ref[base + i]), :] = tile
        srt_ref[slot, _tile_at(lp2_ref[base + i]), :] = tile
        return c

    lax.fori_loop(0, ts, place, 0, unroll=8)

    def run_copy(local_start, sorted_start, size):
        return pltpu.make_async_copy(srt_ref.at[slot, _token_rows(local_start, size)],
                                     xs_ref.at[_token_rows(sorted_start, size)], sem.at[slot])

    _run_copies(step, run_refs, run_copy)

    def drain(sl):
        pltpu.make_async_copy(srt_ref.at[sl], xs_ref.at[_token_rows(0, TOP_K * ts)], sem.at[sl]).wait()

    @pl.when(step > 0)
    def _():
        drain(1 - slot)

    @pl.when(step == nsteps - 1)
    def _():
        drain(slot)


def moe_dispatch(h1, plan, pads, n_used, bsz, ts=MOE_TS):
    s = h1.shape[0]
    d = D_MODEL
    kern = functools.partial(_dispatch_kernel, ts=ts)
    grid_spec = pltpu.PrefetchScalarGridSpec(
        num_scalar_prefetch=len(plan) + len(pads) + 1,
        grid=(bsz, s // ts),
        in_specs=[pl.BlockSpec((ts, d), lambda bi, i, *_: (i, bi))],
        out_specs=pl.BlockSpec(memory_space=pl.ANY),
        scratch_shapes=[pltpu.VMEM((ts * ROW_TILE, LANES), F32),
                        pltpu.VMEM((2, TOP_K * ts * ROW_TILE, LANES), F32),
                        pltpu.VMEM((MOE_TM * ROW_TILE, LANES), F32),
                        pltpu.SemaphoreType.DMA((2,)), pltpu.SemaphoreType.DMA(())],
    )
    return pl.pallas_call(
        kern,
        grid_spec=grid_spec,
        out_shape=jax.ShapeDtypeStruct((moe_rows(bsz * s) * ROW_TILE, LANES), F32),
        compiler_params=_cparams(("arbitrary", "arbitrary")),
        name="moe_dispatch",
    )(*plan, *pads, n_used, h1)


def _experts_kernel(te_ref, nu_ref, xs_ref, wg_ref, wu_ref, wd_ref, ys_ref, wgb_ref, wub_ref, wdb_ref):
    r = pl.program_id(0)

    @pl.when(r < nu_ref[0])
    def _():
        changed = (r == 0) | (te_ref[r] != te_ref[jnp.maximum(r - 1, 0)])

        @pl.when(changed)
        def _():
            wgb_ref[...] = wg_ref[...].astype(BF16)
            wub_ref[...] = wu_ref[...].astype(BF16)
            wdb_ref[...] = wd_ref[...].astype(BF16)

        x = _from_token_tiles(xs_ref, MOE_TM).astype(BF16)
        gate = jnp.dot(x, wgb_ref[...], preferred_element_type=F32)
        up = jnp.dot(x, wub_ref[...], preferred_element_type=F32)
        act = (gate * jax.nn.sigmoid(gate) * up).astype(BF16)
        _to_token_tiles(ys_ref, jnp.dot(act, wdb_ref[...], preferred_element_type=F32))

    @pl.when(r >= nu_ref[0])
    def _():
        ys_ref[...] = jnp.zeros_like(ys_ref)


def moe_experts(xs, tile_expert, n_used, wg, wu, wd, layer):
    d, f = wg.shape[-2:]
    rows = xs.shape[0] // ROW_TILE
    blk = MOE_TM * ROW_TILE

    def tile(r, te, nu):
        return jnp.minimum(r, nu[0] - 1)

    def wspec(a, b):
        return pl.BlockSpec((None, None, a, b), lambda r, te, nu: (layer, te[tile(r, te, nu)], 0, 0))

    grid_spec = pltpu.PrefetchScalarGridSpec(
        num_scalar_prefetch=2,
        grid=(rows // MOE_TM,),
        in_specs=[pl.BlockSpec((blk, LANES), lambda r, te, nu: (tile(r, te, nu), 0)),
                  wspec(d, f), wspec(d, f), wspec(f, d)],
        out_specs=pl.BlockSpec((blk, LANES), lambda r, te, nu: (r, 0)),
        scratch_shapes=[pltpu.VMEM((d, f), BF16), pltpu.VMEM((d, f), BF16), pltpu.VMEM((f, d), BF16)],
    )
    return pl.pallas_call(
        _experts_kernel,
        grid_spec=grid_spec,
        out_shape=jax.ShapeDtypeStruct(xs.shape, F32),
        compiler_params=_cparams(("arbitrary",)),
        name="moe_experts",
    )(tile_expert, n_used, xs, wg, wu, wd)


def _combine_kernel(lp1_ref, lp2_ref, run_start_ref, run_len_ref, run_local_ref,
                    ys_ref, route_ref, h_ref, g_ref, b_ref, *refs, ts):
    o_ref, maybe_ob_ref = refs[0], refs[1:-4]
    srt_ref, t1_ref, t2_ref, sem = refs[-4:]
    run_refs = (run_start_ref, run_len_ref, run_local_ref)
    nsteps = pl.num_programs(0) * pl.num_programs(1)
    step = pl.program_id(0) * pl.num_programs(1) + pl.program_id(1)

    def issue(blk, sl):
        def run_copy(local_start, sorted_start, size):
            return pltpu.make_async_copy(ys_ref.at[_token_rows(sorted_start, size)],
                                         srt_ref.at[sl, _token_rows(local_start, size)], sem.at[sl])
        _run_copies(blk, run_refs, run_copy)

    @pl.when(step == 0)
    def _():
        issue(0, 0)

    @pl.when(step + 1 < nsteps)
    def _():
        issue(step + 1, (step + 1) % 2)

    slot = step % 2
    pltpu.make_async_copy(ys_ref.at[_token_rows(0, TOP_K * ts)], srt_ref.at[slot], sem.at[slot]).wait()

    base = step * ts

    def pick(i, c):
        t1_ref[_token_rows(i, 1), :] = srt_ref[slot, _tile_at(lp1_ref[base + i]), :]
        t2_ref[_token_rows(i, 1), :] = srt_ref[slot, _tile_at(lp2_ref[base + i]), :]
        return c

    lax.fori_loop(0, ts, pick, 0, unroll=8)

    route = route_ref[...]
    w1 = route[:, 2:3]
    w2 = route[:, 3:4]
    ffn = w1 * _from_token_tiles(t1_ref, ts) + w2 * _from_token_tiles(t2_ref, ts)
    h2 = _layer_norm(DN_ALPHA * h_ref[...] + ffn, g_ref[...], b_ref[...])
    o_ref[...] = h2
    for ob_ref in maybe_ob_ref:
        ob_ref[...] = h2.astype(BF16)


def moe_combine(ys, plan, route, h1, g, b, bsz, last, ts=MOE_TS):
    s = h1.shape[0]
    d = D_MODEL
    g2, b2 = g.reshape(1, d), b.reshape(1, d)
    if last:
        o_specs = [pl.BlockSpec((None, ts, d), lambda bi, i, *_: (bi, i, 0))]
        o_shapes = [jax.ShapeDtypeStruct((bsz, s, d), F32)]
    else:
        o_specs = [pl.BlockSpec((ts, d), lambda bi, i, *_: (i, bi))] * 2
        o_shapes = [jax.ShapeDtypeStruct((s, bsz * d), F32), jax.ShapeDtypeStruct((s, bsz * d), BF16)]
    kern = functools.partial(_combine_kernel, ts=ts)
    grid_spec = pltpu.PrefetchScalarGridSpec(
        num_scalar_prefetch=len(plan),
        grid=(bsz, s // ts),
        in_specs=[pl.BlockSpec(memory_space=pl.ANY),
                  pl.BlockSpec((ts, ROUTE_LANES), lambda bi, i, *_: (i, bi)),
                  pl.BlockSpec((ts, d), lambda bi, i, *_: (i, bi)),
                  pl.BlockSpec((1, d), lambda bi, i, *_: (0, 0)),
                  pl.BlockSpec((1, d), lambda bi, i, *_: (0, 0))],
        out_specs=o_specs,
        scratch_shapes=[pltpu.VMEM((2, TOP_K * ts * ROW_TILE, LANES), F32),
                        pltpu.VMEM((ts * ROW_TILE, LANES), F32), pltpu.VMEM((ts * ROW_TILE, LANES), F32),
                        pltpu.SemaphoreType.DMA((2,))],
    )
    return pl.pallas_call(
        kern,
        grid_spec=grid_spec,
        out_shape=o_shapes,
        compiler_params=_cparams(("arbitrary", "arbitrary")),
        name="moe_combine",
    )(*plan, ys, route, h1, g2, b2)


def kernel(x, ln_in_g, ln_in_b, w_in, s5_lambda_re, s5_lambda_im, s5_log_step, s5_b_re, s5_b_im,
           s5_c_re, s5_c_im, s5_d, w_glu, w_branch_ret, w_branch_s5, w_out, ln_mix_g, ln_mix_b,
           w_router_group, b_router_group, w_router_expert, b_router_expert, w_exp_gate, w_exp_up,
           w_exp_down, ln_ffn_g, ln_ffn_b):
    bsz, s, d = x.shape
    depth = w_in.shape[0]
    tabs = retention_tables(s)
    out = None
    for l in range(depth):
        if l == 0:
            proj, h_sb = ln_in_proj(x, ln_in_g, ln_in_b, w_in, l)
        else:
            proj = in_proj(hb_sb, w_in, l, bsz)
        ro = retention(proj, tabs, bsz, s)
        a_re, a_im, bb_re, bb_im = s5_params(s5_lambda_re[l], s5_lambda_im[l], s5_log_step[l],
                                             s5_b_re[l], s5_b_im[l])
        mats = s5_matrices(a_re, a_im, bb_re, bb_im, s5_c_re[l], s5_c_im[l])
        y = s5_scan(proj, mats, s5_d[l], bsz, s)
        pad = ROUTE_LANES - MOE_GROUPS - N_EXPERTS
        wr = jnp.concatenate([w_router_group[l], w_router_expert[l], jnp.zeros((d, pad), F32)], axis=1)
        wr_hi = wr.astype(BF16)
        wr = jnp.concatenate([wr_hi, (wr - wr_hi.astype(F32)).astype(BF16)], axis=1)
        br = jnp.concatenate([b_router_group[l], b_router_expert[l], jnp.zeros((pad,), F32)]).reshape(1, -1)
        h1, route, cnt, blk, lp = mix_out(ro, y, proj, h_sb, w_glu, w_branch_ret, w_branch_s5, w_out, l,
                                          ln_mix_g[l], ln_mix_b[l], wr, br, bsz)
        last = l == depth - 1
        plan, pads, tile_expert, n_used = moe_plan(lp, cnt, blk, bsz, s)
        xs = moe_dispatch(h1, plan, pads, n_used, bsz)
        ys = moe_experts(xs, tile_expert, n_used, w_exp_gate, w_exp_up, w_exp_down, l)
        outs = moe_combine(ys, plan, route, h1, ln_ffn_g[l], ln_ffn_b[l], bsz, last)
        if last:
            out = outs[0]
        else:
            h_sb, hb_sb = outs
    return out
```
